```python
import math
import jax, jax.numpy as jnp
from jax import lax
import numpy as np

D_MODEL = 2048
BATCH = 2
SEQ = 4096
DEPTH = 1

HYENA_WIDTH = D_MODEL // 2
HYENA_ORDER = 2
FILTER_BANDS = 16
FILTER_EMB_DIM = 1 + 2 * FILTER_BANDS
FILTER_HIDDEN = 64
DECAY_TARGET = 1e-2
FAST_DECAY_PCT = 0.3
SLOW_DECAY_PCT = 1.5

HEAD_DIM = 128
ATTN_GROUPS = ((128, 1), (512, 4), (2048, 16))
HEADS_PER_GROUP = 4
N_ATTN_HEADS = HEADS_PER_GROUP * len(ATTN_GROUPS)
ATTN_OUT_WIDTH = HEADS_PER_GROUP * HEAD_DIM
ROT_DIM = HEAD_DIM // 4
ROPE_THETA = 500000.0

IN_WIDTH = 3 * HYENA_WIDTH + 3 * N_ATTN_HEADS * HEAD_DIM

N_EXPERT_GROUPS = 8
EXPERTS_PER_GROUP = 8
N_EXPERTS = N_EXPERT_GROUPS * EXPERTS_PER_GROUP
TOP_K = 2
EXPERT_FF = D_MODEL // 4
EXPERT_ROW_BLOCK = 128

RMS_EPS = 1e-6
NEG_INF = -1e30

kernel_name = 'hybrid_hyena_dilated_attn_hmoe_block'


def rmsnorm(x, g):
    xf = x.astype(jnp.float32)
    y = xf * lax.rsqrt(jnp.mean(xf * xf, axis=-1, keepdims=True) + RMS_EPS)
    return (y * g.astype(jnp.float32)).astype(x.dtype)


def short_conv(u, w, b):
    up = jnp.pad(u, ((0, 0), (1, 1), (0, 0)))
    return up[:, :-2] * w[0] + up[:, 1:-1] * w[1] + up[:, 2:] * w[2] + b


def hyena_filters(L, w1, b1, w2, b2, w3, freq):
    f32 = jnp.float32
    t = jnp.linspace(0.0, 1.0, L, dtype=f32)[:, None]
    w = 2.0 * math.pi * jnp.arange(L, dtype=f32)[:, None] / L
    bands = jnp.linspace(1e-4, FILTER_BANDS - 1, FILTER_BANDS, dtype=f32)[None, :]
    feats = jnp.concatenate([t, jnp.cos(bands * w), -jnp.sin(bands * w)], axis=-1)
    hid = jnp.sin(freq * (feats @ w1 + b1))
    hid = jnp.sin(freq * (hid @ w2 + b2))
    h = (hid @ w3).astype(f32).reshape(L, HYENA_ORDER, 2, HYENA_WIDTH)
    max_decay = math.log(DECAY_TARGET) / FAST_DECAY_PCT
    min_decay = math.log(DECAY_TARGET) / SLOW_DECAY_PCT
    deltas = jnp.abs(jnp.linspace(min_decay, max_decay, HYENA_WIDTH, dtype=f32))
    decay = jnp.exp(-t * deltas[None, :])
    return h * decay[:, None, None, :]


def bidir_long_conv(z, hf, hb, skip):
    L = z.shape[1]
    k = jnp.concatenate([hf, jnp.zeros((1, hf.shape[1]), hf.dtype), jnp.flip(hb[1:], axis=0)], axis=0)
    kf = jnp.fft.rfft(k, n=2 * L, axis=0)
    zf32 = z.astype(jnp.float32)
    zf = jnp.fft.rfft(zf32, n=2 * L, axis=1)
    y = jnp.fft.irfft(zf * kf[None], n=2 * L, axis=1)[:, :L]
    return y + zf32 * skip.astype(jnp.float32)


def partial_rotary(t, positions):
    half = ROT_DIM // 2
    inv_freq = jnp.power(ROPE_THETA, -2.0 * jnp.arange(half, dtype=jnp.float32) / ROT_DIM)
    ang = positions.astype(jnp.float32)[:, :, None] * inv_freq
    cos = jnp.cos(ang)[:, :, None, :]
    sin = jnp.sin(ang)[:, :, None, :]
    t1 = t[..., :half].astype(jnp.float32)
    t2 = t[..., half:ROT_DIM].astype(jnp.float32)
    rot = jnp.concatenate([t1 * cos - t2 * sin, t2 * cos + t1 * sin], axis=-1).astype(t.dtype)
    return jnp.concatenate([rot, t[..., ROT_DIM:]], axis=-1)


def dilated_window_attention(q, k, v, window, dilation):
    B, S, H, E = q.shape
    n = S // dilation
    half = window // (2 * dilation)
    blk = half
    nb = -(-n // blk)
    n_pad = nb * blk

    def to_sub(t):
        return t.reshape(B, n, dilation, H, E).transpose(0, 2, 3, 1, 4)

    qs, ks, vs = to_sub(q), to_sub(k), to_sub(v)
    qb = jnp.pad(qs, ((0, 0), (0, 0), (0, 0), (0, n_pad - n), (0, 0))).reshape(B, dilation, H, nb, blk, E)

    def key_windows(t):
        tp = jnp.pad(t, ((0, 0), (0, 0), (0, 0), (blk, n_pad - n + blk), (0, 0)))
        tp = tp.reshape(B, dilation, H, nb + 2, blk, E)
        return jnp.concatenate([tp[:, :, :, :-2], tp[:, :, :, 1:-1], tp[:, :, :, 2:]], axis=4)

    kw, vw = key_windows(ks), key_windows(vs)
    qi = jnp.arange(nb)[:, None] * blk + jnp.arange(blk)[None, :]
    kj = jnp.arange(nb)[:, None] * blk - blk + jnp.arange(3 * blk)[None, :]
    valid = ((jnp.abs(qi[:, :, None] - kj[:, None, :]) <= half)
             & (kj[:, None, :] >= 0) & (kj[:, None, :] < n))
    s = jnp.einsum('bdhnqe,bdhnke->bdhnqk', qb, kw).astype(jnp.float32) * (HEAD_DIM ** -0.5)
    s = jnp.where(valid, s, NEG_INF)
    m = jnp.max(s, axis=-1, keepdims=True)
    p = jnp.exp(s - m)
    l = jnp.sum(p, axis=-1)
    o = jnp.einsum('bdhnqk,bdhnke->bdhnqe', p, vw.astype(jnp.float32)) / l[..., None]
    lse = m[..., 0] + jnp.log(l)
    o = o.reshape(B, dilation, H, n_pad, E)[:, :, :, :n].transpose(0, 3, 1, 2, 4).reshape(B, S, H, E)
    lse = lse.reshape(B, dilation, H, n_pad)[:, :, :, :n].transpose(0, 3, 1, 2).reshape(B, S, H)
    return o, lse


def token_mixer(h, positions, w_in, conv_w, conv_b, filt_w1, filt_b1, filt_w2, filt_b2, filt_w3,
                filt_freq, hyena_skip, w_branch_gate, b_branch_gate, w_hy_o, w_at_o, w_out):
    B, S, _ = h.shape
    proj = h @ w_in

    hy = short_conv(proj[..., :3 * HYENA_WIDTH], conv_w, conv_b)
    v_h, x1, x2 = jnp.split(hy, 3, axis=-1)
    filt = hyena_filters(S, filt_w1, filt_b1, filt_w2, filt_b2, filt_w3, filt_freq)
    z = x1 * bidir_long_conv(v_h, filt[:, 0, 0], filt[:, 0, 1], hyena_skip[0])
    z = x2 * bidir_long_conv(z, filt[:, 1, 0], filt[:, 1, 1], hyena_skip[1])
    y_hy = z.astype(h.dtype)

    qkv = proj[..., 3 * HYENA_WIDTH:].reshape(B, S, 3, N_ATTN_HEADS, HEAD_DIM)
    q = partial_rotary(qkv[:, :, 0], positions)
    k = partial_rotary(qkv[:, :, 1], positions)
    v = qkv[:, :, 2]
    outs, lses = [], []
    for gi, (window, dilation) in enumerate(ATTN_GROUPS):
        hs = slice(gi * HEADS_PER_GROUP, (gi + 1) * HEADS_PER_GROUP)
        o, lse = dilated_window_attention(q[:, :, hs], k[:, :, hs], v[:, :, hs], window, dilation)
        outs.append(o)
        lses.append(lse)
    alpha = jax.nn.softmax(jnp.stack(lses, axis=0), axis=0)
    y_at = jnp.sum(alpha[..., None] * jnp.stack(outs, axis=0), axis=0)
    y_at = y_at.reshape(B, S, ATTN_OUT_WIDTH).astype(h.dtype)

    gates = jax.nn.sigmoid(h @ w_branch_gate + b_branch_gate)
    g_hy, g_at = jnp.split(gates, 2, axis=-1)
    merged = g_hy * (y_hy @ w_hy_o) + g_at * (y_at @ w_at_o)
    return (merged @ w_out).astype(h.dtype)


def hierarchical_moe(h, w_group, b_group, w_expert, b_expert, w1, w3, w2):
    B, S, D = h.shape
    T = B * S
    hf = h.reshape(T, D)
    g_logits = (hf @ w_group).astype(jnp.float32) + b_group
    g_prob = jax.nn.softmax(g_logits, axis=-1)
    g_val, g_idx = lax.top_k(g_prob, 1)
    e_logits = jnp.einsum('td,gde->tge', hf, w_expert).astype(jnp.float32) + b_expert
    e_logits = jnp.take_along_axis(e_logits, g_idx[:, :, None], axis=1)[:, 0]
    e_val, e_idx = lax.top_k(e_logits, TOP_K)
    gate = g_val * jax.nn.softmax(e_val, axis=-1)
    expert_id = g_idx * EXPERTS_PER_GROUP + e_idx

    A = T * TOP_K
    flat_e = expert_id.reshape(A)
    flat_gate = gate.reshape(A)
    flat_tok = jnp.arange(A, dtype=jnp.int32) // TOP_K
    order = jnp.argsort(flat_e)
    sorted_e = flat_e[order]
    sorted_tok = flat_tok[order]
    counts = jnp.bincount(flat_e, length=N_EXPERTS)
    starts = jnp.cumsum(counts) - counts
    padded = (counts + EXPERT_ROW_BLOCK - 1) // EXPERT_ROW_BLOCK * EXPERT_ROW_BLOCK
    pends = jnp.cumsum(padded)
    pstarts = pends - padded
    dest = pstarts[sorted_e] + jnp.arange(A, dtype=jnp.int32) - starts[sorted_e]
    n_blocks = -(-A // EXPERT_ROW_BLOCK) + N_EXPERTS
    rows = n_blocks * EXPERT_ROW_BLOCK
    xbuf = jnp.zeros((rows, D), h.dtype).at[dest].set(hf[sorted_tok])
    block_start = jnp.arange(n_blocks, dtype=jnp.int32) * EXPERT_ROW_BLOCK
    block_e = jnp.minimum(jnp.searchsorted(pends, block_start, side='right'), N_EXPERTS - 1)

    def expert_block(args):
        xb, e = args
        return (jax.nn.silu(xb @ w1[e]) * (xb @ w3[e])) @ w2[e]

    ybuf = lax.map(expert_block, (xbuf.reshape(n_blocks, EXPERT_ROW_BLOCK, D), block_e)).reshape(rows, D)
    y = ybuf[dest].astype(jnp.float32) * flat_gate[order][:, None]
    out = jax.ops.segment_sum(y, sorted_tok, num_segments=T)
    return out.reshape(B, S, D).astype(h.dtype)


def setup_inputs(seed: int = 0) -> dict:
    key = jax.random.key(seed)
    keys = list(jax.random.split(key, 40))

    def nrm(shape, std):
        return std * jax.random.normal(keys.pop(), shape, jnp.float32)

    D = D_MODEL
    L = DEPTH
    return {
        'x': nrm((BATCH, SEQ, D), 1.0),
        'c': nrm((BATCH, D), 1.0),
        'positions': jnp.broadcast_to(jnp.arange(SEQ, dtype=jnp.int32), (BATCH, SEQ)),
        'w_ada': nrm((L, D, 6 * D), 0.5 * D ** -0.5),
        'b_ada': nrm((L, 6 * D), 0.02),
        'g_mix_pre': 1.0 + nrm((L, D), 0.05),
        'g_mix_post': 1.0 + nrm((L, D), 0.05),
        'g_ffn_pre': 1.0 + nrm((L, D), 0.05),
        'g_ffn_post': 1.0 + nrm((L, D), 0.05),
        'w_in': nrm((L, D, IN_WIDTH), D ** -0.5),
        'conv_w': nrm((L, 3, 3 * HYENA_WIDTH), 0.5),
        'conv_b': nrm((L, 3 * HYENA_WIDTH), 0.02),
        'filt_w1': nrm((L, FILTER_EMB_DIM, FILTER_HIDDEN), FILTER_EMB_DIM ** -0.5),
        'filt_b1': nrm((L, FILTER_HIDDEN), 0.1),
        'filt_w2': nrm((L, FILTER_HIDDEN, FILTER_HIDDEN), FILTER_HIDDEN ** -0.5),
        'filt_b2': nrm((L, FILTER_HIDDEN), 0.1),
        'filt_w3': nrm((L, FILTER_HIDDEN, HYENA_ORDER * 2 * HYENA_WIDTH), 0.02),
        'filt_freq': 1.0 + nrm((L, FILTER_HIDDEN), 0.1),
        'hyena_skip': nrm((L, HYENA_ORDER, HYENA_WIDTH), 0.5),
        'w_branch_gate': nrm((L, D, 2 * D), D ** -0.5),
        'b_branch_gate': nrm((L, 2 * D), 0.02),
        'w_hy_o': nrm((L, HYENA_WIDTH, D), HYENA_WIDTH ** -0.5),
        'w_at_o': nrm((L, ATTN_OUT_WIDTH, D), ATTN_OUT_WIDTH ** -0.5),
        'w_out': nrm((L, D, D), D ** -0.5),
        'w_group': nrm((L, D, N_EXPERT_GROUPS), D ** -0.5),
        'b_group': nrm((L, N_EXPERT_GROUPS), 0.01),
        'w_expert': nrm((L, N_EXPERT_GROUPS, D, EXPERTS_PER_GROUP), D ** -0.5),
        'b_expert': nrm((L, N_EXPERT_GROUPS, EXPERTS_PER_GROUP), 0.01),
        'w1_exp': nrm((L, N_EXPERTS, D, EXPERT_FF), D ** -0.5),
        'w3_exp': nrm((L, N_EXPERTS, D, EXPERT_FF), D ** -0.5),
        'w2_exp': nrm((L, N_EXPERTS, EXPERT_FF, D), EXPERT_FF ** -0.5),
    }


def reference(x, c, positions, w_ada, b_ada, g_mix_pre, g_mix_post, g_ffn_pre, g_ffn_post,
              w_in, conv_w, conv_b, filt_w1, filt_b1, filt_w2, filt_b2, filt_w3, filt_freq,
              hyena_skip, w_branch_gate, b_branch_gate, w_hy_o, w_at_o, w_out,
              w_group, b_group, w_expert, b_expert, w1_exp, w3_exp, w2_exp):
    B, S, D = x.shape
    cond = jax.nn.silu(c)
    for l in range(DEPTH):
        mod = (cond @ w_ada[l] + b_ada[l]).reshape(B, 6, 1, D)
        shift_m, scale_m, gate_m, shift_f, scale_f, gate_f = (mod[:, i] for i in range(6))
        h = rmsnorm(x, g_mix_pre[l]) * (1.0 + scale_m) + shift_m
        y = token_mixer(h, positions, w_in[l], conv_w[l], conv_b[l], filt_w1[l], filt_b1[l],
                        filt_w2[l], filt_b2[l], filt_w3[l], filt_freq[l], hyena_skip[l],
                        w_branch_gate[l], b_branch_gate[l], w_hy_o[l], w_at_o[l], w_out[l])
        x = x + gate_m * rmsnorm(y, g_mix_post[l])
        h = rmsnorm(x, g_ffn_pre[l]) * (1.0 + scale_f) + shift_f
        y = hierarchical_moe(h, w_group[l], b_group[l], w_expert[l], b_expert[l],
                             w1_exp[l], w3_exp[l], w2_exp[l])
        x = x + gate_f * rmsnorm(y, g_ffn_post[l])
    return x
```

```python
import functools
import math

import numpy as np
import jax
import jax.numpy as jnp
from jax import lax
from jax.experimental import pallas as pl
from jax.experimental.pallas import tpu as pltpu

F32 = jnp.float32
BF16 = jnp.bfloat16

LANES = 128
MIB = 1024 * 1024

RMS_EPS = 1e-6
NEG_INF = -1e30

HEAD_DIM = 128
ROT_DIM = HEAD_DIM // 4
ROPE_THETA = 500000.0
ATTN_GROUPS = ((128, 1), (512, 4), (2048, 16))
HEADS_PER_GROUP = 4
N_ATTN_HEADS = HEADS_PER_GROUP * len(ATTN_GROUPS)

FILTER_BANDS = 16
DECAY_TARGET = 1e-2
FAST_DECAY_PCT = 0.3
SLOW_DECAY_PCT = 1.5

N_EXPERT_GROUPS = 8
EXPERTS_PER_GROUP = 8
N_EXPERTS = N_EXPERT_GROUPS * EXPERTS_PER_GROUP
TOP_K = 2
EXPERT_ROW_BLOCK = 128

FFT_N1 = 128
PITCH = FFT_N1 + 8


def _cparams(sem, vmem_mib):
    return pltpu.CompilerParams(dimension_semantics=sem, vmem_limit_bytes=vmem_mib * MIB)


def _rms(x, g):
    return x * lax.rsqrt(jnp.mean(x * x, axis=-1, keepdims=True) + RMS_EPS) * g


def _adaln_body(c_ref, w_ref, b_ref, o_ref):
    c = c_ref[...]
    cond = c * jax.nn.sigmoid(c)
    o_ref[...] = jnp.dot(cond.astype(BF16), w_ref[...].astype(BF16),
                         preferred_element_type=F32) + b_ref[...]


def _adaln(c_pad, w_ada, b_ada):
    rows, d = c_pad.shape
    n = w_ada.shape[1]
    tn = 1024
    return pl.pallas_call(
        _adaln_body,
        grid=(n // tn,),
        in_specs=[pl.BlockSpec((rows, d), lambda j: (0, 0)),
                  pl.BlockSpec((d, tn), lambda j: (0, j)),
                  pl.BlockSpec((1, tn), lambda j: (0, j))],
        out_specs=pl.BlockSpec((rows, tn), lambda j: (0, j)),
        out_shape=jax.ShapeDtypeStruct((rows, n), F32),
        compiler_params=_cparams(("arbitrary",), 40),
        name="adaln",
    )(c_pad, w_ada, b_ada)


def _in_proj_body(n_gate, x_ref, mod_ref, g_ref, w_ref, b_ref, o_ref, h_scr):
    j = pl.program_id(1)

    @pl.when(j == 0)
    def _():
        x = x_ref[...]
        h = _rms(x, g_ref[...]) * (1.0 + mod_ref[0, 1:2, :]) + mod_ref[0, 0:1, :]
        h_scr[...] = h.astype(BF16)

    acc = jnp.dot(h_scr[...], w_ref[...], preferred_element_type=F32) + b_ref[...]

    @pl.when(j < n_gate)
    def _():
        o_ref[...] = jax.nn.sigmoid(acc).astype(o_ref.dtype)

    @pl.when(j >= n_gate)
    def _():
        o_ref[...] = acc.astype(o_ref.dtype)


def _in_proj(x2d, mod, g_pre, w_cat, b_cat, n_gate_cols, seq):
    t, d = x2d.shape
    n = w_cat.shape[1]
    tm, tn = 512, 512
    per_batch = seq // tm
    return pl.pallas_call(
        functools.partial(_in_proj_body, n_gate_cols // tn),
        grid=(t // tm, n // tn),
        in_specs=[pl.BlockSpec((tm, d), lambda i, j: (i, 0)),
                  pl.BlockSpec((1, 6, d), lambda i, j: (i // per_batch, 0, 0)),
                  pl.BlockSpec((1, d), lambda i, j: (0, 0)),
                  pl.BlockSpec((d, tn), lambda i, j: (0, j)),
                  pl.BlockSpec((1, tn), lambda i, j: (0, j))],
        out_specs=pl.BlockSpec((tm, tn), lambda i, j: (i, j)),
        out_shape=jax.ShapeDtypeStruct((t, n), BF16),
        scratch_shapes=[pltpu.VMEM((tm, d), BF16)],
        compiler_params=_cparams(("arbitrary", "arbitrary"), 40),
        name="in_proj",
    )(x2d, mod, g_pre, w_cat, b_cat)


def _filters_body(seq, band_ref, w1_ref, b1_ref, w2_ref, b2_ref, fr_ref, dl_ref,
                  w3a_ref, w3b_ref, o_ref, hid_scr):
    i = pl.program_id(0)
    j = pl.program_id(1)
    tl = hid_scr.shape[0]
    row = (lax.broadcasted_iota(jnp.int32, (tl, 1), 0) + i * tl).astype(F32)

    @pl.when((j == 0) & (pl.program_id(2) == 0))
    def _():
        lane = lax.broadcasted_iota(jnp.int32, (tl, LANES), 1)
        t = row / (seq - 1.0)
        ang = band_ref[...] * (2.0 * math.pi * row / seq)
        feats = jnp.where(lane == 0, t,
                          jnp.where(lane <= FILTER_BANDS, jnp.cos(ang),
                                    jnp.where(lane <= 2 * FILTER_BANDS, -jnp.sin(ang), 0.0)))
        hi = lax.Precision.HIGHEST
        fr = fr_ref[...]
        hid = jnp.sin(fr * (jnp.dot(feats, w1_ref[...], precision=hi, preferred_element_type=F32)
                            + b1_ref[...]))
        hid = jnp.sin(fr * (jnp.dot(hid, w2_ref[...], precision=hi, preferred_element_type=F32)
                            + b2_ref[...]))
        hid_scr[...] = hid

    hi = lax.Precision.HIGHEST
    hid = hid_scr[...]
    decay = jnp.exp(-(row / (seq - 1.0)) * dl_ref[...])
    hf = jnp.dot(hid, w3a_ref[...], precision=hi, preferred_element_type=F32) * decay
    hb = jnp.dot(hid, w3b_ref[...], precision=hi, preferred_element_type=F32) * decay
    hb = jnp.where(row == 0.0, 0.0, hb)
    o_ref[0, 0] = hf + hb
    o_ref[0, 1] = hf - hb


def _filters(seq, width, bands, w1p, b1, w2, b2, freq, deltas, w3):
    tl, tc = 512, 256
    nct = width // tc
    hidden = w2.shape[0]
    const = lambda i, j, o: (0, 0)
    return pl.pallas_call(
        functools.partial(_filters_body, float(seq)),
        grid=(seq // tl, nct, 2),
        in_specs=[pl.BlockSpec((1, LANES), const),
                  pl.BlockSpec((LANES, hidden), const),
                  pl.BlockSpec((1, hidden), const),
                  pl.BlockSpec((hidden, hidden), const),
                  pl.BlockSpec((1, hidden), const),
                  pl.BlockSpec((1, hidden), const),
                  pl.BlockSpec((1, tc), lambda i, j, o: (0, j)),
                  pl.BlockSpec((hidden, tc), lambda i, j, o: (0, (2 * o) * nct + j)),
                  pl.BlockSpec((hidden, tc), lambda i, j, o: (0, (2 * o + 1) * nct + j))],
        out_specs=pl.BlockSpec((1, 2, tl, tc), lambda i, j, o: (o, 0, i, j)),
        out_shape=jax.ShapeDtypeStruct((2, 2, seq, width), F32),
        scratch_shapes=[pltpu.VMEM((tl, hidden), F32)],
        compiler_params=_cparams(("arbitrary", "arbitrary", "arbitrary"), 32),
        name="filters",
    )(bands, w1p, b1, w2, b2, freq, deltas, w3, w3)


def _fft_tables(n1, n2):
    n = n1 * n2
    q = np.arange(n2)[:, None]
    b = np.arange(n2 // 2)[None, :]
    a = np.arange(n1)[:, None, None]
    ang = -2.0 * np.pi * (a * q[None] / n + (q * b)[None] / n2)
    stage1 = np.concatenate([np.cos(ang), np.sin(ang)], axis=1)
    stage1_inv = np.transpose(stage1, (0, 2, 1)) / n
    p = np.arange(n1)
    ang2 = -2.0 * np.pi * np.outer(p, p) / n1
    fre, fim = np.cos(ang2), np.sin(ang2)
    stage2 = np.block([[fre, -fim], [fim, fre]])
    stage2_inv = np.block([[fre, fim], [-fim, fre]])
    as_bf16 = lambda m: jnp.asarray(m, dtype=F32).astype(BF16)
    return as_bf16(stage1), as_bf16(stage1_inv), as_bf16(stage2), as_bf16(stage2_inv)


def _halves(ref, rows):
    return jnp.concatenate([ref[0, rows, :], ref[1, rows, :]], axis=1)


def _fft_stage1(z_ref, s1_ref, gre_ref, gim_ref):
    n1, two_n2, n2h = s1_ref.shape
    n2 = two_n2 // 2

    def step(a, carry):
        zrows = _halves(z_ref, pl.ds(a, n2h, stride=PITCH))
        g = jnp.dot(s1_ref[a], zrows.astype(BF16), preferred_element_type=F32)
        for h in range(2):
            cols = slice(h * LANES, (h + 1) * LANES)
            gre_ref[h, pl.ds(a, n2, stride=PITCH), :] = g[:n2, cols]
            gim_ref[h, pl.ds(a, n2, stride=PITCH), :] = g[n2:, cols]
        return carry

    lax.fori_loop(0, n1, step, 0)


def _fft_stage1_inv(gre_ref, gim_ref, s1i_ref, y_ref):
    n1, n2h, two_n2 = s1i_ref.shape
    n2 = two_n2 // 2

    def step(a, carry):
        rows = pl.ds(a, n2, stride=PITCH)
        hs = jnp.concatenate([_halves(gre_ref, rows), _halves(gim_ref, rows)], axis=0)
        y = jnp.dot(s1i_ref[a], hs.astype(BF16), preferred_element_type=F32)
        for h in range(2):
            y_ref[h, pl.ds(a, n2h, stride=PITCH), :] = y[:, h * LANES:(h + 1) * LANES]
        return carry

    lax.fori_loop(0, n1, step, 0)


def _stage2_block(gre_ref, gim_ref, s2_ref, q):
    n1 = s2_ref.shape[0] // 2
    rows = pl.ds(pl.multiple_of(q * PITCH, 8), n1)
    gs = jnp.concatenate([_halves(gre_ref, rows), _halves(gim_ref, rows)], axis=0)
    x = jnp.dot(s2_ref[...], gs.astype(BF16), preferred_element_type=F32)
    return x[:n1], x[n1:]


def _spectra_body(uw_ref, s1_ref, s2_ref, k_ref, z_scr, gre_scr, gim_scr):
    n1 = s2_ref.shape[0] // 2
    n2 = s1_ref.shape[1] // 2
    for h in range(2):
        for b in range(n2 // 2):
            z_scr[h, pl.ds(b * PITCH, n1), :] = uw_ref[0, h, pl.ds(b * n1, n1), :]
    _fft_stage1(z_scr, s1_ref, gre_scr, gim_scr)

    def step(q, carry):
        xre, xim = _stage2_block(gre_scr, gim_scr, s2_ref, q)
        rows = pl.ds(pl.multiple_of(q * n1, 8), n1)
        k_ref[0, 0, rows, :] = xre[:, :LANES]
        k_ref[0, 1, rows, :] = xim[:, LANES:]
        return carry

    lax.fori_loop(0, n2, step, 0)


def _spectra(uw, s1, s2):
    _, _, seq, width = uw.shape
    n1 = FFT_N1
    n2 = 2 * seq // n1
    const3 = lambda j, o: (0, 0, 0)
    return pl.pallas_call(
        _spectra_body,
        grid=(width // LANES, 2),
        in_specs=[pl.BlockSpec((1, 2, seq, LANES), lambda j, o: (o, 0, 0, j)),
                  pl.BlockSpec(s1.shape, const3),
                  pl.BlockSpec(s2.shape, lambda j, o: (0, 0))],
        out_specs=pl.BlockSpec((1, 2, 2 * seq, LANES), lambda j, o: (o, 0, 0, j)),
        out_shape=jax.ShapeDtypeStruct((2, 2, 2 * seq, width), F32),
        scratch_shapes=[pltpu.VMEM((2, (n2 // 2) * PITCH, LANES), F32),
                        pltpu.VMEM((2, n2 * PITCH, LANES), F32),
                        pltpu.VMEM((2, n2 * PITCH, LANES), F32)],
        compiler_params=_cparams(("arbitrary", "arbitrary"), 56),
        name="spectra",
    )(uw, s1, s2)


def _short_conv_block(p_ref, b, i, nblk, w_ref, bias_ref):
    n1 = FFT_N1
    start = pl.multiple_of(i * n1, n1)
    cur = p_ref[b, pl.ds(start, n1), :].astype(F32)
    before = p_ref[b, pl.ds(pl.multiple_of(jnp.maximum(start - 16, 0), 16), 16), :].astype(F32)
    after = p_ref[b, pl.ds(pl.multiple_of(jnp.minimum(start + n1, (nblk - 1) * n1), 16), 16), :].astype(F32)
    last_prev = before[15:16] * jnp.where(i > 0, 1.0, 0.0).astype(F32)
    first_next = after[0:1] * jnp.where(i < nblk - 1, 1.0, 0.0).astype(F32)
    row = lax.broadcasted_iota(jnp.int32, (n1, 1), 0)
    prev = jnp.where(row == 0, last_prev, pltpu.roll(cur, 1, 0))
    nxt = jnp.where(row == n1 - 1, first_next, pltpu.roll(cur, n1 - 1, 0))
    return prev * w_ref[0:1, :] + cur * w_ref[1:2, :] + nxt * w_ref[2:3, :] + bias_ref[...]


def _hyena_body(pv_ref, px1_ref, px2_ref, cwv_ref, cw1_ref, cw2_ref, cbv_ref, cb1_ref, cb2_ref,
                skip_ref, k_ref, s1_ref, s1i_ref, s2_ref, s2i_ref, o_ref,
                z_scr, y_scr, gre_scr, gim_scr):
    c = pl.program_id(1)
    g = pl.program_id(2)
    ngroups = pl.num_programs(2)
    n1 = FFT_N1
    n2 = s1_ref.shape[1] // 2
    nblk = n2 // 2
    per_group = n2 // ngroups

    @pl.when(g == 0)
    def _():
        @pl.when(c == 0)
        def _():
            def fill(i, carry):
                for b in range(2):
                    z_scr[b, pl.ds(pl.multiple_of(i * PITCH, 8), n1), :] = _short_conv_block(
                        pv_ref, b, i, nblk, cwv_ref, cbv_ref)
                return carry
            lax.fori_loop(0, nblk, fill, 0)

        _fft_stage1(z_scr, s1_ref, gre_scr, gim_scr)

    def mid(ql, carry):
        q = g * per_group + ql
        xre, xim = _stage2_block(gre_scr, gim_scr, s2_ref, q)
        krows = pl.ds(pl.multiple_of(ql * n1, 8), n1)
        kre = k_ref[0, 0, krows, :]
        kim = k_ref[0, 1, krows, :]
        kre = jnp.concatenate([kre, kre], axis=1)
        kim = jnp.concatenate([kim, kim], axis=1)
        ys = jnp.concatenate([xre * kre - xim * kim, xre * kim + xim * kre], axis=0)
        hh = jnp.dot(s2i_ref[...], ys.astype(BF16), preferred_element_type=F32)
        rows = pl.ds(pl.multiple_of(q * PITCH, 8), n1)
        for b in range(2):
            cols = slice(b * LANES, (b + 1) * LANES)
            gre_scr[b, rows, :] = hh[:n1, cols]
            gim_scr[b, rows, :] = hh[n1:, cols]
        return carry

    lax.fori_loop(0, per_group, mid, 0)

    @pl.when(g == ngroups - 1)
    def _():
        _fft_stage1_inv(gre_scr, gim_scr, s1i_ref, y_scr)

        def post(px_ref, cw_ref, cb_ref, order, store):
            def blk(i, carry):
                rows = pl.ds(pl.multiple_of(i * PITCH, 8), n1)
                for b in range(2):
                    zb = z_scr[b, rows, :]
                    gate = _short_conv_block(px_ref, b, i, nblk, cw_ref, cb_ref)
                    store(b, i, rows, gate * (y_scr[b, rows, :] + zb * skip_ref[order:order + 1, :]))
                return carry
            lax.fori_loop(0, nblk, blk, 0)

        @pl.when(c == 0)
        def _():
            def store(b, i, rows, val):
                z_scr[b, rows, :] = val
            post(px1_ref, cw1_ref, cb1_ref, 0, store)

        @pl.when(c == 1)
        def _():
            def store(b, i, rows, val):
                o_ref[b, pl.ds(pl.multiple_of(i * n1, n1), n1), :] = val.astype(o_ref.dtype)
            post(px2_ref, cw2_ref, cb2_ref, 1, store)


def _hyena(pg3, proj_col0, conv_w, conv_b, skip, kspec, s1, s1i, s2, s2i, width):
    bsz, seq, _ = pg3.shape
    assert bsz == 2
    n1 = FFT_N1
    n2 = 2 * seq // n1
    nct = width // LANES
    ngroups = 4
    krows = (n2 // ngroups) * n1
    col = lambda off: (lambda j, c, g: (0, 0, proj_col0 // LANES + off * nct + j))
    cw = lambda off: (lambda j, c, g: (0, off * nct + j))
    const3 = lambda j, c, g: (0, 0, 0)
    const2 = lambda j, c, g: (0, 0)
    return pl.pallas_call(
        _hyena_body,
        grid=(nct, 2, ngroups),
        in_specs=[pl.BlockSpec((2, seq, LANES), col(0)),
                  pl.BlockSpec((2, seq, LANES), col(1)),
                  pl.BlockSpec((2, seq, LANES), col(2)),
                  pl.BlockSpec((3, LANES), cw(0)),
                  pl.BlockSpec((3, LANES), cw(1)),
                  pl.BlockSpec((3, LANES), cw(2)),
                  pl.BlockSpec((1, LANES), cw(0)),
                  pl.BlockSpec((1, LANES), cw(1)),
                  pl.BlockSpec((1, LANES), cw(2)),
                  pl.BlockSpec((2, LANES), lambda j, c, g: (0, j)),
                  pl.BlockSpec((1, 2, krows, LANES), lambda j, c, g: (c, 0, g, j)),
                  pl.BlockSpec(s1.shape, const3),
                  pl.BlockSpec(s1i.shape, const3),
                  pl.BlockSpec(s2.shape, const2),
                  pl.BlockSpec(s2i.shape, const2)],
        out_specs=pl.BlockSpec((2, seq, LANES), lambda j, c, g: (0, 0, j)),
        out_shape=jax.ShapeDtypeStruct((2, seq, width), BF16),
        scratch_shapes=[pltpu.VMEM((2, (n2 // 2) * PITCH, LANES), F32),
                        pltpu.VMEM((2, (n2 // 2) * PITCH, LANES), F32),
                        pltpu.VMEM((2, n2 * PITCH, LANES), F32),
                        pltpu.VMEM((2, n2 * PITCH, LANES), F32)],
        compiler_params=_cparams(("arbitrary", "arbitrary", "arbitrary"), 56),
        name="hyena",
    )(pg3, pg3, pg3, conv_w, conv_w, conv_w, conv_b, conv_b, conv_b, skip, kspec, s1, s1i, s2, s2i)


ATTN_QBLK = 128


def _attention_body(pos_ref, freq_ref, sign_ref, *refs):
    qkv_refs = refs[:9]
    o_ref = refs[9]
    cos_scr, sin_scr, q_scr, k_scr, v_scr, og_scr, lse_scr = refs[10:]
    seq = q_scr.shape[0]
    chunk = 512
    nchunks = seq // chunk

    @pl.when(pl.program_id(1) == 0)
    def _():
        def trig(i, carry):
            rows = pl.ds(pl.multiple_of(i * chunk, chunk), chunk)
            ang = pos_ref[0, rows, :].astype(F32) * freq_ref[...]
            cos_scr[rows, :] = jnp.cos(ang)
            sin_scr[rows, :] = jnp.sin(ang) * sign_ref[...]
            return carry
        lax.fori_loop(0, nchunks, trig, 0)

    def rotate(src_ref, dst_ref):
        def body(i, carry):
            rows = pl.ds(pl.multiple_of(i * chunk, chunk), chunk)
            t = src_ref[0, rows, :].astype(F32)
            lane = lax.broadcasted_iota(jnp.int32, t.shape, 1)
            partner = jnp.where(lane < ROT_DIM // 2,
                                pltpu.roll(t, LANES - ROT_DIM // 2, 1), pltpu.roll(t, ROT_DIM // 2, 1))
            dst_ref[rows, :] = t * cos_scr[rows, :] + partner * sin_scr[rows, :]
            return carry
        lax.fori_loop(0, nchunks, body, 0)

    def widen(src_ref, dst_ref):
        def body(i, carry):
            rows = pl.ds(pl.multiple_of(i * chunk, chunk), chunk)
            dst_ref[rows, :] = src_ref[0, rows, :].astype(F32)
            return carry
        lax.fori_loop(0, nchunks, body, 0)

    scale = HEAD_DIM ** -0.5
    for gi, (window, dil) in enumerate(ATTN_GROUPS):
        rotate(qkv_refs[3 * gi], q_scr)
        rotate(qkv_refs[3 * gi + 1], k_scr)
        widen(qkv_refs[3 * gi + 2], v_scr)
        n = seq // dil
        half = window // (2 * dil)
        tk = min(n, 3 * ATTN_QBLK)
        blocks_per_res = n // ATTN_QBLK

        def block(u, carry, dil=dil, n=n, half=half, tk=tk, blocks_per_res=blocks_per_res, gi=gi):
            r = u // blocks_per_res
            m = u % blocks_per_res
            q0 = m * ATTN_QBLK
            k0 = jnp.clip(q0 - ATTN_QBLK, 0, n - tk)
            qrows = pl.ds(r + dil * q0, ATTN_QBLK, stride=dil)
            krows = pl.ds(r + dil * k0, tk, stride=dil)
            qb = q_scr[qrows, :].astype(BF16)
            kb = k_scr[krows, :].astype(BF16)
            vb = v_scr[krows, :].astype(BF16)
            s = lax.dot_general(qb, kb, (((1,), (1,)), ((), ())), preferred_element_type=F32) * scale
            qi = q0 + lax.broadcasted_iota(jnp.int32, (ATTN_QBLK, tk), 0)
            kj = k0 + lax.broadcasted_iota(jnp.int32, (ATTN_QBLK, tk), 1)
            s = jnp.where(jnp.abs(qi - kj) <= half, s, NEG_INF)
            mx = jnp.max(s, axis=-1, keepdims=True)
            p = jnp.exp(s - mx)
            l = jnp.sum(p, axis=-1, keepdims=True)
            o = jnp.dot(p.astype(BF16), vb, preferred_element_type=F32) / l
            og_scr[gi, qrows, :] = o
            lse_scr[gi, qrows, :] = jnp.broadcast_to(mx + jnp.log(l), (ATTN_QBLK, LANES))
            return carry

        lax.fori_loop(0, seq // ATTN_QBLK, block, 0)

    def merge(i, carry):
        rows = pl.ds(pl.multiple_of(i * chunk, chunk), chunk)
        lses = [lse_scr[gi, rows, :] for gi in range(len(ATTN_GROUPS))]
        mx = functools.reduce(jnp.maximum, lses)
        ws = [jnp.exp(v - mx) for v in lses]
        den = functools.reduce(lambda a, b: a + b, ws)
        num = functools.reduce(lambda a, b: a + b,
                               [w * og_scr[gi, rows, :] for gi, w in enumerate(ws)])
        o_ref[0, rows, :] = (num / den).astype(o_ref.dtype)
        return carry

    lax.fori_loop(0, nchunks, merge, 0)


def _attention(pg3, pos3, freq_row, sign_row, qkv_col0):
    bsz, seq, _ = pg3.shape
    ng = len(ATTN_GROUPS)
    in_specs = [pl.BlockSpec((1, seq, 1), lambda b, h: (b, 0, 0)),
                pl.BlockSpec((1, LANES), lambda b, h: (0, 0)),
                pl.BlockSpec((1, LANES), lambda b, h: (0, 0))]
    for gi in range(ng):
        for which in range(3):
            base = qkv_col0 // LANES + which * N_ATTN_HEADS + gi * HEADS_PER_GROUP
            in_specs.append(pl.BlockSpec((1, seq, LANES), lambda b, h, base=base: (b, 0, base + h)))
    return pl.pallas_call(
        _attention_body,
        grid=(bsz, HEADS_PER_GROUP),
        in_specs=in_specs,
        out_specs=pl.BlockSpec((1, seq, LANES), lambda b, h: (b, 0, h)),
        out_shape=jax.ShapeDtypeStruct((bsz, seq, HEADS_PER_GROUP * HEAD_DIM), BF16),
        scratch_shapes=[pltpu.VMEM((seq, LANES), F32), pltpu.VMEM((seq, LANES), F32),
                        pltpu.VMEM((seq, LANES), F32), pltpu.VMEM((seq, LANES), F32),
                        pltpu.VMEM((seq, LANES), F32),
                        pltpu.VMEM((ng, seq, LANES), F32), pltpu.VMEM((ng, seq, LANES), F32)],
        compiler_params=_cparams(("arbitrary", "arbitrary"), 56),
        name="attention",
    )(pos3, freq_row, sign_row, *([pg3] * (3 * ng)))


def _mix_out_body(yhy_ref, yat_ref, ghy_ref, gat_ref, x_ref, mod_ref, gpost_ref, gpre_ref,
                  whyo_ref, wato_ref, wout_ref, wr_ref, br_ref, x1_ref, h2_ref, lg_ref):
    a = jnp.dot(yhy_ref[...], whyo_ref[...], preferred_element_type=F32)
    b = jnp.dot(yat_ref[...], wato_ref[...], preferred_element_type=F32)
    merged = ghy_ref[...].astype(F32) * a + gat_ref[...].astype(F32) * b
    y = jnp.dot(merged.astype(BF16), wout_ref[...], preferred_element_type=F32)
    x1 = x_ref[...] + mod_ref[0, 2:3, :] * _rms(y, gpost_ref[...])
    x1_ref[...] = x1
    h2 = _rms(x1, gpre_ref[...]) * (1.0 + mod_ref[0, 4:5, :]) + mod_ref[0, 3:4, :]
    h2_ref[...] = h2
    lg_ref[...] = jnp.dot(h2, wr_ref[...], precision=lax.Precision.HIGHEST,
                          preferred_element_type=F32) + br_ref[...]


def _mix_out(y_hy, y_at, pg, x2d, mod, g_post, g_pre, w_hy_o, w_at_o, w_out, w_r, b_r, seq):
    t, d = x2d.shape
    tm = 256
    per_batch = seq // tm
    gblk = 0
    const = lambda i: (0, 0)
    return pl.pallas_call(
        _mix_out_body,
        grid=(t // tm,),
        in_specs=[pl.BlockSpec((tm, y_hy.shape[1]), lambda i: (i, 0)),
                  pl.BlockSpec((tm, y_at.shape[1]), lambda i: (i, 0)),
                  pl.BlockSpec((tm, d), lambda i: (i, gblk)),
                  pl.BlockSpec((tm, d), lambda i: (i, gblk + 1)),
                  pl.BlockSpec((tm, d), lambda i: (i, 0)),
                  pl.BlockSpec((1, 6, d), lambda i: (i // per_batch, 0, 0)),
                  pl.BlockSpec((1, d), const),
                  pl.BlockSpec((1, d), const),
                  pl.BlockSpec(w_hy_o.shape, const),
                  pl.BlockSpec(w_at_o.shape, const),
                  pl.BlockSpec(w_out.shape, const),
                  pl.BlockSpec(w_r.shape, const),
                  pl.BlockSpec((1, LANES), const)],
        out_specs=[pl.BlockSpec((tm, d), lambda i: (i, 0)),
                   pl.BlockSpec((tm, d), lambda i: (i, 0)),
                   pl.BlockSpec((tm, LANES), lambda i: (i, 0))],
        out_shape=[jax.ShapeDtypeStruct((t, d), F32),
                   jax.ShapeDtypeStruct((t, d), F32),
                   jax.ShapeDtypeStruct((t, LANES), F32)],
        compiler_params=_cparams(("arbitrary",), 56),
        name="mix_out",
    )(y_hy, y_at, pg, pg, x2d, mod, g_post, g_pre, w_hy_o, w_at_o, w_out, w_r, b_r)


def _route_body(lg_ref, eid_ref, gate_ref):
    lg = lg_ref[...]
    lane = lax.broadcasted_iota(jnp.int32, lg.shape, 1)
    big = jnp.int32(1 << 20)

    def first_argmax(vals, mask):
        v = jnp.where(mask, vals, -jnp.inf)
        mx = jnp.max(v, axis=-1, keepdims=True)
        idx = jnp.min(jnp.where(mask & (v == mx), lane, big), axis=-1, keepdims=True)
        return mx, idx

    gmask = lane < N_EXPERT_GROUPS
    gmax, gidx = first_argmax(lg, gmask)
    gval = 1.0 / jnp.sum(jnp.where(gmask, jnp.exp(lg - gmax), 0.0), axis=-1, keepdims=True)
    lo = N_EXPERT_GROUPS + gidx * EXPERTS_PER_GROUP
    emask = (lane >= lo) & (lane < lo + EXPERTS_PER_GROUP)
    v1, i1 = first_argmax(lg, emask)
    v2, i2 = first_argmax(lg, emask & (lane != i1))
    e2 = jnp.exp(v2 - v1)
    p1 = 1.0 / (1.0 + e2)
    p2 = e2 / (1.0 + e2)
    eid = jnp.where(lane == 0, i1, i2) - N_EXPERT_GROUPS
    gate = gval * jnp.where(lane == 0, p1, p2)
    eid_ref[...] = eid[:, :TOP_K]
    gate_ref[...] = gate[:, :TOP_K]


def _route(logits):
    t = logits.shape[0]
    tm = 512
    return pl.pallas_call(
        _route_body,
        grid=(t // tm,),
        in_specs=[pl.BlockSpec((tm, LANES), lambda i: (i, 0))],
        out_specs=[pl.BlockSpec((tm, TOP_K), lambda i: (i, 0)),
                   pl.BlockSpec((tm, TOP_K), lambda i: (i, 0))],
        out_shape=[jax.ShapeDtypeStruct((t, TOP_K), jnp.int32),
                   jax.ShapeDtypeStruct((t, TOP_K), F32)],
        compiler_params=_cparams(("arbitrary",), 32),
        name="route",
    )(logits)


def _rank_body(e_ref, rank_ref, cnt_ref, carry_scr):
    i = pl.program_id(0)
    r = e_ref.shape[0]

    @pl.when(i == 0)
    def _():
        carry_scr[...] = jnp.zeros_like(carry_scr)

    lane = lax.broadcasted_iota(jnp.int32, (r, LANES), 1)
    onehot = (lane == e_ref[...]).astype(F32)
    tri = (lax.broadcasted_iota(jnp.int32, (r, r), 1)
           < lax.broadcasted_iota(jnp.int32, (r, r), 0)).astype(BF16)
    before = jnp.dot(tri, onehot.astype(BF16), preferred_element_type=F32) + carry_scr[0:1, :]
    rank_ref[...] = jnp.sum(onehot * before, axis=-1, keepdims=True).astype(jnp.int32)
    total = carry_scr[0:1, :] + jnp.sum(onehot, axis=0, keepdims=True)
    carry_scr[...] = jnp.broadcast_to(total, carry_scr.shape)
    cnt_ref[...] = jnp.broadcast_to(total, cnt_ref.shape).astype(jnp.int32)


def _rank(flat_e):
    a = flat_e.shape[0]
    r = 512
    return pl.pallas_call(
        _rank_body,
        grid=(a // r,),
        in_specs=[pl.BlockSpec((r, 1), lambda i: (i, 0))],
        out_specs=[pl.BlockSpec((r, 1), lambda i: (i, 0)),
                   pl.BlockSpec((8, LANES), lambda i: (0, 0))],
        out_shape=[jax.ShapeDtypeStruct((a, 1), jnp.int32),
                   jax.ShapeDtypeStruct((8, LANES), jnp.int32)],
        scratch_shapes=[pltpu.VMEM((8, LANES), F32)],
        compiler_params=_cparams(("arbitrary",), 32),
        name="rank",
    )(flat_e)


def _row_copy(src_hbm, dst_vmem, sem, src_row, dst_row):
    return pltpu.make_async_copy(src_hbm.at[pl.ds(src_row, 1)], dst_vmem.at[pl.ds(dst_row, 1)], sem)


def _experts_body(be_ref, src_ref, h_hbm, w1_ref, w3_ref, w2_ref, y_ref, x_scr, sem):
    i = pl.program_id(0)
    nrows = x_scr.shape[0]
    base = i * nrows

    def start(r, carry):
        _row_copy(h_hbm, x_scr, sem, src_ref[base + r], r).start()
        return carry

    lax.fori_loop(0, nrows, start, 0)

    def wait(r, carry):
        _row_copy(h_hbm, x_scr, sem, src_ref[base + r], r).wait()
        return carry

    lax.fori_loop(0, nrows, wait, 0)

    xb = x_scr[...].astype(BF16)
    a = jnp.dot(xb, w1_ref[0].astype(BF16), preferred_element_type=F32)
    b = jnp.dot(xb, w3_ref[0].astype(BF16), preferred_element_type=F32)
    act = (a * jax.nn.sigmoid(a) * b).astype(BF16)
    y_ref[...] = jnp.dot(act, w2_ref[0].astype(BF16), preferred_element_type=F32)


def _experts(block_e, src_tok, h2, w1, w3, w2):
    nblocks = block_e.shape[0]
    d = h2.shape[1]
    ff = w1.shape[2]
    rb = EXPERT_ROW_BLOCK
    grid_spec = pltpu.PrefetchScalarGridSpec(
        num_scalar_prefetch=2,
        grid=(nblocks,),
        in_specs=[pl.BlockSpec(memory_space=pl.ANY),
                  pl.BlockSpec((1, d, ff), lambda i, be, src: (be[i], 0, 0)),
                  pl.BlockSpec((1, d, ff), lambda i, be, src: (be[i], 0, 0)),
                  pl.BlockSpec((1, ff, d), lambda i, be, src: (be[i], 0, 0))],
        out_specs=pl.BlockSpec((rb, d), lambda i, be, src: (i, 0)),
        scratch_shapes=[pltpu.VMEM((rb, d), F32), pltpu.SemaphoreType.DMA(())],
    )
    return pl.pallas_call(
        _experts_body,
        grid_spec=grid_spec,
        out_shape=jax.ShapeDtypeStruct((nblocks * rb, d), F32),
        compiler_params=_cparams(("arbitrary",), 48),
        name="experts",
    )(block_e, src_tok, h2, w1, w3, w2)


def _combine_body(dest_ref, y_hbm, gate_ref, x1_ref, mod_ref, g_ref, o_ref, buf_scr, sem):
    i = pl.program_id(0)
    tm = x1_ref.shape[0]
    base = i * tm * TOP_K

    def copies(fn):
        def body(r, carry):
            for k in range(TOP_K):
                fn(pltpu.make_async_copy(y_hbm.at[pl.ds(dest_ref[base + TOP_K * r + k], 1)],
                                         buf_scr.at[k, pl.ds(r, 1)], sem))
            return carry
        lax.fori_loop(0, tm, body, 0)

    copies(lambda cp: cp.start())
    copies(lambda cp: cp.wait())

    gate = gate_ref[...]
    y = buf_scr[0] * gate[:, 0:1] + buf_scr[1] * gate[:, 1:2]
    o_ref[...] = x1_ref[...] + mod_ref[0, 5:6, :] * _rms(y, g_ref[...])


def _combine(dest, ybuf, gate, x1, mod, g_post, seq):
    t, d = x1.shape
    tm = 128
    per_batch = seq // tm
    grid_spec = pltpu.PrefetchScalarGridSpec(
        num_scalar_prefetch=1,
        grid=(t // tm,),
        in_specs=[pl.BlockSpec(memory_space=pl.ANY),
                  pl.BlockSpec((tm, TOP_K), lambda i, dst: (i, 0)),
                  pl.BlockSpec((tm, d), lambda i, dst: (i, 0)),
                  pl.BlockSpec((1, 6, d), lambda i, dst: (i // per_batch, 0, 0)),
                  pl.BlockSpec((1, d), lambda i, dst: (0, 0))],
        out_specs=pl.BlockSpec((tm, d), lambda i, dst: (i, 0)),
        scratch_shapes=[pltpu.VMEM((TOP_K, tm, d), F32), pltpu.SemaphoreType.DMA(())],
    )
    return pl.pallas_call(
        _combine_body,
        grid_spec=grid_spec,
        out_shape=jax.ShapeDtypeStruct((t, d), F32),
        compiler_params=_cparams(("arbitrary",), 32),
        name="combine",
    )(dest, ybuf, gate, x1, mod, g_post)


def _layer(x, c, positions, w_ada, b_ada, g_mix_pre, g_mix_post, g_ffn_pre, g_ffn_post,
           w_in, conv_w, conv_b, filt_w1, filt_b1, filt_w2, filt_b2, filt_w3, filt_freq,
           hyena_skip, w_branch_gate, b_branch_gate, w_hy_o, w_at_o, w_out,
           w_group, b_group, w_expert, b_expert, w1_exp, w3_exp, w2_exp):
    bsz, seq, d = x.shape
    t = bsz * seq
    width = hyena_skip.shape[1]
    row = lambda v: v.reshape(1, -1)

    c_pad = jnp.pad(c, ((0, 8 - bsz), (0, 0)))
    mod = _adaln(c_pad, w_ada, row(b_ada))[:bsz].reshape(bsz, 6, d)

    x2d = x.reshape(t, d)
    n_gate = w_branch_gate.shape[1]
    w_cat = jnp.concatenate([w_branch_gate, w_in], axis=1).astype(BF16)
    b_cat = jnp.concatenate([b_branch_gate, jnp.zeros((w_in.shape[1],), F32)]).reshape(1, -1)
    pg = _in_proj(x2d, mod, row(g_mix_pre), w_cat, b_cat, n_gate, seq)
    pg3 = pg.reshape(bsz, seq, -1)

    hidden = filt_w2.shape[0]
    bands = np.zeros((1, LANES), np.float32)
    band_vals = np.linspace(1e-4, FILTER_BANDS - 1, FILTER_BANDS, dtype=np.float32)
    bands[0, 1:1 + FILTER_BANDS] = band_vals
    bands[0, 1 + FILTER_BANDS:1 + 2 * FILTER_BANDS] = band_vals
    w1p = jnp.pad(filt_w1, ((0, LANES - filt_w1.shape[0]), (0, 0)))
    max_decay = math.log(DECAY_TARGET) / FAST_DECAY_PCT
    min_decay = math.log(DECAY_TARGET) / SLOW_DECAY_PCT
    deltas = jnp.abs(jnp.linspace(min_decay, max_decay, width, dtype=F32)).reshape(1, -1)
    uw = _filters(seq, width, jnp.asarray(bands), w1p, row(filt_b1), filt_w2, row(filt_b2),
                  row(filt_freq), deltas, filt_w3)
    s1, s1i, s2, s2i = _fft_tables(FFT_N1, 2 * seq // FFT_N1)
    kspec = _spectra(uw, s1, s2)
    y_hy = _hyena(pg3, n_gate, conv_w, row(conv_b), hyena_skip, kspec, s1, s1i, s2, s2i, width)

    half = ROT_DIM // 2
    inv_freq = np.power(ROPE_THETA, -2.0 * np.arange(half, dtype=np.float32) / ROT_DIM).astype(np.float32)
    freq_row = np.zeros((1, LANES), np.float32)
    freq_row[0, :half] = inv_freq
    freq_row[0, half:ROT_DIM] = inv_freq
    sign_row = np.zeros((1, LANES), np.float32)
    sign_row[0, :half] = -1.0
    sign_row[0, half:ROT_DIM] = 1.0
    y_at = _attention(pg3, positions.reshape(bsz, seq, 1), jnp.asarray(freq_row), jnp.asarray(sign_row),
                      n_gate + 3 * width)

    w_r = jnp.concatenate([w_group, jnp.transpose(w_expert, (1, 0, 2)).reshape(d, N_EXPERTS)], axis=1)
    w_r = jnp.pad(w_r, ((0, 0), (0, LANES - w_r.shape[1])))
    b_r = jnp.pad(jnp.concatenate([b_group, b_expert.reshape(-1)]), (0, LANES - N_EXPERT_GROUPS - N_EXPERTS))
    x1, h2, logits = _mix_out(y_hy.reshape(t, width), y_at.reshape(t, -1), pg, x2d, mod,
                              row(g_mix_post), row(g_ffn_pre), w_hy_o.astype(BF16), w_at_o.astype(BF16),
                              w_out.astype(BF16), w_r, b_r.reshape(1, -1), seq)

    eid, gate = _route(logits)
    flat_e = eid.reshape(t * TOP_K, 1)
    rank, counts = _rank(flat_e)
    counts = counts[0, :N_EXPERTS]
    rb = EXPERT_ROW_BLOCK
    padded = (counts + rb - 1) // rb * rb
    pends = jnp.cumsum(padded)
    pstarts = pends - padded
    dest = (pstarts[flat_e[:, 0]] + rank[:, 0]).astype(jnp.int32)
    n_blocks = t * TOP_K // rb + N_EXPERTS
    block_start = jnp.arange(n_blocks, dtype=jnp.int32) * rb
    block_e = jnp.minimum(jnp.searchsorted(pends, block_start, side='right'), N_EXPERTS - 1).astype(jnp.int32)
    tok = jnp.arange(t * TOP_K, dtype=jnp.int32) // TOP_K
    src_tok = jnp.zeros((n_blocks * rb,), jnp.int32).at[dest].set(tok)

    ybuf = _experts(block_e, src_tok, h2, w1_exp, w3_exp, w2_exp)
    out = _combine(dest, ybuf, gate, x1, mod, row(g_ffn_post), seq)
    return out.reshape(bsz, seq, d)


def kernel(x, c, positions, w_ada, b_ada, g_mix_pre, g_mix_post, g_ffn_pre, g_ffn_post, w_in, conv_w, conv_b, filt_w1, filt_b1, filt_w2, filt_b2, filt_w3, filt_freq, hyena_skip, w_branch_gate, b_branch_gate, w_hy_o, w_at_o, w_out, w_group, b_group, w_expert, b_expert, w1_exp, w3_exp, w2_exp):
    depth = w_ada.shape[0]
    for l in range(depth):
        x = _layer(x, c, positions, w_ada[l], b_ada[l], g_mix_pre[l], g_mix_post[l], g_ffn_pre[l],
                   g_ffn_post[l], w_in[l], conv_w[l], conv_b[l], filt_w1[l], filt_b1[l], filt_w2[l],
                   filt_b2[l], filt_w3[l], filt_freq[l], hyena_skip[l], w_branch_gate[l],
                   b_branch_gate[l], w_hy_o[l], w_at_o[l], w_out[l], w_group[l], b_group[l],
                   w_expert[l], b_expert[l], w1_exp[l], w3_exp[l], w2_exp[l])
    return x
```

```python
import functools
import math

import numpy as np
import jax
import jax.numpy as jnp
from jax import lax
from jax.experimental import pallas as pl
from jax.experimental.pallas import tpu as pltpu

F32 = jnp.float32
BF16 = jnp.bfloat16

LANES = 128
MIB = 1024 * 1024

RMS_EPS = 1e-6
NEG_INF = -1e30

HEAD_DIM = 128
ROT_DIM = HEAD_DIM // 4
ROPE_THETA = 500000.0
ATTN_GROUPS = ((128, 1), (512, 4), (2048, 16))
HEADS_PER_GROUP = 4
N_ATTN_HEADS = HEADS_PER_GROUP * len(ATTN_GROUPS)

FILTER_BANDS = 16
DECAY_TARGET = 1e-2
FAST_DECAY_PCT = 0.3
SLOW_DECAY_PCT = 1.5

N_EXPERT_GROUPS = 8
EXPERTS_PER_GROUP = 8
N_EXPERTS = N_EXPERT_GROUPS * EXPERTS_PER_GROUP
TOP_K = 2
EXPERT_ROW_BLOCK = 128

FFT_N1 = 128
PITCH = FFT_N1 + 8
FFT_UNROLL = 8
FFT_MID_UNROLL = 4


def _cparams(sem, vmem_mib):
    return pltpu.CompilerParams(dimension_semantics=sem, vmem_limit_bytes=vmem_mib * MIB)


def _rms(x, g):
    return x * lax.rsqrt(jnp.mean(x * x, axis=-1, keepdims=True) + RMS_EPS) * g


def _adaln_body(c_ref, w_ref, b_ref, o_ref):
    c = c_ref[...]
    cond = c * jax.nn.sigmoid(c)
    o_ref[...] = jnp.dot(cond.astype(BF16), w_ref[...].astype(BF16),
                         preferred_element_type=F32) + b_ref[...]


def _adaln(c_pad, w_ada, b_ada):
    rows, d = c_pad.shape
    n = w_ada.shape[1]
    tn = 1024
    return pl.pallas_call(
        _adaln_body,
        grid=(n // tn,),
        in_specs=[pl.BlockSpec((rows, d), lambda j: (0, 0)),
                  pl.BlockSpec((d, tn), lambda j: (0, j)),
                  pl.BlockSpec((1, tn), lambda j: (0, j))],
        out_specs=pl.BlockSpec((rows, tn), lambda j: (0, j)),
        out_shape=jax.ShapeDtypeStruct((rows, n), F32),
        compiler_params=_cparams(("arbitrary",), 40),
        name="adaln",
    )(c_pad, w_ada, b_ada)


def _in_proj_body(n_gate, x_ref, mod_ref, g_ref, w_ref, b_ref, o_ref, h_scr):
    j = pl.program_id(1)

    @pl.when(j == 0)
    def _():
        x = x_ref[...]
        h = _rms(x, g_ref[...]) * (1.0 + mod_ref[0, 1:2, :]) + mod_ref[0, 0:1, :]
        h_scr[...] = h.astype(BF16)

    acc = jnp.dot(h_scr[...], w_ref[...], preferred_element_type=F32) + b_ref[...]

    @pl.when(j < n_gate)
    def _():
        o_ref[...] = jax.nn.sigmoid(acc).astype(o_ref.dtype)

    @pl.when(j >= n_gate)
    def _():
        o_ref[...] = acc.astype(o_ref.dtype)


def _in_proj(x2d, mod, g_pre, w_cat, b_cat, n_gate_cols, seq):
    t, d = x2d.shape
    n = w_cat.shape[1]
    tm, tn = 512, 512
    per_batch = seq // tm
    return pl.pallas_call(
        functools.partial(_in_proj_body, n_gate_cols // tn),
        grid=(t // tm, n // tn),
        in_specs=[pl.BlockSpec((tm, d), lambda i, j: (i, 0)),
                  pl.BlockSpec((1, 6, d), lambda i, j: (i // per_batch, 0, 0)),
                  pl.BlockSpec((1, d), lambda i, j: (0, 0)),
                  pl.BlockSpec((d, tn), lambda i, j: (0, j)),
                  pl.BlockSpec((1, tn), lambda i, j: (0, j))],
        out_specs=pl.BlockSpec((tm, tn), lambda i, j: (i, j)),
        out_shape=jax.ShapeDtypeStruct((t, n), BF16),
        scratch_shapes=[pltpu.VMEM((tm, d), BF16)],
        compiler_params=_cparams(("arbitrary", "arbitrary"), 40),
        name="in_proj",
    )(x2d, mod, g_pre, w_cat, b_cat)


def _filters_body(seq, band_ref, w1_ref, b1_ref, w2_ref, b2_ref, fr_ref, dl_ref,
                  w3a_ref, w3b_ref, o_ref, hid_scr):
    i = pl.program_id(0)
    j = pl.program_id(1)
    tl = hid_scr.shape[0]
    row = (lax.broadcasted_iota(jnp.int32, (tl, 1), 0) + i * tl).astype(F32)

    @pl.when((j == 0) & (pl.program_id(2) == 0))
    def _():
        lane = lax.broadcasted_iota(jnp.int32, (tl, LANES), 1)
        t = row / (seq - 1.0)
        ang = band_ref[...] * (2.0 * math.pi * row / seq)
        feats = jnp.where(lane == 0, t,
                          jnp.where(lane <= FILTER_BANDS, jnp.cos(ang),
                                    jnp.where(lane <= 2 * FILTER_BANDS, -jnp.sin(ang), 0.0)))
        hi = lax.Precision.HIGHEST
        fr = fr_ref[...]
        hid = jnp.sin(fr * (jnp.dot(feats, w1_ref[...], precision=hi, preferred_element_type=F32)
                            + b1_ref[...]))
        hid = jnp.sin(fr * (jnp.dot(hid, w2_ref[...], precision=hi, preferred_element_type=F32)
                            + b2_ref[...]))
        hid_scr[...] = hid

    hi = lax.Precision.HIGHEST
    hid = hid_scr[...]
    decay = jnp.exp(-(row / (seq - 1.0)) * dl_ref[...])
    hf = jnp.dot(hid, w3a_ref[...], precision=hi, preferred_element_type=F32) * decay
    hb = jnp.dot(hid, w3b_ref[...], precision=hi, preferred_element_type=F32) * decay
    hb = jnp.where(row == 0.0, 0.0, hb)
    o_ref[0, 0] = hf + hb
    o_ref[0, 1] = hf - hb


def _filters(seq, width, bands, w1p, b1, w2, b2, freq, deltas, w3):
    tl, tc = 512, 256
    nct = width // tc
    hidden = w2.shape[0]
    const = lambda i, j, o: (0, 0)
    return pl.pallas_call(
        functools.partial(_filters_body, float(seq)),
        grid=(seq // tl, nct, 2),
        in_specs=[pl.BlockSpec((1, LANES), const),
                  pl.BlockSpec((LANES, hidden), const),
                  pl.BlockSpec((1, hidden), const),
                  pl.BlockSpec((hidden, hidden), const),
                  pl.BlockSpec((1, hidden), const),
                  pl.BlockSpec((1, hidden), const),
                  pl.BlockSpec((1, tc), lambda i, j, o: (0, j)),
                  pl.BlockSpec((hidden, tc), lambda i, j, o: (0, (2 * o) * nct + j)),
                  pl.BlockSpec((hidden, tc), lambda i, j, o: (0, (2 * o + 1) * nct + j))],
        out_specs=pl.BlockSpec((1, 2, tl, tc), lambda i, j, o: (o, 0, i, j)),
        out_shape=jax.ShapeDtypeStruct((2, 2, seq, width), F32),
        scratch_shapes=[pltpu.VMEM((tl, hidden), F32)],
        compiler_params=_cparams(("arbitrary", "arbitrary", "arbitrary"), 32),
        name="filters",
    )(bands, w1p, b1, w2, b2, freq, deltas, w3, w3)


def _fft_tables(n1, n2):
    n = n1 * n2
    q = np.arange(n2)[:, None]
    b = np.arange(n2 // 2)[None, :]
    a = np.arange(n1)[:, None, None]
    ang = -2.0 * np.pi * (a * q[None] / n + (q * b)[None] / n2)
    stage1 = np.concatenate([np.cos(ang), np.sin(ang)], axis=1)
    stage1_inv = np.transpose(stage1, (0, 2, 1)) / n
    p = np.arange(n1)
    ang2 = -2.0 * np.pi * np.outer(p, p) / n1
    fre, fim = np.cos(ang2), np.sin(ang2)
    stage2 = np.block([[fre, -fim], [fim, fre]])
    stage2_inv = np.block([[fre, fim], [-fim, fre]])
    as_bf16 = lambda m: jnp.asarray(m, dtype=F32).astype(BF16)
    return as_bf16(stage1), as_bf16(stage1_inv), as_bf16(stage2), as_bf16(stage2_inv)


def _halves(ref, rows):
    return jnp.concatenate([ref[0, rows, :], ref[1, rows, :]], axis=1)


def _fft_stage1(z_ref, s1_ref, gre_ref, gim_ref):
    n1, two_n2, n2h = s1_ref.shape
    n2 = two_n2 // 2

    def step(a, carry):
        zrows = _halves(z_ref, pl.ds(a, n2h, stride=PITCH))
        g = jnp.dot(s1_ref[a], zrows.astype(BF16), preferred_element_type=F32)
        for h in range(2):
            cols = slice(h * LANES, (h + 1) * LANES)
            gre_ref[h, pl.ds(a, n2, stride=PITCH), :] = g[:n2, cols]
            gim_ref[h, pl.ds(a, n2, stride=PITCH), :] = g[n2:, cols]
        return carry

    lax.fori_loop(0, n1, step, 0, unroll=FFT_UNROLL)


def _fft_stage1_inv(gre_ref, gim_ref, s1i_ref, y_ref):
    n1, n2h, two_n2 = s1i_ref.shape
    n2 = two_n2 // 2

    def step(a, carry):
        rows = pl.ds(a, n2, stride=PITCH)
        hs = jnp.concatenate([_halves(gre_ref, rows), _halves(gim_ref, rows)], axis=0)
        y = jnp.dot(s1i_ref[a], hs.astype(BF16), preferred_element_type=F32)
        for h in range(2):
            y_ref[h, pl.ds(a, n2h, stride=PITCH), :] = y[:, h * LANES:(h + 1) * LANES]
        return carry

    lax.fori_loop(0, n1, step, 0, unroll=FFT_UNROLL)


def _stage2_block(gre_ref, gim_ref, s2_ref, q):
    n1 = s2_ref.shape[0] // 2
    rows = pl.ds(pl.multiple_of(q * PITCH, 8), n1)
    gs = jnp.concatenate([_halves(gre_ref, rows), _halves(gim_ref, rows)], axis=0)
    x = jnp.dot(s2_ref[...], gs.astype(BF16), preferred_element_type=F32)
    return x[:n1], x[n1:]


def _spectra_body(uw_ref, s1_ref, s2_ref, k_ref, z_scr, gre_scr, gim_scr):
    n1 = s2_ref.shape[0] // 2
    n2 = s1_ref.shape[1] // 2
    for h in range(2):
        for b in range(n2 // 2):
            z_scr[h, pl.ds(b * PITCH, n1), :] = uw_ref[0, h, pl.ds(b * n1, n1), :]
    _fft_stage1(z_scr, s1_ref, gre_scr, gim_scr)

    def step(q, carry):
        xre, xim = _stage2_block(gre_scr, gim_scr, s2_ref, q)
        rows = pl.ds(pl.multiple_of(q * n1, 8), n1)
        k_ref[0, 0, rows, :] = xre[:, :LANES]
        k_ref[0, 1, rows, :] = xim[:, LANES:]
        return carry

    lax.fori_loop(0, n2, step, 0, unroll=FFT_MID_UNROLL)


def _spectra(uw, s1, s2):
    _, _, seq, width = uw.shape
    n1 = FFT_N1
    n2 = 2 * seq // n1
    const3 = lambda j, o: (0, 0, 0)
    return pl.pallas_call(
        _spectra_body,
        grid=(width // LANES, 2),
        in_specs=[pl.BlockSpec((1, 2, seq, LANES), lambda j, o: (o, 0, 0, j)),
                  pl.BlockSpec(s1.shape, const3),
                  pl.BlockSpec(s2.shape, lambda j, o: (0, 0))],
        out_specs=pl.BlockSpec((1, 2, 2 * seq, LANES), lambda j, o: (o, 0, 0, j)),
        out_shape=jax.ShapeDtypeStruct((2, 2, 2 * seq, width), F32),
        scratch_shapes=[pltpu.VMEM((2, (n2 // 2) * PITCH, LANES), F32),
                        pltpu.VMEM((2, n2 * PITCH, LANES), F32),
                        pltpu.VMEM((2, n2 * PITCH, LANES), F32)],
        compiler_params=_cparams(("arbitrary", "arbitrary"), 56),
        name="spectra",
    )(uw, s1, s2)


def _short_conv_block(p_ref, b, i, nblk, w_ref, bias_ref):
    n1 = FFT_N1
    start = pl.multiple_of(i * n1, n1)
    cur = p_ref[b, pl.ds(start, n1), :].astype(F32)
    before = p_ref[b, pl.ds(pl.multiple_of(jnp.maximum(start - 16, 0), 16), 16), :].astype(F32)
    after = p_ref[b, pl.ds(pl.multiple_of(jnp.minimum(start + n1, (nblk - 1) * n1), 16), 16), :].astype(F32)
    last_prev = before[15:16] * jnp.where(i > 0, 1.0, 0.0).astype(F32)
    first_next = after[0:1] * jnp.where(i < nblk - 1, 1.0, 0.0).astype(F32)
    row = lax.broadcasted_iota(jnp.int32, (n1, 1), 0)
    prev = jnp.where(row == 0, last_prev, pltpu.roll(cur, 1, 0))
    nxt = jnp.where(row == n1 - 1, first_next, pltpu.roll(cur, n1 - 1, 0))
    return prev * w_ref[0:1, :] + cur * w_ref[1:2, :] + nxt * w_ref[2:3, :] + bias_ref[...]


def _hyena_body(pv_ref, px1_ref, px2_ref, cwv_ref, cw1_ref, cw2_ref, cbv_ref, cb1_ref, cb2_ref,
                skip_ref, k_ref, s1_ref, s1i_ref, s2_ref, s2i_ref, o_ref,
                z_scr, y_scr, gre_scr, gim_scr):
    c = pl.program_id(1)
    g = pl.program_id(2)
    ngroups = pl.num_programs(2)
    n1 = FFT_N1
    n2 = s1_ref.shape[1] // 2
    nblk = n2 // 2
    per_group = n2 // ngroups

    @pl.when(g == 0)
    def _():
        @pl.when(c == 0)
        def _():
            def fill(i, carry):
                for b in range(2):
                    z_scr[b, pl.ds(pl.multiple_of(i * PITCH, 8), n1), :] = _short_conv_block(
                        pv_ref, b, i, nblk, cwv_ref, cbv_ref)
                return carry
            lax.fori_loop(0, nblk, fill, 0)

        _fft_stage1(z_scr, s1_ref, gre_scr, gim_scr)

    def mid(ql, carry):
        q = g * per_group + ql
        xre, xim = _stage2_block(gre_scr, gim_scr, s2_ref, q)
        krows = pl.ds(pl.multiple_of(ql * n1, 8), n1)
        kre = k_ref[0, 0, krows, :]
        kim = k_ref[0, 1, krows, :]
        kre = jnp.concatenate([kre, kre], axis=1)
        kim = jnp.concatenate([kim, kim], axis=1)
        ys = jnp.concatenate([xre * kre - xim * kim, xre * kim + xim * kre], axis=0)
        hh = jnp.dot(s2i_ref[...], ys.astype(BF16), preferred_element_type=F32)
        rows = pl.ds(pl.multiple_of(q * PITCH, 8), n1)
        for b in range(2):
            cols = slice(b * LANES, (b + 1) * LANES)
            gre_scr[b, rows, :] = hh[:n1, cols]
            gim_scr[b, rows, :] = hh[n1:, cols]
        return carry

    lax.fori_loop(0, per_group, mid, 0, unroll=FFT_MID_UNROLL)

    @pl.when(g == ngroups - 1)
    def _():
        _fft_stage1_inv(gre_scr, gim_scr, s1i_ref, y_scr)

        def post(px_ref, cw_ref, cb_ref, order, store):
            def blk(i, carry):
                rows = pl.ds(pl.multiple_of(i * PITCH, 8), n1)
                for b in range(2):
                    zb = z_scr[b, rows, :]
                    gate = _short_conv_block(px_ref, b, i, nblk, cw_ref, cb_ref)
                    store(b, i, rows, gate * (y_scr[b, rows, :] + zb * skip_ref[order:order + 1, :]))
                return carry
            lax.fori_loop(0, nblk, blk, 0)

        @pl.when(c == 0)
        def _():
            def store(b, i, rows, val):
                z_scr[b, rows, :] = val
            post(px1_ref, cw1_ref, cb1_ref, 0, store)

        @pl.when(c == 1)
        def _():
            def store(b, i, rows, val):
                o_ref[b, pl.ds(pl.multiple_of(i * n1, n1), n1), :] = val.astype(o_ref.dtype)
            post(px2_ref, cw2_ref, cb2_ref, 1, store)


def _hyena(pg3, proj_col0, conv_w, conv_b, skip, kspec, s1, s1i, s2, s2i, width):
    bsz, seq, _ = pg3.shape
    assert bsz == 2
    n1 = FFT_N1
    n2 = 2 * seq // n1
    nct = width // LANES
    ngroups = 4
    krows = (n2 // ngroups) * n1
    col = lambda off: (lambda j, c, g: (0, 0, proj_col0 // LANES + off * nct + j))
    cw = lambda off: (lambda j, c, g: (0, off * nct + j))
    const3 = lambda j, c, g: (0, 0, 0)
    const2 = lambda j, c, g: (0, 0)
    return pl.pallas_call(
        _hyena_body,
        grid=(nct, 2, ngroups),
        in_specs=[pl.BlockSpec((2, seq, LANES), col(0)),
                  pl.BlockSpec((2, seq, LANES), col(1)),
                  pl.BlockSpec((2, seq, LANES), col(2)),
                  pl.BlockSpec((3, LANES), cw(0)),
                  pl.BlockSpec((3, LANES), cw(1)),
                  pl.BlockSpec((3, LANES), cw(2)),
                  pl.BlockSpec((1, LANES), cw(0)),
                  pl.BlockSpec((1, LANES), cw(1)),
                  pl.BlockSpec((1, LANES), cw(2)),
                  pl.BlockSpec((2, LANES), lambda j, c, g: (0, j)),
                  pl.BlockSpec((1, 2, krows, LANES), lambda j, c, g: (c, 0, g, j)),
                  pl.BlockSpec(s1.shape, const3),
                  pl.BlockSpec(s1i.shape, const3),
                  pl.BlockSpec(s2.shape, const2),
                  pl.BlockSpec(s2i.shape, const2)],
        out_specs=pl.BlockSpec((2, seq, LANES), lambda j, c, g: (0, 0, j)),
        out_shape=jax.ShapeDtypeStruct((2, seq, width), BF16),
        scratch_shapes=[pltpu.VMEM((2, (n2 // 2) * PITCH, LANES), F32),
                        pltpu.VMEM((2, (n2 // 2) * PITCH, LANES), F32),
                        pltpu.VMEM((2, n2 * PITCH, LANES), F32),
                        pltpu.VMEM((2, n2 * PITCH, LANES), F32)],
        compiler_params=_cparams(("arbitrary", "arbitrary", "arbitrary"), 56),
        name="hyena",
    )(pg3, pg3, pg3, conv_w, conv_w, conv_w, conv_b, conv_b, conv_b, skip, kspec, s1, s1i, s2, s2i)


ATTN_QBLK = 128
ATTN_UNROLL = 4


def _attention_body(pos_ref, freq_ref, sign_ref, *refs):
    qkv_refs = refs[:9]
    o_ref = refs[9]
    cos_scr, sin_scr, q_scr, k_scr, v_scr, og_scr, lse_scr = refs[10:]
    seq = q_scr.shape[0]
    chunk = 512
    nchunks = seq // chunk

    @pl.when(pl.program_id(1) == 0)
    def _():
        def trig(i, carry):
            rows = pl.ds(pl.multiple_of(i * chunk, chunk), chunk)
            ang = pos_ref[0, rows, :].astype(F32) * freq_ref[...]
            cos_scr[rows, :] = jnp.cos(ang)
            sin_scr[rows, :] = jnp.sin(ang) * sign_ref[...]
            return carry
        lax.fori_loop(0, nchunks, trig, 0)

    def rotate(src_ref, dst_ref):
        def body(i, carry):
            rows = pl.ds(pl.multiple_of(i * chunk, chunk), chunk)
            t = src_ref[0, rows, :].astype(F32)
            lane = lax.broadcasted_iota(jnp.int32, t.shape, 1)
            partner = jnp.where(lane < ROT_DIM // 2,
                                pltpu.roll(t, LANES - ROT_DIM // 2, 1), pltpu.roll(t, ROT_DIM // 2, 1))
            dst_ref[rows, :] = t * cos_scr[rows, :] + partner * sin_scr[rows, :]
            return carry
        lax.fori_loop(0, nchunks, body, 0)

    def widen(src_ref, dst_ref):
        def body(i, carry):
            rows = pl.ds(pl.multiple_of(i * chunk, chunk), chunk)
            dst_ref[rows, :] = src_ref[0, rows, :].astype(F32)
            return carry
        lax.fori_loop(0, nchunks, body, 0)

    scale = HEAD_DIM ** -0.5
    for gi, (window, dil) in enumerate(ATTN_GROUPS):
        rotate(qkv_refs[3 * gi], q_scr)
        rotate(qkv_refs[3 * gi + 1], k_scr)
        widen(qkv_refs[3 * gi + 2], v_scr)
        n = seq // dil
        half = window // (2 * dil)
        tk = min(n, 3 * ATTN_QBLK)
        blocks_per_res = n // ATTN_QBLK

        def block(u, carry, dil=dil, n=n, half=half, tk=tk, blocks_per_res=blocks_per_res, gi=gi):
            r = u // blocks_per_res
            m = u % blocks_per_res
            q0 = m * ATTN_QBLK
            k0 = jnp.clip(q0 - ATTN_QBLK, 0, n - tk)
            qrows = pl.ds(r + dil * q0, ATTN_QBLK, stride=dil)
            krows = pl.ds(r + dil * k0, tk, stride=dil)
            qb = q_scr[qrows, :].astype(BF16)
            kb = k_scr[krows, :].astype(BF16)
            vb = v_scr[krows, :].astype(BF16)
            s = lax.dot_general(qb, kb, (((1,), (1,)), ((), ())), preferred_element_type=F32) * scale
            qi = q0 + lax.broadcasted_iota(jnp.int32, (ATTN_QBLK, tk), 0)
            kj = k0 + lax.broadcasted_iota(jnp.int32, (ATTN_QBLK, tk), 1)
            s = jnp.where(jnp.abs(qi - kj) <= half, s, NEG_INF)
            mx = jnp.max(s, axis=-1, keepdims=True)
            p = jnp.exp(s - mx)
            l = jnp.sum(p, axis=-1, keepdims=True)
            o = jnp.dot(p.astype(BF16), vb, preferred_element_type=F32) / l
            og_scr[gi, qrows, :] = o
            lse_scr[gi, qrows, :] = jnp.broadcast_to(mx + jnp.log(l), (ATTN_QBLK, LANES))
            return carry

        lax.fori_loop(0, seq // ATTN_QBLK, block, 0, unroll=ATTN_UNROLL)

    def merge(i, carry):
        rows = pl.ds(pl.multiple_of(i * chunk, chunk), chunk)
        lses = [lse_scr[gi, rows, :] for gi in range(len(ATTN_GROUPS))]
        mx = functools.reduce(jnp.maximum, lses)
        ws = [jnp.exp(v - mx) for v in lses]
        den = functools.reduce(lambda a, b: a + b, ws)
        num = functools.reduce(lambda a, b: a + b,
                               [w * og_scr[gi, rows, :] for gi, w in enumerate(ws)])
        o_ref[0, rows, :] = (num / den).astype(o_ref.dtype)
        return carry

    lax.fori_loop(0, nchunks, merge, 0)


def _attention(pg3, pos3, freq_row, sign_row, qkv_col0):
    bsz, seq, _ = pg3.shape
    ng = len(ATTN_GROUPS)
    in_specs = [pl.BlockSpec((1, seq, 1), lambda b, h: (b, 0, 0)),
                pl.BlockSpec((1, LANES), lambda b, h: (0, 0)),
                pl.BlockSpec((1, LANES), lambda b, h: (0, 0))]
    for gi in range(ng):
        for which in range(3):
            base = qkv_col0 // LANES + which * N_ATTN_HEADS + gi * HEADS_PER_GROUP
            in_specs.append(pl.BlockSpec((1, seq, LANES), lambda b, h, base=base: (b, 0, base + h)))
    return pl.pallas_call(
        _attention_body,
        grid=(bsz, HEADS_PER_GROUP),
        in_specs=in_specs,
        out_specs=pl.BlockSpec((1, seq, LANES), lambda b, h: (b, 0, h)),
        out_shape=jax.ShapeDtypeStruct((bsz, seq, HEADS_PER_GROUP * HEAD_DIM), BF16),
        scratch_shapes=[pltpu.VMEM((seq, LANES), F32), pltpu.VMEM((seq, LANES), F32),
                        pltpu.VMEM((seq, LANES), F32), pltpu.VMEM((seq, LANES), F32),
                        pltpu.VMEM((seq, LANES), F32),
                        pltpu.VMEM((ng, seq, LANES), F32), pltpu.VMEM((ng, seq, LANES), F32)],
        compiler_params=_cparams(("arbitrary", "arbitrary"), 56),
        name="attention",
    )(pos3, freq_row, sign_row, *([pg3] * (3 * ng)))


def _mix_out_body(yhy_ref, yat_ref, ghy_ref, gat_ref, x_ref, mod_ref, gpost_ref, gpre_ref,
                  whyo_ref, wato_ref, wout_ref, wr_ref, br_ref, x1_ref, h2_ref, lg_ref):
    a = jnp.dot(yhy_ref[...], whyo_ref[...], preferred_element_type=F32)
    b = jnp.dot(yat_ref[...], wato_ref[...], preferred_element_type=F32)
    merged = ghy_ref[...].astype(F32) * a + gat_ref[...].astype(F32) * b
    y = jnp.dot(merged.astype(BF16), wout_ref[...], preferred_element_type=F32)
    x1 = x_ref[...] + mod_ref[0, 2:3, :] * _rms(y, gpost_ref[...])
    x1_ref[...] = x1
    h2 = _rms(x1, gpre_ref[...]) * (1.0 + mod_ref[0, 4:5, :]) + mod_ref[0, 3:4, :]
    h2_ref[...] = h2
    lg_ref[...] = jnp.dot(h2, wr_ref[...], precision=lax.Precision.HIGHEST,
                          preferred_element_type=F32) + br_ref[...]


def _mix_out(y_hy, y_at, pg, x2d, mod, g_post, g_pre, w_hy_o, w_at_o, w_out, w_r, b_r, seq):
    t, d = x2d.shape
    tm = 256
    per_batch = seq // tm
    gblk = 0
    const = lambda i: (0, 0)
    return pl.pallas_call(
        _mix_out_body,
        grid=(t // tm,),
        in_specs=[pl.BlockSpec((tm, y_hy.shape[1]), lambda i: (i, 0)),
                  pl.BlockSpec((tm, y_at.shape[1]), lambda i: (i, 0)),
                  pl.BlockSpec((tm, d), lambda i: (i, gblk)),
                  pl.BlockSpec((tm, d), lambda i: (i, gblk + 1)),
                  pl.BlockSpec((tm, d), lambda i: (i, 0)),
                  pl.BlockSpec((1, 6, d), lambda i: (i // per_batch, 0, 0)),
                  pl.BlockSpec((1, d), const),
                  pl.BlockSpec((1, d), const),
                  pl.BlockSpec(w_hy_o.shape, const),
                  pl.BlockSpec(w_at_o.shape, const),
                  pl.BlockSpec(w_out.shape, const),
                  pl.BlockSpec(w_r.shape, const),
                  pl.BlockSpec((1, LANES), const)],
        out_specs=[pl.BlockSpec((tm, d), lambda i: (i, 0)),
                   pl.BlockSpec((tm, d), lambda i: (i, 0)),
                   pl.BlockSpec((tm, LANES), lambda i: (i, 0))],
        out_shape=[jax.ShapeDtypeStruct((t, d), F32),
                   jax.ShapeDtypeStruct((t, d), F32),
                   jax.ShapeDtypeStruct((t, LANES), F32)],
        compiler_params=_cparams(("arbitrary",), 56),
        name="mix_out",
    )(y_hy, y_at, pg, pg, x2d, mod, g_post, g_pre, w_hy_o, w_at_o, w_out, w_r, b_r)


def _route_body(lg_ref, eid_ref, gate_ref):
    lg = lg_ref[...]
    lane = lax.broadcasted_iota(jnp.int32, lg.shape, 1)
    big = jnp.int32(1 << 20)

    def first_argmax(vals, mask):
        v = jnp.where(mask, vals, -jnp.inf)
        mx = jnp.max(v, axis=-1, keepdims=True)
        idx = jnp.min(jnp.where(mask & (v == mx), lane, big), axis=-1, keepdims=True)
        return mx, idx

    gmask = lane < N_EXPERT_GROUPS
    gmax, gidx = first_argmax(lg, gmask)
    gval = 1.0 / jnp.sum(jnp.where(gmask, jnp.exp(lg - gmax), 0.0), axis=-1, keepdims=True)
    lo = N_EXPERT_GROUPS + gidx * EXPERTS_PER_GROUP
    emask = (lane >= lo) & (lane < lo + EXPERTS_PER_GROUP)
    v1, i1 = first_argmax(lg, emask)
    v2, i2 = first_argmax(lg, emask & (lane != i1))
    e2 = jnp.exp(v2 - v1)
    p1 = 1.0 / (1.0 + e2)
    p2 = e2 / (1.0 + e2)
    eid = jnp.where(lane == 0, i1, i2) - N_EXPERT_GROUPS
    gate = gval * jnp.where(lane == 0, p1, p2)
    eid_ref[...] = eid[:, :TOP_K]
    gate_ref[...] = gate[:, :TOP_K]


def _route(logits):
    t = logits.shape[0]
    tm = 512
    return pl.pallas_call(
        _route_body,
        grid=(t // tm,),
        in_specs=[pl.BlockSpec((tm, LANES), lambda i: (i, 0))],
        out_specs=[pl.BlockSpec((tm, TOP_K), lambda i: (i, 0)),
                   pl.BlockSpec((tm, TOP_K), lambda i: (i, 0))],
        out_shape=[jax.ShapeDtypeStruct((t, TOP_K), jnp.int32),
                   jax.ShapeDtypeStruct((t, TOP_K), F32)],
        compiler_params=_cparams(("arbitrary",), 32),
        name="route",
    )(logits)


def _rank_body(e_ref, rank_ref, cnt_ref, carry_scr):
    i = pl.program_id(0)
    r = e_ref.shape[0]

    @pl.when(i == 0)
    def _():
        carry_scr[...] = jnp.zeros_like(carry_scr)

    lane = lax.broadcasted_iota(jnp.int32, (r, LANES), 1)
    onehot = (lane == e_ref[...]).astype(F32)
    tri = (lax.broadcasted_iota(jnp.int32, (r, r), 1)
           < lax.broadcasted_iota(jnp.int32, (r, r), 0)).astype(BF16)
    before = jnp.dot(tri, onehot.astype(BF16), preferred_element_type=F32) + carry_scr[0:1, :]
    rank_ref[...] = jnp.sum(onehot * before, axis=-1, keepdims=True).astype(jnp.int32)
    total = carry_scr[0:1, :] + jnp.sum(onehot, axis=0, keepdims=True)
    carry_scr[...] = jnp.broadcast_to(total, carry_scr.shape)
    cnt_ref[...] = jnp.broadcast_to(total, cnt_ref.shape).astype(jnp.int32)


def _rank(flat_e):
    a = flat_e.shape[0]
    r = 512
    return pl.pallas_call(
        _rank_body,
        grid=(a // r,),
        in_specs=[pl.BlockSpec((r, 1), lambda i: (i, 0))],
        out_specs=[pl.BlockSpec((r, 1), lambda i: (i, 0)),
                   pl.BlockSpec((8, LANES), lambda i: (0, 0))],
        out_shape=[jax.ShapeDtypeStruct((a, 1), jnp.int32),
                   jax.ShapeDtypeStruct((8, LANES), jnp.int32)],
        scratch_shapes=[pltpu.VMEM((8, LANES), F32)],
        compiler_params=_cparams(("arbitrary",), 32),
        name="rank",
    )(flat_e)


def _dest_body(e_ref, rank_ref, ps_ref, o_ref):
    lane = lax.broadcasted_iota(jnp.int32, (e_ref.shape[0], LANES), 1)
    first = jnp.sum(jnp.where(lane == e_ref[...], ps_ref[...], 0.0), axis=-1, keepdims=True)
    o_ref[...] = first.astype(jnp.int32) + rank_ref[...]


def _dest(flat_e, rank, pstarts_row):
    a = flat_e.shape[0]
    r = 2048
    return pl.pallas_call(
        _dest_body,
        grid=(a // r,),
        in_specs=[pl.BlockSpec((r, 1), lambda i: (i, 0)),
                  pl.BlockSpec((r, 1), lambda i: (i, 0)),
                  pl.BlockSpec((1, LANES), lambda i: (0, 0))],
        out_specs=pl.BlockSpec((r, 1), lambda i: (i, 0)),
        out_shape=jax.ShapeDtypeStruct((a, 1), jnp.int32),
        compiler_params=_cparams(("arbitrary",), 32),
        name="dest",
    )(flat_e, rank, pstarts_row)


DMA_UNROLL = 8


def _experts_body(be_ref, src_ref, nused_ref, h_hbm, w1_ref, w3_ref, w2_ref, y_ref, x_scr, sem):
    i = pl.program_id(0)
    nrows = x_scr.shape[1]
    nused = nused_ref[0]

    def gather(blk, act):
        slot = blk % 2
        base = blk * nrows

        def body(r, carry):
            act(pltpu.make_async_copy(h_hbm.at[pl.ds(src_ref[base + r], 1)],
                                      x_scr.at[slot, pl.ds(r, 1)], sem.at[slot]))
            return carry

        lax.fori_loop(0, nrows, body, 0, unroll=DMA_UNROLL)

    @pl.when(i == 0)
    def _():
        gather(i, lambda cp: cp.start())

    @pl.when(i + 1 < nused)
    def _():
        gather(i + 1, lambda cp: cp.start())

    @pl.when(i < nused)
    def _():
        gather(i, lambda cp: cp.wait())
        xb = x_scr[i % 2].astype(BF16)
        a = jnp.dot(xb, w1_ref[0].astype(BF16), preferred_element_type=F32)
        b = jnp.dot(xb, w3_ref[0].astype(BF16), preferred_element_type=F32)
        act = (a * jax.nn.sigmoid(a) * b).astype(BF16)
        y_ref[...] = jnp.dot(act, w2_ref[0].astype(BF16), preferred_element_type=F32)

    @pl.when(i >= nused)
    def _():
        y_ref[...] = jnp.zeros_like(y_ref)


def _experts(block_e, src_tok, nused, h2, w1, w3, w2):
    nblocks = block_e.shape[0]
    d = h2.shape[1]
    ff = w1.shape[2]
    rb = EXPERT_ROW_BLOCK
    wmap = lambda i, be, src, nu: (be[i], 0, 0)
    grid_spec = pltpu.PrefetchScalarGridSpec(
        num_scalar_prefetch=3,
        grid=(nblocks,),
        in_specs=[pl.BlockSpec(memory_space=pl.ANY),
                  pl.BlockSpec((1, d, ff), wmap),
                  pl.BlockSpec((1, d, ff), wmap),
                  pl.BlockSpec((1, ff, d), wmap)],
        out_specs=pl.BlockSpec((rb, d), lambda i, be, src, nu: (i, 0)),
        scratch_shapes=[pltpu.VMEM((2, rb, d), F32), pltpu.SemaphoreType.DMA((2,))],
    )
    return pl.pallas_call(
        _experts_body,
        grid_spec=grid_spec,
        out_shape=jax.ShapeDtypeStruct((nblocks * rb, d), F32),
        compiler_params=_cparams(("arbitrary",), 48),
        name="experts",
    )(block_e, src_tok, nused, h2, w1, w3, w2)


def _combine_body(dest_ref, y_hbm, gate_ref, x1_ref, mod_ref, g_ref, o_ref, buf_scr, sem):
    i = pl.program_id(0)
    tm = x1_ref.shape[0]

    def gather(step, act):
        slot = step % 2
        base = step * tm * TOP_K

        def body(r, carry):
            for k in range(TOP_K):
                act(pltpu.make_async_copy(y_hbm.at[pl.ds(dest_ref[base + TOP_K * r + k], 1)],
                                          buf_scr.at[slot, k, pl.ds(r, 1)], sem.at[slot]))
            return carry

        lax.fori_loop(0, tm, body, 0, unroll=DMA_UNROLL // TOP_K)

    @pl.when(i == 0)
    def _():
        gather(i, lambda cp: cp.start())

    @pl.when(i + 1 < pl.num_programs(0))
    def _():
        gather(i + 1, lambda cp: cp.start())

    gather(i, lambda cp: cp.wait())
    slot = i % 2
    gate = gate_ref[...]
    y = buf_scr[slot, 0] * gate[:, 0:1] + buf_scr[slot, 1] * gate[:, 1:2]
    o_ref[...] = x1_ref[...] + mod_ref[0, 5:6, :] * _rms(y, g_ref[...])


def _combine(dest, ybuf, gate, x1, mod, g_post, seq):
    t, d = x1.shape
    tm = 128
    per_batch = seq // tm
    grid_spec = pltpu.PrefetchScalarGridSpec(
        num_scalar_prefetch=1,
        grid=(t // tm,),
        in_specs=[pl.BlockSpec(memory_space=pl.ANY),
                  pl.BlockSpec((tm, TOP_K), lambda i, dst: (i, 0)),
                  pl.BlockSpec((tm, d), lambda i, dst: (i, 0)),
                  pl.BlockSpec((1, 6, d), lambda i, dst: (i // per_batch, 0, 0)),
                  pl.BlockSpec((1, d), lambda i, dst: (0, 0))],
        out_specs=pl.BlockSpec((tm, d), lambda i, dst: (i, 0)),
        scratch_shapes=[pltpu.VMEM((2, TOP_K, tm, d), F32), pltpu.SemaphoreType.DMA((2,))],
    )
    return pl.pallas_call(
        _combine_body,
        grid_spec=grid_spec,
        out_shape=jax.ShapeDtypeStruct((t, d), F32),
        compiler_params=_cparams(("arbitrary",), 32),
        name="combine",
    )(dest, ybuf, gate, x1, mod, g_post)


def _layer(x, c, positions, w_ada, b_ada, g_mix_pre, g_mix_post, g_ffn_pre, g_ffn_post,
           w_in, conv_w, conv_b, filt_w1, filt_b1, filt_w2, filt_b2, filt_w3, filt_freq,
           hyena_skip, w_branch_gate, b_branch_gate, w_hy_o, w_at_o, w_out,
           w_group, b_group, w_expert, b_expert, w1_exp, w3_exp, w2_exp):
    bsz, seq, d = x.shape
    t = bsz * seq
    width = hyena_skip.shape[1]
    row = lambda v: v.reshape(1, -1)

    c_pad = jnp.pad(c, ((0, 8 - bsz), (0, 0)))
    mod = _adaln(c_pad, w_ada, row(b_ada))[:bsz].reshape(bsz, 6, d)

    x2d = x.reshape(t, d)
    n_gate = w_branch_gate.shape[1]
    w_cat = jnp.concatenate([w_branch_gate, w_in], axis=1).astype(BF16)
    b_cat = jnp.concatenate([b_branch_gate, jnp.zeros((w_in.shape[1],), F32)]).reshape(1, -1)
    pg = _in_proj(x2d, mod, row(g_mix_pre), w_cat, b_cat, n_gate, seq)
    pg3 = pg.reshape(bsz, seq, -1)

    hidden = filt_w2.shape[0]
    bands = np.zeros((1, LANES), np.float32)
    band_vals = np.linspace(1e-4, FILTER_BANDS - 1, FILTER_BANDS, dtype=np.float32)
    bands[0, 1:1 + FILTER_BANDS] = band_vals
    bands[0, 1 + FILTER_BANDS:1 + 2 * FILTER_BANDS] = band_vals
    w1p = jnp.pad(filt_w1, ((0, LANES - filt_w1.shape[0]), (0, 0)))
    max_decay = math.log(DECAY_TARGET) / FAST_DECAY_PCT
    min_decay = math.log(DECAY_TARGET) / SLOW_DECAY_PCT
    deltas = jnp.abs(jnp.linspace(min_decay, max_decay, width, dtype=F32)).reshape(1, -1)
    uw = _filters(seq, width, jnp.asarray(bands), w1p, row(filt_b1), filt_w2, row(filt_b2),
                  row(filt_freq), deltas, filt_w3)
    s1, s1i, s2, s2i = _fft_tables(FFT_N1, 2 * seq // FFT_N1)
    kspec = _spectra(uw, s1, s2)
    y_hy = _hyena(pg3, n_gate, conv_w, row(conv_b), hyena_skip, kspec, s1, s1i, s2, s2i, width)

    half = ROT_DIM // 2
    inv_freq = np.power(ROPE_THETA, -2.0 * np.arange(half, dtype=np.float32) / ROT_DIM).astype(np.float32)
    freq_row = np.zeros((1, LANES), np.float32)
    freq_row[0, :half] = inv_freq
    freq_row[0, half:ROT_DIM] = inv_freq
    sign_row = np.zeros((1, LANES), np.float32)
    sign_row[0, :half] = -1.0
    sign_row[0, half:ROT_DIM] = 1.0
    y_at = _attention(pg3, positions.reshape(bsz, seq, 1), jnp.asarray(freq_row), jnp.asarray(sign_row),
                      n_gate + 3 * width)

    w_r = jnp.concatenate([w_group, jnp.transpose(w_expert, (1, 0, 2)).reshape(d, N_EXPERTS)], axis=1)
    w_r = jnp.pad(w_r, ((0, 0), (0, LANES - w_r.shape[1])))
    b_r = jnp.pad(jnp.concatenate([b_group, b_expert.reshape(-1)]), (0, LANES - N_EXPERT_GROUPS - N_EXPERTS))
    x1, h2, logits = _mix_out(y_hy.reshape(t, width), y_at.reshape(t, -1), pg, x2d, mod,
                              row(g_mix_post), row(g_ffn_pre), w_hy_o.astype(BF16), w_at_o.astype(BF16),
                              w_out.astype(BF16), w_r, b_r.reshape(1, -1), seq)

    eid, gate = _route(logits)
    flat_e = eid.reshape(t * TOP_K, 1)
    rank, counts = _rank(flat_e)
    counts = counts[0, :N_EXPERTS]
    rb = EXPERT_ROW_BLOCK
    padded = (counts + rb - 1) // rb * rb
    pends = jnp.cumsum(padded)
    pstarts = pends - padded
    pstarts_row = jnp.pad(pstarts.astype(F32), (0, LANES - N_EXPERTS)).reshape(1, LANES)
    dest = _dest(flat_e, rank, pstarts_row)[:, 0]
    n_blocks = t * TOP_K // rb + N_EXPERTS
    block_start = jnp.arange(n_blocks, dtype=jnp.int32) * rb
    block_e = jnp.minimum(jnp.searchsorted(pends, block_start, side='right'), N_EXPERTS - 1).astype(jnp.int32)
    tok = jnp.arange(t * TOP_K, dtype=jnp.int32) // TOP_K
    src_tok = jnp.zeros((n_blocks * rb,), jnp.int32).at[dest].set(tok)
    nused = (pends[-1:] // rb).astype(jnp.int32)

    ybuf = _experts(block_e, src_tok, nused, h2, w1_exp, w3_exp, w2_exp)
    out = _combine(dest, ybuf, gate, x1, mod, row(g_ffn_post), seq)
    return out.reshape(bsz, seq, d)


def kernel(x, c, positions, w_ada, b_ada, g_mix_pre, g_mix_post, g_ffn_pre, g_ffn_post, w_in, conv_w, conv_b, filt_w1, filt_b1, filt_w2, filt_b2, filt_w3, filt_freq, hyena_skip, w_branch_gate, b_branch_gate, w_hy_o, w_at_o, w_out, w_group, b_group, w_expert, b_expert, w1_exp, w3_exp, w2_exp):
    depth = w_ada.shape[0]
    for l in range(depth):
        x = _layer(x, c, positions, w_ada[l], b_ada[l], g_mix_pre[l], g_mix_post[l], g_ffn_pre[l],
                   g_ffn_post[l], w_in[l], conv_w[l], conv_b[l], filt_w1[l], filt_b1[l], filt_w2[l],
                   filt_b2[l], filt_w3[l], filt_freq[l], hyena_skip[l], w_branch_gate[l],
                   b_branch_gate[l], w_hy_o[l], w_at_o[l], w_out[l], w_group[l], b_group[l],
                   w_expert[l], b_expert[l], w1_exp[l], w3_exp[l], w2_exp[l])
    return x
```

```python
import functools
import math

import numpy as np
import jax
import jax.numpy as jnp
from jax import lax
from jax.experimental import pallas as pl
from jax.experimental.pallas import tpu as pltpu

F32 = jnp.float32
BF16 = jnp.bfloat16

LANES = 128
MIB = 1024 * 1024

RMS_EPS = 1e-6
NEG_INF = -1e30

HEAD_DIM = 128
ROT_DIM = HEAD_DIM // 4
ROPE_THETA = 500000.0
ATTN_GROUPS = ((128, 1), (512, 4), (2048, 16))
HEADS_PER_GROUP = 4
N_ATTN_HEADS = HEADS_PER_GROUP * len(ATTN_GROUPS)

FILTER_BANDS = 16
DECAY_TARGET = 1e-2
FAST_DECAY_PCT = 0.3
SLOW_DECAY_PCT = 1.5

N_EXPERT_GROUPS = 8
EXPERTS_PER_GROUP = 8
N_EXPERTS = N_EXPERT_GROUPS * EXPERTS_PER_GROUP
TOP_K = 2
EXPERT_ROW_BLOCK = 128

FFT_N1 = 128
PITCH = FFT_N1 + 8
FFT_UNROLL = 16
FFT_MID_UNROLL = 8


def _cparams(sem, vmem_mib):
    return pltpu.CompilerParams(dimension_semantics=sem, vmem_limit_bytes=vmem_mib * MIB)


def _rms(x, g):
    return x * lax.rsqrt(jnp.mean(x * x, axis=-1, keepdims=True) + RMS_EPS) * g


def _adaln_body(c_ref, w_ref, b_ref, o_ref):
    c = c_ref[...]
    cond = c * jax.nn.sigmoid(c)
    o_ref[...] = jnp.dot(cond.astype(BF16), w_ref[...].astype(BF16),
                         preferred_element_type=F32) + b_ref[...]


def _adaln(c_pad, w_ada, b_ada):
    rows, d = c_pad.shape
    n = w_ada.shape[1]
    tn = 1024
    return pl.pallas_call(
        _adaln_body,
        grid=(n // tn,),
        in_specs=[pl.BlockSpec((rows, d), lambda j: (0, 0)),
                  pl.BlockSpec((d, tn), lambda j: (0, j)),
                  pl.BlockSpec((1, tn), lambda j: (0, j))],
        out_specs=pl.BlockSpec((rows, tn), lambda j: (0, j)),
        out_shape=jax.ShapeDtypeStruct((rows, n), F32),
        compiler_params=_cparams(("arbitrary",), 40),
        name="adaln",
    )(c_pad, w_ada, b_ada)


IN_PROJ_TM = 1024
IN_PROJ_TN = 1024


def _in_proj_body(n_gate, x_ref, mod_ref, g_ref, w_ref, b_ref, o_ref, h_scr):
    j = pl.program_id(1)

    @pl.when(j == 0)
    def _():
        x = x_ref[...]
        h = _rms(x, g_ref[...]) * (1.0 + mod_ref[0, 1:2, :]) + mod_ref[0, 0:1, :]
        h_scr[...] = h.astype(BF16)

    acc = jnp.dot(h_scr[...], w_ref[...], preferred_element_type=F32) + b_ref[...]

    @pl.when(j < n_gate)
    def _():
        o_ref[...] = jax.nn.sigmoid(acc).astype(o_ref.dtype)

    @pl.when(j >= n_gate)
    def _():
        o_ref[...] = acc.astype(o_ref.dtype)


def _in_proj(x2d, mod, g_pre, w_cat, b_cat, n_gate_cols, seq):
    t, d = x2d.shape
    n = w_cat.shape[1]
    tm, tn = IN_PROJ_TM, IN_PROJ_TN
    per_batch = seq // tm
    return pl.pallas_call(
        functools.partial(_in_proj_body, n_gate_cols // tn),
        grid=(t // tm, n // tn),
        in_specs=[pl.BlockSpec((tm, d), lambda i, j: (i, 0)),
                  pl.BlockSpec((1, 6, d), lambda i, j: (i // per_batch, 0, 0)),
                  pl.BlockSpec((1, d), lambda i, j: (0, 0)),
                  pl.BlockSpec((d, tn), lambda i, j: (0, j)),
                  pl.BlockSpec((1, tn), lambda i, j: (0, j))],
        out_specs=pl.BlockSpec((tm, tn), lambda i, j: (i, j)),
        out_shape=jax.ShapeDtypeStruct((t, n), BF16),
        scratch_shapes=[pltpu.VMEM((tm, d), BF16)],
        compiler_params=_cparams(("arbitrary", "arbitrary"), 56),
        name="in_proj",
    )(x2d, mod, g_pre, w_cat, b_cat)


def _filters_body(seq, band_ref, w1_ref, b1_ref, w2_ref, b2_ref, fr_ref, dl_ref,
                  w3a_ref, w3b_ref, o_ref, hid_scr):
    i = pl.program_id(0)
    j = pl.program_id(1)
    tl = hid_scr.shape[0]
    row = (lax.broadcasted_iota(jnp.int32, (tl, 1), 0) + i * tl).astype(F32)

    @pl.when((j == 0) & (pl.program_id(2) == 0))
    def _():
        lane = lax.broadcasted_iota(jnp.int32, (tl, LANES), 1)
        t = row / (seq - 1.0)
        ang = band_ref[...] * (2.0 * math.pi * row / seq)
        feats = jnp.where(lane == 0, t,
                          jnp.where(lane <= FILTER_BANDS, jnp.cos(ang),
                                    jnp.where(lane <= 2 * FILTER_BANDS, -jnp.sin(ang), 0.0)))
        hi = lax.Precision.HIGHEST
        fr = fr_ref[...]
        hid = jnp.sin(fr * (jnp.dot(feats, w1_ref[...], precision=hi, preferred_element_type=F32)
                            + b1_ref[...]))
        hid = jnp.sin(fr * (jnp.dot(hid, w2_ref[...], precision=hi, preferred_element_type=F32)
                            + b2_ref[...]))
        hid_scr[...] = hid

    hi = lax.Precision.HIGHEST
    hid = hid_scr[...]
    decay = jnp.exp(-(row / (seq - 1.0)) * dl_ref[...])
    hf = jnp.dot(hid, w3a_ref[...], precision=hi, preferred_element_type=F32) * decay
    hb = jnp.dot(hid, w3b_ref[...], precision=hi, preferred_element_type=F32) * decay
    hb = jnp.where(row == 0.0, 0.0, hb)
    o_ref[0, 0] = hf + hb
    o_ref[0, 1] = hf - hb


def _filters(seq, width, bands, w1p, b1, w2, b2, freq, deltas, w3):
    tl, tc = 512, 256
    nct = width // tc
    hidden = w2.shape[0]
    const = lambda i, j, o: (0, 0)
    return pl.pallas_call(
        functools.partial(_filters_body, float(seq)),
        grid=(seq // tl, nct, 2),
        in_specs=[pl.BlockSpec((1, LANES), const),
                  pl.BlockSpec((LANES, hidden), const),
                  pl.BlockSpec((1, hidden), const),
                  pl.BlockSpec((hidden, hidden), const),
                  pl.BlockSpec((1, hidden), const),
                  pl.BlockSpec((1, hidden), const),
                  pl.BlockSpec((1, tc), lambda i, j, o: (0, j)),
                  pl.BlockSpec((hidden, tc), lambda i, j, o: (0, (2 * o) * nct + j)),
                  pl.BlockSpec((hidden, tc), lambda i, j, o: (0, (2 * o + 1) * nct + j))],
        out_specs=pl.BlockSpec((1, 2, tl, tc), lambda i, j, o: (o, 0, i, j)),
        out_shape=jax.ShapeDtypeStruct((2, 2, seq, width), F32),
        scratch_shapes=[pltpu.VMEM((tl, hidden), F32)],
        compiler_params=_cparams(("arbitrary", "arbitrary", "arbitrary"), 32),
        name="filters",
    )(bands, w1p, b1, w2, b2, freq, deltas, w3, w3)


def _fft_tables(n1, n2):
    n = n1 * n2
    q = np.arange(n2)[:, None]
    b = np.arange(n2 // 2)[None, :]
    a = np.arange(n1)[:, None, None]
    ang = -2.0 * np.pi * (a * q[None] / n + (q * b)[None] / n2)
    stage1 = np.concatenate([np.cos(ang), np.sin(ang)], axis=1)
    stage1_inv = np.transpose(stage1, (0, 2, 1)) / n
    p = np.arange(n1)
    ang2 = -2.0 * np.pi * np.outer(p, p) / n1
    fre, fim = np.cos(ang2), np.sin(ang2)
    stage2 = np.block([[fre, -fim], [fim, fre]])
    stage2_inv = np.block([[fre, fim], [-fim, fre]])
    as_bf16 = lambda m: jnp.asarray(m, dtype=F32).astype(BF16)
    return as_bf16(stage1), as_bf16(stage1_inv), as_bf16(stage2), as_bf16(stage2_inv)


def _halves(ref, rows):
    return jnp.concatenate([ref[0, rows, :], ref[1, rows, :]], axis=1)


def _fft_stage1(z_ref, s1_ref, gre_ref, gim_ref):
    n1, two_n2, n2h = s1_ref.shape
    n2 = two_n2 // 2

    def step(a, carry):
        zrows = _halves(z_ref, pl.ds(a, n2h, stride=PITCH))
        g = jnp.dot(s1_ref[a], zrows.astype(BF16), preferred_element_type=F32)
        for h in range(2):
            cols = slice(h * LANES, (h + 1) * LANES)
            gre_ref[h, pl.ds(a, n2, stride=PITCH), :] = g[:n2, cols]
            gim_ref[h, pl.ds(a, n2, stride=PITCH), :] = g[n2:, cols]
        return carry

    lax.fori_loop(0, n1, step, 0, unroll=FFT_UNROLL)


def _fft_stage1_inv(gre_ref, gim_ref, s1i_ref, y_ref):
    n1, n2h, two_n2 = s1i_ref.shape
    n2 = two_n2 // 2

    def step(a, carry):
        rows = pl.ds(a, n2, stride=PITCH)
        hs = jnp.concatenate([_halves(gre_ref, rows), _halves(gim_ref, rows)], axis=0)
        y = jnp.dot(s1i_ref[a], hs.astype(BF16), preferred_element_type=F32)
        for h in range(2):
            y_ref[h, pl.ds(a, n2h, stride=PITCH), :] = y[:, h * LANES:(h + 1) * LANES]
        return carry

    lax.fori_loop(0, n1, step, 0, unroll=FFT_UNROLL)


def _stage2_block(gre_ref, gim_ref, s2_ref, q):
    n1 = s2_ref.shape[0] // 2
    rows = pl.ds(pl.multiple_of(q * PITCH, 8), n1)
    gs = jnp.concatenate([_halves(gre_ref, rows), _halves(gim_ref, rows)], axis=0)
    x = jnp.dot(s2_ref[...], gs.astype(BF16), preferred_element_type=F32)
    return x[:n1], x[n1:]


def _spectra_body(uw_ref, s1_ref, s2_ref, k_ref, z_scr, gre_scr, gim_scr):
    n1 = s2_ref.shape[0] // 2
    n2 = s1_ref.shape[1] // 2
    for h in range(2):
        for b in range(n2 // 2):
            z_scr[h, pl.ds(b * PITCH, n1), :] = uw_ref[0, h, pl.ds(b * n1, n1), :]
    _fft_stage1(z_scr, s1_ref, gre_scr, gim_scr)

    def step(q, carry):
        xre, xim = _stage2_block(gre_scr, gim_scr, s2_ref, q)
        rows = pl.ds(pl.multiple_of(q * n1, 8), n1)
        k_ref[0, 0, rows, :] = xre[:, :LANES]
        k_ref[0, 1, rows, :] = xim[:, LANES:]
        return carry

    lax.fori_loop(0, n2, step, 0, unroll=FFT_MID_UNROLL)


def _spectra(uw, s1, s2):
    _, _, seq, width = uw.shape
    n1 = FFT_N1
    n2 = 2 * seq // n1
    const3 = lambda j, o: (0, 0, 0)
    return pl.pallas_call(
        _spectra_body,
        grid=(width // LANES, 2),
        in_specs=[pl.BlockSpec((1, 2, seq, LANES), lambda j, o: (o, 0, 0, j)),
                  pl.BlockSpec(s1.shape, const3),
                  pl.BlockSpec(s2.shape, lambda j, o: (0, 0))],
        out_specs=pl.BlockSpec((1, 2, 2 * seq, LANES), lambda j, o: (o, 0, 0, j)),
        out_shape=jax.ShapeDtypeStruct((2, 2, 2 * seq, width), F32),
        scratch_shapes=[pltpu.VMEM((2, (n2 // 2) * PITCH, LANES), F32),
                        pltpu.VMEM((2, n2 * PITCH, LANES), F32),
                        pltpu.VMEM((2, n2 * PITCH, LANES), F32)],
        compiler_params=_cparams(("arbitrary", "arbitrary"), 56),
        name="spectra",
    )(uw, s1, s2)


def _short_conv_block(p_ref, b, i, nblk, w_ref, bias_ref):
    n1 = FFT_N1
    start = pl.multiple_of(i * n1, n1)
    cur = p_ref[b, pl.ds(start, n1), :].astype(F32)
    before = p_ref[b, pl.ds(pl.multiple_of(jnp.maximum(start - 16, 0), 16), 16), :].astype(F32)
    after = p_ref[b, pl.ds(pl.multiple_of(jnp.minimum(start + n1, (nblk - 1) * n1), 16), 16), :].astype(F32)
    last_prev = before[15:16] * jnp.where(i > 0, 1.0, 0.0).astype(F32)
    first_next = after[0:1] * jnp.where(i < nblk - 1, 1.0, 0.0).astype(F32)
    row = lax.broadcasted_iota(jnp.int32, (n1, 1), 0)
    prev = jnp.where(row == 0, last_prev, pltpu.roll(cur, 1, 0))
    nxt = jnp.where(row == n1 - 1, first_next, pltpu.roll(cur, n1 - 1, 0))
    return prev * w_ref[0:1, :] + cur * w_ref[1:2, :] + nxt * w_ref[2:3, :] + bias_ref[...]


def _hyena_body(pv_ref, px1_ref, px2_ref, cwv_ref, cw1_ref, cw2_ref, cbv_ref, cb1_ref, cb2_ref,
                skip_ref, k_ref, s1_ref, s1i_ref, s2_ref, s2i_ref, o_ref,
                z_scr, y_scr, gre_scr, gim_scr):
    c = pl.program_id(1)
    g = pl.program_id(2)
    ngroups = pl.num_programs(2)
    n1 = FFT_N1
    n2 = s1_ref.shape[1] // 2
    nblk = n2 // 2
    per_group = n2 // ngroups

    @pl.when(g == 0)
    def _():
        @pl.when(c == 0)
        def _():
            def fill(i, carry):
                for b in range(2):
                    z_scr[b, pl.ds(pl.multiple_of(i * PITCH, 8), n1), :] = _short_conv_block(
                        pv_ref, b, i, nblk, cwv_ref, cbv_ref)
                return carry
            lax.fori_loop(0, nblk, fill, 0)

        _fft_stage1(z_scr, s1_ref, gre_scr, gim_scr)

    def mid(ql, carry):
        q = g * per_group + ql
        xre, xim = _stage2_block(gre_scr, gim_scr, s2_ref, q)
        krows = pl.ds(pl.multiple_of(ql * n1, 8), n1)
        kre = k_ref[0, 0, krows, :]
        kim = k_ref[0, 1, krows, :]
        kre = jnp.concatenate([kre, kre], axis=1)
        kim = jnp.concatenate([kim, kim], axis=1)
        ys = jnp.concatenate([xre * kre - xim * kim, xre * kim + xim * kre], axis=0)
        hh = jnp.dot(s2i_ref[...], ys.astype(BF16), preferred_element_type=F32)
        rows = pl.ds(pl.multiple_of(q * PITCH, 8), n1)
        for b in range(2):
            cols = slice(b * LANES, (b + 1) * LANES)
            gre_scr[b, rows, :] = hh[:n1, cols]
            gim_scr[b, rows, :] = hh[n1:, cols]
        return carry

    lax.fori_loop(0, per_group, mid, 0, unroll=FFT_MID_UNROLL)

    @pl.when(g == ngroups - 1)
    def _():
        _fft_stage1_inv(gre_scr, gim_scr, s1i_ref, y_scr)

        def post(px_ref, cw_ref, cb_ref, order, store):
            def blk(i, carry):
                rows = pl.ds(pl.multiple_of(i * PITCH, 8), n1)
                for b in range(2):
                    zb = z_scr[b, rows, :]
                    gate = _short_conv_block(px_ref, b, i, nblk, cw_ref, cb_ref)
                    store(b, i, rows, gate * (y_scr[b, rows, :] + zb * skip_ref[order:order + 1, :]))
                return carry
            lax.fori_loop(0, nblk, blk, 0)

        @pl.when(c == 0)
        def _():
            def store(b, i, rows, val):
                z_scr[b, rows, :] = val
            post(px1_ref, cw1_ref, cb1_ref, 0, store)

        @pl.when(c == 1)
        def _():
            def store(b, i, rows, val):
                o_ref[b, pl.ds(pl.multiple_of(i * n1, n1), n1), :] = val.astype(o_ref.dtype)
            post(px2_ref, cw2_ref, cb2_ref, 1, store)


def _hyena(pg3, proj_col0, conv_w, conv_b, skip, kspec, s1, s1i, s2, s2i, width):
    bsz, seq, _ = pg3.shape
    assert bsz == 2
    n1 = FFT_N1
    n2 = 2 * seq // n1
    nct = width // LANES
    ngroups = 4
    krows = (n2 // ngroups) * n1
    col = lambda off: (lambda j, c, g: (0, 0, proj_col0 // LANES + off * nct + j))
    cw = lambda off: (lambda j, c, g: (0, off * nct + j))
    const3 = lambda j, c, g: (0, 0, 0)
    const2 = lambda j, c, g: (0, 0)
    return pl.pallas_call(
        _hyena_body,
        grid=(nct, 2, ngroups),
        in_specs=[pl.BlockSpec((2, seq, LANES), col(0)),
                  pl.BlockSpec((2, seq, LANES), col(1)),
                  pl.BlockSpec((2, seq, LANES), col(2)),
                  pl.BlockSpec((3, LANES), cw(0)),
                  pl.BlockSpec((3, LANES), cw(1)),
                  pl.BlockSpec((3, LANES), cw(2)),
                  pl.BlockSpec((1, LANES), cw(0)),
                  pl.BlockSpec((1, LANES), cw(1)),
                  pl.BlockSpec((1, LANES), cw(2)),
                  pl.BlockSpec((2, LANES), lambda j, c, g: (0, j)),
                  pl.BlockSpec((1, 2, krows, LANES), lambda j, c, g: (c, 0, g, j)),
                  pl.BlockSpec(s1.shape, const3),
                  pl.BlockSpec(s1i.shape, const3),
                  pl.BlockSpec(s2.shape, const2),
                  pl.BlockSpec(s2i.shape, const2)],
        out_specs=pl.BlockSpec((2, seq, LANES), lambda j, c, g: (0, 0, j)),
        out_shape=jax.ShapeDtypeStruct((2, seq, width), BF16),
        scratch_shapes=[pltpu.VMEM((2, (n2 // 2) * PITCH, LANES), F32),
                        pltpu.VMEM((2, (n2 // 2) * PITCH, LANES), F32),
                        pltpu.VMEM((2, n2 * PITCH, LANES), F32),
                        pltpu.VMEM((2, n2 * PITCH, LANES), F32)],
        compiler_params=_cparams(("arbitrary", "arbitrary", "arbitrary"), 56),
        name="hyena",
    )(pg3, pg3, pg3, conv_w, conv_w, conv_w, conv_b, conv_b, conv_b, skip, kspec, s1, s1i, s2, s2i)


ATTN_QBLK = 128
ATTN_UNROLL = 4


def _attention_body(pos_ref, freq_ref, sign_ref, *refs):
    qkv_refs = refs[:9]
    o_ref = refs[9]
    cos_scr, sin_scr, q_scr, k_scr, v_scr, og_scr, lse_scr = refs[10:]
    seq = q_scr.shape[0]
    chunk = 512
    nchunks = seq // chunk

    @pl.when(pl.program_id(1) == 0)
    def _():
        def trig(i, carry):
            rows = pl.ds(pl.multiple_of(i * chunk, chunk), chunk)
            ang = pos_ref[0, rows, :].astype(F32) * freq_ref[...]
            cos_scr[rows, :] = jnp.cos(ang)
            sin_scr[rows, :] = jnp.sin(ang) * sign_ref[...]
            return carry
        lax.fori_loop(0, nchunks, trig, 0)

    def rotate(src_ref, dst_ref):
        def body(i, carry):
            rows = pl.ds(pl.multiple_of(i * chunk, chunk), chunk)
            t = src_ref[0, rows, :].astype(F32)
            lane = lax.broadcasted_iota(jnp.int32, t.shape, 1)
            partner = jnp.where(lane < ROT_DIM // 2,
                                pltpu.roll(t, LANES - ROT_DIM // 2, 1), pltpu.roll(t, ROT_DIM // 2, 1))
            dst_ref[rows, :] = t * cos_scr[rows, :] + partner * sin_scr[rows, :]
            return carry
        lax.fori_loop(0, nchunks, body, 0)

    def widen(src_ref, dst_ref):
        def body(i, carry):
            rows = pl.ds(pl.multiple_of(i * chunk, chunk), chunk)
            dst_ref[rows, :] = src_ref[0, rows, :].astype(F32)
            return carry
        lax.fori_loop(0, nchunks, body, 0)

    scale = HEAD_DIM ** -0.5
    for gi, (window, dil) in enumerate(ATTN_GROUPS):
        rotate(qkv_refs[3 * gi], q_scr)
        rotate(qkv_refs[3 * gi + 1], k_scr)
        widen(qkv_refs[3 * gi + 2], v_scr)
        n = seq // dil
        half = window // (2 * dil)
        tk = min(n, ATTN_QBLK + 2 * half)
        blocks_per_res = n // ATTN_QBLK

        def block(u, carry, dil=dil, n=n, half=half, tk=tk, blocks_per_res=blocks_per_res, gi=gi):
            r = u // blocks_per_res
            m = u % blocks_per_res
            q0 = m * ATTN_QBLK
            k0 = jnp.clip(q0 - half, 0, n - tk)
            qrows = pl.ds(r + dil * q0, ATTN_QBLK, stride=dil)
            krows = pl.ds(r + dil * k0, tk, stride=dil)
            qb = q_scr[qrows, :].astype(BF16)
            kb = k_scr[krows, :].astype(BF16)
            vb = v_scr[krows, :].astype(BF16)
            s = lax.dot_general(qb, kb, (((1,), (1,)), ((), ())), preferred_element_type=F32) * scale
            qi = q0 + lax.broadcasted_iota(jnp.int32, (ATTN_QBLK, tk), 0)
            kj = k0 + lax.broadcasted_iota(jnp.int32, (ATTN_QBLK, tk), 1)
            s = jnp.where(jnp.abs(qi - kj) <= half, s, NEG_INF)
            mx = jnp.max(s, axis=-1, keepdims=True)
            p = jnp.exp(s - mx)
            l = jnp.sum(p, axis=-1, keepdims=True)
            o = jnp.dot(p.astype(BF16), vb, preferred_element_type=F32) / l
            og_scr[gi, qrows, :] = o
            lse_scr[gi, qrows, :] = jnp.broadcast_to(mx + jnp.log(l), (ATTN_QBLK, LANES))
            return carry

        lax.fori_loop(0, seq // ATTN_QBLK, block, 0, unroll=ATTN_UNROLL)

    def merge(i, carry):
        rows = pl.ds(pl.multiple_of(i * chunk, chunk), chunk)
        lses = [lse_scr[gi, rows, :] for gi in range(len(ATTN_GROUPS))]
        mx = functools.reduce(jnp.maximum, lses)
        ws = [jnp.exp(v - mx) for v in lses]
        den = functools.reduce(lambda a, b: a + b, ws)
        num = functools.reduce(lambda a, b: a + b,
                               [w * og_scr[gi, rows, :] for gi, w in enumerate(ws)])
        o_ref[0, rows, :] = (num / den).astype(o_ref.dtype)
        return carry

    lax.fori_loop(0, nchunks, merge, 0)


def _attention(pg3, pos3, freq_row, sign_row, qkv_col0):
    bsz, seq, _ = pg3.shape
    ng = len(ATTN_GROUPS)
    in_specs = [pl.BlockSpec((1, seq, 1), lambda b, h: (b, 0, 0)),
                pl.BlockSpec((1, LANES), lambda b, h: (0, 0)),
                pl.BlockSpec((1, LANES), lambda b, h: (0, 0))]
    for gi in range(ng):
        for which in range(3):
            base = qkv_col0 // LANES + which * N_ATTN_HEADS + gi * HEADS_PER_GROUP
            in_specs.append(pl.BlockSpec((1, seq, LANES), lambda b, h, base=base: (b, 0, base + h)))
    return pl.pallas_call(
        _attention_body,
        grid=(bsz, HEADS_PER_GROUP),
        in_specs=in_specs,
        out_specs=pl.BlockSpec((1, seq, LANES), lambda b, h: (b, 0, h)),
        out_shape=jax.ShapeDtypeStruct((bsz, seq, HEADS_PER_GROUP * HEAD_DIM), BF16),
        scratch_shapes=[pltpu.VMEM((seq, LANES), F32), pltpu.VMEM((seq, LANES), F32),
                        pltpu.VMEM((seq, LANES), F32), pltpu.VMEM((seq, LANES), F32),
                        pltpu.VMEM((seq, LANES), F32),
                        pltpu.VMEM((ng, seq, LANES), F32), pltpu.VMEM((ng, seq, LANES), F32)],
        compiler_params=_cparams(("arbitrary", "arbitrary"), 56),
        name="attention",
    )(pos3, freq_row, sign_row, *([pg3] * (3 * ng)))


def _mix_out_body(yhy_ref, yat_ref, ghy_ref, gat_ref, x_ref, mod_ref, gpost_ref, gpre_ref,
                  whyo_ref, wato_ref, wout_ref, wr_ref, br_ref, x1_ref, h2_ref, lg_ref):
    a = jnp.dot(yhy_ref[...], whyo_ref[...], preferred_element_type=F32)
    b = jnp.dot(yat_ref[...], wato_ref[...], preferred_element_type=F32)
    merged = ghy_ref[...].astype(F32) * a + gat_ref[...].astype(F32) * b
    y = jnp.dot(merged.astype(BF16), wout_ref[...], preferred_element_type=F32)
    x1 = x_ref[...] + mod_ref[0, 2:3, :] * _rms(y, gpost_ref[...])
    x1_ref[...] = x1
    h2 = _rms(x1, gpre_ref[...]) * (1.0 + mod_ref[0, 4:5, :]) + mod_ref[0, 3:4, :]
    h2_ref[...] = h2
    lg_ref[...] = jnp.dot(h2, wr_ref[...], precision=lax.Precision.HIGHEST,
                          preferred_element_type=F32) + br_ref[...]


def _mix_out(y_hy, y_at, pg, x2d, mod, g_post, g_pre, w_hy_o, w_at_o, w_out, w_r, b_r, seq):
    t, d = x2d.shape
    tm = 256
    per_batch = seq // tm
    gblk = 0
    const = lambda i: (0, 0)
    return pl.pallas_call(
        _mix_out_body,
        grid=(t // tm,),
        in_specs=[pl.BlockSpec((tm, y_hy.shape[1]), lambda i: (i, 0)),
                  pl.BlockSpec((tm, y_at.shape[1]), lambda i: (i, 0)),
                  pl.BlockSpec((tm, d), lambda i: (i, gblk)),
                  pl.BlockSpec((tm, d), lambda i: (i, gblk + 1)),
                  pl.BlockSpec((tm, d), lambda i: (i, 0)),
                  pl.BlockSpec((1, 6, d), lambda i: (i // per_batch, 0, 0)),
                  pl.BlockSpec((1, d), const),
                  pl.BlockSpec((1, d), const),
                  pl.BlockSpec(w_hy_o.shape, const),
                  pl.BlockSpec(w_at_o.shape, const),
                  pl.BlockSpec(w_out.shape, const),
                  pl.BlockSpec(w_r.shape, const),
                  pl.BlockSpec((1, LANES), const)],
        out_specs=[pl.BlockSpec((tm, d), lambda i: (i, 0)),
                   pl.BlockSpec((tm, d), lambda i: (i, 0)),
                   pl.BlockSpec((tm, LANES), lambda i: (i, 0))],
        out_shape=[jax.ShapeDtypeStruct((t, d), F32),
                   jax.ShapeDtypeStruct((t, d), F32),
                   jax.ShapeDtypeStruct((t, LANES), F32)],
        compiler_params=_cparams(("arbitrary",), 56),
        name="mix_out",
    )(y_hy, y_at, pg, pg, x2d, mod, g_post, g_pre, w_hy_o, w_at_o, w_out, w_r, b_r)


def _route_body(lg_ref, eid_ref, gate_ref):
    lg = lg_ref[...]
    lane = lax.broadcasted_iota(jnp.int32, lg.shape, 1)
    big = jnp.int32(1 << 20)

    def first_argmax(vals, mask):
        v = jnp.where(mask, vals, -jnp.inf)
        mx = jnp.max(v, axis=-1, keepdims=True)
        idx = jnp.min(jnp.where(mask & (v == mx), lane, big), axis=-1, keepdims=True)
        return mx, idx

    gmask = lane < N_EXPERT_GROUPS
    gmax, gidx = first_argmax(lg, gmask)
    gval = 1.0 / jnp.sum(jnp.where(gmask, jnp.exp(lg - gmax), 0.0), axis=-1, keepdims=True)
    lo = N_EXPERT_GROUPS + gidx * EXPERTS_PER_GROUP
    emask = (lane >= lo) & (lane < lo + EXPERTS_PER_GROUP)
    v1, i1 = first_argmax(lg, emask)
    v2, i2 = first_argmax(lg, emask & (lane != i1))
    e2 = jnp.exp(v2 - v1)
    p1 = 1.0 / (1.0 + e2)
    p2 = e2 / (1.0 + e2)
    eid = jnp.where(lane == 0, i1, i2) - N_EXPERT_GROUPS
    gate = gval * jnp.where(lane == 0, p1, p2)
    eid_ref[...] = eid[:, :TOP_K]
    gate_ref[...] = gate[:, :TOP_K]


def _route(logits):
    t = logits.shape[0]
    tm = 512
    return pl.pallas_call(
        _route_body,
        grid=(t // tm,),
        in_specs=[pl.BlockSpec((tm, LANES), lambda i: (i, 0))],
        out_specs=[pl.BlockSpec((tm, TOP_K), lambda i: (i, 0)),
                   pl.BlockSpec((tm, TOP_K), lambda i: (i, 0))],
        out_shape=[jax.ShapeDtypeStruct((t, TOP_K), jnp.int32),
                   jax.ShapeDtypeStruct((t, TOP_K), F32)],
        compiler_params=_cparams(("arbitrary",), 32),
        name="route",
    )(logits)


def _rank_body(e_ref, rank_ref, cnt_ref, carry_scr):
    i = pl.program_id(0)
    r = e_ref.shape[0]

    @pl.when(i == 0)
    def _():
        carry_scr[...] = jnp.zeros_like(carry_scr)

    lane = lax.broadcasted_iota(jnp.int32, (r, LANES), 1)
    onehot = (lane == e_ref[...]).astype(F32)
    tri = (lax.broadcasted_iota(jnp.int32, (r, r), 1)
           < lax.broadcasted_iota(jnp.int32, (r, r), 0)).astype(BF16)
    before = jnp.dot(tri, onehot.astype(BF16), preferred_element_type=F32) + carry_scr[0:1, :]
    rank_ref[...] = jnp.sum(onehot * before, axis=-1, keepdims=True).astype(jnp.int32)
    total = carry_scr[0:1, :] + jnp.sum(onehot, axis=0, keepdims=True)
    carry_scr[...] = jnp.broadcast_to(total, carry_scr.shape)
    cnt_ref[...] = jnp.broadcast_to(total, cnt_ref.shape).astype(jnp.int32)


def _rank(flat_e):
    a = flat_e.shape[0]
    r = 512
    return pl.pallas_call(
        _rank_body,
        grid=(a // r,),
        in_specs=[pl.BlockSpec((r, 1), lambda i: (i, 0))],
        out_specs=[pl.BlockSpec((r, 1), lambda i: (i, 0)),
                   pl.BlockSpec((8, LANES), lambda i: (0, 0))],
        out_shape=[jax.ShapeDtypeStruct((a, 1), jnp.int32),
                   jax.ShapeDtypeStruct((8, LANES), jnp.int32)],
        scratch_shapes=[pltpu.VMEM((8, LANES), F32)],
        compiler_params=_cparams(("arbitrary",), 32),
        name="rank",
    )(flat_e)


def _dest_body(e_ref, rank_ref, ps_ref, o_ref):
    lane = lax.broadcasted_iota(jnp.int32, (e_ref.shape[0], LANES), 1)
    first = jnp.sum(jnp.where(lane == e_ref[...], ps_ref[...], 0.0), axis=-1, keepdims=True)
    o_ref[...] = first.astype(jnp.int32) + rank_ref[...]


def _dest(flat_e, rank, pstarts_row):
    a = flat_e.shape[0]
    r = 2048
    return pl.pallas_call(
        _dest_body,
        grid=(a // r,),
        in_specs=[pl.BlockSpec((r, 1), lambda i: (i, 0)),
                  pl.BlockSpec((r, 1), lambda i: (i, 0)),
                  pl.BlockSpec((1, LANES), lambda i: (0, 0))],
        out_specs=pl.BlockSpec((r, 1), lambda i: (i, 0)),
        out_shape=jax.ShapeDtypeStruct((a, 1), jnp.int32),
        compiler_params=_cparams(("arbitrary",), 32),
        name="dest",
    )(flat_e, rank, pstarts_row)


DMA_UNROLL = 8


def _experts_body(be_ref, src_ref, nused_ref, h_hbm, w1_ref, w3_ref, w2_ref, y_ref, x_scr, sem):
    i = pl.program_id(0)
    nrows = x_scr.shape[1]
    nused = nused_ref[0]

    def gather(blk, act):
        slot = blk % 2
        base = blk * nrows

        def body(r, carry):
            act(pltpu.make_async_copy(h_hbm.at[pl.ds(src_ref[base + r], 1)],
                                      x_scr.at[slot, pl.ds(r, 1)], sem.at[slot]))
            return carry

        lax.fori_loop(0, nrows, body, 0, unroll=DMA_UNROLL)

    @pl.when(i == 0)
    def _():
        gather(i, lambda cp: cp.start())

    @pl.when(i + 1 < nused)
    def _():
        gather(i + 1, lambda cp: cp.start())

    @pl.when(i < nused)
    def _():
        gather(i, lambda cp: cp.wait())
        xb = x_scr[i % 2].astype(BF16)
        a = jnp.dot(xb, w1_ref[0].astype(BF16), preferred_element_type=F32)
        b = jnp.dot(xb, w3_ref[0].astype(BF16), preferred_element_type=F32)
        act = (a * jax.nn.sigmoid(a) * b).astype(BF16)
        y_ref[...] = jnp.dot(act, w2_ref[0].astype(BF16), preferred_element_type=F32)

    @pl.when(i >= nused)
    def _():
        y_ref[...] = jnp.zeros_like(y_ref)


def _experts(block_e, src_tok, nused, h2, w1, w3, w2):
    nblocks = block_e.shape[0]
    d = h2.shape[1]
    ff = w1.shape[2]
    rb = EXPERT_ROW_BLOCK
    wmap = lambda i, be, src, nu: (be[i], 0, 0)
    grid_spec = pltpu.PrefetchScalarGridSpec(
        num_scalar_prefetch=3,
        grid=(nblocks,),
        in_specs=[pl.BlockSpec(memory_space=pl.ANY),
                  pl.BlockSpec((1, d, ff), wmap),
                  pl.BlockSpec((1, d, ff), wmap),
                  pl.BlockSpec((1, ff, d), wmap)],
        out_specs=pl.BlockSpec((rb, d), lambda i, be, src, nu: (i, 0)),
        scratch_shapes=[pltpu.VMEM((2, rb, d), F32), pltpu.SemaphoreType.DMA((2,))],
    )
    return pl.pallas_call(
        _experts_body,
        grid_spec=grid_spec,
        out_shape=jax.ShapeDtypeStruct((nblocks * rb, d), F32),
        compiler_params=_cparams(("arbitrary",), 56),
        name="experts",
    )(block_e, src_tok, nused, h2, w1, w3, w2)


def _combine_body(dest_ref, y_hbm, gate_ref, x1_ref, mod_ref, g_ref, o_ref, buf_scr, sem):
    i = pl.program_id(0)
    tm = x1_ref.shape[0]

    def gather(step, act):
        slot = step % 2
        base = step * tm * TOP_K

        def body(r, carry):
            for k in range(TOP_K):
                act(pltpu.make_async_copy(y_hbm.at[pl.ds(dest_ref[base + TOP_K * r + k], 1)],
                                          buf_scr.at[slot, k, pl.ds(r, 1)], sem.at[slot]))
            return carry

        lax.fori_loop(0, tm, body, 0, unroll=DMA_UNROLL // TOP_K)

    @pl.when(i == 0)
    def _():
        gather(i, lambda cp: cp.start())

    @pl.when(i + 1 < pl.num_programs(0))
    def _():
        gather(i + 1, lambda cp: cp.start())

    gather(i, lambda cp: cp.wait())
    slot = i % 2
    gate = gate_ref[...]
    y = buf_scr[slot, 0] * gate[:, 0:1] + buf_scr[slot, 1] * gate[:, 1:2]
    o_ref[...] = x1_ref[...] + mod_ref[0, 5:6, :] * _rms(y, g_ref[...])


def _combine(dest, ybuf, gate, x1, mod, g_post, seq):
    t, d = x1.shape
    tm = 128
    per_batch = seq // tm
    grid_spec = pltpu.PrefetchScalarGridSpec(
        num_scalar_prefetch=1,
        grid=(t // tm,),
        in_specs=[pl.BlockSpec(memory_space=pl.ANY),
                  pl.BlockSpec((tm, TOP_K), lambda i, dst: (i, 0)),
                  pl.BlockSpec((tm, d), lambda i, dst: (i, 0)),
                  pl.BlockSpec((1, 6, d), lambda i, dst: (i // per_batch, 0, 0)),
                  pl.BlockSpec((1, d), lambda i, dst: (0, 0))],
        out_specs=pl.BlockSpec((tm, d), lambda i, dst: (i, 0)),
        scratch_shapes=[pltpu.VMEM((2, TOP_K, tm, d), F32), pltpu.SemaphoreType.DMA((2,))],
    )
    return pl.pallas_call(
        _combine_body,
        grid_spec=grid_spec,
        out_shape=jax.ShapeDtypeStruct((t, d), F32),
        compiler_params=_cparams(("arbitrary",), 32),
        name="combine",
    )(dest, ybuf, gate, x1, mod, g_post)


def _layer(x, c, positions, w_ada, b_ada, g_mix_pre, g_mix_post, g_ffn_pre, g_ffn_post,
           w_in, conv_w, conv_b, filt_w1, filt_b1, filt_w2, filt_b2, filt_w3, filt_freq,
           hyena_skip, w_branch_gate, b_branch_gate, w_hy_o, w_at_o, w_out,
           w_group, b_group, w_expert, b_expert, w1_exp, w3_exp, w2_exp):
    bsz, seq, d = x.shape
    t = bsz * seq
    width = hyena_skip.shape[1]
    row = lambda v: v.reshape(1, -1)

    c_pad = jnp.pad(c, ((0, 8 - bsz), (0, 0)))
    mod = _adaln(c_pad, w_ada, row(b_ada))[:bsz].reshape(bsz, 6, d)

    x2d = x.reshape(t, d)
    n_gate = w_branch_gate.shape[1]
    n_cat = -(-(n_gate + w_in.shape[1]) // IN_PROJ_TN) * IN_PROJ_TN
    n_zero = n_cat - n_gate - w_in.shape[1]
    w_cat = jnp.concatenate([w_branch_gate.astype(BF16), w_in.astype(BF16),
                             jnp.zeros((d, n_zero), BF16)], axis=1)
    b_cat = jnp.concatenate([b_branch_gate, jnp.zeros((n_cat - n_gate,), F32)]).reshape(1, -1)
    pg = _in_proj(x2d, mod, row(g_mix_pre), w_cat, b_cat, n_gate, seq)
    pg3 = pg.reshape(bsz, seq, -1)

    hidden = filt_w2.shape[0]
    bands = np.zeros((1, LANES), np.float32)
    band_vals = np.linspace(1e-4, FILTER_BANDS - 1, FILTER_BANDS, dtype=np.float32)
    bands[0, 1:1 + FILTER_BANDS] = band_vals
    bands[0, 1 + FILTER_BANDS:1 + 2 * FILTER_BANDS] = band_vals
    w1p = jnp.pad(filt_w1, ((0, LANES - filt_w1.shape[0]), (0, 0)))
    max_decay = math.log(DECAY_TARGET) / FAST_DECAY_PCT
    min_decay = math.log(DECAY_TARGET) / SLOW_DECAY_PCT
    deltas = jnp.abs(jnp.linspace(min_decay, max_decay, width, dtype=F32)).reshape(1, -1)
    uw = _filters(seq, width, jnp.asarray(bands), w1p, row(filt_b1), filt_w2, row(filt_b2),
                  row(filt_freq), deltas, filt_w3)
    s1, s1i, s2, s2i = _fft_tables(FFT_N1, 2 * seq // FFT_N1)
    kspec = _spectra(uw, s1, s2)
    y_hy = _hyena(pg3, n_gate, conv_w, row(conv_b), hyena_skip, kspec, s1, s1i, s2, s2i, width)

    half = ROT_DIM // 2
    inv_freq = np.power(ROPE_THETA, -2.0 * np.arange(half, dtype=np.float32) / ROT_DIM).astype(np.float32)
    freq_row = np.zeros((1, LANES), np.float32)
    freq_row[0, :half] = inv_freq
    freq_row[0, half:ROT_DIM] = inv_freq
    sign_row = np.zeros((1, LANES), np.float32)
    sign_row[0, :half] = -1.0
    sign_row[0, half:ROT_DIM] = 1.0
    y_at = _attention(pg3, positions.reshape(bsz, seq, 1), jnp.asarray(freq_row), jnp.asarray(sign_row),
                      n_gate + 3 * width)

    w_r = jnp.concatenate([w_group, jnp.transpose(w_expert, (1, 0, 2)).reshape(d, N_EXPERTS)], axis=1)
    w_r = jnp.pad(w_r, ((0, 0), (0, LANES - w_r.shape[1])))
    b_r = jnp.pad(jnp.concatenate([b_group, b_expert.reshape(-1)]), (0, LANES - N_EXPERT_GROUPS - N_EXPERTS))
    x1, h2, logits = _mix_out(y_hy.reshape(t, width), y_at.reshape(t, -1), pg, x2d, mod,
                              row(g_mix_post), row(g_ffn_pre), w_hy_o.astype(BF16), w_at_o.astype(BF16),
                              w_out.astype(BF16), w_r, b_r.reshape(1, -1), seq)

    eid, gate = _route(logits)
    flat_e = eid.reshape(t * TOP_K, 1)
    rank, counts = _rank(flat_e)
    counts = counts[0, :N_EXPERTS]
    rb = EXPERT_ROW_BLOCK
    padded = (counts + rb - 1) // rb * rb
    pends = jnp.cumsum(padded)
    pstarts = pends - padded
    pstarts_row = jnp.pad(pstarts.astype(F32), (0, LANES - N_EXPERTS)).reshape(1, LANES)
    dest = _dest(flat_e, rank, pstarts_row)[:, 0]
    n_blocks = t * TOP_K // rb + N_EXPERTS
    block_start = jnp.arange(n_blocks, dtype=jnp.int32) * rb
    block_e = jnp.minimum(jnp.searchsorted(pends, block_start, side='right'), N_EXPERTS - 1).astype(jnp.int32)
    tok = jnp.arange(t * TOP_K, dtype=jnp.int32) // TOP_K
    src_tok = jnp.zeros((n_blocks * rb,), jnp.int32).at[dest].set(tok)
    nused = (pends[-1:] // rb).astype(jnp.int32)

    ybuf = _experts(block_e, src_tok, nused, h2, w1_exp, w3_exp, w2_exp)
    out = _combine(dest, ybuf, gate, x1, mod, row(g_ffn_post), seq)
    return out.reshape(bsz, seq, d)


def kernel(x, c, positions, w_ada, b_ada, g_mix_pre, g_mix_post, g_ffn_pre, g_ffn_post, w_in, conv_w, conv_b, filt_w1, filt_b1, filt_w2, filt_b2, filt_w3, filt_freq, hyena_skip, w_branch_gate, b_branch_gate, w_hy_o, w_at_o, w_out, w_group, b_group, w_expert, b_expert, w1_exp, w3_exp, w2_exp):
    depth = w_ada.shape[0]
    for l in range(depth):
        x = _layer(x, c, positions, w_ada[l], b_ada[l], g_mix_pre[l], g_mix_post[l], g_ffn_pre[l],
                   g_ffn_post[l], w_in[l], conv_w[l], conv_b[l], filt_w1[l], filt_b1[l], filt_w2[l],
                   filt_b2[l], filt_w3[l], filt_freq[l], hyena_skip[l], w_branch_gate[l],
                   b_branch_gate[l], w_hy_o[l], w_at_o[l], w_out[l], w_group[l], b_group[l],
                   w_expert[l], b_expert[l], w1_exp[l], w3_exp[l], w2_exp[l])
    return x
```

```python
import functools
import math

import numpy as np
import jax
import jax.numpy as jnp
from jax import lax
from jax.experimental import pallas as pl
from jax.experimental.pallas import tpu as pltpu

F32 = jnp.float32
BF16 = jnp.bfloat16

LANES = 128
MIB = 1024 * 1024

RMS_EPS = 1e-6
NEG_INF = -1e30

HEAD_DIM = 128
ROT_DIM = HEAD_DIM // 4
ROPE_THETA = 500000.0
ATTN_GROUPS = ((128, 1), (512, 4), (2048, 16))
HEADS_PER_GROUP = 4
N_ATTN_HEADS = HEADS_PER_GROUP * len(ATTN_GROUPS)

FILTER_BANDS = 16
DECAY_TARGET = 1e-2
FAST_DECAY_PCT = 0.3
SLOW_DECAY_PCT = 1.5

N_EXPERT_GROUPS = 8
EXPERTS_PER_GROUP = 8
N_EXPERTS = N_EXPERT_GROUPS * EXPERTS_PER_GROUP
TOP_K = 2
EXPERT_ROW_BLOCK = 128

FFT_N1 = 128
PITCH = FFT_N1 + 8
FFT_UNROLL = 16
FFT_MID_UNROLL = 8


def _cparams(sem, vmem_mib):
    return pltpu.CompilerParams(dimension_semantics=sem, vmem_limit_bytes=vmem_mib * MIB)


def _rms(x, g):
    return x * lax.rsqrt(jnp.mean(x * x, axis=-1, keepdims=True) + RMS_EPS) * g


def _adaln_body(c_ref, w_ref, b_ref, o_ref):
    c = c_ref[...]
    cond = c * jax.nn.sigmoid(c)
    o_ref[...] = jnp.dot(cond.astype(BF16), w_ref[...].astype(BF16),
                         preferred_element_type=F32) + b_ref[...]


def _adaln(c_pad, w_ada, b_ada):
    rows, d = c_pad.shape
    n = w_ada.shape[1]
    tn = 1024
    return pl.pallas_call(
        _adaln_body,
        grid=(n // tn,),
        in_specs=[pl.BlockSpec((rows, d), lambda j: (0, 0)),
                  pl.BlockSpec((d, tn), lambda j: (0, j)),
                  pl.BlockSpec((1, tn), lambda j: (0, j))],
        out_specs=pl.BlockSpec((rows, tn), lambda j: (0, j)),
        out_shape=jax.ShapeDtypeStruct((rows, n), F32),
        compiler_params=_cparams(("arbitrary",), 40),
        name="adaln",
    )(c_pad, w_ada, b_ada)


IN_PROJ_TM = 1024
IN_PROJ_TN = 1024


def _in_proj_body(n_gate, x_ref, mod_ref, g_ref, w_ref, b_ref, o_ref, h_scr):
    j = pl.program_id(1)

    @pl.when(j == 0)
    def _():
        x = x_ref[...]
        h = _rms(x, g_ref[...]) * (1.0 + mod_ref[0, 1:2, :]) + mod_ref[0, 0:1, :]
        h_scr[...] = h.astype(BF16)

    acc = jnp.dot(h_scr[...], w_ref[...], preferred_element_type=F32) + b_ref[...]

    @pl.when(j < n_gate)
    def _():
        o_ref[...] = jax.nn.sigmoid(acc).astype(o_ref.dtype)

    @pl.when(j >= n_gate)
    def _():
        o_ref[...] = acc.astype(o_ref.dtype)


def _in_proj(x2d, mod, g_pre, w_cat, b_cat, n_gate_cols, seq):
    t, d = x2d.shape
    n = w_cat.shape[1]
    tm, tn = IN_PROJ_TM, IN_PROJ_TN
    per_batch = seq // tm
    return pl.pallas_call(
        functools.partial(_in_proj_body, n_gate_cols // tn),
        grid=(t // tm, n // tn),
        in_specs=[pl.BlockSpec((tm, d), lambda i, j: (i, 0)),
                  pl.BlockSpec((1, 6, d), lambda i, j: (i // per_batch, 0, 0)),
                  pl.BlockSpec((1, d), lambda i, j: (0, 0)),
                  pl.BlockSpec((d, tn), lambda i, j: (0, j)),
                  pl.BlockSpec((1, tn), lambda i, j: (0, j))],
        out_specs=pl.BlockSpec((tm, tn), lambda i, j: (i, j)),
        out_shape=jax.ShapeDtypeStruct((t, n), BF16),
        scratch_shapes=[pltpu.VMEM((tm, d), BF16)],
        compiler_params=_cparams(("arbitrary", "arbitrary"), 56),
        name="in_proj",
    )(x2d, mod, g_pre, w_cat, b_cat)


def _filters_body(seq, band_ref, w1_ref, b1_ref, w2_ref, b2_ref, fr_ref, dl_ref,
                  w3a_ref, w3b_ref, o_ref, hid_scr):
    i = pl.program_id(0)
    j = pl.program_id(1)
    tl = hid_scr.shape[0]
    row = (lax.broadcasted_iota(jnp.int32, (tl, 1), 0) + i * tl).astype(F32)

    @pl.when((j == 0) & (pl.program_id(2) == 0))
    def _():
        lane = lax.broadcasted_iota(jnp.int32, (tl, LANES), 1)
        t = row / (seq - 1.0)
        ang = band_ref[...] * (2.0 * math.pi * row / seq)
        feats = jnp.where(lane == 0, t,
                          jnp.where(lane <= FILTER_BANDS, jnp.cos(ang),
                                    jnp.where(lane <= 2 * FILTER_BANDS, -jnp.sin(ang), 0.0)))
        hi = lax.Precision.HIGHEST
        fr = fr_ref[...]
        hid = jnp.sin(fr * (jnp.dot(feats, w1_ref[...], precision=hi, preferred_element_type=F32)
                            + b1_ref[...]))
        hid = jnp.sin(fr * (jnp.dot(hid, w2_ref[...], precision=hi, preferred_element_type=F32)
                            + b2_ref[...]))
        hid_scr[...] = hid

    hi = lax.Precision.HIGHEST
    hid = hid_scr[...]
    decay = jnp.exp(-(row / (seq - 1.0)) * dl_ref[...])
    hf = jnp.dot(hid, w3a_ref[...], precision=hi, preferred_element_type=F32) * decay
    hb = jnp.dot(hid, w3b_ref[...], precision=hi, preferred_element_type=F32) * decay
    hb = jnp.where(row == 0.0, 0.0, hb)
    o_ref[0, 0] = hf + hb
    o_ref[0, 1] = hf - hb


def _filters(seq, width, bands, w1p, b1, w2, b2, freq, deltas, w3):
    tl, tc = 512, 256
    nct = width // tc
    hidden = w2.shape[0]
    const = lambda i, j, o: (0, 0)
    return pl.pallas_call(
        functools.partial(_filters_body, float(seq)),
        grid=(seq // tl, nct, 2),
        in_specs=[pl.BlockSpec((1, LANES), const),
                  pl.BlockSpec((LANES, hidden), const),
                  pl.BlockSpec((1, hidden), const),
                  pl.BlockSpec((hidden, hidden), const),
                  pl.BlockSpec((1, hidden), const),
                  pl.BlockSpec((1, hidden), const),
                  pl.BlockSpec((1, tc), lambda i, j, o: (0, j)),
                  pl.BlockSpec((hidden, tc), lambda i, j, o: (0, (2 * o) * nct + j)),
                  pl.BlockSpec((hidden, tc), lambda i, j, o: (0, (2 * o + 1) * nct + j))],
        out_specs=pl.BlockSpec((1, 2, tl, tc), lambda i, j, o: (o, 0, i, j)),
        out_shape=jax.ShapeDtypeStruct((2, 2, seq, width), F32),
        scratch_shapes=[pltpu.VMEM((tl, hidden), F32)],
        compiler_params=_cparams(("arbitrary", "arbitrary", "arbitrary"), 32),
        name="filters",
    )(bands, w1p, b1, w2, b2, freq, deltas, w3, w3)


def _fft_tables(n1, n2):
    n = n1 * n2
    q = np.arange(n2)[:, None]
    b = np.arange(n2 // 2)[None, :]
    a = np.arange(n1)[:, None, None]
    ang = -2.0 * np.pi * (a * q[None] / n + (q * b)[None] / n2)
    stage1 = np.concatenate([np.cos(ang), np.sin(ang)], axis=1)
    stage1_inv = np.transpose(stage1, (0, 2, 1)) / n
    p = np.arange(n1)
    ang2 = -2.0 * np.pi * np.outer(p, p) / n1
    fre, fim = np.cos(ang2), np.sin(ang2)
    stage2 = np.block([[fre, -fim], [fim, fre]])
    stage2_inv = np.block([[fre, fim], [-fim, fre]])
    as_bf16 = lambda m: jnp.asarray(m, dtype=F32).astype(BF16)
    return as_bf16(stage1), as_bf16(stage1_inv), as_bf16(stage2), as_bf16(stage2_inv)


def _halves(ref, rows):
    return jnp.concatenate([ref[0, rows, :], ref[1, rows, :]], axis=1)


def _fft_stage1(z_ref, s1_ref, gre_ref, gim_ref):
    n1, two_n2, n2h = s1_ref.shape
    n2 = two_n2 // 2

    def step(a, carry):
        zrows = _halves(z_ref, pl.ds(a, n2h, stride=PITCH))
        g = jnp.dot(s1_ref[a], zrows.astype(BF16), preferred_element_type=F32)
        for h in range(2):
            cols = slice(h * LANES, (h + 1) * LANES)
            gre_ref[h, pl.ds(a, n2, stride=PITCH), :] = g[:n2, cols]
            gim_ref[h, pl.ds(a, n2, stride=PITCH), :] = g[n2:, cols]
        return carry

    lax.fori_loop(0, n1, step, 0, unroll=FFT_UNROLL)


def _fft_stage1_inv(gre_ref, gim_ref, s1i_ref, y_ref):
    n1, n2h, two_n2 = s1i_ref.shape
    n2 = two_n2 // 2

    def step(a, carry):
        rows = pl.ds(a, n2, stride=PITCH)
        hs = jnp.concatenate([_halves(gre_ref, rows), _halves(gim_ref, rows)], axis=0)
        y = jnp.dot(s1i_ref[a], hs.astype(BF16), preferred_element_type=F32)
        for h in range(2):
            y_ref[h, pl.ds(a, n2h, stride=PITCH), :] = y[:, h * LANES:(h + 1) * LANES]
        return carry

    lax.fori_loop(0, n1, step, 0, unroll=FFT_UNROLL)


def _stage2_block(gre_ref, gim_ref, s2_ref, q):
    n1 = s2_ref.shape[0] // 2
    rows = pl.ds(pl.multiple_of(q * PITCH, 8), n1)
    gs = jnp.concatenate([_halves(gre_ref, rows), _halves(gim_ref, rows)], axis=0)
    x = jnp.dot(s2_ref[...], gs.astype(BF16), preferred_element_type=F32)
    return x[:n1], x[n1:]


def _spectra_body(uw_ref, s1_ref, s2_ref, k_ref, z_scr, gre_scr, gim_scr):
    n1 = s2_ref.shape[0] // 2
    n2 = s1_ref.shape[1] // 2
    for h in range(2):
        for b in range(n2 // 2):
            z_scr[h, pl.ds(b * PITCH, n1), :] = uw_ref[0, h, pl.ds(b * n1, n1), :]
    _fft_stage1(z_scr, s1_ref, gre_scr, gim_scr)

    def step(q, carry):
        xre, xim = _stage2_block(gre_scr, gim_scr, s2_ref, q)
        rows = pl.ds(pl.multiple_of(q * n1, 8), n1)
        k_ref[0, 0, rows, :] = xre[:, :LANES]
        k_ref[0, 1, rows, :] = xim[:, LANES:]
        return carry

    lax.fori_loop(0, n2, step, 0, unroll=FFT_MID_UNROLL)


def _spectra(uw, s1, s2):
    _, _, seq, width = uw.shape
    n1 = FFT_N1
    n2 = 2 * seq // n1
    const3 = lambda j, o: (0, 0, 0)
    return pl.pallas_call(
        _spectra_body,
        grid=(width // LANES, 2),
        in_specs=[pl.BlockSpec((1, 2, seq, LANES), lambda j, o: (o, 0, 0, j)),
                  pl.BlockSpec(s1.shape, const3),
                  pl.BlockSpec(s2.shape, lambda j, o: (0, 0))],
        out_specs=pl.BlockSpec((1, 2, 2 * seq, LANES), lambda j, o: (o, 0, 0, j)),
        out_shape=jax.ShapeDtypeStruct((2, 2, 2 * seq, width), F32),
        scratch_shapes=[pltpu.VMEM((2, (n2 // 2) * PITCH, LANES), F32),
                        pltpu.VMEM((2, n2 * PITCH, LANES), F32),
                        pltpu.VMEM((2, n2 * PITCH, LANES), F32)],
        compiler_params=_cparams(("arbitrary", "arbitrary"), 56),
        name="spectra",
    )(uw, s1, s2)


def _short_conv_block(p_ref, b, i, nblk, w_ref, bias_ref):
    n1 = FFT_N1
    start = pl.multiple_of(i * n1, n1)
    cur = p_ref[b, pl.ds(start, n1), :].astype(F32)
    before = p_ref[b, pl.ds(pl.multiple_of(jnp.maximum(start - 16, 0), 16), 16), :].astype(F32)
    after = p_ref[b, pl.ds(pl.multiple_of(jnp.minimum(start + n1, (nblk - 1) * n1), 16), 16), :].astype(F32)
    last_prev = before[15:16] * jnp.where(i > 0, 1.0, 0.0).astype(F32)
    first_next = after[0:1] * jnp.where(i < nblk - 1, 1.0, 0.0).astype(F32)
    row = lax.broadcasted_iota(jnp.int32, (n1, 1), 0)
    prev = jnp.where(row == 0, last_prev, pltpu.roll(cur, 1, 0))
    nxt = jnp.where(row == n1 - 1, first_next, pltpu.roll(cur, n1 - 1, 0))
    return prev * w_ref[0:1, :] + cur * w_ref[1:2, :] + nxt * w_ref[2:3, :] + bias_ref[...]


def _hyena_body(pv_ref, px1_ref, px2_ref, cwv_ref, cw1_ref, cw2_ref, cbv_ref, cb1_ref, cb2_ref,
                skip_ref, k_ref, s1_ref, s1i_ref, s2_ref, s2i_ref, o_ref,
                z_scr, y_scr, gre_scr, gim_scr):
    c = pl.program_id(1)
    g = pl.program_id(2)
    ngroups = pl.num_programs(2)
    n1 = FFT_N1
    n2 = s1_ref.shape[1] // 2
    nblk = n2 // 2
    per_group = n2 // ngroups

    @pl.when(g == 0)
    def _():
        @pl.when(c == 0)
        def _():
            def fill(i, carry):
                for b in range(2):
                    z_scr[b, pl.ds(pl.multiple_of(i * PITCH, 8), n1), :] = _short_conv_block(
                        pv_ref, b, i, nblk, cwv_ref, cbv_ref)
                return carry
            lax.fori_loop(0, nblk, fill, 0)

        _fft_stage1(z_scr, s1_ref, gre_scr, gim_scr)

    def mid(ql, carry):
        q = g * per_group + ql
        xre, xim = _stage2_block(gre_scr, gim_scr, s2_ref, q)
        krows = pl.ds(pl.multiple_of(ql * n1, 8), n1)
        kre = k_ref[0, 0, krows, :]
        kim = k_ref[0, 1, krows, :]
        kre = jnp.concatenate([kre, kre], axis=1)
        kim = jnp.concatenate([kim, kim], axis=1)
        ys = jnp.concatenate([xre * kre - xim * kim, xre * kim + xim * kre], axis=0)
        hh = jnp.dot(s2i_ref[...], ys.astype(BF16), preferred_element_type=F32)
        rows = pl.ds(pl.multiple_of(q * PITCH, 8), n1)
        for b in range(2):
            cols = slice(b * LANES, (b + 1) * LANES)
            gre_scr[b, rows, :] = hh[:n1, cols]
            gim_scr[b, rows, :] = hh[n1:, cols]
        return carry

    lax.fori_loop(0, per_group, mid, 0, unroll=FFT_MID_UNROLL)

    @pl.when(g == ngroups - 1)
    def _():
        _fft_stage1_inv(gre_scr, gim_scr, s1i_ref, y_scr)

        def post(px_ref, cw_ref, cb_ref, order, store):
            def blk(i, carry):
                rows = pl.ds(pl.multiple_of(i * PITCH, 8), n1)
                for b in range(2):
                    zb = z_scr[b, rows, :]
                    gate = _short_conv_block(px_ref, b, i, nblk, cw_ref, cb_ref)
                    store(b, i, rows, gate * (y_scr[b, rows, :] + zb * skip_ref[order:order + 1, :]))
                return carry
            lax.fori_loop(0, nblk, blk, 0)

        @pl.when(c == 0)
        def _():
            def store(b, i, rows, val):
                z_scr[b, rows, :] = val
            post(px1_ref, cw1_ref, cb1_ref, 0, store)

        @pl.when(c == 1)
        def _():
            def store(b, i, rows, val):
                o_ref[b, pl.ds(pl.multiple_of(i * n1, n1), n1), :] = val.astype(o_ref.dtype)
            post(px2_ref, cw2_ref, cb2_ref, 1, store)


def _hyena(pg3, proj_col0, conv_w, conv_b, skip, kspec, s1, s1i, s2, s2i, width):
    bsz, seq, _ = pg3.shape
    assert bsz == 2
    n1 = FFT_N1
    n2 = 2 * seq // n1
    nct = width // LANES
    ngroups = 4
    krows = (n2 // ngroups) * n1
    col = lambda off: (lambda j, c, g: (0, 0, proj_col0 // LANES + off * nct + j))
    cw = lambda off: (lambda j, c, g: (0, off * nct + j))
    const3 = lambda j, c, g: (0, 0, 0)
    const2 = lambda j, c, g: (0, 0)
    return pl.pallas_call(
        _hyena_body,
        grid=(nct, 2, ngroups),
        in_specs=[pl.BlockSpec((2, seq, LANES), col(0)),
                  pl.BlockSpec((2, seq, LANES), col(1)),
                  pl.BlockSpec((2, seq, LANES), col(2)),
                  pl.BlockSpec((3, LANES), cw(0)),
                  pl.BlockSpec((3, LANES), cw(1)),
                  pl.BlockSpec((3, LANES), cw(2)),
                  pl.BlockSpec((1, LANES), cw(0)),
                  pl.BlockSpec((1, LANES), cw(1)),
                  pl.BlockSpec((1, LANES), cw(2)),
                  pl.BlockSpec((2, LANES), lambda j, c, g: (0, j)),
                  pl.BlockSpec((1, 2, krows, LANES), lambda j, c, g: (c, 0, g, j)),
                  pl.BlockSpec(s1.shape, const3),
                  pl.BlockSpec(s1i.shape, const3),
                  pl.BlockSpec(s2.shape, const2),
                  pl.BlockSpec(s2i.shape, const2)],
        out_specs=pl.BlockSpec((2, seq, LANES), lambda j, c, g: (0, 0, j)),
        out_shape=jax.ShapeDtypeStruct((2, seq, width), BF16),
        scratch_shapes=[pltpu.VMEM((2, (n2 // 2) * PITCH, LANES), F32),
                        pltpu.VMEM((2, (n2 // 2) * PITCH, LANES), F32),
                        pltpu.VMEM((2, n2 * PITCH, LANES), F32),
                        pltpu.VMEM((2, n2 * PITCH, LANES), F32)],
        compiler_params=_cparams(("arbitrary", "arbitrary", "arbitrary"), 56),
        name="hyena",
    )(pg3, pg3, pg3, conv_w, conv_w, conv_w, conv_b, conv_b, conv_b, skip, kspec, s1, s1i, s2, s2i)


ATTN_QBLK = 128
ATTN_UNROLL = 4


def _attention_body(pos_ref, freq_ref, sign_ref, *refs):
    qkv_refs = refs[:9]
    o_ref = refs[9]
    cos_scr, sin_scr, q_scr, k_scr, v_scr, og_scr, lse_scr = refs[10:]
    seq = q_scr.shape[0]
    chunk = 512
    nchunks = seq // chunk

    @pl.when(pl.program_id(1) == 0)
    def _():
        def trig(i, carry):
            rows = pl.ds(pl.multiple_of(i * chunk, chunk), chunk)
            ang = pos_ref[0, rows, :].astype(F32) * freq_ref[...]
            cos_scr[rows, :] = jnp.cos(ang)
            sin_scr[rows, :] = jnp.sin(ang) * sign_ref[...]
            return carry
        lax.fori_loop(0, nchunks, trig, 0)

    def rotate(src_ref, dst_ref):
        def body(i, carry):
            rows = pl.ds(pl.multiple_of(i * chunk, chunk), chunk)
            t = src_ref[0, rows, :].astype(F32)
            lane = lax.broadcasted_iota(jnp.int32, t.shape, 1)
            partner = jnp.where(lane < ROT_DIM // 2,
                                pltpu.roll(t, LANES - ROT_DIM // 2, 1), pltpu.roll(t, ROT_DIM // 2, 1))
            dst_ref[rows, :] = t * cos_scr[rows, :] + partner * sin_scr[rows, :]
            return carry
        lax.fori_loop(0, nchunks, body, 0)

    def widen(src_ref, dst_ref):
        def body(i, carry):
            rows = pl.ds(pl.multiple_of(i * chunk, chunk), chunk)
            dst_ref[rows, :] = src_ref[0, rows, :].astype(F32)
            return carry
        lax.fori_loop(0, nchunks, body, 0)

    scale = HEAD_DIM ** -0.5
    for gi, (window, dil) in enumerate(ATTN_GROUPS):
        rotate(qkv_refs[3 * gi], q_scr)
        rotate(qkv_refs[3 * gi + 1], k_scr)
        widen(qkv_refs[3 * gi + 2], v_scr)
        n = seq // dil
        half = window // (2 * dil)
        tk = min(n, ATTN_QBLK + 2 * half)
        blocks_per_res = n // ATTN_QBLK

        def block(u, carry, dil=dil, n=n, half=half, tk=tk, blocks_per_res=blocks_per_res, gi=gi):
            r = u // blocks_per_res
            m = u % blocks_per_res
            q0 = m * ATTN_QBLK
            k0 = jnp.clip(q0 - half, 0, n - tk)
            qrows = pl.ds(r + dil * q0, ATTN_QBLK, stride=dil)
            krows = pl.ds(r + dil * k0, tk, stride=dil)
            qb = q_scr[qrows, :].astype(BF16)
            kb = k_scr[krows, :].astype(BF16)
            vb = v_scr[krows, :].astype(BF16)
            s = lax.dot_general(qb, kb, (((1,), (1,)), ((), ())), preferred_element_type=F32) * scale
            qi = q0 + lax.broadcasted_iota(jnp.int32, (ATTN_QBLK, tk), 0)
            kj = k0 + lax.broadcasted_iota(jnp.int32, (ATTN_QBLK, tk), 1)
            s = jnp.where(jnp.abs(qi - kj) <= half, s, NEG_INF)
            mx = jnp.max(s, axis=-1, keepdims=True)
            p = jnp.exp(s - mx)
            l = jnp.sum(p, axis=-1, keepdims=True)
            o = jnp.dot(p.astype(BF16), vb, preferred_element_type=F32) / l
            og_scr[gi, qrows, :] = o
            lse_scr[gi, qrows, :] = jnp.broadcast_to(mx + jnp.log(l), (ATTN_QBLK, LANES))
            return carry

        lax.fori_loop(0, seq // ATTN_QBLK, block, 0, unroll=ATTN_UNROLL)

    def merge(i, carry):
        rows = pl.ds(pl.multiple_of(i * chunk, chunk), chunk)
        lses = [lse_scr[gi, rows, :] for gi in range(len(ATTN_GROUPS))]
        mx = functools.reduce(jnp.maximum, lses)
        ws = [jnp.exp(v - mx) for v in lses]
        den = functools.reduce(lambda a, b: a + b, ws)
        num = functools.reduce(lambda a, b: a + b,
                               [w * og_scr[gi, rows, :] for gi, w in enumerate(ws)])
        o_ref[0, rows, :] = (num / den).astype(o_ref.dtype)
        return carry

    lax.fori_loop(0, nchunks, merge, 0)


def _attention(pg3, pos3, freq_row, sign_row, qkv_col0):
    bsz, seq, _ = pg3.shape
    ng = len(ATTN_GROUPS)
    in_specs = [pl.BlockSpec((1, seq, 1), lambda b, h: (b, 0, 0)),
                pl.BlockSpec((1, LANES), lambda b, h: (0, 0)),
                pl.BlockSpec((1, LANES), lambda b, h: (0, 0))]
    for gi in range(ng):
        for which in range(3):
            base = qkv_col0 // LANES + which * N_ATTN_HEADS + gi * HEADS_PER_GROUP
            in_specs.append(pl.BlockSpec((1, seq, LANES), lambda b, h, base=base: (b, 0, base + h)))
    return pl.pallas_call(
        _attention_body,
        grid=(bsz, HEADS_PER_GROUP),
        in_specs=in_specs,
        out_specs=pl.BlockSpec((1, seq, LANES), lambda b, h: (b, 0, h)),
        out_shape=jax.ShapeDtypeStruct((bsz, seq, HEADS_PER_GROUP * HEAD_DIM), BF16),
        scratch_shapes=[pltpu.VMEM((seq, LANES), F32), pltpu.VMEM((seq, LANES), F32),
                        pltpu.VMEM((seq, LANES), F32), pltpu.VMEM((seq, LANES), F32),
                        pltpu.VMEM((seq, LANES), F32),
                        pltpu.VMEM((ng, seq, LANES), F32), pltpu.VMEM((ng, seq, LANES), F32)],
        compiler_params=_cparams(("arbitrary", "arbitrary"), 56),
        name="attention",
    )(pos3, freq_row, sign_row, *([pg3] * (3 * ng)))


def _mix_out_body(yhy_ref, yat_ref, ghy_ref, gat_ref, x_ref, mod_ref, gpost_ref, gpre_ref,
                  whyo_ref, wato_ref, wout_ref, wrhi_ref, wrlo_ref, br_ref, x1_ref, h2_ref, lg_ref):
    a = jnp.dot(yhy_ref[...], whyo_ref[...], preferred_element_type=F32)
    b = jnp.dot(yat_ref[...], wato_ref[...], preferred_element_type=F32)
    merged = ghy_ref[...].astype(F32) * a + gat_ref[...].astype(F32) * b
    y = jnp.dot(merged.astype(BF16), wout_ref[...], preferred_element_type=F32)
    x1 = x_ref[...] + mod_ref[0, 2:3, :] * _rms(y, gpost_ref[...])
    x1_ref[...] = x1
    h2 = _rms(x1, gpre_ref[...]) * (1.0 + mod_ref[0, 4:5, :]) + mod_ref[0, 3:4, :]
    h2_ref[...] = h2
    h2_hi = h2.astype(BF16)
    h2_lo = (h2 - h2_hi.astype(F32)).astype(BF16)
    w_hi = wrhi_ref[...]
    w_lo = wrlo_ref[...]
    lg_ref[...] = (jnp.dot(h2_hi, w_hi, preferred_element_type=F32)
                   + jnp.dot(h2_lo, w_hi, preferred_element_type=F32)
                   + jnp.dot(h2_hi, w_lo, preferred_element_type=F32)
                   + jnp.dot(h2_lo, w_lo, preferred_element_type=F32)) + br_ref[...]


def _mix_out(y_hy, y_at, pg, x2d, mod, g_post, g_pre, w_hy_o, w_at_o, w_out, w_r_hi, w_r_lo, b_r, seq):
    t, d = x2d.shape
    tm = 256
    per_batch = seq // tm
    gblk = 0
    const = lambda i: (0, 0)
    return pl.pallas_call(
        _mix_out_body,
        grid=(t // tm,),
        in_specs=[pl.BlockSpec((tm, y_hy.shape[1]), lambda i: (i, 0)),
                  pl.BlockSpec((tm, y_at.shape[1]), lambda i: (i, 0)),
                  pl.BlockSpec((tm, d), lambda i: (i, gblk)),
                  pl.BlockSpec((tm, d), lambda i: (i, gblk + 1)),
                  pl.BlockSpec((tm, d), lambda i: (i, 0)),
                  pl.BlockSpec((1, 6, d), lambda i: (i // per_batch, 0, 0)),
                  pl.BlockSpec((1, d), const),
                  pl.BlockSpec((1, d), const),
                  pl.BlockSpec(w_hy_o.shape, const),
                  pl.BlockSpec(w_at_o.shape, const),
                  pl.BlockSpec(w_out.shape, const),
                  pl.BlockSpec(w_r_hi.shape, const),
                  pl.BlockSpec(w_r_lo.shape, const),
                  pl.BlockSpec((1, LANES), const)],
        out_specs=[pl.BlockSpec((tm, d), lambda i: (i, 0)),
                   pl.BlockSpec((tm, d), lambda i: (i, 0)),
                   pl.BlockSpec((tm, LANES), lambda i: (i, 0))],
        out_shape=[jax.ShapeDtypeStruct((t, d), F32),
                   jax.ShapeDtypeStruct((t, d), F32),
                   jax.ShapeDtypeStruct((t, LANES), F32)],
        compiler_params=_cparams(("arbitrary",), 56),
        name="mix_out",
    )(y_hy, y_at, pg, pg, x2d, mod, g_post, g_pre, w_hy_o, w_at_o, w_out, w_r_hi, w_r_lo, b_r)


def _route_body(lg_ref, eid_ref, gate_ref):
    lg = lg_ref[...]
    lane = lax.broadcasted_iota(jnp.int32, lg.shape, 1)
    big = jnp.int32(1 << 20)

    def first_argmax(vals, mask):
        v = jnp.where(mask, vals, -jnp.inf)
        mx = jnp.max(v, axis=-1, keepdims=True)
        idx = jnp.min(jnp.where(mask & (v == mx), lane, big), axis=-1, keepdims=True)
        return mx, idx

    gmask = lane < N_EXPERT_GROUPS
    gmax, gidx = first_argmax(lg, gmask)
    gval = 1.0 / jnp.sum(jnp.where(gmask, jnp.exp(lg - gmax), 0.0), axis=-1, keepdims=True)
    lo = N_EXPERT_GROUPS + gidx * EXPERTS_PER_GROUP
    emask = (lane >= lo) & (lane < lo + EXPERTS_PER_GROUP)
    v1, i1 = first_argmax(lg, emask)
    v2, i2 = first_argmax(lg, emask & (lane != i1))
    e2 = jnp.exp(v2 - v1)
    p1 = 1.0 / (1.0 + e2)
    p2 = e2 / (1.0 + e2)
    eid = jnp.where(lane == 0, i1, i2) - N_EXPERT_GROUPS
    gate = gval * jnp.where(lane == 0, p1, p2)
    eid_ref[...] = eid[:, :TOP_K]
    gate_ref[...] = gate[:, :TOP_K]


def _route(logits):
    t = logits.shape[0]
    tm = 512
    return pl.pallas_call(
        _route_body,
        grid=(t // tm,),
        in_specs=[pl.BlockSpec((tm, LANES), lambda i: (i, 0))],
        out_specs=[pl.BlockSpec((tm, TOP_K), lambda i: (i, 0)),
                   pl.BlockSpec((tm, TOP_K), lambda i: (i, 0))],
        out_shape=[jax.ShapeDtypeStruct((t, TOP_K), jnp.int32),
                   jax.ShapeDtypeStruct((t, TOP_K), F32)],
        compiler_params=_cparams(("arbitrary",), 32),
        name="route",
    )(logits)


def _rank_body(e_ref, rank_ref, cnt_ref, carry_scr):
    i = pl.program_id(0)
    r = e_ref.shape[0]

    @pl.when(i == 0)
    def _():
        carry_scr[...] = jnp.zeros_like(carry_scr)

    lane = lax.broadcasted_iota(jnp.int32, (r, LANES), 1)
    onehot = (lane == e_ref[...]).astype(F32)
    tri = (lax.broadcasted_iota(jnp.int32, (r, r), 1)
           < lax.broadcasted_iota(jnp.int32, (r, r), 0)).astype(BF16)
    before = jnp.dot(tri, onehot.astype(BF16), preferred_element_type=F32) + carry_scr[0:1, :]
    rank_ref[...] = jnp.sum(onehot * before, axis=-1, keepdims=True).astype(jnp.int32)
    total = carry_scr[0:1, :] + jnp.sum(onehot, axis=0, keepdims=True)
    carry_scr[...] = jnp.broadcast_to(total, carry_scr.shape)
    cnt_ref[...] = jnp.broadcast_to(total, cnt_ref.shape).astype(jnp.int32)


def _rank(flat_e):
    a = flat_e.shape[0]
    r = 512
    return pl.pallas_call(
        _rank_body,
        grid=(a // r,),
        in_specs=[pl.BlockSpec((r, 1), lambda i: (i, 0))],
        out_specs=[pl.BlockSpec((r, 1), lambda i: (i, 0)),
                   pl.BlockSpec((8, LANES), lambda i: (0, 0))],
        out_shape=[jax.ShapeDtypeStruct((a, 1), jnp.int32),
                   jax.ShapeDtypeStruct((8, LANES), jnp.int32)],
        scratch_shapes=[pltpu.VMEM((8, LANES), F32)],
        compiler_params=_cparams(("arbitrary",), 32),
        name="rank",
    )(flat_e)


def _dest_body(e_ref, rank_ref, ps_ref, o_ref):
    lane = lax.broadcasted_iota(jnp.int32, (e_ref.shape[0], LANES), 1)
    first = jnp.sum(jnp.where(lane == e_ref[...], ps_ref[...], 0.0), axis=-1, keepdims=True)
    o_ref[...] = first.astype(jnp.int32) + rank_ref[...]


def _dest(flat_e, rank, pstarts_row):
    a = flat_e.shape[0]
    r = 2048
    return pl.pallas_call(
        _dest_body,
        grid=(a // r,),
        in_specs=[pl.BlockSpec((r, 1), lambda i: (i, 0)),
                  pl.BlockSpec((r, 1), lambda i: (i, 0)),
                  pl.BlockSpec((1, LANES), lambda i: (0, 0))],
        out_specs=pl.BlockSpec((r, 1), lambda i: (i, 0)),
        out_shape=jax.ShapeDtypeStruct((a, 1), jnp.int32),
        compiler_params=_cparams(("arbitrary",), 32),
        name="dest",
    )(flat_e, rank, pstarts_row)


DMA_UNROLL = 8


def _experts_body(ord_ref, elist_ref, src_ref, cnt_ref, h_hbm, w1_hbm, w3_hbm, w2_hbm, y_ref,
                  x_scr, w1_scr, w3_scr, w2_scr, sem, wsem):
    i = pl.program_id(0)
    nrows = x_scr.shape[1]
    nused = cnt_ref[0]
    nexp = cnt_ref[1]
    k = ord_ref[i]
    first = (i == 0) | (k != ord_ref[jnp.maximum(i - 1, 0)])

    def weights(kk, act):
        e = elist_ref[kk]
        slot = kk % 2
        act(pltpu.make_async_copy(w1_hbm.at[e], w1_scr.at[slot], wsem.at[slot, 0]))
        act(pltpu.make_async_copy(w3_hbm.at[e], w3_scr.at[slot], wsem.at[slot, 1]))
        act(pltpu.make_async_copy(w2_hbm.at[e], w2_scr.at[slot], wsem.at[slot, 2]))

    def gather(blk, act):
        slot = blk % 2
        base = blk * nrows

        def body(r, carry):
            act(pltpu.make_async_copy(h_hbm.at[pl.ds(src_ref[base + r], 1)],
                                      x_scr.at[slot, pl.ds(r, 1)], sem.at[slot]))
            return carry

        lax.fori_loop(0, nrows, body, 0, unroll=DMA_UNROLL)

    @pl.when(i == 0)
    def _():
        weights(k, lambda cp: cp.start())
        gather(i, lambda cp: cp.start())

    @pl.when((i < nused) & first & (k + 1 < nexp))
    def _():
        weights(k + 1, lambda cp: cp.start())

    @pl.when(i + 1 < nused)
    def _():
        gather(i + 1, lambda cp: cp.start())

    @pl.when((i < nused) & first)
    def _():
        weights(k, lambda cp: cp.wait())

    @pl.when(i < nused)
    def _():
        gather(i, lambda cp: cp.wait())
        wslot = k % 2
        xb = x_scr[i % 2].astype(BF16)
        a = jnp.dot(xb, w1_scr[wslot].astype(BF16), preferred_element_type=F32)
        b = jnp.dot(xb, w3_scr[wslot].astype(BF16), preferred_element_type=F32)
        act = (a * jax.nn.sigmoid(a) * b).astype(BF16)
        y_ref[...] = jnp.dot(act, w2_scr[wslot].astype(BF16), preferred_element_type=F32)

    @pl.when(i >= nused)
    def _():
        y_ref[...] = jnp.zeros_like(y_ref)


def _experts(block_ord, elist, src_tok, counts2, h2, w1, w3, w2):
    nblocks = block_ord.shape[0]
    d = h2.shape[1]
    ff = w1.shape[2]
    rb = EXPERT_ROW_BLOCK
    grid_spec = pltpu.PrefetchScalarGridSpec(
        num_scalar_prefetch=4,
        grid=(nblocks,),
        in_specs=[pl.BlockSpec(memory_space=pl.ANY)] * 4,
        out_specs=pl.BlockSpec((rb, d), lambda i, *_: (i, 0)),
        scratch_shapes=[pltpu.VMEM((2, rb, d), F32),
                        pltpu.VMEM((2, d, ff), F32), pltpu.VMEM((2, d, ff), F32),
                        pltpu.VMEM((2, ff, d), F32),
                        pltpu.SemaphoreType.DMA((2,)), pltpu.SemaphoreType.DMA((2, 3))],
    )
    return pl.pallas_call(
        _experts_body,
        grid_spec=grid_spec,
        out_shape=jax.ShapeDtypeStruct((nblocks * rb, d), F32),
        compiler_params=_cparams(("arbitrary",), 56),
        name="experts",
    )(block_ord, elist, src_tok, counts2, h2, w1, w3, w2)


def _combine_body(dest_ref, y_hbm, gate_ref, x1_ref, mod_ref, g_ref, o_ref, buf_scr, sem):
    i = pl.program_id(0)
    tm = x1_ref.shape[0]

    def gather(step, act):
        slot = step % 2
        base = step * tm * TOP_K

        def body(r, carry):
            for k in range(TOP_K):
                act(pltpu.make_async_copy(y_hbm.at[pl.ds(dest_ref[base + TOP_K * r + k], 1)],
                                          buf_scr.at[slot, k, pl.ds(r, 1)], sem.at[slot]))
            return carry

        lax.fori_loop(0, tm, body, 0, unroll=DMA_UNROLL // TOP_K)

    @pl.when(i == 0)
    def _():
        gather(i, lambda cp: cp.start())

    @pl.when(i + 1 < pl.num_programs(0))
    def _():
        gather(i + 1, lambda cp: cp.start())

    gather(i, lambda cp: cp.wait())
    slot = i % 2
    gate = gate_ref[...]
    y = buf_scr[slot, 0] * gate[:, 0:1] + buf_scr[slot, 1] * gate[:, 1:2]
    o_ref[...] = x1_ref[...] + mod_ref[0, 5:6, :] * _rms(y, g_ref[...])


def _combine(dest, ybuf, gate, x1, mod, g_post, seq):
    t, d = x1.shape
    tm = 128
    per_batch = seq // tm
    grid_spec = pltpu.PrefetchScalarGridSpec(
        num_scalar_prefetch=1,
        grid=(t // tm,),
        in_specs=[pl.BlockSpec(memory_space=pl.ANY),
                  pl.BlockSpec((tm, TOP_K), lambda i, dst: (i, 0)),
                  pl.BlockSpec((tm, d), lambda i, dst: (i, 0)),
                  pl.BlockSpec((1, 6, d), lambda i, dst: (i // per_batch, 0, 0)),
                  pl.BlockSpec((1, d), lambda i, dst: (0, 0))],
        out_specs=pl.BlockSpec((tm, d), lambda i, dst: (i, 0)),
        scratch_shapes=[pltpu.VMEM((2, TOP_K, tm, d), F32), pltpu.SemaphoreType.DMA((2,))],
    )
    return pl.pallas_call(
        _combine_body,
        grid_spec=grid_spec,
        out_shape=jax.ShapeDtypeStruct((t, d), F32),
        compiler_params=_cparams(("arbitrary",), 32),
        name="combine",
    )(dest, ybuf, gate, x1, mod, g_post)


def _layer(x, c, positions, w_ada, b_ada, g_mix_pre, g_mix_post, g_ffn_pre, g_ffn_post,
           w_in, conv_w, conv_b, filt_w1, filt_b1, filt_w2, filt_b2, filt_w3, filt_freq,
           hyena_skip, w_branch_gate, b_branch_gate, w_hy_o, w_at_o, w_out,
           w_group, b_group, w_expert, b_expert, w1_exp, w3_exp, w2_exp):
    bsz, seq, d = x.shape
    t = bsz * seq
    width = hyena_skip.shape[1]
    row = lambda v: v.reshape(1, -1)

    c_pad = jnp.pad(c, ((0, 8 - bsz), (0, 0)))
    mod = _adaln(c_pad, w_ada, row(b_ada))[:bsz].reshape(bsz, 6, d)

    x2d = x.reshape(t, d)
    n_gate = w_branch_gate.shape[1]
    n_cat = -(-(n_gate + w_in.shape[1]) // IN_PROJ_TN) * IN_PROJ_TN
    n_zero = n_cat - n_gate - w_in.shape[1]
    w_cat = jnp.concatenate([w_branch_gate.astype(BF16), w_in.astype(BF16),
                             jnp.zeros((d, n_zero), BF16)], axis=1)
    b_cat = jnp.concatenate([b_branch_gate, jnp.zeros((n_cat - n_gate,), F32)]).reshape(1, -1)
    pg = _in_proj(x2d, mod, row(g_mix_pre), w_cat, b_cat, n_gate, seq)
    pg3 = pg.reshape(bsz, seq, -1)

    hidden = filt_w2.shape[0]
    bands = np.zeros((1, LANES), np.float32)
    band_vals = np.linspace(1e-4, FILTER_BANDS - 1, FILTER_BANDS, dtype=np.float32)
    bands[0, 1:1 + FILTER_BANDS] = band_vals
    bands[0, 1 + FILTER_BANDS:1 + 2 * FILTER_BANDS] = band_vals
    w1p = jnp.pad(filt_w1, ((0, LANES - filt_w1.shape[0]), (0, 0)))
    max_decay = math.log(DECAY_TARGET) / FAST_DECAY_PCT
    min_decay = math.log(DECAY_TARGET) / SLOW_DECAY_PCT
    deltas = jnp.abs(jnp.linspace(min_decay, max_decay, width, dtype=F32)).reshape(1, -1)
    uw = _filters(seq, width, jnp.asarray(bands), w1p, row(filt_b1), filt_w2, row(filt_b2),
                  row(filt_freq), deltas, filt_w3)
    s1, s1i, s2, s2i = _fft_tables(FFT_N1, 2 * seq // FFT_N1)
    kspec = _spectra(uw, s1, s2)
    y_hy = _hyena(pg3, n_gate, conv_w, row(conv_b), hyena_skip, kspec, s1, s1i, s2, s2i, width)

    half = ROT_DIM // 2
    inv_freq = np.power(ROPE_THETA, -2.0 * np.arange(half, dtype=np.float32) / ROT_DIM).astype(np.float32)
    freq_row = np.zeros((1, LANES), np.float32)
    freq_row[0, :half] = inv_freq
    freq_row[0, half:ROT_DIM] = inv_freq
    sign_row = np.zeros((1, LANES), np.float32)
    sign_row[0, :half] = -1.0
    sign_row[0, half:ROT_DIM] = 1.0
    y_at = _attention(pg3, positions.reshape(bsz, seq, 1), jnp.asarray(freq_row), jnp.asarray(sign_row),
                      n_gate + 3 * width)

    w_r = jnp.concatenate([w_group, jnp.transpose(w_expert, (1, 0, 2)).reshape(d, N_EXPERTS)], axis=1)
    w_r = jnp.pad(w_r, ((0, 0), (0, LANES - w_r.shape[1])))
    w_r_hi = w_r.astype(BF16)
    w_r_lo = (w_r - w_r_hi.astype(F32)).astype(BF16)
    b_r = jnp.pad(jnp.concatenate([b_group, b_expert.reshape(-1)]), (0, LANES - N_EXPERT_GROUPS - N_EXPERTS))
    x1, h2, logits = _mix_out(y_hy.reshape(t, width), y_at.reshape(t, -1), pg, x2d, mod,
                              row(g_mix_post), row(g_ffn_pre), w_hy_o.astype(BF16), w_at_o.astype(BF16),
                              w_out.astype(BF16), w_r_hi, w_r_lo, b_r.reshape(1, -1), seq)

    eid, gate = _route(logits)
    flat_e = eid.reshape(t * TOP_K, 1)
    rank, counts = _rank(flat_e)
    counts = counts[0, :N_EXPERTS]
    rb = EXPERT_ROW_BLOCK
    padded = (counts + rb - 1) // rb * rb
    pends = jnp.cumsum(padded)
    pstarts = pends - padded
    pstarts_row = jnp.pad(pstarts.astype(F32), (0, LANES - N_EXPERTS)).reshape(1, LANES)
    dest = _dest(flat_e, rank, pstarts_row)[:, 0]
    n_blocks = t * TOP_K // rb + N_EXPERTS
    block_start = jnp.arange(n_blocks, dtype=jnp.int32) * rb
    block_e = jnp.minimum(jnp.searchsorted(pends, block_start, side='right'), N_EXPERTS - 1).astype(jnp.int32)
    tok = jnp.arange(t * TOP_K, dtype=jnp.int32) // TOP_K
    src_tok = jnp.zeros((n_blocks * rb,), jnp.int32).at[dest].set(tok)
    has_rows = jnp.cumsum((counts > 0).astype(jnp.int32))
    elist = jnp.minimum(jnp.searchsorted(has_rows, jnp.arange(1, N_EXPERTS + 1, dtype=jnp.int32), side='left'),
                        N_EXPERTS - 1).astype(jnp.int32)
    block_ord = (has_rows - 1)[block_e].astype(jnp.int32)
    counts2 = jnp.stack([pends[-1] // rb, has_rows[-1]]).astype(jnp.int32)

    ybuf = _experts(block_ord, elist, src_tok, counts2, h2, w1_exp, w3_exp, w2_exp)
    out = _combine(dest, ybuf, gate, x1, mod, row(g_ffn_post), seq)
    return out.reshape(bsz, seq, d)


def kernel(x, c, positions, w_ada, b_ada, g_mix_pre, g_mix_post, g_ffn_pre, g_ffn_post, w_in, conv_w, conv_b, filt_w1, filt_b1, filt_w2, filt_b2, filt_w3, filt_freq, hyena_skip, w_branch_gate, b_branch_gate, w_hy_o, w_at_o, w_out, w_group, b_group, w_expert, b_expert, w1_exp, w3_exp, w2_exp):
    depth = w_ada.shape[0]
    for l in range(depth):
        x = _layer(x, c, positions, w_ada[l], b_ada[l], g_mix_pre[l], g_mix_post[l], g_ffn_pre[l],
                   g_ffn_post[l], w_in[l], conv_w[l], conv_b[l], filt_w1[l], filt_b1[l], filt_w2[l],
                   filt_b2[l], filt_w3[l], filt_freq[l], hyena_skip[l], w_branch_gate[l],
                   b_branch_gate[l], w_hy_o[l], w_at_o[l], w_out[l], w_group[l], b_group[l],
                   w_expert[l], b_expert[l], w1_exp[l], w3_exp[l], w2_exp[l])
    return x
```

```python
import functools
import math

import numpy as np
import jax
import jax.numpy as jnp
from jax import lax
from jax.experimental import pallas as pl
from jax.experimental.pallas import tpu as pltpu

F32 = jnp.float32
BF16 = jnp.bfloat16

LANES = 128
MIB = 1024 * 1024

RMS_EPS = 1e-6
NEG_INF = -1e30

HEAD_DIM = 128
ROT_DIM = HEAD_DIM // 4
ROPE_THETA = 500000.0
ATTN_GROUPS = ((128, 1), (512, 4), (2048, 16))
HEADS_PER_GROUP = 4
N_ATTN_HEADS = HEADS_PER_GROUP * len(ATTN_GROUPS)

FILTER_BANDS = 16
DECAY_TARGET = 1e-2
FAST_DECAY_PCT = 0.3
SLOW_DECAY_PCT = 1.5

N_EXPERT_GROUPS = 8
EXPERTS_PER_GROUP = 8
N_EXPERTS = N_EXPERT_GROUPS * EXPERTS_PER_GROUP
TOP_K = 2
EXPERT_ROW_BLOCK = 128

FFT_N1 = 128
PITCH = FFT_N1 + 8
FFT_UNROLL = 16
FFT_MID_UNROLL = 8


def _cparams(sem, vmem_mib):
    return pltpu.CompilerParams(dimension_semantics=sem, vmem_limit_bytes=vmem_mib * MIB)


def _rms(x, g):
    return x * lax.rsqrt(jnp.mean(x * x, axis=-1, keepdims=True) + RMS_EPS) * g


def _adaln_body(c_ref, w_ref, b_ref, o_ref):
    c = c_ref[...]
    cond = c * jax.nn.sigmoid(c)
    o_ref[...] = jnp.dot(cond.astype(BF16), w_ref[...].astype(BF16),
                         preferred_element_type=F32) + b_ref[...]


def _adaln(c_pad, w_ada, b_ada):
    rows, d = c_pad.shape
    n = w_ada.shape[1]
    tn = 1024
    return pl.pallas_call(
        _adaln_body,
        grid=(n // tn,),
        in_specs=[pl.BlockSpec((rows, d), lambda j: (0, 0)),
                  pl.BlockSpec((d, tn), lambda j: (0, j)),
                  pl.BlockSpec((1, tn), lambda j: (0, j))],
        out_specs=pl.BlockSpec((rows, tn), lambda j: (0, j)),
        out_shape=jax.ShapeDtypeStruct((rows, n), F32),
        compiler_params=_cparams(("arbitrary",), 40),
        name="adaln",
    )(c_pad, w_ada, b_ada)


IN_PROJ_TM = 1024
IN_PROJ_TN = 1024


def _in_proj_body(n_gate, x_ref, mod_ref, g_ref, w_ref, b_ref, o_ref, h_scr):
    j = pl.program_id(1)

    @pl.when(j == 0)
    def _():
        x = x_ref[...]
        h = _rms(x, g_ref[...]) * (1.0 + mod_ref[0, 1:2, :]) + mod_ref[0, 0:1, :]
        h_scr[...] = h.astype(BF16)

    acc = jnp.dot(h_scr[...], w_ref[...], preferred_element_type=F32) + b_ref[...]

    @pl.when(j < n_gate)
    def _():
        o_ref[...] = jax.nn.sigmoid(acc).astype(o_ref.dtype)

    @pl.when(j >= n_gate)
    def _():
        o_ref[...] = acc.astype(o_ref.dtype)


def _in_proj(x2d, mod, g_pre, w_cat, b_cat, n_gate_cols, seq):
    t, d = x2d.shape
    n = w_cat.shape[1]
    tm, tn = IN_PROJ_TM, IN_PROJ_TN
    per_batch = seq // tm
    return pl.pallas_call(
        functools.partial(_in_proj_body, n_gate_cols // tn),
        grid=(t // tm, n // tn),
        in_specs=[pl.BlockSpec((tm, d), lambda i, j: (i, 0)),
                  pl.BlockSpec((1, 6, d), lambda i, j: (i // per_batch, 0, 0)),
                  pl.BlockSpec((1, d), lambda i, j: (0, 0)),
                  pl.BlockSpec((d, tn), lambda i, j: (0, j)),
                  pl.BlockSpec((1, tn), lambda i, j: (0, j))],
        out_specs=pl.BlockSpec((tm, tn), lambda i, j: (i, j)),
        out_shape=jax.ShapeDtypeStruct((t, n), BF16),
        scratch_shapes=[pltpu.VMEM((tm, d), BF16)],
        compiler_params=_cparams(("arbitrary", "arbitrary"), 56),
        name="in_proj",
    )(x2d, mod, g_pre, w_cat, b_cat)


def _filters_body(seq, band_ref, w1_ref, b1_ref, w2_ref, b2_ref, fr_ref, dl_ref,
                  w3a_ref, w3b_ref, o_ref, hid_scr):
    i = pl.program_id(0)
    j = pl.program_id(1)
    tl = hid_scr.shape[0]
    row = (lax.broadcasted_iota(jnp.int32, (tl, 1), 0) + i * tl).astype(F32)

    @pl.when((j == 0) & (pl.program_id(2) == 0))
    def _():
        lane = lax.broadcasted_iota(jnp.int32, (tl, LANES), 1)
        t = row / (seq - 1.0)
        ang = band_ref[...] * (2.0 * math.pi * row / seq)
        feats = jnp.where(lane == 0, t,
                          jnp.where(lane <= FILTER_BANDS, jnp.cos(ang),
                                    jnp.where(lane <= 2 * FILTER_BANDS, -jnp.sin(ang), 0.0)))
        hi = lax.Precision.HIGHEST
        fr = fr_ref[...]
        hid = jnp.sin(fr * (jnp.dot(feats, w1_ref[...], precision=hi, preferred_element_type=F32)
                            + b1_ref[...]))
        hid = jnp.sin(fr * (jnp.dot(hid, w2_ref[...], precision=hi, preferred_element_type=F32)
                            + b2_ref[...]))
        hid_scr[...] = hid

    hi = lax.Precision.HIGHEST
    hid = hid_scr[...]
    decay = jnp.exp(-(row / (seq - 1.0)) * dl_ref[...])
    hf = jnp.dot(hid, w3a_ref[...], precision=hi, preferred_element_type=F32) * decay
    hb = jnp.dot(hid, w3b_ref[...], precision=hi, preferred_element_type=F32) * decay
    hb = jnp.where(row == 0.0, 0.0, hb)
    o_ref[0, 0] = hf + hb
    o_ref[0, 1] = hf - hb


def _filters(seq, width, bands, w1p, b1, w2, b2, freq, deltas, w3):
    tl, tc = 512, 256
    nct = width // tc
    hidden = w2.shape[0]
    const = lambda i, j, o: (0, 0)
    return pl.pallas_call(
        functools.partial(_filters_body, float(seq)),
        grid=(seq // tl, nct, 2),
        in_specs=[pl.BlockSpec((1, LANES), const),
                  pl.BlockSpec((LANES, hidden), const),
                  pl.BlockSpec((1, hidden), const),
                  pl.BlockSpec((hidden, hidden), const),
                  pl.BlockSpec((1, hidden), const),
                  pl.BlockSpec((1, hidden), const),
                  pl.BlockSpec((1, tc), lambda i, j, o: (0, j)),
                  pl.BlockSpec((hidden, tc), lambda i, j, o: (0, (2 * o) * nct + j)),
                  pl.BlockSpec((hidden, tc), lambda i, j, o: (0, (2 * o + 1) * nct + j))],
        out_specs=pl.BlockSpec((1, 2, tl, tc), lambda i, j, o: (o, 0, i, j)),
        out_shape=jax.ShapeDtypeStruct((2, 2, seq, width), F32),
        scratch_shapes=[pltpu.VMEM((tl, hidden), F32)],
        compiler_params=_cparams(("arbitrary", "arbitrary", "arbitrary"), 32),
        name="filters",
    )(bands, w1p, b1, w2, b2, freq, deltas, w3, w3)


def _fft_tables(n1, n2):
    n = n1 * n2
    q = np.arange(n2)[:, None]
    b = np.arange(n2 // 2)[None, :]
    a = np.arange(n1)[:, None, None]
    ang = -2.0 * np.pi * (a * q[None] / n + (q * b)[None] / n2)
    stage1 = np.concatenate([np.cos(ang), np.sin(ang)], axis=1)
    stage1_inv = np.transpose(stage1, (0, 2, 1)) / n
    p = np.arange(n1)
    ang2 = -2.0 * np.pi * np.outer(p, p) / n1
    fre, fim = np.cos(ang2), np.sin(ang2)
    stage2 = np.block([[fre, -fim], [fim, fre]])
    stage2_inv = np.block([[fre, fim], [-fim, fre]])
    as_bf16 = lambda m: jnp.asarray(m, dtype=F32).astype(BF16)
    return as_bf16(stage1), as_bf16(stage1_inv), as_bf16(stage2), as_bf16(stage2_inv)


def _halves(ref, rows):
    return jnp.concatenate([ref[0, rows, :], ref[1, rows, :]], axis=1)


def _fft_stage1(z_ref, s1_ref, gre_ref, gim_ref):
    n1, two_n2, n2h = s1_ref.shape
    n2 = two_n2 // 2

    def step(a, carry):
        zrows = _halves(z_ref, pl.ds(a, n2h, stride=PITCH))
        g = jnp.dot(s1_ref[a], zrows.astype(BF16), preferred_element_type=F32)
        for h in range(2):
            cols = slice(h * LANES, (h + 1) * LANES)
            gre_ref[h, pl.ds(a, n2, stride=PITCH), :] = g[:n2, cols]
            gim_ref[h, pl.ds(a, n2, stride=PITCH), :] = g[n2:, cols]
        return carry

    lax.fori_loop(0, n1, step, 0, unroll=FFT_UNROLL)


def _fft_stage1_inv(gre_ref, gim_ref, s1i_ref, y_ref):
    n1, n2h, two_n2 = s1i_ref.shape
    n2 = two_n2 // 2

    def step(a, carry):
        rows = pl.ds(a, n2, stride=PITCH)
        hs = jnp.concatenate([_halves(gre_ref, rows), _halves(gim_ref, rows)], axis=0)
        y = jnp.dot(s1i_ref[a], hs.astype(BF16), preferred_element_type=F32)
        for h in range(2):
            y_ref[h, pl.ds(a, n2h, stride=PITCH), :] = y[:, h * LANES:(h + 1) * LANES]
        return carry

    lax.fori_loop(0, n1, step, 0, unroll=FFT_UNROLL)


def _stage2_block(gre_ref, gim_ref, s2_ref, q):
    n1 = s2_ref.shape[0] // 2
    rows = pl.ds(pl.multiple_of(q * PITCH, 8), n1)
    gs = jnp.concatenate([_halves(gre_ref, rows), _halves(gim_ref, rows)], axis=0)
    x = jnp.dot(s2_ref[...], gs.astype(BF16), preferred_element_type=F32)
    return x[:n1], x[n1:]


def _spectra_body(uw_ref, s1_ref, s2_ref, k_ref, z_scr, gre_scr, gim_scr):
    n1 = s2_ref.shape[0] // 2
    n2 = s1_ref.shape[1] // 2
    for h in range(2):
        for b in range(n2 // 2):
            z_scr[h, pl.ds(b * PITCH, n1), :] = uw_ref[0, h, pl.ds(b * n1, n1), :]
    _fft_stage1(z_scr, s1_ref, gre_scr, gim_scr)

    def step(q, carry):
        xre, xim = _stage2_block(gre_scr, gim_scr, s2_ref, q)
        rows = pl.ds(pl.multiple_of(q * n1, 8), n1)
        k_ref[0, 0, rows, :] = xre[:, :LANES]
        k_ref[0, 1, rows, :] = xim[:, LANES:]
        return carry

    lax.fori_loop(0, n2, step, 0, unroll=FFT_MID_UNROLL)


def _spectra(uw, s1, s2):
    _, _, seq, width = uw.shape
    n1 = FFT_N1
    n2 = 2 * seq // n1
    const3 = lambda j, o: (0, 0, 0)
    return pl.pallas_call(
        _spectra_body,
        grid=(width // LANES, 2),
        in_specs=[pl.BlockSpec((1, 2, seq, LANES), lambda j, o: (o, 0, 0, j)),
                  pl.BlockSpec(s1.shape, const3),
                  pl.BlockSpec(s2.shape, lambda j, o: (0, 0))],
        out_specs=pl.BlockSpec((1, 2, 2 * seq, LANES), lambda j, o: (o, 0, 0, j)),
        out_shape=jax.ShapeDtypeStruct((2, 2, 2 * seq, width), F32),
        scratch_shapes=[pltpu.VMEM((2, (n2 // 2) * PITCH, LANES), F32),
                        pltpu.VMEM((2, n2 * PITCH, LANES), F32),
                        pltpu.VMEM((2, n2 * PITCH, LANES), F32)],
        compiler_params=_cparams(("arbitrary", "arbitrary"), 56),
        name="spectra",
    )(uw, s1, s2)


def _short_conv_block(p_ref, b, i, nblk, w_ref, bias_ref):
    n1 = FFT_N1
    start = pl.multiple_of(i * n1, n1)
    cur = p_ref[b, pl.ds(start, n1), :].astype(F32)
    before = p_ref[b, pl.ds(pl.multiple_of(jnp.maximum(start - 16, 0), 16), 16), :].astype(F32)
    after = p_ref[b, pl.ds(pl.multiple_of(jnp.minimum(start + n1, (nblk - 1) * n1), 16), 16), :].astype(F32)
    last_prev = before[15:16] * jnp.where(i > 0, 1.0, 0.0).astype(F32)
    first_next = after[0:1] * jnp.where(i < nblk - 1, 1.0, 0.0).astype(F32)
    row = lax.broadcasted_iota(jnp.int32, (n1, 1), 0)
    prev = jnp.where(row == 0, last_prev, pltpu.roll(cur, 1, 0))
    nxt = jnp.where(row == n1 - 1, first_next, pltpu.roll(cur, n1 - 1, 0))
    return prev * w_ref[0:1, :] + cur * w_ref[1:2, :] + nxt * w_ref[2:3, :] + bias_ref[...]


def _hyena_body(pv_ref, px1_ref, px2_ref, cwv_ref, cw1_ref, cw2_ref, cbv_ref, cb1_ref, cb2_ref,
                skip_ref, k_ref, s1_ref, s1i_ref, s2_ref, s2i_ref, o_ref,
                z_scr, y_scr, gre_scr, gim_scr):
    c = pl.program_id(1)
    g = pl.program_id(2)
    ngroups = pl.num_programs(2)
    n1 = FFT_N1
    n2 = s1_ref.shape[1] // 2
    nblk = n2 // 2
    per_group = n2 // ngroups

    @pl.when(g == 0)
    def _():
        @pl.when(c == 0)
        def _():
            def fill(i, carry):
                for b in range(2):
                    z_scr[b, pl.ds(pl.multiple_of(i * PITCH, 8), n1), :] = _short_conv_block(
                        pv_ref, b, i, nblk, cwv_ref, cbv_ref)
                return carry
            lax.fori_loop(0, nblk, fill, 0)

        _fft_stage1(z_scr, s1_ref, gre_scr, gim_scr)

    def mid(ql, carry):
        q = g * per_group + ql
        xre, xim = _stage2_block(gre_scr, gim_scr, s2_ref, q)
        krows = pl.ds(pl.multiple_of(ql * n1, 8), n1)
        kre = k_ref[0, 0, krows, :]
        kim = k_ref[0, 1, krows, :]
        kre = jnp.concatenate([kre, kre], axis=1)
        kim = jnp.concatenate([kim, kim], axis=1)
        ys = jnp.concatenate([xre * kre - xim * kim, xre * kim + xim * kre], axis=0)
        hh = jnp.dot(s2i_ref[...], ys.astype(BF16), preferred_element_type=F32)
        rows = pl.ds(pl.multiple_of(q * PITCH, 8), n1)
        for b in range(2):
            cols = slice(b * LANES, (b + 1) * LANES)
            gre_scr[b, rows, :] = hh[:n1, cols]
            gim_scr[b, rows, :] = hh[n1:, cols]
        return carry

    lax.fori_loop(0, per_group, mid, 0, unroll=FFT_MID_UNROLL)

    @pl.when(g == ngroups - 1)
    def _():
        _fft_stage1_inv(gre_scr, gim_scr, s1i_ref, y_scr)

        def post(px_ref, cw_ref, cb_ref, order, store):
            def blk(i, carry):
                rows = pl.ds(pl.multiple_of(i * PITCH, 8), n1)
                for b in range(2):
                    zb = z_scr[b, rows, :]
                    gate = _short_conv_block(px_ref, b, i, nblk, cw_ref, cb_ref)
                    store(b, i, rows, gate * (y_scr[b, rows, :] + zb * skip_ref[order:order + 1, :]))
                return carry
            lax.fori_loop(0, nblk, blk, 0)

        @pl.when(c == 0)
        def _():
            def store(b, i, rows, val):
                z_scr[b, rows, :] = val
            post(px1_ref, cw1_ref, cb1_ref, 0, store)

        @pl.when(c == 1)
        def _():
            def store(b, i, rows, val):
                o_ref[b, pl.ds(pl.multiple_of(i * n1, n1), n1), :] = val.astype(o_ref.dtype)
            post(px2_ref, cw2_ref, cb2_ref, 1, store)


def _hyena(pg3, proj_col0, conv_w, conv_b, skip, kspec, s1, s1i, s2, s2i, width):
    bsz, seq, _ = pg3.shape
    assert bsz == 2
    n1 = FFT_N1
    n2 = 2 * seq // n1
    nct = width // LANES
    ngroups = 4
    krows = (n2 // ngroups) * n1
    col = lambda off: (lambda j, c, g: (0, 0, proj_col0 // LANES + off * nct + j))
    cw = lambda off: (lambda j, c, g: (0, off * nct + j))
    const3 = lambda j, c, g: (0, 0, 0)
    const2 = lambda j, c, g: (0, 0)
    return pl.pallas_call(
        _hyena_body,
        grid=(nct, 2, ngroups),
        in_specs=[pl.BlockSpec((2, seq, LANES), col(0)),
                  pl.BlockSpec((2, seq, LANES), col(1)),
                  pl.BlockSpec((2, seq, LANES), col(2)),
                  pl.BlockSpec((3, LANES), cw(0)),
                  pl.BlockSpec((3, LANES), cw(1)),
                  pl.BlockSpec((3, LANES), cw(2)),
                  pl.BlockSpec((1, LANES), cw(0)),
                  pl.BlockSpec((1, LANES), cw(1)),
                  pl.BlockSpec((1, LANES), cw(2)),
                  pl.BlockSpec((2, LANES), lambda j, c, g: (0, j)),
                  pl.BlockSpec((1, 2, krows, LANES), lambda j, c, g: (c, 0, g, j)),
                  pl.BlockSpec(s1.shape, const3),
                  pl.BlockSpec(s1i.shape, const3),
                  pl.BlockSpec(s2.shape, const2),
                  pl.BlockSpec(s2i.shape, const2)],
        out_specs=pl.BlockSpec((2, seq, LANES), lambda j, c, g: (0, 0, j)),
        out_shape=jax.ShapeDtypeStruct((2, seq, width), BF16),
        scratch_shapes=[pltpu.VMEM((2, (n2 // 2) * PITCH, LANES), F32),
                        pltpu.VMEM((2, (n2 // 2) * PITCH, LANES), F32),
                        pltpu.VMEM((2, n2 * PITCH, LANES), F32),
                        pltpu.VMEM((2, n2 * PITCH, LANES), F32)],
        compiler_params=_cparams(("arbitrary", "arbitrary", "arbitrary"), 56),
        name="hyena",
    )(pg3, pg3, pg3, conv_w, conv_w, conv_w, conv_b, conv_b, conv_b, skip, kspec, s1, s1i, s2, s2i)


ATTN_QBLK = 128
ATTN_UNROLL = 4


def _attention_body(pos_ref, freq_ref, sign_ref, *refs):
    qkv_refs = refs[:9]
    o_ref = refs[9]
    cos_scr, sin_scr, q_scr, k_scr, v_scr, og_scr, lse_scr = refs[10:]
    seq = q_scr.shape[0]
    chunk = 512
    nchunks = seq // chunk

    @pl.when(pl.program_id(1) == 0)
    def _():
        def trig(i, carry):
            rows = pl.ds(pl.multiple_of(i * chunk, chunk), chunk)
            ang = pos_ref[0, rows, :].astype(F32) * freq_ref[...]
            cos_scr[rows, :] = jnp.cos(ang)
            sin_scr[rows, :] = jnp.sin(ang) * sign_ref[...]
            return carry
        lax.fori_loop(0, nchunks, trig, 0)

    def rotate(src_ref, dst_ref):
        def body(i, carry):
            rows = pl.ds(pl.multiple_of(i * chunk, chunk), chunk)
            t = src_ref[0, rows, :].astype(F32)
            lane = lax.broadcasted_iota(jnp.int32, t.shape, 1)
            partner = jnp.where(lane < ROT_DIM // 2,
                                pltpu.roll(t, LANES - ROT_DIM // 2, 1), pltpu.roll(t, ROT_DIM // 2, 1))
            dst_ref[rows, :] = t * cos_scr[rows, :] + partner * sin_scr[rows, :]
            return carry
        lax.fori_loop(0, nchunks, body, 0)

    def widen(src_ref, dst_ref):
        def body(i, carry):
            rows = pl.ds(pl.multiple_of(i * chunk, chunk), chunk)
            dst_ref[rows, :] = src_ref[0, rows, :].astype(F32)
            return carry
        lax.fori_loop(0, nchunks, body, 0)

    scale = HEAD_DIM ** -0.5
    for gi, (window, dil) in enumerate(ATTN_GROUPS):
        rotate(qkv_refs[3 * gi], q_scr)
        rotate(qkv_refs[3 * gi + 1], k_scr)
        widen(qkv_refs[3 * gi + 2], v_scr)
        n = seq // dil
        half = window // (2 * dil)
        tk = min(n, ATTN_QBLK + 2 * half)
        blocks_per_res = n // ATTN_QBLK

        def block(u, carry, dil=dil, n=n, half=half, tk=tk, blocks_per_res=blocks_per_res, gi=gi):
            r = u // blocks_per_res
            m = u % blocks_per_res
            q0 = m * ATTN_QBLK
            k0 = jnp.clip(q0 - half, 0, n - tk)
            qrows = pl.ds(r + dil * q0, ATTN_QBLK, stride=dil)
            krows = pl.ds(r + dil * k0, tk, stride=dil)
            qb = q_scr[qrows, :].astype(BF16)
            kb = k_scr[krows, :].astype(BF16)
            vb = v_scr[krows, :].astype(BF16)
            s = lax.dot_general(qb, kb, (((1,), (1,)), ((), ())), preferred_element_type=F32) * scale
            qi = q0 + lax.broadcasted_iota(jnp.int32, (ATTN_QBLK, tk), 0)
            kj = k0 + lax.broadcasted_iota(jnp.int32, (ATTN_QBLK, tk), 1)
            s = jnp.where(jnp.abs(qi - kj) <= half, s, NEG_INF)
            mx = jnp.max(s, axis=-1, keepdims=True)
            p = jnp.exp(s - mx)
            l = jnp.sum(p, axis=-1, keepdims=True)
            o = jnp.dot(p.astype(BF16), vb, preferred_element_type=F32) / l
            og_scr[gi, qrows, :] = o
            lse_scr[gi, qrows, :] = jnp.broadcast_to(mx + jnp.log(l), (ATTN_QBLK, LANES))
            return carry

        lax.fori_loop(0, seq // ATTN_QBLK, block, 0, unroll=ATTN_UNROLL)

    def merge(i, carry):
        rows = pl.ds(pl.multiple_of(i * chunk, chunk), chunk)
        lses = [lse_scr[gi, rows, :] for gi in range(len(ATTN_GROUPS))]
        mx = functools.reduce(jnp.maximum, lses)
        ws = [jnp.exp(v - mx) for v in lses]
        den = functools.reduce(lambda a, b: a + b, ws)
        num = functools.reduce(lambda a, b: a + b,
                               [w * og_scr[gi, rows, :] for gi, w in enumerate(ws)])
        o_ref[0, rows, :] = (num / den).astype(o_ref.dtype)
        return carry

    lax.fori_loop(0, nchunks, merge, 0)


def _attention(pg3, pos3, freq_row, sign_row, qkv_col0):
    bsz, seq, _ = pg3.shape
    ng = len(ATTN_GROUPS)
    in_specs = [pl.BlockSpec((1, seq, 1), lambda b, h: (b, 0, 0)),
                pl.BlockSpec((1, LANES), lambda b, h: (0, 0)),
                pl.BlockSpec((1, LANES), lambda b, h: (0, 0))]
    for gi in range(ng):
        for which in range(3):
            base = qkv_col0 // LANES + which * N_ATTN_HEADS + gi * HEADS_PER_GROUP
            in_specs.append(pl.BlockSpec((1, seq, LANES), lambda b, h, base=base: (b, 0, base + h)))
    return pl.pallas_call(
        _attention_body,
        grid=(bsz, HEADS_PER_GROUP),
        in_specs=in_specs,
        out_specs=pl.BlockSpec((1, seq, LANES), lambda b, h: (b, 0, h)),
        out_shape=jax.ShapeDtypeStruct((bsz, seq, HEADS_PER_GROUP * HEAD_DIM), BF16),
        scratch_shapes=[pltpu.VMEM((seq, LANES), F32), pltpu.VMEM((seq, LANES), F32),
                        pltpu.VMEM((seq, LANES), F32), pltpu.VMEM((seq, LANES), F32),
                        pltpu.VMEM((seq, LANES), F32),
                        pltpu.VMEM((ng, seq, LANES), F32), pltpu.VMEM((ng, seq, LANES), F32)],
        compiler_params=_cparams(("arbitrary", "arbitrary"), 56),
        name="attention",
    )(pos3, freq_row, sign_row, *([pg3] * (3 * ng)))


def _mix_out_body(yhy_ref, yat_ref, ghy_ref, gat_ref, x_ref, mod_ref, gpost_ref, gpre_ref,
                  whyo_ref, wato_ref, wout_ref, wrhi_ref, wrlo_ref, br_ref, x1_ref, h2_ref, lg_ref):
    a = jnp.dot(yhy_ref[...], whyo_ref[...], preferred_element_type=F32)
    b = jnp.dot(yat_ref[...], wato_ref[...], preferred_element_type=F32)
    merged = ghy_ref[...].astype(F32) * a + gat_ref[...].astype(F32) * b
    y = jnp.dot(merged.astype(BF16), wout_ref[...], preferred_element_type=F32)
    x1 = x_ref[...] + mod_ref[0, 2:3, :] * _rms(y, gpost_ref[...])
    x1_ref[...] = x1
    h2 = _rms(x1, gpre_ref[...]) * (1.0 + mod_ref[0, 4:5, :]) + mod_ref[0, 3:4, :]
    h2_ref[...] = h2
    h2_hi = h2.astype(BF16)
    h2_lo = (h2 - h2_hi.astype(F32)).astype(BF16)
    w_hi = wrhi_ref[...]
    w_lo = wrlo_ref[...]
    lg_ref[...] = (jnp.dot(h2_hi, w_hi, preferred_element_type=F32)
                   + jnp.dot(h2_lo, w_hi, preferred_element_type=F32)
                   + jnp.dot(h2_hi, w_lo, preferred_element_type=F32)
                   + jnp.dot(h2_lo, w_lo, preferred_element_type=F32)) + br_ref[...]


def _mix_out(y_hy, y_at, pg, x2d, mod, g_post, g_pre, w_hy_o, w_at_o, w_out, w_r_hi, w_r_lo, b_r, seq):
    t, d = x2d.shape
    tm = 256
    per_batch = seq // tm
    gblk = 0
    const = lambda i: (0, 0)
    return pl.pallas_call(
        _mix_out_body,
        grid=(t // tm,),
        in_specs=[pl.BlockSpec((tm, y_hy.shape[1]), lambda i: (i, 0)),
                  pl.BlockSpec((tm, y_at.shape[1]), lambda i: (i, 0)),
                  pl.BlockSpec((tm, d), lambda i: (i, gblk)),
                  pl.BlockSpec((tm, d), lambda i: (i, gblk + 1)),
                  pl.BlockSpec((tm, d), lambda i: (i, 0)),
                  pl.BlockSpec((1, 6, d), lambda i: (i // per_batch, 0, 0)),
                  pl.BlockSpec((1, d), const),
                  pl.BlockSpec((1, d), const),
                  pl.BlockSpec(w_hy_o.shape, const),
                  pl.BlockSpec(w_at_o.shape, const),
                  pl.BlockSpec(w_out.shape, const),
                  pl.BlockSpec(w_r_hi.shape, const),
                  pl.BlockSpec(w_r_lo.shape, const),
                  pl.BlockSpec((1, LANES), const)],
        out_specs=[pl.BlockSpec((tm, d), lambda i: (i, 0)),
                   pl.BlockSpec((tm, d), lambda i: (i, 0)),
                   pl.BlockSpec((tm, LANES), lambda i: (i, 0))],
        out_shape=[jax.ShapeDtypeStruct((t, d), F32),
                   jax.ShapeDtypeStruct((t, d), F32),
                   jax.ShapeDtypeStruct((t, LANES), F32)],
        compiler_params=_cparams(("arbitrary",), 56),
        name="mix_out",
    )(y_hy, y_at, pg, pg, x2d, mod, g_post, g_pre, w_hy_o, w_at_o, w_out, w_r_hi, w_r_lo, b_r)


def _route_body(lg_ref, eid_ref, gate_ref):
    lg = lg_ref[...]
    lane = lax.broadcasted_iota(jnp.int32, lg.shape, 1)
    big = jnp.int32(1 << 20)

    def first_argmax(vals, mask):
        v = jnp.where(mask, vals, -jnp.inf)
        mx = jnp.max(v, axis=-1, keepdims=True)
        idx = jnp.min(jnp.where(mask & (v == mx), lane, big), axis=-1, keepdims=True)
        return mx, idx

    gmask = lane < N_EXPERT_GROUPS
    gmax, gidx = first_argmax(lg, gmask)
    gval = 1.0 / jnp.sum(jnp.where(gmask, jnp.exp(lg - gmax), 0.0), axis=-1, keepdims=True)
    lo = N_EXPERT_GROUPS + gidx * EXPERTS_PER_GROUP
    emask = (lane >= lo) & (lane < lo + EXPERTS_PER_GROUP)
    v1, i1 = first_argmax(lg, emask)
    v2, i2 = first_argmax(lg, emask & (lane != i1))
    e2 = jnp.exp(v2 - v1)
    p1 = 1.0 / (1.0 + e2)
    p2 = e2 / (1.0 + e2)
    eid = jnp.where(lane == 0, i1, i2) - N_EXPERT_GROUPS
    gate = gval * jnp.where(lane == 0, p1, p2)
    eid_ref[...] = eid[:, :TOP_K]
    gate_ref[...] = gate[:, :TOP_K]


def _route(logits):
    t = logits.shape[0]
    tm = 512
    return pl.pallas_call(
        _route_body,
        grid=(t // tm,),
        in_specs=[pl.BlockSpec((tm, LANES), lambda i: (i, 0))],
        out_specs=[pl.BlockSpec((tm, TOP_K), lambda i: (i, 0)),
                   pl.BlockSpec((tm, TOP_K), lambda i: (i, 0))],
        out_shape=[jax.ShapeDtypeStruct((t, TOP_K), jnp.int32),
                   jax.ShapeDtypeStruct((t, TOP_K), F32)],
        compiler_params=_cparams(("arbitrary",), 32),
        name="route",
    )(logits)


def _rank_body(e_ref, rank_ref, cnt_ref, carry_scr):
    i = pl.program_id(0)
    r = e_ref.shape[0]

    @pl.when(i == 0)
    def _():
        carry_scr[...] = jnp.zeros_like(carry_scr)

    lane = lax.broadcasted_iota(jnp.int32, (r, LANES), 1)
    onehot = (lane == e_ref[...]).astype(F32)
    tri = (lax.broadcasted_iota(jnp.int32, (r, r), 1)
           < lax.broadcasted_iota(jnp.int32, (r, r), 0)).astype(BF16)
    before = jnp.dot(tri, onehot.astype(BF16), preferred_element_type=F32) + carry_scr[0:1, :]
    rank_ref[...] = jnp.sum(onehot * before, axis=-1, keepdims=True).astype(jnp.int32)
    total = carry_scr[0:1, :] + jnp.sum(onehot, axis=0, keepdims=True)
    carry_scr[...] = jnp.broadcast_to(total, carry_scr.shape)
    cnt_ref[...] = jnp.broadcast_to(total, cnt_ref.shape).astype(jnp.int32)


def _rank(flat_e):
    a = flat_e.shape[0]
    r = 512
    return pl.pallas_call(
        _rank_body,
        grid=(a // r,),
        in_specs=[pl.BlockSpec((r, 1), lambda i: (i, 0))],
        out_specs=[pl.BlockSpec((r, 1), lambda i: (i, 0)),
                   pl.BlockSpec((8, LANES), lambda i: (0, 0))],
        out_shape=[jax.ShapeDtypeStruct((a, 1), jnp.int32),
                   jax.ShapeDtypeStruct((8, LANES), jnp.int32)],
        scratch_shapes=[pltpu.VMEM((8, LANES), F32)],
        compiler_params=_cparams(("arbitrary",), 32),
        name="rank",
    )(flat_e)


def _dest_body(e_ref, rank_ref, ps_ref, o_ref):
    lane = lax.broadcasted_iota(jnp.int32, (e_ref.shape[0], LANES), 1)
    first = jnp.sum(jnp.where(lane == e_ref[...], ps_ref[...], 0.0), axis=-1, keepdims=True)
    o_ref[...] = first.astype(jnp.int32) + rank_ref[...]


def _dest(flat_e, rank, pstarts_row):
    a = flat_e.shape[0]
    r = 2048
    return pl.pallas_call(
        _dest_body,
        grid=(a // r,),
        in_specs=[pl.BlockSpec((r, 1), lambda i: (i, 0)),
                  pl.BlockSpec((r, 1), lambda i: (i, 0)),
                  pl.BlockSpec((1, LANES), lambda i: (0, 0))],
        out_specs=pl.BlockSpec((r, 1), lambda i: (i, 0)),
        out_shape=jax.ShapeDtypeStruct((a, 1), jnp.int32),
        compiler_params=_cparams(("arbitrary",), 32),
        name="dest",
    )(flat_e, rank, pstarts_row)


DMA_UNROLL = 8


DISPATCH_CHUNK = 256


def _dispatch_body(src_ref, h_hbm, x_hbm, sem):
    nchunks = x_hbm.shape[0] // DISPATCH_CHUNK

    def chunk(c, act):
        base = c * DISPATCH_CHUNK

        def body(r, carry):
            act(pltpu.make_async_copy(h_hbm.at[pl.ds(src_ref[base + r], 1)],
                                      x_hbm.at[pl.ds(base + r, 1)], sem.at[c % 2]))
            return carry

        lax.fori_loop(0, DISPATCH_CHUNK, body, 0, unroll=DMA_UNROLL)

    def step(c, carry):
        chunk(c, lambda cp: cp.start())

        @pl.when(c > 0)
        def _():
            chunk(c - 1, lambda cp: cp.wait())

        return carry

    lax.fori_loop(0, nchunks, step, 0)
    chunk(nchunks - 1, lambda cp: cp.wait())


def _dispatch(src_tok, h2):
    grid_spec = pltpu.PrefetchScalarGridSpec(
        num_scalar_prefetch=1,
        grid=(1,),
        in_specs=[pl.BlockSpec(memory_space=pl.ANY)],
        out_specs=pl.BlockSpec(memory_space=pl.ANY),
        scratch_shapes=[pltpu.SemaphoreType.DMA((2,))],
    )
    return pl.pallas_call(
        _dispatch_body,
        grid_spec=grid_spec,
        out_shape=jax.ShapeDtypeStruct((src_tok.shape[0], h2.shape[1]), h2.dtype),
        compiler_params=_cparams(("arbitrary",), 32),
        name="dispatch",
    )(src_tok, h2)


def _experts_body(ord_ref, elist_ref, cnt_ref, x_ref, w1_hbm, w3_hbm, w2_hbm, y_ref,
                  w1_scr, w3_scr, w2_scr, wsem):
    i = pl.program_id(0)
    nused = cnt_ref[0]
    nexp = cnt_ref[1]
    k = ord_ref[i]
    first = (i == 0) | (k != ord_ref[jnp.maximum(i - 1, 0)])

    def weights(kk, act):
        e = elist_ref[kk]
        slot = kk % 2
        act(pltpu.make_async_copy(w1_hbm.at[e], w1_scr.at[slot], wsem.at[slot, 0]))
        act(pltpu.make_async_copy(w3_hbm.at[e], w3_scr.at[slot], wsem.at[slot, 1]))
        act(pltpu.make_async_copy(w2_hbm.at[e], w2_scr.at[slot], wsem.at[slot, 2]))

    @pl.when(i == 0)
    def _():
        weights(k, lambda cp: cp.start())

    @pl.when((i < nused) & first & (k + 1 < nexp))
    def _():
        weights(k + 1, lambda cp: cp.start())

    @pl.when((i < nused) & first)
    def _():
        weights(k, lambda cp: cp.wait())

    @pl.when(i < nused)
    def _():
        wslot = k % 2
        xb = x_ref[...].astype(BF16)
        a = jnp.dot(xb, w1_scr[wslot].astype(BF16), preferred_element_type=F32)
        b = jnp.dot(xb, w3_scr[wslot].astype(BF16), preferred_element_type=F32)
        act = (a * jax.nn.sigmoid(a) * b).astype(BF16)
        y_ref[...] = jnp.dot(act, w2_scr[wslot].astype(BF16), preferred_element_type=F32)

    @pl.when(i >= nused)
    def _():
        y_ref[...] = jnp.zeros_like(y_ref)


def _experts(block_ord, elist, counts2, xbuf, w1, w3, w2):
    nblocks = block_ord.shape[0]
    d = xbuf.shape[1]
    ff = w1.shape[2]
    rb = EXPERT_ROW_BLOCK
    grid_spec = pltpu.PrefetchScalarGridSpec(
        num_scalar_prefetch=3,
        grid=(nblocks,),
        in_specs=[pl.BlockSpec((rb, d), lambda i, od, el, cnt: (jnp.minimum(i, cnt[0] - 1), 0))]
        + [pl.BlockSpec(memory_space=pl.ANY)] * 3,
        out_specs=pl.BlockSpec((rb, d), lambda i, *_: (i, 0)),
        scratch_shapes=[pltpu.VMEM((2, d, ff), F32), pltpu.VMEM((2, d, ff), F32),
                        pltpu.VMEM((2, ff, d), F32), pltpu.SemaphoreType.DMA((2, 3))],
    )
    return pl.pallas_call(
        _experts_body,
        grid_spec=grid_spec,
        out_shape=jax.ShapeDtypeStruct((nblocks * rb, d), F32),
        compiler_params=_cparams(("arbitrary",), 56),
        name="experts",
    )(block_ord, elist, counts2, xbuf, w1, w3, w2)


def _combine_body(dest_ref, y_hbm, gate_ref, x1_ref, mod_ref, g_ref, o_ref, buf_scr, sem):
    i = pl.program_id(0)
    tm = x1_ref.shape[0]

    def gather(step, act):
        slot = step % 2
        base = step * tm * TOP_K

        def body(r, carry):
            for k in range(TOP_K):
                act(pltpu.make_async_copy(y_hbm.at[pl.ds(dest_ref[base + TOP_K * r + k], 1)],
                                          buf_scr.at[slot, k, pl.ds(r, 1)], sem.at[slot]))
            return carry

        lax.fori_loop(0, tm, body, 0, unroll=DMA_UNROLL // TOP_K)

    @pl.when(i == 0)
    def _():
        gather(i, lambda cp: cp.start())

    @pl.when(i + 1 < pl.num_programs(0))
    def _():
        gather(i + 1, lambda cp: cp.start())

    gather(i, lambda cp: cp.wait())
    slot = i % 2
    gate = gate_ref[...]
    y = buf_scr[slot, 0] * gate[:, 0:1] + buf_scr[slot, 1] * gate[:, 1:2]
    o_ref[...] = x1_ref[...] + mod_ref[0, 5:6, :] * _rms(y, g_ref[...])


def _combine(dest, ybuf, gate, x1, mod, g_post, seq):
    t, d = x1.shape
    tm = 128
    per_batch = seq // tm
    grid_spec = pltpu.PrefetchScalarGridSpec(
        num_scalar_prefetch=1,
        grid=(t // tm,),
        in_specs=[pl.BlockSpec(memory_space=pl.ANY),
                  pl.BlockSpec((tm, TOP_K), lambda i, dst: (i, 0)),
                  pl.BlockSpec((tm, d), lambda i, dst: (i, 0)),
                  pl.BlockSpec((1, 6, d), lambda i, dst: (i // per_batch, 0, 0)),
                  pl.BlockSpec((1, d), lambda i, dst: (0, 0))],
        out_specs=pl.BlockSpec((tm, d), lambda i, dst: (i, 0)),
        scratch_shapes=[pltpu.VMEM((2, TOP_K, tm, d), F32), pltpu.SemaphoreType.DMA((2,))],
    )
    return pl.pallas_call(
        _combine_body,
        grid_spec=grid_spec,
        out_shape=jax.ShapeDtypeStruct((t, d), F32),
        compiler_params=_cparams(("arbitrary",), 32),
        name="combine",
    )(dest, ybuf, gate, x1, mod, g_post)


def _layer(x, c, positions, w_ada, b_ada, g_mix_pre, g_mix_post, g_ffn_pre, g_ffn_post,
           w_in, conv_w, conv_b, filt_w1, filt_b1, filt_w2, filt_b2, filt_w3, filt_freq,
           hyena_skip, w_branch_gate, b_branch_gate, w_hy_o, w_at_o, w_out,
           w_group, b_group, w_expert, b_expert, w1_exp, w3_exp, w2_exp):
    bsz, seq, d = x.shape
    t = bsz * seq
    width = hyena_skip.shape[1]
    row = lambda v: v.reshape(1, -1)

    c_pad = jnp.pad(c, ((0, 8 - bsz), (0, 0)))
    mod = _adaln(c_pad, w_ada, row(b_ada))[:bsz].reshape(bsz, 6, d)

    x2d = x.reshape(t, d)
    n_gate = w_branch_gate.shape[1]
    n_cat = -(-(n_gate + w_in.shape[1]) // IN_PROJ_TN) * IN_PROJ_TN
    n_zero = n_cat - n_gate - w_in.shape[1]
    w_cat = jnp.concatenate([w_branch_gate.astype(BF16), w_in.astype(BF16),
                             jnp.zeros((d, n_zero), BF16)], axis=1)
    b_cat = jnp.concatenate([b_branch_gate, jnp.zeros((n_cat - n_gate,), F32)]).reshape(1, -1)
    pg = _in_proj(x2d, mod, row(g_mix_pre), w_cat, b_cat, n_gate, seq)
    pg3 = pg.reshape(bsz, seq, -1)

    hidden = filt_w2.shape[0]
    bands = np.zeros((1, LANES), np.float32)
    band_vals = np.linspace(1e-4, FILTER_BANDS - 1, FILTER_BANDS, dtype=np.float32)
    bands[0, 1:1 + FILTER_BANDS] = band_vals
    bands[0, 1 + FILTER_BANDS:1 + 2 * FILTER_BANDS] = band_vals
    w1p = jnp.pad(filt_w1, ((0, LANES - filt_w1.shape[0]), (0, 0)))
    max_decay = math.log(DECAY_TARGET) / FAST_DECAY_PCT
    min_decay = math.log(DECAY_TARGET) / SLOW_DECAY_PCT
    deltas = jnp.abs(jnp.linspace(min_decay, max_decay, width, dtype=F32)).reshape(1, -1)
    uw = _filters(seq, width, jnp.asarray(bands), w1p, row(filt_b1), filt_w2, row(filt_b2),
                  row(filt_freq), deltas, filt_w3)
    s1, s1i, s2, s2i = _fft_tables(FFT_N1, 2 * seq // FFT_N1)
    kspec = _spectra(uw, s1, s2)
    y_hy = _hyena(pg3, n_gate, conv_w, row(conv_b), hyena_skip, kspec, s1, s1i, s2, s2i, width)

    half = ROT_DIM // 2
    inv_freq = np.power(ROPE_THETA, -2.0 * np.arange(half, dtype=np.float32) / ROT_DIM).astype(np.float32)
    freq_row = np.zeros((1, LANES), np.float32)
    freq_row[0, :half] = inv_freq
    freq_row[0, half:ROT_DIM] = inv_freq
    sign_row = np.zeros((1, LANES), np.float32)
    sign_row[0, :half] = -1.0
    sign_row[0, half:ROT_DIM] = 1.0
    y_at = _attention(pg3, positions.reshape(bsz, seq, 1), jnp.asarray(freq_row), jnp.asarray(sign_row),
                      n_gate + 3 * width)

    w_r = jnp.concatenate([w_group, jnp.transpose(w_expert, (1, 0, 2)).reshape(d, N_EXPERTS)], axis=1)
    w_r = jnp.pad(w_r, ((0, 0), (0, LANES - w_r.shape[1])))
    w_r_hi = w_r.astype(BF16)
    w_r_lo = (w_r - w_r_hi.astype(F32)).astype(BF16)
    b_r = jnp.pad(jnp.concatenate([b_group, b_expert.reshape(-1)]), (0, LANES - N_EXPERT_GROUPS - N_EXPERTS))
    x1, h2, logits = _mix_out(y_hy.reshape(t, width), y_at.reshape(t, -1), pg, x2d, mod,
                              row(g_mix_post), row(g_ffn_pre), w_hy_o.astype(BF16), w_at_o.astype(BF16),
                              w_out.astype(BF16), w_r_hi, w_r_lo, b_r.reshape(1, -1), seq)

    eid, gate = _route(logits)
    flat_e = eid.reshape(t * TOP_K, 1)
    rank, counts = _rank(flat_e)
    counts = counts[0, :N_EXPERTS]
    rb = EXPERT_ROW_BLOCK
    padded = (counts + rb - 1) // rb * rb
    pends = jnp.cumsum(padded)
    pstarts = pends - padded
    pstarts_row = jnp.pad(pstarts.astype(F32), (0, LANES - N_EXPERTS)).reshape(1, LANES)
    dest = _dest(flat_e, rank, pstarts_row)[:, 0]
    n_blocks = t * TOP_K // rb + N_EXPERTS
    block_start = jnp.arange(n_blocks, dtype=jnp.int32) * rb
    block_e = jnp.minimum(jnp.searchsorted(pends, block_start, side='right'), N_EXPERTS - 1).astype(jnp.int32)
    tok = jnp.arange(t * TOP_K, dtype=jnp.int32) // TOP_K
    src_tok = jnp.zeros((n_blocks * rb,), jnp.int32).at[dest].set(tok)
    has_rows = jnp.cumsum((counts > 0).astype(jnp.int32))
    elist = jnp.minimum(jnp.searchsorted(has_rows, jnp.arange(1, N_EXPERTS + 1, dtype=jnp.int32), side='left'),
                        N_EXPERTS - 1).astype(jnp.int32)
    block_ord = (has_rows - 1)[block_e].astype(jnp.int32)
    counts2 = jnp.stack([pends[-1] // rb, has_rows[-1]]).astype(jnp.int32)

    xbuf = _dispatch(src_tok, h2)
    ybuf = _experts(block_ord, elist, counts2, xbuf, w1_exp, w3_exp, w2_exp)
    out = _combine(dest, ybuf, gate, x1, mod, row(g_ffn_post), seq)
    return out.reshape(bsz, seq, d)


def kernel(x, c, positions, w_ada, b_ada, g_mix_pre, g_mix_post, g_ffn_pre, g_ffn_post, w_in, conv_w, conv_b, filt_w1, filt_b1, filt_w2, filt_b2, filt_w3, filt_freq, hyena_skip, w_branch_gate, b_branch_gate, w_hy_o, w_at_o, w_out, w_group, b_group, w_expert, b_expert, w1_exp, w3_exp, w2_exp):
    depth = w_ada.shape[0]
    for l in range(depth):
        x = _layer(x, c, positions, w_ada[l], b_ada[l], g_mix_pre[l], g_mix_post[l], g_ffn_pre[l],
                   g_ffn_post[l], w_in[l], conv_w[l], conv_b[l], filt_w1[l], filt_b1[l], filt_w2[l],
                   filt_b2[l], filt_w3[l], filt_freq[l], hyena_skip[l], w_branch_gate[l],
                   b_branch_gate[l], w_hy_o[l], w_at_o[l], w_out[l], w_group[l], b_group[l],
                   w_expert[l], b_expert[l], w1_exp[l], w3_exp[l], w2_exp[l])
    return x
```

```python
import functools
import math

import numpy as np
import jax
import jax.numpy as jnp
from jax import lax
from jax.experimental import pallas as pl
from jax.experimental.pallas import tpu as pltpu

F32 = jnp.float32
BF16 = jnp.bfloat16

LANES = 128
MIB = 1024 * 1024

RMS_EPS = 1e-6
NEG_INF = -1e30

HEAD_DIM = 128
ROT_DIM = HEAD_DIM // 4
ROPE_THETA = 500000.0
ATTN_GROUPS = ((128, 1), (512, 4), (2048, 16))
HEADS_PER_GROUP = 4
N_ATTN_HEADS = HEADS_PER_GROUP * len(ATTN_GROUPS)

FILTER_BANDS = 16
DECAY_TARGET = 1e-2
FAST_DECAY_PCT = 0.3
SLOW_DECAY_PCT = 1.5

N_EXPERT_GROUPS = 8
EXPERTS_PER_GROUP = 8
N_EXPERTS = N_EXPERT_GROUPS * EXPERTS_PER_GROUP
TOP_K = 2
EXPERT_ROW_BLOCK = 128

FFT_N1 = 128
PITCH = FFT_N1 + 8
FFT_UNROLL = 16
FFT_MID_UNROLL = 8


def _cparams(sem, vmem_mib):
    return pltpu.CompilerParams(dimension_semantics=sem, vmem_limit_bytes=vmem_mib * MIB)


def _rms(x, g):
    return x * lax.rsqrt(jnp.mean(x * x, axis=-1, keepdims=True) + RMS_EPS) * g


def _adaln_body(c_ref, w_ref, b_ref, o_ref):
    c = c_ref[...]
    cond = c * jax.nn.sigmoid(c)
    o_ref[...] = jnp.dot(cond.astype(BF16), w_ref[...].astype(BF16),
                         preferred_element_type=F32) + b_ref[...]


def _adaln(c_pad, w_ada, b_ada):
    rows, d = c_pad.shape
    n = w_ada.shape[1]
    tn = 1024
    return pl.pallas_call(
        _adaln_body,
        grid=(n // tn,),
        in_specs=[pl.BlockSpec((rows, d), lambda j: (0, 0)),
                  pl.BlockSpec((d, tn), lambda j: (0, j)),
                  pl.BlockSpec((1, tn), lambda j: (0, j))],
        out_specs=pl.BlockSpec((rows, tn), lambda j: (0, j)),
        out_shape=jax.ShapeDtypeStruct((rows, n), F32),
        compiler_params=_cparams(("arbitrary",), 40),
        name="adaln",
    )(c_pad, w_ada, b_ada)


IN_PROJ_TM = 1024
IN_PROJ_TN = 1024


def _in_proj_body(n_gate, x_ref, mod_ref, g_ref, w_ref, b_ref, o_ref, h_scr):
    j = pl.program_id(1)

    @pl.when(j == 0)
    def _():
        x = x_ref[...]
        h = _rms(x, g_ref[...]) * (1.0 + mod_ref[0, 1:2, :]) + mod_ref[0, 0:1, :]
        h_scr[...] = h.astype(BF16)

    acc = jnp.dot(h_scr[...], w_ref[...], preferred_element_type=F32) + b_ref[...]

    @pl.when(j < n_gate)
    def _():
        o_ref[...] = jax.nn.sigmoid(acc).astype(o_ref.dtype)

    @pl.when(j >= n_gate)
    def _():
        o_ref[...] = acc.astype(o_ref.dtype)


def _in_proj(x2d, mod, g_pre, w_cat, b_cat, n_gate_cols, seq):
    t, d = x2d.shape
    n = w_cat.shape[1]
    tm, tn = IN_PROJ_TM, IN_PROJ_TN
    per_batch = seq // tm
    return pl.pallas_call(
        functools.partial(_in_proj_body, n_gate_cols // tn),
        grid=(t // tm, n // tn),
        in_specs=[pl.BlockSpec((tm, d), lambda i, j: (i, 0)),
                  pl.BlockSpec((1, 6, d), lambda i, j: (i // per_batch, 0, 0)),
                  pl.BlockSpec((1, d), lambda i, j: (0, 0)),
                  pl.BlockSpec((d, tn), lambda i, j: (0, j)),
                  pl.BlockSpec((1, tn), lambda i, j: (0, j))],
        out_specs=pl.BlockSpec((tm, tn), lambda i, j: (i, j)),
        out_shape=jax.ShapeDtypeStruct((t, n), BF16),
        scratch_shapes=[pltpu.VMEM((tm, d), BF16)],
        compiler_params=_cparams(("arbitrary", "arbitrary"), 56),
        name="in_proj",
    )(x2d, mod, g_pre, w_cat, b_cat)


def _filters_body(seq, band_ref, w1_ref, b1_ref, w2_ref, b2_ref, fr_ref, dl_ref,
                  w3a_ref, w3b_ref, o_ref, hid_scr):
    i = pl.program_id(0)
    j = pl.program_id(1)
    tl = hid_scr.shape[0]
    row = (lax.broadcasted_iota(jnp.int32, (tl, 1), 0) + i * tl).astype(F32)

    @pl.when((j == 0) & (pl.program_id(2) == 0))
    def _():
        lane = lax.broadcasted_iota(jnp.int32, (tl, LANES), 1)
        t = row / (seq - 1.0)
        ang = band_ref[...] * (2.0 * math.pi * row / seq)
        feats = jnp.where(lane == 0, t,
                          jnp.where(lane <= FILTER_BANDS, jnp.cos(ang),
                                    jnp.where(lane <= 2 * FILTER_BANDS, -jnp.sin(ang), 0.0)))
        hi = lax.Precision.HIGHEST
        fr = fr_ref[...]
        hid = jnp.sin(fr * (jnp.dot(feats, w1_ref[...], precision=hi, preferred_element_type=F32)
                            + b1_ref[...]))
        hid = jnp.sin(fr * (jnp.dot(hid, w2_ref[...], precision=hi, preferred_element_type=F32)
                            + b2_ref[...]))
        hid_scr[...] = hid

    hi = lax.Precision.HIGHEST
    hid = hid_scr[...]
    decay = jnp.exp(-(row / (seq - 1.0)) * dl_ref[...])
    hf = jnp.dot(hid, w3a_ref[...], precision=hi, preferred_element_type=F32) * decay
    hb = jnp.dot(hid, w3b_ref[...], precision=hi, preferred_element_type=F32) * decay
    hb = jnp.where(row == 0.0, 0.0, hb)
    o_ref[0, 0] = hf + hb
    o_ref[0, 1] = hf - hb


def _filters(seq, width, bands, w1p, b1, w2, b2, freq, deltas, w3):
    tl, tc = 512, 256
    nct = width // tc
    hidden = w2.shape[0]
    const = lambda i, j, o: (0, 0)
    return pl.pallas_call(
        functools.partial(_filters_body, float(seq)),
        grid=(seq // tl, nct, 2),
        in_specs=[pl.BlockSpec((1, LANES), const),
                  pl.BlockSpec((LANES, hidden), const),
                  pl.BlockSpec((1, hidden), const),
                  pl.BlockSpec((hidden, hidden), const),
                  pl.BlockSpec((1, hidden), const),
                  pl.BlockSpec((1, hidden), const),
                  pl.BlockSpec((1, tc), lambda i, j, o: (0, j)),
                  pl.BlockSpec((hidden, tc), lambda i, j, o: (0, (2 * o) * nct + j)),
                  pl.BlockSpec((hidden, tc), lambda i, j, o: (0, (2 * o + 1) * nct + j))],
        out_specs=pl.BlockSpec((1, 2, tl, tc), lambda i, j, o: (o, 0, i, j)),
        out_shape=jax.ShapeDtypeStruct((2, 2, seq, width), F32),
        scratch_shapes=[pltpu.VMEM((tl, hidden), F32)],
        compiler_params=_cparams(("arbitrary", "arbitrary", "arbitrary"), 32),
        name="filters",
    )(bands, w1p, b1, w2, b2, freq, deltas, w3, w3)


def _fft_tables(n1, n2):
    n = n1 * n2
    q = np.arange(n2)[:, None]
    b = np.arange(n2 // 2)[None, :]
    a = np.arange(n1)[:, None, None]
    ang = -2.0 * np.pi * (a * q[None] / n + (q * b)[None] / n2)
    stage1 = np.concatenate([np.cos(ang), np.sin(ang)], axis=1)
    stage1_inv = np.transpose(stage1, (0, 2, 1)) / n
    p = np.arange(n1)
    ang2 = -2.0 * np.pi * np.outer(p, p) / n1
    fre, fim = np.cos(ang2), np.sin(ang2)
    stage2 = np.block([[fre, -fim], [fim, fre]])
    stage2_inv = np.block([[fre, fim], [-fim, fre]])
    as_bf16 = lambda m: jnp.asarray(m, dtype=F32).astype(BF16)
    return as_bf16(stage1), as_bf16(stage1_inv), as_bf16(stage2), as_bf16(stage2_inv)


def _halves(ref, rows):
    return jnp.concatenate([ref[0, rows, :], ref[1, rows, :]], axis=1)


def _fft_stage1(z_ref, s1_ref, gre_ref, gim_ref):
    n1, two_n2, n2h = s1_ref.shape
    n2 = two_n2 // 2

    def step(a, carry):
        zrows = _halves(z_ref, pl.ds(a, n2h, stride=PITCH))
        g = jnp.dot(s1_ref[a], zrows.astype(BF16), preferred_element_type=F32)
        for h in range(2):
            cols = slice(h * LANES, (h + 1) * LANES)
            gre_ref[h, pl.ds(a, n2, stride=PITCH), :] = g[:n2, cols]
            gim_ref[h, pl.ds(a, n2, stride=PITCH), :] = g[n2:, cols]
        return carry

    lax.fori_loop(0, n1, step, 0, unroll=FFT_UNROLL)


def _fft_stage1_inv(gre_ref, gim_ref, s1i_ref, y_ref):
    n1, n2h, two_n2 = s1i_ref.shape
    n2 = two_n2 // 2

    def step(a, carry):
        rows = pl.ds(a, n2, stride=PITCH)
        hs = jnp.concatenate([_halves(gre_ref, rows), _halves(gim_ref, rows)], axis=0)
        y = jnp.dot(s1i_ref[a], hs.astype(BF16), preferred_element_type=F32)
        for h in range(2):
            y_ref[h, pl.ds(a, n2h, stride=PITCH), :] = y[:, h * LANES:(h + 1) * LANES]
        return carry

    lax.fori_loop(0, n1, step, 0, unroll=FFT_UNROLL)


def _stage2_block(gre_ref, gim_ref, s2_ref, q):
    n1 = s2_ref.shape[0] // 2
    rows = pl.ds(pl.multiple_of(q * PITCH, 8), n1)
    gs = jnp.concatenate([_halves(gre_ref, rows), _halves(gim_ref, rows)], axis=0)
    x = jnp.dot(s2_ref[...], gs.astype(BF16), preferred_element_type=F32)
    return x[:n1], x[n1:]


def _spectra_body(uw_ref, s1_ref, s2_ref, k_ref, z_scr, gre_scr, gim_scr):
    n1 = s2_ref.shape[0] // 2
    n2 = s1_ref.shape[1] // 2
    for h in range(2):
        for b in range(n2 // 2):
            z_scr[h, pl.ds(b * PITCH, n1), :] = uw_ref[0, h, pl.ds(b * n1, n1), :]
    _fft_stage1(z_scr, s1_ref, gre_scr, gim_scr)

    def step(q, carry):
        xre, xim = _stage2_block(gre_scr, gim_scr, s2_ref, q)
        rows = pl.ds(pl.multiple_of(q * n1, 8), n1)
        k_ref[0, 0, rows, :] = xre[:, :LANES]
        k_ref[0, 1, rows, :] = xim[:, LANES:]
        return carry

    lax.fori_loop(0, n2, step, 0, unroll=FFT_MID_UNROLL)


def _spectra(uw, s1, s2):
    _, _, seq, width = uw.shape
    n1 = FFT_N1
    n2 = 2 * seq // n1
    const3 = lambda j, o: (0, 0, 0)
    return pl.pallas_call(
        _spectra_body,
        grid=(width // LANES, 2),
        in_specs=[pl.BlockSpec((1, 2, seq, LANES), lambda j, o: (o, 0, 0, j)),
                  pl.BlockSpec(s1.shape, const3),
                  pl.BlockSpec(s2.shape, lambda j, o: (0, 0))],
        out_specs=pl.BlockSpec((1, 2, 2 * seq, LANES), lambda j, o: (o, 0, 0, j)),
        out_shape=jax.ShapeDtypeStruct((2, 2, 2 * seq, width), F32),
        scratch_shapes=[pltpu.VMEM((2, (n2 // 2) * PITCH, LANES), F32),
                        pltpu.VMEM((2, n2 * PITCH, LANES), F32),
                        pltpu.VMEM((2, n2 * PITCH, LANES), F32)],
        compiler_params=_cparams(("arbitrary", "arbitrary"), 56),
        name="spectra",
    )(uw, s1, s2)


def _short_conv_block(p_ref, b, i, nblk, w_ref, bias_ref):
    n1 = FFT_N1
    start = pl.multiple_of(i * n1, n1)
    cur = p_ref[b, pl.ds(start, n1), :].astype(F32)
    before = p_ref[b, pl.ds(pl.multiple_of(jnp.maximum(start - 16, 0), 16), 16), :].astype(F32)
    after = p_ref[b, pl.ds(pl.multiple_of(jnp.minimum(start + n1, (nblk - 1) * n1), 16), 16), :].astype(F32)
    last_prev = before[15:16] * jnp.where(i > 0, 1.0, 0.0).astype(F32)
    first_next = after[0:1] * jnp.where(i < nblk - 1, 1.0, 0.0).astype(F32)
    row = lax.broadcasted_iota(jnp.int32, (n1, 1), 0)
    prev = jnp.where(row == 0, last_prev, pltpu.roll(cur, 1, 0))
    nxt = jnp.where(row == n1 - 1, first_next, pltpu.roll(cur, n1 - 1, 0))
    return prev * w_ref[0:1, :] + cur * w_ref[1:2, :] + nxt * w_ref[2:3, :] + bias_ref[...]


def _hyena_body(pv_ref, px1_ref, px2_ref, cwv_ref, cw1_ref, cw2_ref, cbv_ref, cb1_ref, cb2_ref,
                skip_ref, k_ref, s1_ref, s1i_ref, s2_ref, s2i_ref, o_ref,
                z_scr, y_scr, gre_scr, gim_scr):
    c = pl.program_id(1)
    g = pl.program_id(2)
    ngroups = pl.num_programs(2)
    n1 = FFT_N1
    n2 = s1_ref.shape[1] // 2
    nblk = n2 // 2
    per_group = n2 // ngroups

    @pl.when(g == 0)
    def _():
        @pl.when(c == 0)
        def _():
            def fill(i, carry):
                for b in range(2):
                    z_scr[b, pl.ds(pl.multiple_of(i * PITCH, 8), n1), :] = _short_conv_block(
                        pv_ref, b, i, nblk, cwv_ref, cbv_ref)
                return carry
            lax.fori_loop(0, nblk, fill, 0)

        _fft_stage1(z_scr, s1_ref, gre_scr, gim_scr)

    def mid(ql, carry):
        q = g * per_group + ql
        xre, xim = _stage2_block(gre_scr, gim_scr, s2_ref, q)
        krows = pl.ds(pl.multiple_of(ql * n1, 8), n1)
        kre = k_ref[0, 0, krows, :]
        kim = k_ref[0, 1, krows, :]
        kre = jnp.concatenate([kre, kre], axis=1)
        kim = jnp.concatenate([kim, kim], axis=1)
        ys = jnp.concatenate([xre * kre - xim * kim, xre * kim + xim * kre], axis=0)
        hh = jnp.dot(s2i_ref[...], ys.astype(BF16), preferred_element_type=F32)
        rows = pl.ds(pl.multiple_of(q * PITCH, 8), n1)
        for b in range(2):
            cols = slice(b * LANES, (b + 1) * LANES)
            gre_scr[b, rows, :] = hh[:n1, cols]
            gim_scr[b, rows, :] = hh[n1:, cols]
        return carry

    lax.fori_loop(0, per_group, mid, 0, unroll=FFT_MID_UNROLL)

    @pl.when(g == ngroups - 1)
    def _():
        _fft_stage1_inv(gre_scr, gim_scr, s1i_ref, y_scr)

        def post(px_ref, cw_ref, cb_ref, order, store):
            def blk(i, carry):
                rows = pl.ds(pl.multiple_of(i * PITCH, 8), n1)
                for b in range(2):
                    zb = z_scr[b, rows, :]
                    gate = _short_conv_block(px_ref, b, i, nblk, cw_ref, cb_ref)
                    store(b, i, rows, gate * (y_scr[b, rows, :] + zb * skip_ref[order:order + 1, :]))
                return carry
            lax.fori_loop(0, nblk, blk, 0)

        @pl.when(c == 0)
        def _():
            def store(b, i, rows, val):
                z_scr[b, rows, :] = val
            post(px1_ref, cw1_ref, cb1_ref, 0, store)

        @pl.when(c == 1)
        def _():
            def store(b, i, rows, val):
                o_ref[b, pl.ds(pl.multiple_of(i * n1, n1), n1), :] = val.astype(o_ref.dtype)
            post(px2_ref, cw2_ref, cb2_ref, 1, store)


def _hyena(pg3, proj_col0, conv_w, conv_b, skip, kspec, s1, s1i, s2, s2i, width):
    bsz, seq, _ = pg3.shape
    assert bsz == 2
    n1 = FFT_N1
    n2 = 2 * seq // n1
    nct = width // LANES
    ngroups = 4
    krows = (n2 // ngroups) * n1
    col = lambda off: (lambda j, c, g: (0, 0, proj_col0 // LANES + off * nct + j))
    cw = lambda off: (lambda j, c, g: (0, off * nct + j))
    const3 = lambda j, c, g: (0, 0, 0)
    const2 = lambda j, c, g: (0, 0)
    return pl.pallas_call(
        _hyena_body,
        grid=(nct, 2, ngroups),
        in_specs=[pl.BlockSpec((2, seq, LANES), col(0)),
                  pl.BlockSpec((2, seq, LANES), col(1)),
                  pl.BlockSpec((2, seq, LANES), col(2)),
                  pl.BlockSpec((3, LANES), cw(0)),
                  pl.BlockSpec((3, LANES), cw(1)),
                  pl.BlockSpec((3, LANES), cw(2)),
                  pl.BlockSpec((1, LANES), cw(0)),
                  pl.BlockSpec((1, LANES), cw(1)),
                  pl.BlockSpec((1, LANES), cw(2)),
                  pl.BlockSpec((2, LANES), lambda j, c, g: (0, j)),
                  pl.BlockSpec((1, 2, krows, LANES), lambda j, c, g: (c, 0, g, j)),
                  pl.BlockSpec(s1.shape, const3),
                  pl.BlockSpec(s1i.shape, const3),
                  pl.BlockSpec(s2.shape, const2),
                  pl.BlockSpec(s2i.shape, const2)],
        out_specs=pl.BlockSpec((2, seq, LANES), lambda j, c, g: (0, 0, j)),
        out_shape=jax.ShapeDtypeStruct((2, seq, width), BF16),
        scratch_shapes=[pltpu.VMEM((2, (n2 // 2) * PITCH, LANES), F32),
                        pltpu.VMEM((2, (n2 // 2) * PITCH, LANES), F32),
                        pltpu.VMEM((2, n2 * PITCH, LANES), F32),
                        pltpu.VMEM((2, n2 * PITCH, LANES), F32)],
        compiler_params=_cparams(("arbitrary", "arbitrary", "arbitrary"), 56),
        name="hyena",
    )(pg3, pg3, pg3, conv_w, conv_w, conv_w, conv_b, conv_b, conv_b, skip, kspec, s1, s1i, s2, s2i)


ATTN_QBLK = 128
ATTN_UNROLL = 4


def _attention_body(pos_ref, freq_ref, sign_ref, *refs):
    qkv_refs = refs[:9]
    o_ref = refs[9]
    cos_scr, sin_scr, q_scr, k_scr, v_scr, og_scr, lse_scr = refs[10:]
    seq = q_scr.shape[0]
    chunk = 512
    nchunks = seq // chunk

    @pl.when(pl.program_id(1) == 0)
    def _():
        def trig(i, carry):
            rows = pl.ds(pl.multiple_of(i * chunk, chunk), chunk)
            ang = pos_ref[0, rows, :].astype(F32) * freq_ref[...]
            cos_scr[rows, :] = jnp.cos(ang)
            sin_scr[rows, :] = jnp.sin(ang) * sign_ref[...]
            return carry
        lax.fori_loop(0, nchunks, trig, 0)

    def rotate(src_ref, dst_ref):
        def body(i, carry):
            rows = pl.ds(pl.multiple_of(i * chunk, chunk), chunk)
            t = src_ref[0, rows, :].astype(F32)
            lane = lax.broadcasted_iota(jnp.int32, t.shape, 1)
            partner = jnp.where(lane < ROT_DIM // 2,
                                pltpu.roll(t, LANES - ROT_DIM // 2, 1), pltpu.roll(t, ROT_DIM // 2, 1))
            dst_ref[rows, :] = t * cos_scr[rows, :] + partner * sin_scr[rows, :]
            return carry
        lax.fori_loop(0, nchunks, body, 0)

    def widen(src_ref, dst_ref):
        def body(i, carry):
            rows = pl.ds(pl.multiple_of(i * chunk, chunk), chunk)
            dst_ref[rows, :] = src_ref[0, rows, :].astype(F32)
            return carry
        lax.fori_loop(0, nchunks, body, 0)

    scale = HEAD_DIM ** -0.5
    for gi, (window, dil) in enumerate(ATTN_GROUPS):
        rotate(qkv_refs[3 * gi], q_scr)
        rotate(qkv_refs[3 * gi + 1], k_scr)
        widen(qkv_refs[3 * gi + 2], v_scr)
        n = seq // dil
        half = window // (2 * dil)
        tk = min(n, ATTN_QBLK + 2 * half)
        blocks_per_res = n // ATTN_QBLK

        def block(u, carry, dil=dil, n=n, half=half, tk=tk, blocks_per_res=blocks_per_res, gi=gi):
            r = u // blocks_per_res
            m = u % blocks_per_res
            q0 = m * ATTN_QBLK
            k0 = jnp.clip(q0 - half, 0, n - tk)
            qrows = pl.ds(r + dil * q0, ATTN_QBLK, stride=dil)
            krows = pl.ds(r + dil * k0, tk, stride=dil)
            qb = q_scr[qrows, :].astype(BF16)
            kb = k_scr[krows, :].astype(BF16)
            vb = v_scr[krows, :].astype(BF16)
            s = lax.dot_general(qb, kb, (((1,), (1,)), ((), ())), preferred_element_type=F32) * scale
            qi = q0 + lax.broadcasted_iota(jnp.int32, (ATTN_QBLK, tk), 0)
            kj = k0 + lax.broadcasted_iota(jnp.int32, (ATTN_QBLK, tk), 1)
            s = jnp.where(jnp.abs(qi - kj) <= half, s, NEG_INF)
            mx = jnp.max(s, axis=-1, keepdims=True)
            p = jnp.exp(s - mx)
            l = jnp.sum(p, axis=-1, keepdims=True)
            o = jnp.dot(p.astype(BF16), vb, preferred_element_type=F32) / l
            og_scr[gi, qrows, :] = o
            lse_scr[gi, qrows, :] = jnp.broadcast_to(mx + jnp.log(l), (ATTN_QBLK, LANES))
            return carry

        lax.fori_loop(0, seq // ATTN_QBLK, block, 0, unroll=ATTN_UNROLL)

    def merge(i, carry):
        rows = pl.ds(pl.multiple_of(i * chunk, chunk), chunk)
        lses = [lse_scr[gi, rows, :] for gi in range(len(ATTN_GROUPS))]
        mx = functools.reduce(jnp.maximum, lses)
        ws = [jnp.exp(v - mx) for v in lses]
        den = functools.reduce(lambda a, b: a + b, ws)
        num = functools.reduce(lambda a, b: a + b,
                               [w * og_scr[gi, rows, :] for gi, w in enumerate(ws)])
        o_ref[0, rows, :] = (num / den).astype(o_ref.dtype)
        return carry

    lax.fori_loop(0, nchunks, merge, 0)


def _attention(pg3, pos3, freq_row, sign_row, qkv_col0):
    bsz, seq, _ = pg3.shape
    ng = len(ATTN_GROUPS)
    in_specs = [pl.BlockSpec((1, seq, 1), lambda b, h: (b, 0, 0)),
                pl.BlockSpec((1, LANES), lambda b, h: (0, 0)),
                pl.BlockSpec((1, LANES), lambda b, h: (0, 0))]
    for gi in range(ng):
        for which in range(3):
            base = qkv_col0 // LANES + which * N_ATTN_HEADS + gi * HEADS_PER_GROUP
            in_specs.append(pl.BlockSpec((1, seq, LANES), lambda b, h, base=base: (b, 0, base + h)))
    return pl.pallas_call(
        _attention_body,
        grid=(bsz, HEADS_PER_GROUP),
        in_specs=in_specs,
        out_specs=pl.BlockSpec((1, seq, LANES), lambda b, h: (b, 0, h)),
        out_shape=jax.ShapeDtypeStruct((bsz, seq, HEADS_PER_GROUP * HEAD_DIM), BF16),
        scratch_shapes=[pltpu.VMEM((seq, LANES), F32), pltpu.VMEM((seq, LANES), F32),
                        pltpu.VMEM((seq, LANES), F32), pltpu.VMEM((seq, LANES), F32),
                        pltpu.VMEM((seq, LANES), F32),
                        pltpu.VMEM((ng, seq, LANES), F32), pltpu.VMEM((ng, seq, LANES), F32)],
        compiler_params=_cparams(("arbitrary", "arbitrary"), 56),
        name="attention",
    )(pos3, freq_row, sign_row, *([pg3] * (3 * ng)))


def _mix_out_body(yhy_ref, yat_ref, ghy_ref, gat_ref, x_ref, mod_ref, gpost_ref, gpre_ref,
                  whyo_ref, wato_ref, wout_ref, wrhi_ref, wrlo_ref, br_ref, x1_ref, h2_ref, lg_ref):
    a = jnp.dot(yhy_ref[...], whyo_ref[...], preferred_element_type=F32)
    b = jnp.dot(yat_ref[...], wato_ref[...], preferred_element_type=F32)
    merged = ghy_ref[...].astype(F32) * a + gat_ref[...].astype(F32) * b
    y = jnp.dot(merged.astype(BF16), wout_ref[...], preferred_element_type=F32)
    x1 = x_ref[...] + mod_ref[0, 2:3, :] * _rms(y, gpost_ref[...])
    x1_ref[...] = x1
    h2 = _rms(x1, gpre_ref[...]) * (1.0 + mod_ref[0, 4:5, :]) + mod_ref[0, 3:4, :]
    h2_ref[...] = h2
    h2_hi = h2.astype(BF16)
    h2_lo = (h2 - h2_hi.astype(F32)).astype(BF16)
    w_hi = wrhi_ref[...]
    w_lo = wrlo_ref[...]
    lg_ref[...] = (jnp.dot(h2_hi, w_hi, preferred_element_type=F32)
                   + jnp.dot(h2_lo, w_hi, preferred_element_type=F32)
                   + jnp.dot(h2_hi, w_lo, preferred_element_type=F32)
                   + jnp.dot(h2_lo, w_lo, preferred_element_type=F32)) + br_ref[...]


def _mix_out(y_hy, y_at, pg, x2d, mod, g_post, g_pre, w_hy_o, w_at_o, w_out, w_r_hi, w_r_lo, b_r, seq):
    t, d = x2d.shape
    tm = 256
    per_batch = seq // tm
    gblk = 0
    const = lambda i: (0, 0)
    return pl.pallas_call(
        _mix_out_body,
        grid=(t // tm,),
        in_specs=[pl.BlockSpec((tm, y_hy.shape[1]), lambda i: (i, 0)),
                  pl.BlockSpec((tm, y_at.shape[1]), lambda i: (i, 0)),
                  pl.BlockSpec((tm, d), lambda i: (i, gblk)),
                  pl.BlockSpec((tm, d), lambda i: (i, gblk + 1)),
                  pl.BlockSpec((tm, d), lambda i: (i, 0)),
                  pl.BlockSpec((1, 6, d), lambda i: (i // per_batch, 0, 0)),
                  pl.BlockSpec((1, d), const),
                  pl.BlockSpec((1, d), const),
                  pl.BlockSpec(w_hy_o.shape, const),
                  pl.BlockSpec(w_at_o.shape, const),
                  pl.BlockSpec(w_out.shape, const),
                  pl.BlockSpec(w_r_hi.shape, const),
                  pl.BlockSpec(w_r_lo.shape, const),
                  pl.BlockSpec((1, LANES), const)],
        out_specs=[pl.BlockSpec((tm, d), lambda i: (i, 0)),
                   pl.BlockSpec((tm, d), lambda i: (i, 0)),
                   pl.BlockSpec((tm, LANES), lambda i: (i, 0))],
        out_shape=[jax.ShapeDtypeStruct((t, d), F32),
                   jax.ShapeDtypeStruct((t, d), F32),
                   jax.ShapeDtypeStruct((t, LANES), F32)],
        compiler_params=_cparams(("arbitrary",), 56),
        name="mix_out",
    )(y_hy, y_at, pg, pg, x2d, mod, g_post, g_pre, w_hy_o, w_at_o, w_out, w_r_hi, w_r_lo, b_r)


def _route_body(lg_ref, eid_ref, gate_ref):
    lg = lg_ref[...]
    lane = lax.broadcasted_iota(jnp.int32, lg.shape, 1)
    big = jnp.int32(1 << 20)

    def first_argmax(vals, mask):
        v = jnp.where(mask, vals, -jnp.inf)
        mx = jnp.max(v, axis=-1, keepdims=True)
        idx = jnp.min(jnp.where(mask & (v == mx), lane, big), axis=-1, keepdims=True)
        return mx, idx

    gmask = lane < N_EXPERT_GROUPS
    gmax, gidx = first_argmax(lg, gmask)
    gval = 1.0 / jnp.sum(jnp.where(gmask, jnp.exp(lg - gmax), 0.0), axis=-1, keepdims=True)
    lo = N_EXPERT_GROUPS + gidx * EXPERTS_PER_GROUP
    emask = (lane >= lo) & (lane < lo + EXPERTS_PER_GROUP)
    v1, i1 = first_argmax(lg, emask)
    v2, i2 = first_argmax(lg, emask & (lane != i1))
    e2 = jnp.exp(v2 - v1)
    p1 = 1.0 / (1.0 + e2)
    p2 = e2 / (1.0 + e2)
    eid = jnp.where(lane == 0, i1, i2) - N_EXPERT_GROUPS
    gate = gval * jnp.where(lane == 0, p1, p2)
    eid_ref[...] = eid[:, :TOP_K]
    gate_ref[...] = gate[:, :TOP_K]


def _route(logits):
    t = logits.shape[0]
    tm = 512
    return pl.pallas_call(
        _route_body,
        grid=(t // tm,),
        in_specs=[pl.BlockSpec((tm, LANES), lambda i: (i, 0))],
        out_specs=[pl.BlockSpec((tm, TOP_K), lambda i: (i, 0)),
                   pl.BlockSpec((tm, TOP_K), lambda i: (i, 0))],
        out_shape=[jax.ShapeDtypeStruct((t, TOP_K), jnp.int32),
                   jax.ShapeDtypeStruct((t, TOP_K), F32)],
        compiler_params=_cparams(("arbitrary",), 32),
        name="route",
    )(logits)


def _rank_body(e_ref, rank_ref, cnt_ref, carry_scr):
    i = pl.program_id(0)
    r = e_ref.shape[0]

    @pl.when(i == 0)
    def _():
        carry_scr[...] = jnp.zeros_like(carry_scr)

    lane = lax.broadcasted_iota(jnp.int32, (r, LANES), 1)
    onehot = (lane == e_ref[...]).astype(F32)
    tri = (lax.broadcasted_iota(jnp.int32, (r, r), 1)
           < lax.broadcasted_iota(jnp.int32, (r, r), 0)).astype(BF16)
    before = jnp.dot(tri, onehot.astype(BF16), preferred_element_type=F32) + carry_scr[0:1, :]
    rank_ref[...] = jnp.sum(onehot * before, axis=-1, keepdims=True).astype(jnp.int32)
    total = carry_scr[0:1, :] + jnp.sum(onehot, axis=0, keepdims=True)
    carry_scr[...] = jnp.broadcast_to(total, carry_scr.shape)
    cnt_ref[...] = jnp.broadcast_to(total, cnt_ref.shape).astype(jnp.int32)


def _rank(flat_e):
    a = flat_e.shape[0]
    r = 512
    return pl.pallas_call(
        _rank_body,
        grid=(a // r,),
        in_specs=[pl.BlockSpec((r, 1), lambda i: (i, 0))],
        out_specs=[pl.BlockSpec((r, 1), lambda i: (i, 0)),
                   pl.BlockSpec((8, LANES), lambda i: (0, 0))],
        out_shape=[jax.ShapeDtypeStruct((a, 1), jnp.int32),
                   jax.ShapeDtypeStruct((8, LANES), jnp.int32)],
        scratch_shapes=[pltpu.VMEM((8, LANES), F32)],
        compiler_params=_cparams(("arbitrary",), 32),
        name="rank",
    )(flat_e)


def _dest_body(e_ref, rank_ref, ps_ref, o_ref):
    lane = lax.broadcasted_iota(jnp.int32, (e_ref.shape[0], LANES), 1)
    first = jnp.sum(jnp.where(lane == e_ref[...], ps_ref[...], 0.0), axis=-1, keepdims=True)
    o_ref[...] = first.astype(jnp.int32) + rank_ref[...]


def _dest(flat_e, rank, pstarts_row):
    a = flat_e.shape[0]
    r = 2048
    return pl.pallas_call(
        _dest_body,
        grid=(a // r,),
        in_specs=[pl.BlockSpec((r, 1), lambda i: (i, 0)),
                  pl.BlockSpec((r, 1), lambda i: (i, 0)),
                  pl.BlockSpec((1, LANES), lambda i: (0, 0))],
        out_specs=pl.BlockSpec((r, 1), lambda i: (i, 0)),
        out_shape=jax.ShapeDtypeStruct((a, 1), jnp.int32),
        compiler_params=_cparams(("arbitrary",), 32),
        name="dest",
    )(flat_e, rank, pstarts_row)


DMA_UNROLL = 8


DISPATCH_CHUNK = 256


def _dispatch_body(src_ref, h_hbm, x_ref, x_scr, sem):
    c = pl.program_id(0)

    def chunk(cc, act):
        base = cc * DISPATCH_CHUNK

        def body(r, carry):
            act(pltpu.make_async_copy(h_hbm.at[pl.ds(src_ref[base + r], 1)],
                                      x_scr.at[cc % 2, pl.ds(r, 1)], sem.at[cc % 2]))
            return carry

        lax.fori_loop(0, DISPATCH_CHUNK, body, 0, unroll=DMA_UNROLL)

    @pl.when(c == 0)
    def _():
        chunk(c, lambda cp: cp.start())

    @pl.when(c + 1 < pl.num_programs(0))
    def _():
        chunk(c + 1, lambda cp: cp.start())

    chunk(c, lambda cp: cp.wait())
    x_ref[...] = x_scr[c % 2]


def _dispatch(src_tok, h2):
    d = h2.shape[1]
    grid_spec = pltpu.PrefetchScalarGridSpec(
        num_scalar_prefetch=1,
        grid=(src_tok.shape[0] // DISPATCH_CHUNK,),
        in_specs=[pl.BlockSpec(memory_space=pl.ANY)],
        out_specs=pl.BlockSpec((DISPATCH_CHUNK, d), lambda c, src: (c, 0)),
        scratch_shapes=[pltpu.VMEM((2, DISPATCH_CHUNK, d), h2.dtype), pltpu.SemaphoreType.DMA((2,))],
    )
    return pl.pallas_call(
        _dispatch_body,
        grid_spec=grid_spec,
        out_shape=jax.ShapeDtypeStruct((src_tok.shape[0], h2.shape[1]), h2.dtype),
        compiler_params=_cparams(("arbitrary",), 32),
        name="dispatch",
    )(src_tok, h2)


def _experts_body(ord_ref, elist_ref, cnt_ref, x_ref, w1_hbm, w3_hbm, w2_hbm, y_ref,
                  w1_scr, w3_scr, w2_scr, wsem):
    i = pl.program_id(0)
    nused = cnt_ref[0]
    nexp = cnt_ref[1]
    k = ord_ref[i]
    first = (i == 0) | (k != ord_ref[jnp.maximum(i - 1, 0)])

    def weights(kk, act):
        e = elist_ref[kk]
        slot = kk % 2
        act(pltpu.make_async_copy(w1_hbm.at[e], w1_scr.at[slot], wsem.at[slot, 0]))
        act(pltpu.make_async_copy(w3_hbm.at[e], w3_scr.at[slot], wsem.at[slot, 1]))
        act(pltpu.make_async_copy(w2_hbm.at[e], w2_scr.at[slot], wsem.at[slot, 2]))

    @pl.when(i == 0)
    def _():
        weights(k, lambda cp: cp.start())

    @pl.when((i < nused) & first & (k + 1 < nexp))
    def _():
        weights(k + 1, lambda cp: cp.start())

    @pl.when((i < nused) & first)
    def _():
        weights(k, lambda cp: cp.wait())

    @pl.when(i < nused)
    def _():
        wslot = k % 2
        xb = x_ref[...].astype(BF16)
        a = jnp.dot(xb, w1_scr[wslot].astype(BF16), preferred_element_type=F32)
        b = jnp.dot(xb, w3_scr[wslot].astype(BF16), preferred_element_type=F32)
        act = (a * jax.nn.sigmoid(a) * b).astype(BF16)
        y_ref[...] = jnp.dot(act, w2_scr[wslot].astype(BF16), preferred_element_type=F32)

    @pl.when(i >= nused)
    def _():
        y_ref[...] = jnp.zeros_like(y_ref)


def _experts(block_ord, elist, counts2, xbuf, w1, w3, w2):
    nblocks = block_ord.shape[0]
    d = xbuf.shape[1]
    ff = w1.shape[2]
    rb = EXPERT_ROW_BLOCK
    grid_spec = pltpu.PrefetchScalarGridSpec(
        num_scalar_prefetch=3,
        grid=(nblocks,),
        in_specs=[pl.BlockSpec((rb, d), lambda i, od, el, cnt: (jnp.minimum(i, cnt[0] - 1), 0))]
        + [pl.BlockSpec(memory_space=pl.ANY)] * 3,
        out_specs=pl.BlockSpec((rb, d), lambda i, *_: (i, 0)),
        scratch_shapes=[pltpu.VMEM((2, d, ff), F32), pltpu.VMEM((2, d, ff), F32),
                        pltpu.VMEM((2, ff, d), F32), pltpu.SemaphoreType.DMA((2, 3))],
    )
    return pl.pallas_call(
        _experts_body,
        grid_spec=grid_spec,
        out_shape=jax.ShapeDtypeStruct((nblocks * rb, d), F32),
        compiler_params=_cparams(("arbitrary",), 56),
        name="experts",
    )(block_ord, elist, counts2, xbuf, w1, w3, w2)


def _combine_body(dest_ref, y_hbm, gate_ref, x1_ref, mod_ref, g_ref, o_ref, buf_scr, sem):
    i = pl.program_id(0)
    tm = x1_ref.shape[0]

    def gather(step, act):
        slot = step % 2
        base = step * tm * TOP_K

        def body(r, carry):
            for k in range(TOP_K):
                act(pltpu.make_async_copy(y_hbm.at[pl.ds(dest_ref[base + TOP_K * r + k], 1)],
                                          buf_scr.at[slot, k, pl.ds(r, 1)], sem.at[slot]))
            return carry

        lax.fori_loop(0, tm, body, 0, unroll=DMA_UNROLL // TOP_K)

    @pl.when(i == 0)
    def _():
        gather(i, lambda cp: cp.start())

    @pl.when(i + 1 < pl.num_programs(0))
    def _():
        gather(i + 1, lambda cp: cp.start())

    gather(i, lambda cp: cp.wait())
    slot = i % 2
    gate = gate_ref[...]
    y = buf_scr[slot, 0] * gate[:, 0:1] + buf_scr[slot, 1] * gate[:, 1:2]
    o_ref[...] = x1_ref[...] + mod_ref[0, 5:6, :] * _rms(y, g_ref[...])


def _combine(dest, ybuf, gate, x1, mod, g_post, seq):
    t, d = x1.shape
    tm = 128
    per_batch = seq // tm
    grid_spec = pltpu.PrefetchScalarGridSpec(
        num_scalar_prefetch=1,
        grid=(t // tm,),
        in_specs=[pl.BlockSpec(memory_space=pl.ANY),
                  pl.BlockSpec((tm, TOP_K), lambda i, dst: (i, 0)),
                  pl.BlockSpec((tm, d), lambda i, dst: (i, 0)),
                  pl.BlockSpec((1, 6, d), lambda i, dst: (i // per_batch, 0, 0)),
                  pl.BlockSpec((1, d), lambda i, dst: (0, 0))],
        out_specs=pl.BlockSpec((tm, d), lambda i, dst: (i, 0)),
        scratch_shapes=[pltpu.VMEM((2, TOP_K, tm, d), F32), pltpu.SemaphoreType.DMA((2,))],
    )
    return pl.pallas_call(
        _combine_body,
        grid_spec=grid_spec,
        out_shape=jax.ShapeDtypeStruct((t, d), F32),
        compiler_params=_cparams(("arbitrary",), 32),
        name="combine",
    )(dest, ybuf, gate, x1, mod, g_post)


def _layer(x, c, positions, w_ada, b_ada, g_mix_pre, g_mix_post, g_ffn_pre, g_ffn_post,
           w_in, conv_w, conv_b, filt_w1, filt_b1, filt_w2, filt_b2, filt_w3, filt_freq,
           hyena_skip, w_branch_gate, b_branch_gate, w_hy_o, w_at_o, w_out,
           w_group, b_group, w_expert, b_expert, w1_exp, w3_exp, w2_exp):
    bsz, seq, d = x.shape
    t = bsz * seq
    width = hyena_skip.shape[1]
    row = lambda v: v.reshape(1, -1)

    c_pad = jnp.pad(c, ((0, 8 - bsz), (0, 0)))
    mod = _adaln(c_pad, w_ada, row(b_ada))[:bsz].reshape(bsz, 6, d)

    x2d = x.reshape(t, d)
    n_gate = w_branch_gate.shape[1]
    n_cat = -(-(n_gate + w_in.shape[1]) // IN_PROJ_TN) * IN_PROJ_TN
    n_zero = n_cat - n_gate - w_in.shape[1]
    w_cat = jnp.concatenate([w_branch_gate.astype(BF16), w_in.astype(BF16),
                             jnp.zeros((d, n_zero), BF16)], axis=1)
    b_cat = jnp.concatenate([b_branch_gate, jnp.zeros((n_cat - n_gate,), F32)]).reshape(1, -1)
    pg = _in_proj(x2d, mod, row(g_mix_pre), w_cat, b_cat, n_gate, seq)
    pg3 = pg.reshape(bsz, seq, -1)

    hidden = filt_w2.shape[0]
    bands = np.zeros((1, LANES), np.float32)
    band_vals = np.linspace(1e-4, FILTER_BANDS - 1, FILTER_BANDS, dtype=np.float32)
    bands[0, 1:1 + FILTER_BANDS] = band_vals
    bands[0, 1 + FILTER_BANDS:1 + 2 * FILTER_BANDS] = band_vals
    w1p = jnp.pad(filt_w1, ((0, LANES - filt_w1.shape[0]), (0, 0)))
    max_decay = math.log(DECAY_TARGET) / FAST_DECAY_PCT
    min_decay = math.log(DECAY_TARGET) / SLOW_DECAY_PCT
    deltas = jnp.abs(jnp.linspace(min_decay, max_decay, width, dtype=F32)).reshape(1, -1)
    uw = _filters(seq, width, jnp.asarray(bands), w1p, row(filt_b1), filt_w2, row(filt_b2),
                  row(filt_freq), deltas, filt_w3)
    s1, s1i, s2, s2i = _fft_tables(FFT_N1, 2 * seq // FFT_N1)
    kspec = _spectra(uw, s1, s2)
    y_hy = _hyena(pg3, n_gate, conv_w, row(conv_b), hyena_skip, kspec, s1, s1i, s2, s2i, width)

    half = ROT_DIM // 2
    inv_freq = np.power(ROPE_THETA, -2.0 * np.arange(half, dtype=np.float32) / ROT_DIM).astype(np.float32)
    freq_row = np.zeros((1, LANES), np.float32)
    freq_row[0, :half] = inv_freq
    freq_row[0, half:ROT_DIM] = inv_freq
    sign_row = np.zeros((1, LANES), np.float32)
    sign_row[0, :half] = -1.0
    sign_row[0, half:ROT_DIM] = 1.0
    y_at = _attention(pg3, positions.reshape(bsz, seq, 1), jnp.asarray(freq_row), jnp.asarray(sign_row),
                      n_gate + 3 * width)

    w_r = jnp.concatenate([w_group, jnp.transpose(w_expert, (1, 0, 2)).reshape(d, N_EXPERTS)], axis=1)
    w_r = jnp.pad(w_r, ((0, 0), (0, LANES - w_r.shape[1])))
    w_r_hi = w_r.astype(BF16)
    w_r_lo = (w_r - w_r_hi.astype(F32)).astype(BF16)
    b_r = jnp.pad(jnp.concatenate([b_group, b_expert.reshape(-1)]), (0, LANES - N_EXPERT_GROUPS - N_EXPERTS))
    x1, h2, logits = _mix_out(y_hy.reshape(t, width), y_at.reshape(t, -1), pg, x2d, mod,
                              row(g_mix_post), row(g_ffn_pre), w_hy_o.astype(BF16), w_at_o.astype(BF16),
                              w_out.astype(BF16), w_r_hi, w_r_lo, b_r.reshape(1, -1), seq)

    eid, gate = _route(logits)
    flat_e = eid.reshape(t * TOP_K, 1)
    rank, counts = _rank(flat_e)
    counts = counts[0, :N_EXPERTS]
    rb = EXPERT_ROW_BLOCK
    padded = (counts + rb - 1) // rb * rb
    pends = jnp.cumsum(padded)
    pstarts = pends - padded
    pstarts_row = jnp.pad(pstarts.astype(F32), (0, LANES - N_EXPERTS)).reshape(1, LANES)
    dest = _dest(flat_e, rank, pstarts_row)[:, 0]
    n_blocks = t * TOP_K // rb + N_EXPERTS
    block_start = jnp.arange(n_blocks, dtype=jnp.int32) * rb
    block_e = jnp.minimum(jnp.searchsorted(pends, block_start, side='right'), N_EXPERTS - 1).astype(jnp.int32)
    tok = jnp.arange(t * TOP_K, dtype=jnp.int32) // TOP_K
    src_tok = jnp.zeros((n_blocks * rb,), jnp.int32).at[dest].set(tok)
    has_rows = jnp.cumsum((counts > 0).astype(jnp.int32))
    elist = jnp.minimum(jnp.searchsorted(has_rows, jnp.arange(1, N_EXPERTS + 1, dtype=jnp.int32), side='left'),
                        N_EXPERTS - 1).astype(jnp.int32)
    block_ord = (has_rows - 1)[block_e].astype(jnp.int32)
    counts2 = jnp.stack([pends[-1] // rb, has_rows[-1]]).astype(jnp.int32)

    xbuf = _dispatch(src_tok, h2)
    ybuf = _experts(block_ord, elist, counts2, xbuf, w1_exp, w3_exp, w2_exp)
    out = _combine(dest, ybuf, gate, x1, mod, row(g_ffn_post), seq)
    return out.reshape(bsz, seq, d)


def kernel(x, c, positions, w_ada, b_ada, g_mix_pre, g_mix_post, g_ffn_pre, g_ffn_post, w_in, conv_w, conv_b, filt_w1, filt_b1, filt_w2, filt_b2, filt_w3, filt_freq, hyena_skip, w_branch_gate, b_branch_gate, w_hy_o, w_at_o, w_out, w_group, b_group, w_expert, b_expert, w1_exp, w3_exp, w2_exp):
    depth = w_ada.shape[0]
    for l in range(depth):
        x = _layer(x, c, positions, w_ada[l], b_ada[l], g_mix_pre[l], g_mix_post[l], g_ffn_pre[l],
                   g_ffn_post[l], w_in[l], conv_w[l], conv_b[l], filt_w1[l], filt_b1[l], filt_w2[l],
                   filt_b2[l], filt_w3[l], filt_freq[l], hyena_skip[l], w_branch_gate[l],
                   b_branch_gate[l], w_hy_o[l], w_at_o[l], w_out[l], w_group[l], b_group[l],
                   w_expert[l], b_expert[l], w1_exp[l], w3_exp[l], w2_exp[l])
    return x
```

```python
import functools
import math

import numpy as np
import jax
import jax.numpy as jnp
from jax import lax
from jax.experimental import pallas as pl
from jax.experimental.pallas import tpu as pltpu

F32 = jnp.float32
BF16 = jnp.bfloat16

LANES = 128
MIB = 1024 * 1024

RMS_EPS = 1e-6
NEG_INF = -1e30

HEAD_DIM = 128
ROT_DIM = HEAD_DIM // 4
ROPE_THETA = 500000.0
ATTN_GROUPS = ((128, 1), (512, 4), (2048, 16))
HEADS_PER_GROUP = 4
N_ATTN_HEADS = HEADS_PER_GROUP * len(ATTN_GROUPS)

FILTER_BANDS = 16
DECAY_TARGET = 1e-2
FAST_DECAY_PCT = 0.3
SLOW_DECAY_PCT = 1.5

N_EXPERT_GROUPS = 8
EXPERTS_PER_GROUP = 8
N_EXPERTS = N_EXPERT_GROUPS * EXPERTS_PER_GROUP
TOP_K = 2
EXPERT_ROW_BLOCK = 128

FFT_N1 = 128
PITCH = FFT_N1 + 8
FFT_UNROLL = 16
FFT_MID_UNROLL = 8


def _cparams(sem, vmem_mib):
    return pltpu.CompilerParams(dimension_semantics=sem, vmem_limit_bytes=vmem_mib * MIB)


def _rms(x, g):
    return x * lax.rsqrt(jnp.mean(x * x, axis=-1, keepdims=True) + RMS_EPS) * g


def _adaln_body(c_ref, w_ref, b_ref, o_ref):
    c = c_ref[...]
    cond = c * jax.nn.sigmoid(c)
    o_ref[...] = jnp.dot(cond.astype(BF16), w_ref[...].astype(BF16),
                         preferred_element_type=F32) + b_ref[...]


def _adaln(c_pad, w_ada, b_ada):
    rows, d = c_pad.shape
    n = w_ada.shape[1]
    tn = 1024
    return pl.pallas_call(
        _adaln_body,
        grid=(n // tn,),
        in_specs=[pl.BlockSpec((rows, d), lambda j: (0, 0)),
                  pl.BlockSpec((d, tn), lambda j: (0, j)),
                  pl.BlockSpec((1, tn), lambda j: (0, j))],
        out_specs=pl.BlockSpec((rows, tn), lambda j: (0, j)),
        out_shape=jax.ShapeDtypeStruct((rows, n), F32),
        compiler_params=_cparams(("arbitrary",), 40),
        name="adaln",
    )(c_pad, w_ada, b_ada)


IN_PROJ_TM = 1024
IN_PROJ_TN = 1024
IN_PROJ_SUB_M = 256
IN_PROJ_SUB_N = 512


def _in_proj_body(n_gate, x_ref, mod_ref, g_ref, w_ref, b_ref, o_ref, h_scr):
    j = pl.program_id(1)

    @pl.when(j == 0)
    def _():
        x = x_ref[...]
        h = _rms(x, g_ref[...]) * (1.0 + mod_ref[0, 1:2, :]) + mod_ref[0, 0:1, :]
        h_scr[...] = h.astype(BF16)

    is_gate = j < n_gate
    tm, tn = o_ref.shape
    for mi in range(tm // IN_PROJ_SUB_M):
        rows = slice(mi * IN_PROJ_SUB_M, (mi + 1) * IN_PROJ_SUB_M)
        for ni in range(tn // IN_PROJ_SUB_N):
            cols = slice(ni * IN_PROJ_SUB_N, (ni + 1) * IN_PROJ_SUB_N)
            acc = jnp.dot(h_scr[rows, :], w_ref[:, cols], preferred_element_type=F32) + b_ref[:, cols]
            o_ref[rows, cols] = jnp.where(is_gate, jax.nn.sigmoid(acc), acc).astype(o_ref.dtype)


def _in_proj(x2d, mod, g_pre, w_cat, b_cat, n_gate_cols, seq):
    t, d = x2d.shape
    n = w_cat.shape[1]
    tm, tn = IN_PROJ_TM, IN_PROJ_TN
    per_batch = seq // tm
    return pl.pallas_call(
        functools.partial(_in_proj_body, n_gate_cols // tn),
        grid=(t // tm, n // tn),
        in_specs=[pl.BlockSpec((tm, d), lambda i, j: (i, 0)),
                  pl.BlockSpec((1, 6, d), lambda i, j: (i // per_batch, 0, 0)),
                  pl.BlockSpec((1, d), lambda i, j: (0, 0)),
                  pl.BlockSpec((d, tn), lambda i, j: (0, j)),
                  pl.BlockSpec((1, tn), lambda i, j: (0, j))],
        out_specs=pl.BlockSpec((tm, tn), lambda i, j: (i, j)),
        out_shape=jax.ShapeDtypeStruct((t, n), BF16),
        scratch_shapes=[pltpu.VMEM((tm, d), BF16)],
        compiler_params=_cparams(("arbitrary", "arbitrary"), 56),
        name="in_proj",
    )(x2d, mod, g_pre, w_cat, b_cat)


def _filters_body(seq, band_ref, w1_ref, b1_ref, w2_ref, b2_ref, fr_ref, dl_ref,
                  w3a_ref, w3b_ref, o_ref, hid_scr):
    i = pl.program_id(0)
    j = pl.program_id(1)
    tl = hid_scr.shape[0]
    row = (lax.broadcasted_iota(jnp.int32, (tl, 1), 0) + i * tl).astype(F32)

    @pl.when((j == 0) & (pl.program_id(2) == 0))
    def _():
        lane = lax.broadcasted_iota(jnp.int32, (tl, LANES), 1)
        t = row / (seq - 1.0)
        ang = band_ref[...] * (2.0 * math.pi * row / seq)
        feats = jnp.where(lane == 0, t,
                          jnp.where(lane <= FILTER_BANDS, jnp.cos(ang),
                                    jnp.where(lane <= 2 * FILTER_BANDS, -jnp.sin(ang), 0.0)))
        hi = lax.Precision.HIGHEST
        fr = fr_ref[...]
        hid = jnp.sin(fr * (jnp.dot(feats, w1_ref[...], precision=hi, preferred_element_type=F32)
                            + b1_ref[...]))
        hid = jnp.sin(fr * (jnp.dot(hid, w2_ref[...], precision=hi, preferred_element_type=F32)
                            + b2_ref[...]))
        hid_scr[...] = hid

    hi = lax.Precision.HIGHEST
    hid = hid_scr[...]
    decay = jnp.exp(-(row / (seq - 1.0)) * dl_ref[...])
    hf = jnp.dot(hid, w3a_ref[...], precision=hi, preferred_element_type=F32) * decay
    hb = jnp.dot(hid, w3b_ref[...], precision=hi, preferred_element_type=F32) * decay
    hb = jnp.where(row == 0.0, 0.0, hb)
    o_ref[0, 0] = hf + hb
    o_ref[0, 1] = hf - hb


def _filters(seq, width, bands, w1p, b1, w2, b2, freq, deltas, w3):
    tl, tc = 512, 256
    nct = width // tc
    hidden = w2.shape[0]
    const = lambda i, j, o: (0, 0)
    return pl.pallas_call(
        functools.partial(_filters_body, float(seq)),
        grid=(seq // tl, nct, 2),
        in_specs=[pl.BlockSpec((1, LANES), const),
                  pl.BlockSpec((LANES, hidden), const),
                  pl.BlockSpec((1, hidden), const),
                  pl.BlockSpec((hidden, hidden), const),
                  pl.BlockSpec((1, hidden), const),
                  pl.BlockSpec((1, hidden), const),
                  pl.BlockSpec((1, tc), lambda i, j, o: (0, j)),
                  pl.BlockSpec((hidden, tc), lambda i, j, o: (0, (2 * o) * nct + j)),
                  pl.BlockSpec((hidden, tc), lambda i, j, o: (0, (2 * o + 1) * nct + j))],
        out_specs=pl.BlockSpec((1, 2, tl, tc), lambda i, j, o: (o, 0, i, j)),
        out_shape=jax.ShapeDtypeStruct((2, 2, seq, width), F32),
        scratch_shapes=[pltpu.VMEM((tl, hidden), F32)],
        compiler_params=_cparams(("arbitrary", "arbitrary", "arbitrary"), 32),
        name="filters",
    )(bands, w1p, b1, w2, b2, freq, deltas, w3, w3)


def _fft_tables(n1, n2):
    n = n1 * n2
    q = np.arange(n2)[:, None]
    b = np.arange(n2 // 2)[None, :]
    a = np.arange(n1)[:, None, None]
    ang = -2.0 * np.pi * (a * q[None] / n + (q * b)[None] / n2)
    stage1 = np.concatenate([np.cos(ang), np.sin(ang)], axis=1)
    stage1_inv = np.transpose(stage1, (0, 2, 1)) / n
    p = np.arange(n1)
    ang2 = -2.0 * np.pi * np.outer(p, p) / n1
    fre, fim = np.cos(ang2), np.sin(ang2)
    stage2 = np.block([[fre, -fim], [fim, fre]])
    stage2_inv = np.block([[fre, fim], [-fim, fre]])
    as_bf16 = lambda m: jnp.asarray(m, dtype=F32).astype(BF16)
    return as_bf16(stage1), as_bf16(stage1_inv), as_bf16(stage2), as_bf16(stage2_inv)


def _halves(ref, rows):
    return jnp.concatenate([ref[0, rows, :], ref[1, rows, :]], axis=1)


def _fft_stage1(z_ref, s1_ref, gre_ref, gim_ref):
    n1, two_n2, n2h = s1_ref.shape
    n2 = two_n2 // 2

    def step(a, carry):
        zrows = _halves(z_ref, pl.ds(a, n2h, stride=PITCH))
        g = jnp.dot(s1_ref[a], zrows.astype(BF16), preferred_element_type=F32)
        for h in range(2):
            cols = slice(h * LANES, (h + 1) * LANES)
            gre_ref[h, pl.ds(a, n2, stride=PITCH), :] = g[:n2, cols]
            gim_ref[h, pl.ds(a, n2, stride=PITCH), :] = g[n2:, cols]
        return carry

    lax.fori_loop(0, n1, step, 0, unroll=FFT_UNROLL)


def _fft_stage1_inv(gre_ref, gim_ref, s1i_ref, y_ref):
    n1, n2h, two_n2 = s1i_ref.shape
    n2 = two_n2 // 2

    def step(a, carry):
        rows = pl.ds(a, n2, stride=PITCH)
        hs = jnp.concatenate([_halves(gre_ref, rows), _halves(gim_ref, rows)], axis=0)
        y = jnp.dot(s1i_ref[a], hs.astype(BF16), preferred_element_type=F32)
        for h in range(2):
            y_ref[h, pl.ds(a, n2h, stride=PITCH), :] = y[:, h * LANES:(h + 1) * LANES]
        return carry

    lax.fori_loop(0, n1, step, 0, unroll=FFT_UNROLL)


def _stage2_block(gre_ref, gim_ref, s2_ref, q):
    n1 = s2_ref.shape[0] // 2
    rows = pl.ds(pl.multiple_of(q * PITCH, 8), n1)
    gs = jnp.concatenate([_halves(gre_ref, rows), _halves(gim_ref, rows)], axis=0)
    x = jnp.dot(s2_ref[...], gs.astype(BF16), preferred_element_type=F32)
    return x[:n1], x[n1:]


def _spectra_body(uw_ref, s1_ref, s2_ref, k_ref, z_scr, gre_scr, gim_scr):
    n1 = s2_ref.shape[0] // 2
    n2 = s1_ref.shape[1] // 2
    for h in range(2):
        for b in range(n2 // 2):
            z_scr[h, pl.ds(b * PITCH, n1), :] = uw_ref[0, h, pl.ds(b * n1, n1), :]
    _fft_stage1(z_scr, s1_ref, gre_scr, gim_scr)

    def step(q, carry):
        xre, xim = _stage2_block(gre_scr, gim_scr, s2_ref, q)
        rows = pl.ds(pl.multiple_of(q * n1, 8), n1)
        k_ref[0, 0, rows, :] = xre[:, :LANES]
        k_ref[0, 1, rows, :] = xim[:, LANES:]
        return carry

    lax.fori_loop(0, n2, step, 0, unroll=FFT_MID_UNROLL)


def _spectra(uw, s1, s2):
    _, _, seq, width = uw.shape
    n1 = FFT_N1
    n2 = 2 * seq // n1
    const3 = lambda j, o: (0, 0, 0)
    return pl.pallas_call(
        _spectra_body,
        grid=(width // LANES, 2),
        in_specs=[pl.BlockSpec((1, 2, seq, LANES), lambda j, o: (o, 0, 0, j)),
                  pl.BlockSpec(s1.shape, const3),
                  pl.BlockSpec(s2.shape, lambda j, o: (0, 0))],
        out_specs=pl.BlockSpec((1, 2, 2 * seq, LANES), lambda j, o: (o, 0, 0, j)),
        out_shape=jax.ShapeDtypeStruct((2, 2, 2 * seq, width), F32),
        scratch_shapes=[pltpu.VMEM((2, (n2 // 2) * PITCH, LANES), F32),
                        pltpu.VMEM((2, n2 * PITCH, LANES), F32),
                        pltpu.VMEM((2, n2 * PITCH, LANES), F32)],
        compiler_params=_cparams(("arbitrary", "arbitrary"), 56),
        name="spectra",
    )(uw, s1, s2)


def _short_conv_block(p_ref, b, i, nblk, w_ref, bias_ref):
    n1 = FFT_N1
    start = pl.multiple_of(i * n1, n1)
    cur = p_ref[b, pl.ds(start, n1), :].astype(F32)
    before = p_ref[b, pl.ds(pl.multiple_of(jnp.maximum(start - 16, 0), 16), 16), :].astype(F32)
    after = p_ref[b, pl.ds(pl.multiple_of(jnp.minimum(start + n1, (nblk - 1) * n1), 16), 16), :].astype(F32)
    last_prev = before[15:16] * jnp.where(i > 0, 1.0, 0.0).astype(F32)
    first_next = after[0:1] * jnp.where(i < nblk - 1, 1.0, 0.0).astype(F32)
    row = lax.broadcasted_iota(jnp.int32, (n1, 1), 0)
    prev = jnp.where(row == 0, last_prev, pltpu.roll(cur, 1, 0))
    nxt = jnp.where(row == n1 - 1, first_next, pltpu.roll(cur, n1 - 1, 0))
    return prev * w_ref[0:1, :] + cur * w_ref[1:2, :] + nxt * w_ref[2:3, :] + bias_ref[...]


def _hyena_body(pv_ref, px1_ref, px2_ref, cwv_ref, cw1_ref, cw2_ref, cbv_ref, cb1_ref, cb2_ref,
                skip_ref, k_ref, s1_ref, s1i_ref, s2_ref, s2i_ref, o_ref,
                z_scr, y_scr, gre_scr, gim_scr):
    c = pl.program_id(1)
    g = pl.program_id(2)
    ngroups = pl.num_programs(2)
    n1 = FFT_N1
    n2 = s1_ref.shape[1] // 2
    nblk = n2 // 2
    per_group = n2 // ngroups

    @pl.when(g == 0)
    def _():
        @pl.when(c == 0)
        def _():
            def fill(i, carry):
                for b in range(2):
                    z_scr[b, pl.ds(pl.multiple_of(i * PITCH, 8), n1), :] = _short_conv_block(
                        pv_ref, b, i, nblk, cwv_ref, cbv_ref)
                return carry
            lax.fori_loop(0, nblk, fill, 0)

        _fft_stage1(z_scr, s1_ref, gre_scr, gim_scr)

    def mid(ql, carry):
        q = g * per_group + ql
        xre, xim = _stage2_block(gre_scr, gim_scr, s2_ref, q)
        krows = pl.ds(pl.multiple_of(ql * n1, 8), n1)
        kre = k_ref[0, 0, krows, :]
        kim = k_ref[0, 1, krows, :]
        kre = jnp.concatenate([kre, kre], axis=1)
        kim = jnp.concatenate([kim, kim], axis=1)
        ys = jnp.concatenate([xre * kre - xim * kim, xre * kim + xim * kre], axis=0)
        hh = jnp.dot(s2i_ref[...], ys.astype(BF16), preferred_element_type=F32)
        rows = pl.ds(pl.multiple_of(q * PITCH, 8), n1)
        for b in range(2):
            cols = slice(b * LANES, (b + 1) * LANES)
            gre_scr[b, rows, :] = hh[:n1, cols]
            gim_scr[b, rows, :] = hh[n1:, cols]
        return carry

    lax.fori_loop(0, per_group, mid, 0, unroll=FFT_MID_UNROLL)

    @pl.when(g == ngroups - 1)
    def _():
        _fft_stage1_inv(gre_scr, gim_scr, s1i_ref, y_scr)

        def post(px_ref, cw_ref, cb_ref, order, store):
            def blk(i, carry):
                rows = pl.ds(pl.multiple_of(i * PITCH, 8), n1)
                for b in range(2):
                    zb = z_scr[b, rows, :]
                    gate = _short_conv_block(px_ref, b, i, nblk, cw_ref, cb_ref)
                    store(b, i, rows, gate * (y_scr[b, rows, :] + zb * skip_ref[order:order + 1, :]))
                return carry
            lax.fori_loop(0, nblk, blk, 0)

        @pl.when(c == 0)
        def _():
            def store(b, i, rows, val):
                z_scr[b, rows, :] = val
            post(px1_ref, cw1_ref, cb1_ref, 0, store)

        @pl.when(c == 1)
        def _():
            def store(b, i, rows, val):
                o_ref[b, pl.ds(pl.multiple_of(i * n1, n1), n1), :] = val.astype(o_ref.dtype)
            post(px2_ref, cw2_ref, cb2_ref, 1, store)


def _hyena(pg3, proj_col0, conv_w, conv_b, skip, kspec, s1, s1i, s2, s2i, width):
    bsz, seq, _ = pg3.shape
    assert bsz == 2
    n1 = FFT_N1
    n2 = 2 * seq // n1
    nct = width // LANES
    ngroups = 4
    krows = (n2 // ngroups) * n1
    col = lambda off: (lambda j, c, g: (0, 0, proj_col0 // LANES + off * nct + j))
    cw = lambda off: (lambda j, c, g: (0, off * nct + j))
    const3 = lambda j, c, g: (0, 0, 0)
    const2 = lambda j, c, g: (0, 0)
    return pl.pallas_call(
        _hyena_body,
        grid=(nct, 2, ngroups),
        in_specs=[pl.BlockSpec((2, seq, LANES), col(0)),
                  pl.BlockSpec((2, seq, LANES), col(1)),
                  pl.BlockSpec((2, seq, LANES), col(2)),
                  pl.BlockSpec((3, LANES), cw(0)),
                  pl.BlockSpec((3, LANES), cw(1)),
                  pl.BlockSpec((3, LANES), cw(2)),
                  pl.BlockSpec((1, LANES), cw(0)),
                  pl.BlockSpec((1, LANES), cw(1)),
                  pl.BlockSpec((1, LANES), cw(2)),
                  pl.BlockSpec((2, LANES), lambda j, c, g: (0, j)),
                  pl.BlockSpec((1, 2, krows, LANES), lambda j, c, g: (c, 0, g, j)),
                  pl.BlockSpec(s1.shape, const3),
                  pl.BlockSpec(s1i.shape, const3),
                  pl.BlockSpec(s2.shape, const2),
                  pl.BlockSpec(s2i.shape, const2)],
        out_specs=pl.BlockSpec((2, seq, LANES), lambda j, c, g: (0, 0, j)),
        out_shape=jax.ShapeDtypeStruct((2, seq, width), BF16),
        scratch_shapes=[pltpu.VMEM((2, (n2 // 2) * PITCH, LANES), F32),
                        pltpu.VMEM((2, (n2 // 2) * PITCH, LANES), F32),
                        pltpu.VMEM((2, n2 * PITCH, LANES), F32),
                        pltpu.VMEM((2, n2 * PITCH, LANES), F32)],
        compiler_params=_cparams(("arbitrary", "arbitrary", "arbitrary"), 56),
        name="hyena",
    )(pg3, pg3, pg3, conv_w, conv_w, conv_w, conv_b, conv_b, conv_b, skip, kspec, s1, s1i, s2, s2i)


ATTN_QBLK = 128
ATTN_UNROLL = 4


def _attention_body(pos_ref, freq_ref, sign_ref, *refs):
    qkv_refs = refs[:9]
    o_ref = refs[9]
    cos_scr, sin_scr, q_scr, k_scr, v_scr, og_scr, lse_scr = refs[10:]
    seq = q_scr.shape[0]
    chunk = 512
    nchunks = seq // chunk

    @pl.when(pl.program_id(1) == 0)
    def _():
        def trig(i, carry):
            rows = pl.ds(pl.multiple_of(i * chunk, chunk), chunk)
            ang = pos_ref[0, rows, :].astype(F32) * freq_ref[...]
            cos_scr[rows, :] = jnp.cos(ang)
            sin_scr[rows, :] = jnp.sin(ang) * sign_ref[...]
            return carry
        lax.fori_loop(0, nchunks, trig, 0)

    def rotate(src_ref, dst_ref):
        def body(i, carry):
            rows = pl.ds(pl.multiple_of(i * chunk, chunk), chunk)
            t = src_ref[0, rows, :].astype(F32)
            lane = lax.broadcasted_iota(jnp.int32, t.shape, 1)
            partner = jnp.where(lane < ROT_DIM // 2,
                                pltpu.roll(t, LANES - ROT_DIM // 2, 1), pltpu.roll(t, ROT_DIM // 2, 1))
            dst_ref[rows, :] = t * cos_scr[rows, :] + partner * sin_scr[rows, :]
            return carry
        lax.fori_loop(0, nchunks, body, 0)

    def widen(src_ref, dst_ref):
        def body(i, carry):
            rows = pl.ds(pl.multiple_of(i * chunk, chunk), chunk)
            dst_ref[rows, :] = src_ref[0, rows, :].astype(F32)
            return carry
        lax.fori_loop(0, nchunks, body, 0)

    scale = HEAD_DIM ** -0.5
    for gi, (window, dil) in enumerate(ATTN_GROUPS):
        rotate(qkv_refs[3 * gi], q_scr)
        rotate(qkv_refs[3 * gi + 1], k_scr)
        widen(qkv_refs[3 * gi + 2], v_scr)
        n = seq // dil
        half = window // (2 * dil)
        tk = min(n, ATTN_QBLK + 2 * half)
        blocks_per_res = n // ATTN_QBLK

        def block(u, carry, dil=dil, n=n, half=half, tk=tk, blocks_per_res=blocks_per_res, gi=gi):
            r = u // blocks_per_res
            m = u % blocks_per_res
            q0 = m * ATTN_QBLK
            k0 = jnp.clip(q0 - half, 0, n - tk)
            qrows = pl.ds(r + dil * q0, ATTN_QBLK, stride=dil)
            krows = pl.ds(r + dil * k0, tk, stride=dil)
            qb = q_scr[qrows, :].astype(BF16)
            kb = k_scr[krows, :].astype(BF16)
            vb = v_scr[krows, :].astype(BF16)
            s = lax.dot_general(qb, kb, (((1,), (1,)), ((), ())), preferred_element_type=F32) * scale
            qi = q0 + lax.broadcasted_iota(jnp.int32, (ATTN_QBLK, tk), 0)
            kj = k0 + lax.broadcasted_iota(jnp.int32, (ATTN_QBLK, tk), 1)
            s = jnp.where(jnp.abs(qi - kj) <= half, s, NEG_INF)
            mx = jnp.max(s, axis=-1, keepdims=True)
            p = jnp.exp(s - mx)
            l = jnp.sum(p, axis=-1, keepdims=True)
            o = jnp.dot(p.astype(BF16), vb, preferred_element_type=F32) / l
            og_scr[gi, qrows, :] = o
            lse_scr[gi, qrows, :] = jnp.broadcast_to(mx + jnp.log(l), (ATTN_QBLK, LANES))
            return carry

        lax.fori_loop(0, seq // ATTN_QBLK, block, 0, unroll=ATTN_UNROLL)

    def merge(i, carry):
        rows = pl.ds(pl.multiple_of(i * chunk, chunk), chunk)
        lses = [lse_scr[gi, rows, :] for gi in range(len(ATTN_GROUPS))]
        mx = functools.reduce(jnp.maximum, lses)
        ws = [jnp.exp(v - mx) for v in lses]
        den = functools.reduce(lambda a, b: a + b, ws)
        num = functools.reduce(lambda a, b: a + b,
                               [w * og_scr[gi, rows, :] for gi, w in enumerate(ws)])
        o_ref[0, rows, :] = (num / den).astype(o_ref.dtype)
        return carry

    lax.fori_loop(0, nchunks, merge, 0)


def _attention(pg3, pos3, freq_row, sign_row, qkv_col0):
    bsz, seq, _ = pg3.shape
    ng = len(ATTN_GROUPS)
    in_specs = [pl.BlockSpec((1, seq, 1), lambda b, h: (b, 0, 0)),
                pl.BlockSpec((1, LANES), lambda b, h: (0, 0)),
                pl.BlockSpec((1, LANES), lambda b, h: (0, 0))]
    for gi in range(ng):
        for which in range(3):
            base = qkv_col0 // LANES + which * N_ATTN_HEADS + gi * HEADS_PER_GROUP
            in_specs.append(pl.BlockSpec((1, seq, LANES), lambda b, h, base=base: (b, 0, base + h)))
    return pl.pallas_call(
        _attention_body,
        grid=(bsz, HEADS_PER_GROUP),
        in_specs=in_specs,
        out_specs=pl.BlockSpec((1, seq, LANES), lambda b, h: (b, 0, h)),
        out_shape=jax.ShapeDtypeStruct((bsz, seq, HEADS_PER_GROUP * HEAD_DIM), BF16),
        scratch_shapes=[pltpu.VMEM((seq, LANES), F32), pltpu.VMEM((seq, LANES), F32),
                        pltpu.VMEM((seq, LANES), F32), pltpu.VMEM((seq, LANES), F32),
                        pltpu.VMEM((seq, LANES), F32),
                        pltpu.VMEM((ng, seq, LANES), F32), pltpu.VMEM((ng, seq, LANES), F32)],
        compiler_params=_cparams(("arbitrary", "arbitrary"), 56),
        name="attention",
    )(pos3, freq_row, sign_row, *([pg3] * (3 * ng)))


def _mix_out_body(yhy_ref, yat_ref, ghy_ref, gat_ref, x_ref, mod_ref, gpost_ref, gpre_ref,
                  whyo_ref, wato_ref, wout_ref, wrhi_ref, wrlo_ref, br_ref, x1_ref, h2_ref, lg_ref):
    a = jnp.dot(yhy_ref[...], whyo_ref[...], preferred_element_type=F32)
    b = jnp.dot(yat_ref[...], wato_ref[...], preferred_element_type=F32)
    merged = ghy_ref[...].astype(F32) * a + gat_ref[...].astype(F32) * b
    y = jnp.dot(merged.astype(BF16), wout_ref[...], preferred_element_type=F32)
    x1 = x_ref[...] + mod_ref[0, 2:3, :] * _rms(y, gpost_ref[...])
    x1_ref[...] = x1
    h2 = _rms(x1, gpre_ref[...]) * (1.0 + mod_ref[0, 4:5, :]) + mod_ref[0, 3:4, :]
    h2_ref[...] = h2
    h2_hi = h2.astype(BF16)
    h2_lo = (h2 - h2_hi.astype(F32)).astype(BF16)
    w_hi = wrhi_ref[...]
    w_lo = wrlo_ref[...]
    lg_ref[...] = (jnp.dot(h2_hi, w_hi, preferred_element_type=F32)
                   + jnp.dot(h2_lo, w_hi, preferred_element_type=F32)
                   + jnp.dot(h2_hi, w_lo, preferred_element_type=F32)
                   + jnp.dot(h2_lo, w_lo, preferred_element_type=F32)) + br_ref[...]


def _mix_out(y_hy, y_at, pg, x2d, mod, g_post, g_pre, w_hy_o, w_at_o, w_out, w_r_hi, w_r_lo, b_r, seq):
    t, d = x2d.shape
    tm = 256
    per_batch = seq // tm
    gblk = 0
    const = lambda i: (0, 0)
    return pl.pallas_call(
        _mix_out_body,
        grid=(t // tm,),
        in_specs=[pl.BlockSpec((tm, y_hy.shape[1]), lambda i: (i, 0)),
                  pl.BlockSpec((tm, y_at.shape[1]), lambda i: (i, 0)),
                  pl.BlockSpec((tm, d), lambda i: (i, gblk)),
                  pl.BlockSpec((tm, d), lambda i: (i, gblk + 1)),
                  pl.BlockSpec((tm, d), lambda i: (i, 0)),
                  pl.BlockSpec((1, 6, d), lambda i: (i // per_batch, 0, 0)),
                  pl.BlockSpec((1, d), const),
                  pl.BlockSpec((1, d), const),
                  pl.BlockSpec(w_hy_o.shape, const),
                  pl.BlockSpec(w_at_o.shape, const),
                  pl.BlockSpec(w_out.shape, const),
                  pl.BlockSpec(w_r_hi.shape, const),
                  pl.BlockSpec(w_r_lo.shape, const),
                  pl.BlockSpec((1, LANES), const)],
        out_specs=[pl.BlockSpec((tm, d), lambda i: (i, 0)),
                   pl.BlockSpec((tm, d), lambda i: (i, 0)),
                   pl.BlockSpec((tm, LANES), lambda i: (i, 0))],
        out_shape=[jax.ShapeDtypeStruct((t, d), F32),
                   jax.ShapeDtypeStruct((t, d), F32),
                   jax.ShapeDtypeStruct((t, LANES), F32)],
        compiler_params=_cparams(("arbitrary",), 56),
        name="mix_out",
    )(y_hy, y_at, pg, pg, x2d, mod, g_post, g_pre, w_hy_o, w_at_o, w_out, w_r_hi, w_r_lo, b_r)


def _route_body(lg_ref, eid_ref, gate_ref):
    lg = lg_ref[...]
    lane = lax.broadcasted_iota(jnp.int32, lg.shape, 1)
    big = jnp.int32(1 << 20)

    def first_argmax(vals, mask):
        v = jnp.where(mask, vals, -jnp.inf)
        mx = jnp.max(v, axis=-1, keepdims=True)
        idx = jnp.min(jnp.where(mask & (v == mx), lane, big), axis=-1, keepdims=True)
        return mx, idx

    gmask = lane < N_EXPERT_GROUPS
    gmax, gidx = first_argmax(lg, gmask)
    gval = 1.0 / jnp.sum(jnp.where(gmask, jnp.exp(lg - gmax), 0.0), axis=-1, keepdims=True)
    lo = N_EXPERT_GROUPS + gidx * EXPERTS_PER_GROUP
    emask = (lane >= lo) & (lane < lo + EXPERTS_PER_GROUP)
    v1, i1 = first_argmax(lg, emask)
    v2, i2 = first_argmax(lg, emask & (lane != i1))
    e2 = jnp.exp(v2 - v1)
    p1 = 1.0 / (1.0 + e2)
    p2 = e2 / (1.0 + e2)
    eid = jnp.where(lane == 0, i1, i2) - N_EXPERT_GROUPS
    gate = gval * jnp.where(lane == 0, p1, p2)
    eid_ref[...] = eid[:, :TOP_K]
    gate_ref[...] = gate[:, :TOP_K]


def _route(logits):
    t = logits.shape[0]
    tm = 512
    return pl.pallas_call(
        _route_body,
        grid=(t // tm,),
        in_specs=[pl.BlockSpec((tm, LANES), lambda i: (i, 0))],
        out_specs=[pl.BlockSpec((tm, TOP_K), lambda i: (i, 0)),
                   pl.BlockSpec((tm, TOP_K), lambda i: (i, 0))],
        out_shape=[jax.ShapeDtypeStruct((t, TOP_K), jnp.int32),
                   jax.ShapeDtypeStruct((t, TOP_K), F32)],
        compiler_params=_cparams(("arbitrary",), 32),
        name="route",
    )(logits)


def _rank_body(e_ref, rank_ref, cnt_ref, carry_scr):
    i = pl.program_id(0)
    r = e_ref.shape[0]

    @pl.when(i == 0)
    def _():
        carry_scr[...] = jnp.zeros_like(carry_scr)

    lane = lax.broadcasted_iota(jnp.int32, (r, LANES), 1)
    onehot = (lane == e_ref[...]).astype(F32)
    tri = (lax.broadcasted_iota(jnp.int32, (r, r), 1)
           < lax.broadcasted_iota(jnp.int32, (r, r), 0)).astype(BF16)
    before = jnp.dot(tri, onehot.astype(BF16), preferred_element_type=F32) + carry_scr[0:1, :]
    rank_ref[...] = jnp.sum(onehot * before, axis=-1, keepdims=True).astype(jnp.int32)
    total = carry_scr[0:1, :] + jnp.sum(onehot, axis=0, keepdims=True)
    carry_scr[...] = jnp.broadcast_to(total, carry_scr.shape)
    cnt_ref[...] = jnp.broadcast_to(total, cnt_ref.shape).astype(jnp.int32)


def _rank(flat_e):
    a = flat_e.shape[0]
    r = 512
    return pl.pallas_call(
        _rank_body,
        grid=(a // r,),
        in_specs=[pl.BlockSpec((r, 1), lambda i: (i, 0))],
        out_specs=[pl.BlockSpec((r, 1), lambda i: (i, 0)),
                   pl.BlockSpec((8, LANES), lambda i: (0, 0))],
        out_shape=[jax.ShapeDtypeStruct((a, 1), jnp.int32),
                   jax.ShapeDtypeStruct((8, LANES), jnp.int32)],
        scratch_shapes=[pltpu.VMEM((8, LANES), F32)],
        compiler_params=_cparams(("arbitrary",), 32),
        name="rank",
    )(flat_e)


def _dest_body(e_ref, rank_ref, ps_ref, o_ref):
    lane = lax.broadcasted_iota(jnp.int32, (e_ref.shape[0], LANES), 1)
    first = jnp.sum(jnp.where(lane == e_ref[...], ps_ref[...], 0.0), axis=-1, keepdims=True)
    o_ref[...] = first.astype(jnp.int32) + rank_ref[...]


def _dest(flat_e, rank, pstarts_row):
    a = flat_e.shape[0]
    r = 2048
    return pl.pallas_call(
        _dest_body,
        grid=(a // r,),
        in_specs=[pl.BlockSpec((r, 1), lambda i: (i, 0)),
                  pl.BlockSpec((r, 1), lambda i: (i, 0)),
                  pl.BlockSpec((1, LANES), lambda i: (0, 0))],
        out_specs=pl.BlockSpec((r, 1), lambda i: (i, 0)),
        out_shape=jax.ShapeDtypeStruct((a, 1), jnp.int32),
        compiler_params=_cparams(("arbitrary",), 32),
        name="dest",
    )(flat_e, rank, pstarts_row)


DMA_UNROLL = 8


DISPATCH_CHUNK = 256


def _dispatch_body(src_ref, h_hbm, x_ref, x_scr, sem):
    c = pl.program_id(0)

    def chunk(cc, act):
        base = cc * DISPATCH_CHUNK

        def body(r, carry):
            act(pltpu.make_async_copy(h_hbm.at[pl.ds(src_ref[base + r], 1)],
                                      x_scr.at[cc % 2, pl.ds(r, 1)], sem.at[cc % 2]))
            return carry

        lax.fori_loop(0, DISPATCH_CHUNK, body, 0, unroll=DMA_UNROLL)

    @pl.when(c == 0)
    def _():
        chunk(c, lambda cp: cp.start())

    @pl.when(c + 1 < pl.num_programs(0))
    def _():
        chunk(c + 1, lambda cp: cp.start())

    chunk(c, lambda cp: cp.wait())
    x_ref[...] = x_scr[c % 2]


def _dispatch(src_tok, h2):
    d = h2.shape[1]
    grid_spec = pltpu.PrefetchScalarGridSpec(
        num_scalar_prefetch=1,
        grid=(src_tok.shape[0] // DISPATCH_CHUNK,),
        in_specs=[pl.BlockSpec(memory_space=pl.ANY)],
        out_specs=pl.BlockSpec((DISPATCH_CHUNK, d), lambda c, src: (c, 0)),
        scratch_shapes=[pltpu.VMEM((2, DISPATCH_CHUNK, d), h2.dtype), pltpu.SemaphoreType.DMA((2,))],
    )
    return pl.pallas_call(
        _dispatch_body,
        grid_spec=grid_spec,
        out_shape=jax.ShapeDtypeStruct((src_tok.shape[0], h2.shape[1]), h2.dtype),
        compiler_params=_cparams(("arbitrary",), 32),
        name="dispatch",
    )(src_tok, h2)


def _experts_body(ord_ref, elist_ref, cnt_ref, x_ref, w1_hbm, w3_hbm, w2_hbm, y_ref,
                  w1_scr, w3_scr, w2_scr, wsem):
    i = pl.program_id(0)
    nused = cnt_ref[0]
    nexp = cnt_ref[1]
    k = ord_ref[i]
    first = (i == 0) | (k != ord_ref[jnp.maximum(i - 1, 0)])

    def weights(kk, act):
        e = elist_ref[kk]
        slot = kk % 2
        act(pltpu.make_async_copy(w1_hbm.at[e], w1_scr.at[slot], wsem.at[slot, 0]))
        act(pltpu.make_async_copy(w3_hbm.at[e], w3_scr.at[slot], wsem.at[slot, 1]))
        act(pltpu.make_async_copy(w2_hbm.at[e], w2_scr.at[slot], wsem.at[slot, 2]))

    @pl.when(i == 0)
    def _():
        weights(k, lambda cp: cp.start())

    @pl.when((i < nused) & first & (k + 1 < nexp))
    def _():
        weights(k + 1, lambda cp: cp.start())

    @pl.when((i < nused) & first)
    def _():
        weights(k, lambda cp: cp.wait())

    @pl.when(i < nused)
    def _():
        wslot = k % 2
        xb = x_ref[...].astype(BF16)
        a = jnp.dot(xb, w1_scr[wslot].astype(BF16), preferred_element_type=F32)
        b = jnp.dot(xb, w3_scr[wslot].astype(BF16), preferred_element_type=F32)
        act = (a * jax.nn.sigmoid(a) * b).astype(BF16)
        y_ref[...] = jnp.dot(act, w2_scr[wslot].astype(BF16), preferred_element_type=F32)

    @pl.when(i >= nused)
    def _():
        y_ref[...] = jnp.zeros_like(y_ref)


def _experts(block_ord, elist, counts2, xbuf, w1, w3, w2):
    nblocks = block_ord.shape[0]
    d = xbuf.shape[1]
    ff = w1.shape[2]
    rb = EXPERT_ROW_BLOCK
    grid_spec = pltpu.PrefetchScalarGridSpec(
        num_scalar_prefetch=3,
        grid=(nblocks,),
        in_specs=[pl.BlockSpec((rb, d), lambda i, od, el, cnt: (jnp.minimum(i, cnt[0] - 1), 0))]
        + [pl.BlockSpec(memory_space=pl.ANY)] * 3,
        out_specs=pl.BlockSpec((rb, d), lambda i, *_: (i, 0)),
        scratch_shapes=[pltpu.VMEM((2, d, ff), F32), pltpu.VMEM((2, d, ff), F32),
                        pltpu.VMEM((2, ff, d), F32), pltpu.SemaphoreType.DMA((2, 3))],
    )
    return pl.pallas_call(
        _experts_body,
        grid_spec=grid_spec,
        out_shape=jax.ShapeDtypeStruct((nblocks * rb, d), F32),
        compiler_params=_cparams(("arbitrary",), 56),
        name="experts",
    )(block_ord, elist, counts2, xbuf, w1, w3, w2)


def _combine_body(dest_ref, y_hbm, gate_ref, x1_ref, mod_ref, g_ref, o_ref, buf_scr, sem):
    i = pl.program_id(0)
    tm = x1_ref.shape[0]

    def gather(step, act):
        slot = step % 2
        base = step * tm * TOP_K

        def body(r, carry):
            for k in range(TOP_K):
                act(pltpu.make_async_copy(y_hbm.at[pl.ds(dest_ref[base + TOP_K * r + k], 1)],
                                          buf_scr.at[slot, k, pl.ds(r, 1)], sem.at[slot]))
            return carry

        lax.fori_loop(0, tm, body, 0, unroll=DMA_UNROLL // TOP_K)

    @pl.when(i == 0)
    def _():
        gather(i, lambda cp: cp.start())

    @pl.when(i + 1 < pl.num_programs(0))
    def _():
        gather(i + 1, lambda cp: cp.start())

    gather(i, lambda cp: cp.wait())
    slot = i % 2
    gate = gate_ref[...]
    y = buf_scr[slot, 0] * gate[:, 0:1] + buf_scr[slot, 1] * gate[:, 1:2]
    o_ref[...] = x1_ref[...] + mod_ref[0, 5:6, :] * _rms(y, g_ref[...])


def _combine(dest, ybuf, gate, x1, mod, g_post, seq):
    t, d = x1.shape
    tm = 128
    per_batch = seq // tm
    grid_spec = pltpu.PrefetchScalarGridSpec(
        num_scalar_prefetch=1,
        grid=(t // tm,),
        in_specs=[pl.BlockSpec(memory_space=pl.ANY),
                  pl.BlockSpec((tm, TOP_K), lambda i, dst: (i, 0)),
                  pl.BlockSpec((tm, d), lambda i, dst: (i, 0)),
                  pl.BlockSpec((1, 6, d), lambda i, dst: (i // per_batch, 0, 0)),
                  pl.BlockSpec((1, d), lambda i, dst: (0, 0))],
        out_specs=pl.BlockSpec((tm, d), lambda i, dst: (i, 0)),
        scratch_shapes=[pltpu.VMEM((2, TOP_K, tm, d), F32), pltpu.SemaphoreType.DMA((2,))],
    )
    return pl.pallas_call(
        _combine_body,
        grid_spec=grid_spec,
        out_shape=jax.ShapeDtypeStruct((t, d), F32),
        compiler_params=_cparams(("arbitrary",), 32),
        name="combine",
    )(dest, ybuf, gate, x1, mod, g_post)


def _layer(x, c, positions, w_ada, b_ada, g_mix_pre, g_mix_post, g_ffn_pre, g_ffn_post,
           w_in, conv_w, conv_b, filt_w1, filt_b1, filt_w2, filt_b2, filt_w3, filt_freq,
           hyena_skip, w_branch_gate, b_branch_gate, w_hy_o, w_at_o, w_out,
           w_group, b_group, w_expert, b_expert, w1_exp, w3_exp, w2_exp):
    bsz, seq, d = x.shape
    t = bsz * seq
    width = hyena_skip.shape[1]
    row = lambda v: v.reshape(1, -1)

    c_pad = jnp.pad(c, ((0, 8 - bsz), (0, 0)))
    mod = _adaln(c_pad, w_ada, row(b_ada))[:bsz].reshape(bsz, 6, d)

    x2d = x.reshape(t, d)
    n_gate = w_branch_gate.shape[1]
    n_cat = -(-(n_gate + w_in.shape[1]) // IN_PROJ_TN) * IN_PROJ_TN
    n_zero = n_cat - n_gate - w_in.shape[1]
    w_cat = jnp.concatenate([w_branch_gate.astype(BF16), w_in.astype(BF16),
                             jnp.zeros((d, n_zero), BF16)], axis=1)
    b_cat = jnp.concatenate([b_branch_gate, jnp.zeros((n_cat - n_gate,), F32)]).reshape(1, -1)
    pg = _in_proj(x2d, mod, row(g_mix_pre), w_cat, b_cat, n_gate, seq)
    pg3 = pg.reshape(bsz, seq, -1)

    hidden = filt_w2.shape[0]
    bands = np.zeros((1, LANES), np.float32)
    band_vals = np.linspace(1e-4, FILTER_BANDS - 1, FILTER_BANDS, dtype=np.float32)
    bands[0, 1:1 + FILTER_BANDS] = band_vals
    bands[0, 1 + FILTER_BANDS:1 + 2 * FILTER_BANDS] = band_vals
    w1p = jnp.pad(filt_w1, ((0, LANES - filt_w1.shape[0]), (0, 0)))
    max_decay = math.log(DECAY_TARGET) / FAST_DECAY_PCT
    min_decay = math.log(DECAY_TARGET) / SLOW_DECAY_PCT
    deltas = jnp.abs(jnp.linspace(min_decay, max_decay, width, dtype=F32)).reshape(1, -1)
    uw = _filters(seq, width, jnp.asarray(bands), w1p, row(filt_b1), filt_w2, row(filt_b2),
                  row(filt_freq), deltas, filt_w3)
    s1, s1i, s2, s2i = _fft_tables(FFT_N1, 2 * seq // FFT_N1)
    kspec = _spectra(uw, s1, s2)
    y_hy = _hyena(pg3, n_gate, conv_w, row(conv_b), hyena_skip, kspec, s1, s1i, s2, s2i, width)

    half = ROT_DIM // 2
    inv_freq = np.power(ROPE_THETA, -2.0 * np.arange(half, dtype=np.float32) / ROT_DIM).astype(np.float32)
    freq_row = np.zeros((1, LANES), np.float32)
    freq_row[0, :half] = inv_freq
    freq_row[0, half:ROT_DIM] = inv_freq
    sign_row = np.zeros((1, LANES), np.float32)
    sign_row[0, :half] = -1.0
    sign_row[0, half:ROT_DIM] = 1.0
    y_at = _attention(pg3, positions.reshape(bsz, seq, 1), jnp.asarray(freq_row), jnp.asarray(sign_row),
                      n_gate + 3 * width)

    w_r = jnp.concatenate([w_group, jnp.transpose(w_expert, (1, 0, 2)).reshape(d, N_EXPERTS)], axis=1)
    w_r = jnp.pad(w_r, ((0, 0), (0, LANES - w_r.shape[1])))
    w_r_hi = w_r.astype(BF16)
    w_r_lo = (w_r - w_r_hi.astype(F32)).astype(BF16)
    b_r = jnp.pad(jnp.concatenate([b_group, b_expert.reshape(-1)]), (0, LANES - N_EXPERT_GROUPS - N_EXPERTS))
    x1, h2, logits = _mix_out(y_hy.reshape(t, width), y_at.reshape(t, -1), pg, x2d, mod,
                              row(g_mix_post), row(g_ffn_pre), w_hy_o.astype(BF16), w_at_o.astype(BF16),
                              w_out.astype(BF16), w_r_hi, w_r_lo, b_r.reshape(1, -1), seq)

    eid, gate = _route(logits)
    flat_e = eid.reshape(t * TOP_K, 1)
    rank, counts = _rank(flat_e)
    counts = counts[0, :N_EXPERTS]
    rb = EXPERT_ROW_BLOCK
    padded = (counts + rb - 1) // rb * rb
    pends = jnp.cumsum(padded)
    pstarts = pends - padded
    pstarts_row = jnp.pad(pstarts.astype(F32), (0, LANES - N_EXPERTS)).reshape(1, LANES)
    dest = _dest(flat_e, rank, pstarts_row)[:, 0]
    n_blocks = t * TOP_K // rb + N_EXPERTS
    block_start = jnp.arange(n_blocks, dtype=jnp.int32) * rb
    block_e = jnp.minimum(jnp.searchsorted(pends, block_start, side='right'), N_EXPERTS - 1).astype(jnp.int32)
    tok = jnp.arange(t * TOP_K, dtype=jnp.int32) // TOP_K
    src_tok = (jnp.arange(n_blocks * rb, dtype=jnp.int32) % t).at[dest].set(tok)
    has_rows = jnp.cumsum((counts > 0).astype(jnp.int32))
    elist = jnp.minimum(jnp.searchsorted(has_rows, jnp.arange(1, N_EXPERTS + 1, dtype=jnp.int32), side='left'),
                        N_EXPERTS - 1).astype(jnp.int32)
    block_ord = (has_rows - 1)[block_e].astype(jnp.int32)
    counts2 = jnp.stack([pends[-1] // rb, has_rows[-1]]).astype(jnp.int32)

    xbuf = _dispatch(src_tok, h2)
    ybuf = _experts(block_ord, elist, counts2, xbuf, w1_exp, w3_exp, w2_exp)
    out = _combine(dest, ybuf, gate, x1, mod, row(g_ffn_post), seq)
    return out.reshape(bsz, seq, d)


def kernel(x, c, positions, w_ada, b_ada, g_mix_pre, g_mix_post, g_ffn_pre, g_ffn_post, w_in, conv_w, conv_b, filt_w1, filt_b1, filt_w2, filt_b2, filt_w3, filt_freq, hyena_skip, w_branch_gate, b_branch_gate, w_hy_o, w_at_o, w_out, w_group, b_group, w_expert, b_expert, w1_exp, w3_exp, w2_exp):
    depth = w_ada.shape[0]
    for l in range(depth):
        x = _layer(x, c, positions, w_ada[l], b_ada[l], g_mix_pre[l], g_mix_post[l], g_ffn_pre[l],
                   g_ffn_post[l], w_in[l], conv_w[l], conv_b[l], filt_w1[l], filt_b1[l], filt_w2[l],
                   filt_b2[l], filt_w3[l], filt_freq[l], hyena_skip[l], w_branch_gate[l],
                   b_branch_gate[l], w_hy_o[l], w_at_o[l], w_out[l], w_group[l], b_group[l],
                   w_expert[l], b_expert[l], w1_exp[l], w3_exp[l], w2_exp[l])
    return x
```

```python
import functools
import math

import numpy as np
import jax
import jax.numpy as jnp
from jax import lax
from jax.experimental import pallas as pl
from jax.experimental.pallas import tpu as pltpu

F32 = jnp.float32
BF16 = jnp.bfloat16

LANES = 128
MIB = 1024 * 1024

RMS_EPS = 1e-6
NEG_INF = -1e30

HEAD_DIM = 128
ROT_DIM = HEAD_DIM // 4
ROPE_THETA = 500000.0
ATTN_GROUPS = ((128, 1), (512, 4), (2048, 16))
HEADS_PER_GROUP = 4
N_ATTN_HEADS = HEADS_PER_GROUP * len(ATTN_GROUPS)

FILTER_BANDS = 16
DECAY_TARGET = 1e-2
FAST_DECAY_PCT = 0.3
SLOW_DECAY_PCT = 1.5

N_EXPERT_GROUPS = 8
EXPERTS_PER_GROUP = 8
N_EXPERTS = N_EXPERT_GROUPS * EXPERTS_PER_GROUP
TOP_K = 2
EXPERT_ROW_BLOCK = 128

FFT_N1 = 128
PITCH = FFT_N1 + 8
FFT_UNROLL = 16
FFT_MID_UNROLL = 8


def _cparams(sem, vmem_mib):
    return pltpu.CompilerParams(dimension_semantics=sem, vmem_limit_bytes=vmem_mib * MIB)


def _rms(x, g):
    return x * lax.rsqrt(jnp.mean(x * x, axis=-1, keepdims=True) + RMS_EPS) * g


def _adaln_body(c_ref, w_ref, b_ref, o_ref):
    c = c_ref[...]
    cond = c * jax.nn.sigmoid(c)
    o_ref[...] = jnp.dot(cond.astype(BF16), w_ref[...].astype(BF16),
                         preferred_element_type=F32) + b_ref[...]


def _adaln(c_pad, w_ada, b_ada):
    rows, d = c_pad.shape
    n = w_ada.shape[1]
    tn = 1024
    return pl.pallas_call(
        _adaln_body,
        grid=(n // tn,),
        in_specs=[pl.BlockSpec((rows, d), lambda j: (0, 0)),
                  pl.BlockSpec((d, tn), lambda j: (0, j)),
                  pl.BlockSpec((1, tn), lambda j: (0, j))],
        out_specs=pl.BlockSpec((rows, tn), lambda j: (0, j)),
        out_shape=jax.ShapeDtypeStruct((rows, n), F32),
        compiler_params=_cparams(("arbitrary",), 40),
        name="adaln",
    )(c_pad, w_ada, b_ada)


IN_PROJ_TM = 1024
IN_PROJ_TN = 1024
IN_PROJ_SUB_M = 256
IN_PROJ_SUB_N = 512


def _in_proj_body(n_gate, x_ref, mod_ref, g_ref, w_ref, b_ref, o_ref, h_scr):
    j = pl.program_id(1)

    @pl.when(j == 0)
    def _():
        x = x_ref[...]
        h = _rms(x, g_ref[...]) * (1.0 + mod_ref[0, 1:2, :]) + mod_ref[0, 0:1, :]
        h_scr[...] = h.astype(BF16)

    is_gate = j < n_gate
    tm, tn = o_ref.shape
    for mi in range(tm // IN_PROJ_SUB_M):
        rows = slice(mi * IN_PROJ_SUB_M, (mi + 1) * IN_PROJ_SUB_M)
        for ni in range(tn // IN_PROJ_SUB_N):
            cols = slice(ni * IN_PROJ_SUB_N, (ni + 1) * IN_PROJ_SUB_N)
            acc = jnp.dot(h_scr[rows, :], w_ref[:, cols], preferred_element_type=F32) + b_ref[:, cols]
            o_ref[rows, cols] = jnp.where(is_gate, jax.nn.sigmoid(acc), acc).astype(o_ref.dtype)


def _in_proj(x2d, mod, g_pre, w_cat, b_cat, n_gate_cols, seq):
    t, d = x2d.shape
    n = w_cat.shape[1]
    tm, tn = IN_PROJ_TM, IN_PROJ_TN
    per_batch = seq // tm
    return pl.pallas_call(
        functools.partial(_in_proj_body, n_gate_cols // tn),
        grid=(t // tm, n // tn),
        in_specs=[pl.BlockSpec((tm, d), lambda i, j: (i, 0)),
                  pl.BlockSpec((1, 6, d), lambda i, j: (i // per_batch, 0, 0)),
                  pl.BlockSpec((1, d), lambda i, j: (0, 0)),
                  pl.BlockSpec((d, tn), lambda i, j: (0, j)),
                  pl.BlockSpec((1, tn), lambda i, j: (0, j))],
        out_specs=pl.BlockSpec((tm, tn), lambda i, j: (i, j)),
        out_shape=jax.ShapeDtypeStruct((t, n), BF16),
        scratch_shapes=[pltpu.VMEM((tm, d), BF16)],
        compiler_params=_cparams(("arbitrary", "arbitrary"), 56),
        name="in_proj",
    )(x2d, mod, g_pre, w_cat, b_cat)


def _filters_body(seq, band_ref, w1_ref, b1_ref, w2_ref, b2_ref, fr_ref, dl_ref,
                  w3a_ref, w3b_ref, o_ref, hid_scr):
    i = pl.program_id(0)
    j = pl.program_id(1)
    tl = hid_scr.shape[0]
    row = (lax.broadcasted_iota(jnp.int32, (tl, 1), 0) + i * tl).astype(F32)

    @pl.when((j == 0) & (pl.program_id(2) == 0))
    def _():
        lane = lax.broadcasted_iota(jnp.int32, (tl, LANES), 1)
        t = row / (seq - 1.0)
        ang = band_ref[...] * (2.0 * math.pi * row / seq)
        feats = jnp.where(lane == 0, t,
                          jnp.where(lane <= FILTER_BANDS, jnp.cos(ang),
                                    jnp.where(lane <= 2 * FILTER_BANDS, -jnp.sin(ang), 0.0)))
        hi = lax.Precision.HIGHEST
        fr = fr_ref[...]
        hid = jnp.sin(fr * (jnp.dot(feats, w1_ref[...], precision=hi, preferred_element_type=F32)
                            + b1_ref[...]))
        hid = jnp.sin(fr * (jnp.dot(hid, w2_ref[...], precision=hi, preferred_element_type=F32)
                            + b2_ref[...]))
        hid_scr[...] = hid

    hi = lax.Precision.HIGHEST
    hid = hid_scr[...]
    decay = jnp.exp(-(row / (seq - 1.0)) * dl_ref[...])
    hf = jnp.dot(hid, w3a_ref[...], precision=hi, preferred_element_type=F32) * decay
    hb = jnp.dot(hid, w3b_ref[...], precision=hi, preferred_element_type=F32) * decay
    hb = jnp.where(row == 0.0, 0.0, hb)
    o_ref[0, 0] = hf + hb
    o_ref[0, 1] = hf - hb


def _filters(seq, width, bands, w1p, b1, w2, b2, freq, deltas, w3):
    tl, tc = 512, 256
    nct = width // tc
    hidden = w2.shape[0]
    const = lambda i, j, o: (0, 0)
    return pl.pallas_call(
        functools.partial(_filters_body, float(seq)),
        grid=(seq // tl, nct, 2),
        in_specs=[pl.BlockSpec((1, LANES), const),
                  pl.BlockSpec((LANES, hidden), const),
                  pl.BlockSpec((1, hidden), const),
                  pl.BlockSpec((hidden, hidden), const),
                  pl.BlockSpec((1, hidden), const),
                  pl.BlockSpec((1, hidden), const),
                  pl.BlockSpec((1, tc), lambda i, j, o: (0, j)),
                  pl.BlockSpec((hidden, tc), lambda i, j, o: (0, (2 * o) * nct + j)),
                  pl.BlockSpec((hidden, tc), lambda i, j, o: (0, (2 * o + 1) * nct + j))],
        out_specs=pl.BlockSpec((1, 2, tl, tc), lambda i, j, o: (o, 0, i, j)),
        out_shape=jax.ShapeDtypeStruct((2, 2, seq, width), F32),
        scratch_shapes=[pltpu.VMEM((tl, hidden), F32)],
        compiler_params=_cparams(("arbitrary", "arbitrary", "arbitrary"), 32),
        name="filters",
    )(bands, w1p, b1, w2, b2, freq, deltas, w3, w3)


def _fft_tables(n1, n2):
    n = n1 * n2
    q = np.arange(n2)[:, None]
    b = np.arange(n2 // 2)[None, :]
    a = np.arange(n1)[:, None, None]
    ang = -2.0 * np.pi * (a * q[None] / n + (q * b)[None] / n2)
    stage1 = np.concatenate([np.cos(ang), np.sin(ang)], axis=1)
    stage1_inv = np.transpose(stage1, (0, 2, 1)) / n
    p = np.arange(n1)
    ang2 = -2.0 * np.pi * np.outer(p, p) / n1
    fre, fim = np.cos(ang2), np.sin(ang2)
    stage2 = np.block([[fre, -fim], [fim, fre]])
    stage2_inv = np.block([[fre, fim], [-fim, fre]])
    as_bf16 = lambda m: jnp.asarray(m, dtype=F32).astype(BF16)
    return as_bf16(stage1), as_bf16(stage1_inv), as_bf16(stage2), as_bf16(stage2_inv)


def _halves(ref, rows):
    return jnp.concatenate([ref[0, rows, :], ref[1, rows, :]], axis=1)


def _fft_stage1(z_ref, s1_ref, gre_ref, gim_ref):
    n1, two_n2, n2h = s1_ref.shape
    n2 = two_n2 // 2

    def step(a, carry):
        zrows = _halves(z_ref, pl.ds(a, n2h, stride=PITCH))
        g = jnp.dot(s1_ref[a], zrows.astype(BF16), preferred_element_type=F32)
        for h in range(2):
            cols = slice(h * LANES, (h + 1) * LANES)
            gre_ref[h, pl.ds(a, n2, stride=PITCH), :] = g[:n2, cols]
            gim_ref[h, pl.ds(a, n2, stride=PITCH), :] = g[n2:, cols]
        return carry

    lax.fori_loop(0, n1, step, 0, unroll=FFT_UNROLL)


def _fft_stage1_inv(gre_ref, gim_ref, s1i_ref, y_ref):
    n1, n2h, two_n2 = s1i_ref.shape
    n2 = two_n2 // 2

    def step(a, carry):
        rows = pl.ds(a, n2, stride=PITCH)
        hs = jnp.concatenate([_halves(gre_ref, rows), _halves(gim_ref, rows)], axis=0)
        y = jnp.dot(s1i_ref[a], hs.astype(BF16), preferred_element_type=F32)
        for h in range(2):
            y_ref[h, pl.ds(a, n2h, stride=PITCH), :] = y[:, h * LANES:(h + 1) * LANES]
        return carry

    lax.fori_loop(0, n1, step, 0, unroll=FFT_UNROLL)


def _stage2_block(gre_ref, gim_ref, s2_ref, q):
    n1 = s2_ref.shape[0] // 2
    rows = pl.ds(pl.multiple_of(q * PITCH, 8), n1)
    gs = jnp.concatenate([_halves(gre_ref, rows), _halves(gim_ref, rows)], axis=0)
    x = jnp.dot(s2_ref[...], gs.astype(BF16), preferred_element_type=F32)
    return x[:n1], x[n1:]


def _spectra_body(uw_ref, s1_ref, s2_ref, k_ref, z_scr, gre_scr, gim_scr):
    n1 = s2_ref.shape[0] // 2
    n2 = s1_ref.shape[1] // 2
    for h in range(2):
        for b in range(n2 // 2):
            z_scr[h, pl.ds(b * PITCH, n1), :] = uw_ref[0, h, pl.ds(b * n1, n1), :]
    _fft_stage1(z_scr, s1_ref, gre_scr, gim_scr)

    def step(q, carry):
        xre, xim = _stage2_block(gre_scr, gim_scr, s2_ref, q)
        rows = pl.ds(pl.multiple_of(q * n1, 8), n1)
        k_ref[0, 0, rows, :] = xre[:, :LANES]
        k_ref[0, 1, rows, :] = xim[:, LANES:]
        return carry

    lax.fori_loop(0, n2, step, 0, unroll=FFT_MID_UNROLL)


def _spectra(uw, s1, s2):
    _, _, seq, width = uw.shape
    n1 = FFT_N1
    n2 = 2 * seq // n1
    const3 = lambda j, o: (0, 0, 0)
    return pl.pallas_call(
        _spectra_body,
        grid=(width // LANES, 2),
        in_specs=[pl.BlockSpec((1, 2, seq, LANES), lambda j, o: (o, 0, 0, j)),
                  pl.BlockSpec(s1.shape, const3),
                  pl.BlockSpec(s2.shape, lambda j, o: (0, 0))],
        out_specs=pl.BlockSpec((1, 2, 2 * seq, LANES), lambda j, o: (o, 0, 0, j)),
        out_shape=jax.ShapeDtypeStruct((2, 2, 2 * seq, width), F32),
        scratch_shapes=[pltpu.VMEM((2, (n2 // 2) * PITCH, LANES), F32),
                        pltpu.VMEM((2, n2 * PITCH, LANES), F32),
                        pltpu.VMEM((2, n2 * PITCH, LANES), F32)],
        compiler_params=_cparams(("arbitrary", "arbitrary"), 56),
        name="spectra",
    )(uw, s1, s2)


def _short_conv_block(p_ref, b, i, nblk, w_ref, bias_ref):
    n1 = FFT_N1
    start = pl.multiple_of(i * n1, n1)
    cur = p_ref[b, pl.ds(start, n1), :].astype(F32)
    before = p_ref[b, pl.ds(pl.multiple_of(jnp.maximum(start - 16, 0), 16), 16), :].astype(F32)
    after = p_ref[b, pl.ds(pl.multiple_of(jnp.minimum(start + n1, (nblk - 1) * n1), 16), 16), :].astype(F32)
    last_prev = before[15:16] * jnp.where(i > 0, 1.0, 0.0).astype(F32)
    first_next = after[0:1] * jnp.where(i < nblk - 1, 1.0, 0.0).astype(F32)
    row = lax.broadcasted_iota(jnp.int32, (n1, 1), 0)
    prev = jnp.where(row == 0, last_prev, pltpu.roll(cur, 1, 0))
    nxt = jnp.where(row == n1 - 1, first_next, pltpu.roll(cur, n1 - 1, 0))
    return prev * w_ref[0:1, :] + cur * w_ref[1:2, :] + nxt * w_ref[2:3, :] + bias_ref[...]


def _hyena_body(pv_ref, px1_ref, px2_ref, cwv_ref, cw1_ref, cw2_ref, cbv_ref, cb1_ref, cb2_ref,
                skip_ref, k_ref, s1_ref, s1i_ref, s2_ref, s2i_ref, o_ref,
                z_scr, y_scr, gre_scr, gim_scr):
    c = pl.program_id(1)
    g = pl.program_id(2)
    ngroups = pl.num_programs(2)
    n1 = FFT_N1
    n2 = s1_ref.shape[1] // 2
    nblk = n2 // 2
    per_group = n2 // ngroups

    @pl.when(g == 0)
    def _():
        @pl.when(c == 0)
        def _():
            def fill(i, carry):
                for b in range(2):
                    z_scr[b, pl.ds(pl.multiple_of(i * PITCH, 8), n1), :] = _short_conv_block(
                        pv_ref, b, i, nblk, cwv_ref, cbv_ref)
                return carry
            lax.fori_loop(0, nblk, fill, 0)

        _fft_stage1(z_scr, s1_ref, gre_scr, gim_scr)

    def mid(ql, carry):
        q = g * per_group + ql
        xre, xim = _stage2_block(gre_scr, gim_scr, s2_ref, q)
        krows = pl.ds(pl.multiple_of(ql * n1, 8), n1)
        kre = k_ref[0, 0, krows, :]
        kim = k_ref[0, 1, krows, :]
        kre = jnp.concatenate([kre, kre], axis=1)
        kim = jnp.concatenate([kim, kim], axis=1)
        ys = jnp.concatenate([xre * kre - xim * kim, xre * kim + xim * kre], axis=0)
        hh = jnp.dot(s2i_ref[...], ys.astype(BF16), preferred_element_type=F32)
        rows = pl.ds(pl.multiple_of(q * PITCH, 8), n1)
        for b in range(2):
            cols = slice(b * LANES, (b + 1) * LANES)
            gre_scr[b, rows, :] = hh[:n1, cols]
            gim_scr[b, rows, :] = hh[n1:, cols]
        return carry

    lax.fori_loop(0, per_group, mid, 0, unroll=FFT_MID_UNROLL)

    @pl.when(g == ngroups - 1)
    def _():
        _fft_stage1_inv(gre_scr, gim_scr, s1i_ref, y_scr)

        def post(px_ref, cw_ref, cb_ref, order, store):
            def blk(i, carry):
                rows = pl.ds(pl.multiple_of(i * PITCH, 8), n1)
                for b in range(2):
                    zb = z_scr[b, rows, :]
                    gate = _short_conv_block(px_ref, b, i, nblk, cw_ref, cb_ref)
                    store(b, i, rows, gate * (y_scr[b, rows, :] + zb * skip_ref[order:order + 1, :]))
                return carry
            lax.fori_loop(0, nblk, blk, 0)

        @pl.when(c == 0)
        def _():
            def store(b, i, rows, val):
                z_scr[b, rows, :] = val
            post(px1_ref, cw1_ref, cb1_ref, 0, store)

        @pl.when(c == 1)
        def _():
            def store(b, i, rows, val):
                o_ref[b, pl.ds(pl.multiple_of(i * n1, n1), n1), :] = val.astype(o_ref.dtype)
            post(px2_ref, cw2_ref, cb2_ref, 1, store)


def _hyena(pg3, proj_col0, conv_w, conv_b, skip, kspec, s1, s1i, s2, s2i, width):
    bsz, seq, _ = pg3.shape
    assert bsz == 2
    n1 = FFT_N1
    n2 = 2 * seq // n1
    nct = width // LANES
    ngroups = 4
    krows = (n2 // ngroups) * n1
    col = lambda off: (lambda j, c, g: (0, 0, proj_col0 // LANES + off * nct + j))
    cw = lambda off: (lambda j, c, g: (0, off * nct + j))
    const3 = lambda j, c, g: (0, 0, 0)
    const2 = lambda j, c, g: (0, 0)
    return pl.pallas_call(
        _hyena_body,
        grid=(nct, 2, ngroups),
        in_specs=[pl.BlockSpec((2, seq, LANES), col(0)),
                  pl.BlockSpec((2, seq, LANES), col(1)),
                  pl.BlockSpec((2, seq, LANES), col(2)),
                  pl.BlockSpec((3, LANES), cw(0)),
                  pl.BlockSpec((3, LANES), cw(1)),
                  pl.BlockSpec((3, LANES), cw(2)),
                  pl.BlockSpec((1, LANES), cw(0)),
                  pl.BlockSpec((1, LANES), cw(1)),
                  pl.BlockSpec((1, LANES), cw(2)),
                  pl.BlockSpec((2, LANES), lambda j, c, g: (0, j)),
                  pl.BlockSpec((1, 2, krows, LANES), lambda j, c, g: (c, 0, g, j)),
                  pl.BlockSpec(s1.shape, const3),
                  pl.BlockSpec(s1i.shape, const3),
                  pl.BlockSpec(s2.shape, const2),
                  pl.BlockSpec(s2i.shape, const2)],
        out_specs=pl.BlockSpec((2, seq, LANES), lambda j, c, g: (0, 0, j)),
        out_shape=jax.ShapeDtypeStruct((2, seq, width), BF16),
        scratch_shapes=[pltpu.VMEM((2, (n2 // 2) * PITCH, LANES), F32),
                        pltpu.VMEM((2, (n2 // 2) * PITCH, LANES), F32),
                        pltpu.VMEM((2, n2 * PITCH, LANES), F32),
                        pltpu.VMEM((2, n2 * PITCH, LANES), F32)],
        compiler_params=_cparams(("arbitrary", "arbitrary", "arbitrary"), 56),
        name="hyena",
    )(pg3, pg3, pg3, conv_w, conv_w, conv_w, conv_b, conv_b, conv_b, skip, kspec, s1, s1i, s2, s2i)


ATTN_QBLK = 128
ATTN_UNROLL = 4


def _attention_body(pos_ref, freq_ref, sign_ref, *refs):
    qkv_refs = refs[:9]
    o_ref = refs[9]
    cos_scr, sin_scr, q_scr, k_scr, v_scr, og_scr, lse_scr = refs[10:]
    seq = q_scr.shape[0]
    chunk = 512
    nchunks = seq // chunk

    @pl.when(pl.program_id(1) == 0)
    def _():
        def trig(i, carry):
            rows = pl.ds(pl.multiple_of(i * chunk, chunk), chunk)
            ang = pos_ref[0, rows, :].astype(F32) * freq_ref[...]
            cos_scr[rows, :] = jnp.cos(ang)
            sin_scr[rows, :] = jnp.sin(ang) * sign_ref[...]
            return carry
        lax.fori_loop(0, nchunks, trig, 0)

    def rotate(src_ref, dst_ref):
        def body(i, carry):
            rows = pl.ds(pl.multiple_of(i * chunk, chunk), chunk)
            t = src_ref[0, rows, :].astype(F32)
            lane = lax.broadcasted_iota(jnp.int32, t.shape, 1)
            partner = jnp.where(lane < ROT_DIM // 2,
                                pltpu.roll(t, LANES - ROT_DIM // 2, 1), pltpu.roll(t, ROT_DIM // 2, 1))
            dst_ref[rows, :] = t * cos_scr[rows, :] + partner * sin_scr[rows, :]
            return carry
        lax.fori_loop(0, nchunks, body, 0)

    def widen(src_ref, dst_ref):
        def body(i, carry):
            rows = pl.ds(pl.multiple_of(i * chunk, chunk), chunk)
            dst_ref[rows, :] = src_ref[0, rows, :].astype(F32)
            return carry
        lax.fori_loop(0, nchunks, body, 0)

    scale = HEAD_DIM ** -0.5
    for gi, (window, dil) in enumerate(ATTN_GROUPS):
        rotate(qkv_refs[3 * gi], q_scr)
        rotate(qkv_refs[3 * gi + 1], k_scr)
        widen(qkv_refs[3 * gi + 2], v_scr)
        n = seq // dil
        half = window // (2 * dil)
        tk = min(n, ATTN_QBLK + 2 * half)
        blocks_per_res = n // ATTN_QBLK

        def block(u, carry, dil=dil, n=n, half=half, tk=tk, blocks_per_res=blocks_per_res, gi=gi):
            r = u // blocks_per_res
            m = u % blocks_per_res
            q0 = m * ATTN_QBLK
            k0 = jnp.clip(q0 - half, 0, n - tk)
            qrows = pl.ds(r + dil * q0, ATTN_QBLK, stride=dil)
            krows = pl.ds(r + dil * k0, tk, stride=dil)
            qb = q_scr[qrows, :].astype(BF16)
            kb = k_scr[krows, :].astype(BF16)
            vb = v_scr[krows, :].astype(BF16)
            s = lax.dot_general(qb, kb, (((1,), (1,)), ((), ())), preferred_element_type=F32) * scale
            qi = q0 + lax.broadcasted_iota(jnp.int32, (ATTN_QBLK, tk), 0)
            kj = k0 + lax.broadcasted_iota(jnp.int32, (ATTN_QBLK, tk), 1)
            s = jnp.where(jnp.abs(qi - kj) <= half, s, NEG_INF)
            mx = jnp.max(s, axis=-1, keepdims=True)
            p = jnp.exp(s - mx)
            l = jnp.sum(p, axis=-1, keepdims=True)
            o = jnp.dot(p.astype(BF16), vb, preferred_element_type=F32) / l
            og_scr[gi, qrows, :] = o
            lse_scr[gi, qrows, :] = jnp.broadcast_to(mx + jnp.log(l), (ATTN_QBLK, LANES))
            return carry

        lax.fori_loop(0, seq // ATTN_QBLK, block, 0, unroll=ATTN_UNROLL)

    def merge(i, carry):
        rows = pl.ds(pl.multiple_of(i * chunk, chunk), chunk)
        lses = [lse_scr[gi, rows, :] for gi in range(len(ATTN_GROUPS))]
        mx = functools.reduce(jnp.maximum, lses)
        ws = [jnp.exp(v - mx) for v in lses]
        den = functools.reduce(lambda a, b: a + b, ws)
        num = functools.reduce(lambda a, b: a + b,
                               [w * og_scr[gi, rows, :] for gi, w in enumerate(ws)])
        o_ref[0, rows, :] = (num / den).astype(o_ref.dtype)
        return carry

    lax.fori_loop(0, nchunks, merge, 0)


def _attention(pg3, pos3, freq_row, sign_row, qkv_col0):
    bsz, seq, _ = pg3.shape
    ng = len(ATTN_GROUPS)
    in_specs = [pl.BlockSpec((1, seq, 1), lambda b, h: (b, 0, 0)),
                pl.BlockSpec((1, LANES), lambda b, h: (0, 0)),
                pl.BlockSpec((1, LANES), lambda b, h: (0, 0))]
    for gi in range(ng):
        for which in range(3):
            base = qkv_col0 // LANES + which * N_ATTN_HEADS + gi * HEADS_PER_GROUP
            in_specs.append(pl.BlockSpec((1, seq, LANES), lambda b, h, base=base: (b, 0, base + h)))
    return pl.pallas_call(
        _attention_body,
        grid=(bsz, HEADS_PER_GROUP),
        in_specs=in_specs,
        out_specs=pl.BlockSpec((1, seq, LANES), lambda b, h: (b, 0, h)),
        out_shape=jax.ShapeDtypeStruct((bsz, seq, HEADS_PER_GROUP * HEAD_DIM), BF16),
        scratch_shapes=[pltpu.VMEM((seq, LANES), F32), pltpu.VMEM((seq, LANES), F32),
                        pltpu.VMEM((seq, LANES), F32), pltpu.VMEM((seq, LANES), F32),
                        pltpu.VMEM((seq, LANES), F32),
                        pltpu.VMEM((ng, seq, LANES), F32), pltpu.VMEM((ng, seq, LANES), F32)],
        compiler_params=_cparams(("arbitrary", "arbitrary"), 56),
        name="attention",
    )(pos3, freq_row, sign_row, *([pg3] * (3 * ng)))


def _mix_out_body(yhy_ref, yat_ref, ghy_ref, gat_ref, x_ref, mod_ref, gpost_ref, gpre_ref,
                  whyo_ref, wato_ref, wout_ref, wrhi_ref, wrlo_ref, br_ref, x1_ref, h2_ref, lg_ref):
    a = jnp.dot(yhy_ref[...], whyo_ref[...], preferred_element_type=F32)
    b = jnp.dot(yat_ref[...], wato_ref[...], preferred_element_type=F32)
    merged = ghy_ref[...].astype(F32) * a + gat_ref[...].astype(F32) * b
    y = jnp.dot(merged.astype(BF16), wout_ref[...], preferred_element_type=F32)
    x1 = x_ref[...] + mod_ref[0, 2:3, :] * _rms(y, gpost_ref[...])
    x1_ref[...] = x1
    h2 = _rms(x1, gpre_ref[...]) * (1.0 + mod_ref[0, 4:5, :]) + mod_ref[0, 3:4, :]
    h2_ref[...] = h2
    h2_hi = h2.astype(BF16)
    h2_lo = (h2 - h2_hi.astype(F32)).astype(BF16)
    w_hi = wrhi_ref[...]
    w_lo = wrlo_ref[...]
    lg_ref[...] = (jnp.dot(h2_hi, w_hi, preferred_element_type=F32)
                   + jnp.dot(h2_lo, w_hi, preferred_element_type=F32)
                   + jnp.dot(h2_hi, w_lo, preferred_element_type=F32)
                   + jnp.dot(h2_lo, w_lo, preferred_element_type=F32)) + br_ref[...]


def _mix_out(y_hy, y_at, pg, x2d, mod, g_post, g_pre, w_hy_o, w_at_o, w_out, w_r_hi, w_r_lo, b_r, seq):
    t, d = x2d.shape
    tm = 256
    per_batch = seq // tm
    gblk = 0
    const = lambda i: (0, 0)
    return pl.pallas_call(
        _mix_out_body,
        grid=(t // tm,),
        in_specs=[pl.BlockSpec((tm, y_hy.shape[1]), lambda i: (i, 0)),
                  pl.BlockSpec((tm, y_at.shape[1]), lambda i: (i, 0)),
                  pl.BlockSpec((tm, d), lambda i: (i, gblk)),
                  pl.BlockSpec((tm, d), lambda i: (i, gblk + 1)),
                  pl.BlockSpec((tm, d), lambda i: (i, 0)),
                  pl.BlockSpec((1, 6, d), lambda i: (i // per_batch, 0, 0)),
                  pl.BlockSpec((1, d), const),
                  pl.BlockSpec((1, d), const),
                  pl.BlockSpec(w_hy_o.shape, const),
                  pl.BlockSpec(w_at_o.shape, const),
                  pl.BlockSpec(w_out.shape, const),
                  pl.BlockSpec(w_r_hi.shape, const),
                  pl.BlockSpec(w_r_lo.shape, const),
                  pl.BlockSpec((1, LANES), const)],
        out_specs=[pl.BlockSpec((tm, d), lambda i: (i, 0)),
                   pl.BlockSpec((tm, d), lambda i: (i, 0)),
                   pl.BlockSpec((tm, LANES), lambda i: (i, 0))],
        out_shape=[jax.ShapeDtypeStruct((t, d), F32),
                   jax.ShapeDtypeStruct((t, d), F32),
                   jax.ShapeDtypeStruct((t, LANES), F32)],
        compiler_params=_cparams(("arbitrary",), 56),
        name="mix_out",
    )(y_hy, y_at, pg, pg, x2d, mod, g_post, g_pre, w_hy_o, w_at_o, w_out, w_r_hi, w_r_lo, b_r)


def _route_body(lg_ref, eid_ref, gate_ref):
    lg = lg_ref[...]
    lane = lax.broadcasted_iota(jnp.int32, lg.shape, 1)
    big = jnp.int32(1 << 20)

    def first_argmax(vals, mask):
        v = jnp.where(mask, vals, -jnp.inf)
        mx = jnp.max(v, axis=-1, keepdims=True)
        idx = jnp.min(jnp.where(mask & (v == mx), lane, big), axis=-1, keepdims=True)
        return mx, idx

    gmask = lane < N_EXPERT_GROUPS
    gmax, gidx = first_argmax(lg, gmask)
    gval = 1.0 / jnp.sum(jnp.where(gmask, jnp.exp(lg - gmax), 0.0), axis=-1, keepdims=True)
    lo = N_EXPERT_GROUPS + gidx * EXPERTS_PER_GROUP
    emask = (lane >= lo) & (lane < lo + EXPERTS_PER_GROUP)
    v1, i1 = first_argmax(lg, emask)
    v2, i2 = first_argmax(lg, emask & (lane != i1))
    e2 = jnp.exp(v2 - v1)
    p1 = 1.0 / (1.0 + e2)
    p2 = e2 / (1.0 + e2)
    eid = jnp.where(lane == 0, i1, i2) - N_EXPERT_GROUPS
    gate = gval * jnp.where(lane == 0, p1, p2)
    eid_ref[...] = eid[:, :TOP_K]
    gate_ref[...] = gate[:, :TOP_K]


def _route(logits):
    t = logits.shape[0]
    tm = 512
    return pl.pallas_call(
        _route_body,
        grid=(t // tm,),
        in_specs=[pl.BlockSpec((tm, LANES), lambda i: (i, 0))],
        out_specs=[pl.BlockSpec((tm, TOP_K), lambda i: (i, 0)),
                   pl.BlockSpec((tm, TOP_K), lambda i: (i, 0))],
        out_shape=[jax.ShapeDtypeStruct((t, TOP_K), jnp.int32),
                   jax.ShapeDtypeStruct((t, TOP_K), F32)],
        compiler_params=_cparams(("arbitrary",), 32),
        name="route",
    )(logits)


def _rank_body(e_ref, rank_ref, cnt_ref, carry_scr):
    i = pl.program_id(0)
    r = e_ref.shape[0]

    @pl.when(i == 0)
    def _():
        carry_scr[...] = jnp.zeros_like(carry_scr)

    lane = lax.broadcasted_iota(jnp.int32, (r, LANES), 1)
    onehot = (lane == e_ref[...]).astype(F32)
    tri = (lax.broadcasted_iota(jnp.int32, (r, r), 1)
           < lax.broadcasted_iota(jnp.int32, (r, r), 0)).astype(BF16)
    before = jnp.dot(tri, onehot.astype(BF16), preferred_element_type=F32) + carry_scr[0:1, :]
    rank_ref[...] = jnp.sum(onehot * before, axis=-1, keepdims=True).astype(jnp.int32)
    total = carry_scr[0:1, :] + jnp.sum(onehot, axis=0, keepdims=True)
    carry_scr[...] = jnp.broadcast_to(total, carry_scr.shape)
    cnt_ref[...] = jnp.broadcast_to(total, cnt_ref.shape).astype(jnp.int32)


def _rank(flat_e):
    a = flat_e.shape[0]
    r = 512
    return pl.pallas_call(
        _rank_body,
        grid=(a // r,),
        in_specs=[pl.BlockSpec((r, 1), lambda i: (i, 0))],
        out_specs=[pl.BlockSpec((r, 1), lambda i: (i, 0)),
                   pl.BlockSpec((8, LANES), lambda i: (0, 0))],
        out_shape=[jax.ShapeDtypeStruct((a, 1), jnp.int32),
                   jax.ShapeDtypeStruct((8, LANES), jnp.int32)],
        scratch_shapes=[pltpu.VMEM((8, LANES), F32)],
        compiler_params=_cparams(("arbitrary",), 32),
        name="rank",
    )(flat_e)


def _dest_body(e_ref, rank_ref, ps_ref, o_ref):
    lane = lax.broadcasted_iota(jnp.int32, (e_ref.shape[0], LANES), 1)
    first = jnp.sum(jnp.where(lane == e_ref[...], ps_ref[...], 0.0), axis=-1, keepdims=True)
    o_ref[...] = first.astype(jnp.int32) + rank_ref[...]


def _dest(flat_e, rank, pstarts_row):
    a = flat_e.shape[0]
    r = 2048
    return pl.pallas_call(
        _dest_body,
        grid=(a // r,),
        in_specs=[pl.BlockSpec((r, 1), lambda i: (i, 0)),
                  pl.BlockSpec((r, 1), lambda i: (i, 0)),
                  pl.BlockSpec((1, LANES), lambda i: (0, 0))],
        out_specs=pl.BlockSpec((r, 1), lambda i: (i, 0)),
        out_shape=jax.ShapeDtypeStruct((a, 1), jnp.int32),
        compiler_params=_cparams(("arbitrary",), 32),
        name="dest",
    )(flat_e, rank, pstarts_row)


DMA_UNROLL = 8


def _experts_body(ord_ref, elist_ref, src_ref, cnt_ref, h_hbm, w1_hbm, w3_hbm, w2_hbm, y_ref,
                  x_scr, w1_scr, w3_scr, w2_scr, sem, wsem):
    i = pl.program_id(0)
    nrows = x_scr.shape[1]
    nused = cnt_ref[0]
    nexp = cnt_ref[1]
    k = ord_ref[i]
    first = (i == 0) | (k != ord_ref[jnp.maximum(i - 1, 0)])

    def weights(kk, act):
        e = elist_ref[kk]
        slot = kk % 2
        act(pltpu.make_async_copy(w1_hbm.at[e], w1_scr.at[slot], wsem.at[slot, 0]))
        act(pltpu.make_async_copy(w3_hbm.at[e], w3_scr.at[slot], wsem.at[slot, 1]))
        act(pltpu.make_async_copy(w2_hbm.at[e], w2_scr.at[slot], wsem.at[slot, 2]))

    def gather(blk, act):
        slot = blk % 2
        base = blk * nrows

        def body(r, carry):
            act(pltpu.make_async_copy(h_hbm.at[pl.ds(src_ref[base + r], 1)],
                                      x_scr.at[slot, pl.ds(r, 1)], sem.at[slot]))
            return carry

        lax.fori_loop(0, nrows, body, 0, unroll=DMA_UNROLL)

    @pl.when(i == 0)
    def _():
        weights(k, lambda cp: cp.start())
        gather(i, lambda cp: cp.start())

    @pl.when(i + 1 < nused)
    def _():
        gather(i + 1, lambda cp: cp.start())

    @pl.when((i < nused) & first & (k + 1 < nexp))
    def _():
        weights(k + 1, lambda cp: cp.start())

    @pl.when((i < nused) & first)
    def _():
        weights(k, lambda cp: cp.wait())

    @pl.when(i < nused)
    def _():
        gather(i, lambda cp: cp.wait())
        wslot = k % 2
        xb = x_scr[i % 2].astype(BF16)
        a = jnp.dot(xb, w1_scr[wslot].astype(BF16), preferred_element_type=F32)
        b = jnp.dot(xb, w3_scr[wslot].astype(BF16), preferred_element_type=F32)
        act = (a * jax.nn.sigmoid(a) * b).astype(BF16)
        y_ref[...] = jnp.dot(act, w2_scr[wslot].astype(BF16), preferred_element_type=F32)

    @pl.when(i >= nused)
    def _():
        y_ref[...] = jnp.zeros_like(y_ref)


def _experts(block_ord, elist, src_tok, counts2, h2, w1, w3, w2):
    nblocks = block_ord.shape[0]
    d = h2.shape[1]
    ff = w1.shape[2]
    rb = EXPERT_ROW_BLOCK
    grid_spec = pltpu.PrefetchScalarGridSpec(
        num_scalar_prefetch=4,
        grid=(nblocks,),
        in_specs=[pl.BlockSpec(memory_space=pl.ANY)] * 4,
        out_specs=pl.BlockSpec((rb, d), lambda i, *_: (i, 0)),
        scratch_shapes=[pltpu.VMEM((2, rb, d), F32),
                        pltpu.VMEM((2, d, ff), F32), pltpu.VMEM((2, d, ff), F32),
                        pltpu.VMEM((2, ff, d), F32),
                        pltpu.SemaphoreType.DMA((2,)), pltpu.SemaphoreType.DMA((2, 3))],
    )
    return pl.pallas_call(
        _experts_body,
        grid_spec=grid_spec,
        out_shape=jax.ShapeDtypeStruct((nblocks * rb, d), F32),
        compiler_params=_cparams(("arbitrary",), 56),
        name="experts",
    )(block_ord, elist, src_tok, counts2, h2, w1, w3, w2)


def _combine_body(dest_ref, y_hbm, gate_ref, x1_ref, mod_ref, g_ref, o_ref, buf_scr, sem):
    i = pl.program_id(0)
    tm = x1_ref.shape[0]

    def gather(step, act):
        slot = step % 2
        base = step * tm * TOP_K

        def body(r, carry):
            for k in range(TOP_K):
                act(pltpu.make_async_copy(y_hbm.at[pl.ds(dest_ref[base + TOP_K * r + k], 1)],
                                          buf_scr.at[slot, k, pl.ds(r, 1)], sem.at[slot]))
            return carry

        lax.fori_loop(0, tm, body, 0, unroll=DMA_UNROLL // TOP_K)

    @pl.when(i == 0)
    def _():
        gather(i, lambda cp: cp.start())

    @pl.when(i + 1 < pl.num_programs(0))
    def _():
        gather(i + 1, lambda cp: cp.start())

    gather(i, lambda cp: cp.wait())
    slot = i % 2
    gate = gate_ref[...]
    y = buf_scr[slot, 0] * gate[:, 0:1] + buf_scr[slot, 1] * gate[:, 1:2]
    o_ref[...] = x1_ref[...] + mod_ref[0, 5:6, :] * _rms(y, g_ref[...])


def _combine(dest, ybuf, gate, x1, mod, g_post, seq):
    t, d = x1.shape
    tm = 128
    per_batch = seq // tm
    grid_spec = pltpu.PrefetchScalarGridSpec(
        num_scalar_prefetch=1,
        grid=(t // tm,),
        in_specs=[pl.BlockSpec(memory_space=pl.ANY),
                  pl.BlockSpec((tm, TOP_K), lambda i, dst: (i, 0)),
                  pl.BlockSpec((tm, d), lambda i, dst: (i, 0)),
                  pl.BlockSpec((1, 6, d), lambda i, dst: (i // per_batch, 0, 0)),
                  pl.BlockSpec((1, d), lambda i, dst: (0, 0))],
        out_specs=pl.BlockSpec((tm, d), lambda i, dst: (i, 0)),
        scratch_shapes=[pltpu.VMEM((2, TOP_K, tm, d), F32), pltpu.SemaphoreType.DMA((2,))],
    )
    return pl.pallas_call(
        _combine_body,
        grid_spec=grid_spec,
        out_shape=jax.ShapeDtypeStruct((t, d), F32),
        compiler_params=_cparams(("arbitrary",), 32),
        name="combine",
    )(dest, ybuf, gate, x1, mod, g_post)


def _layer(x, c, positions, w_ada, b_ada, g_mix_pre, g_mix_post, g_ffn_pre, g_ffn_post,
           w_in, conv_w, conv_b, filt_w1, filt_b1, filt_w2, filt_b2, filt_w3, filt_freq,
           hyena_skip, w_branch_gate, b_branch_gate, w_hy_o, w_at_o, w_out,
           w_group, b_group, w_expert, b_expert, w1_exp, w3_exp, w2_exp):
    bsz, seq, d = x.shape
    t = bsz * seq
    width = hyena_skip.shape[1]
    row = lambda v: v.reshape(1, -1)

    c_pad = jnp.pad(c, ((0, 8 - bsz), (0, 0)))
    mod = _adaln(c_pad, w_ada, row(b_ada))[:bsz].reshape(bsz, 6, d)

    x2d = x.reshape(t, d)
    n_gate = w_branch_gate.shape[1]
    n_cat = -(-(n_gate + w_in.shape[1]) // IN_PROJ_TN) * IN_PROJ_TN
    n_zero = n_cat - n_gate - w_in.shape[1]
    w_cat = jnp.concatenate([w_branch_gate.astype(BF16), w_in.astype(BF16),
                             jnp.zeros((d, n_zero), BF16)], axis=1)
    b_cat = jnp.concatenate([b_branch_gate, jnp.zeros((n_cat - n_gate,), F32)]).reshape(1, -1)
    pg = _in_proj(x2d, mod, row(g_mix_pre), w_cat, b_cat, n_gate, seq)
    pg3 = pg.reshape(bsz, seq, -1)

    hidden = filt_w2.shape[0]
    bands = np.zeros((1, LANES), np.float32)
    band_vals = np.linspace(1e-4, FILTER_BANDS - 1, FILTER_BANDS, dtype=np.float32)
    bands[0, 1:1 + FILTER_BANDS] = band_vals
    bands[0, 1 + FILTER_BANDS:1 + 2 * FILTER_BANDS] = band_vals
    w1p = jnp.pad(filt_w1, ((0, LANES - filt_w1.shape[0]), (0, 0)))
    max_decay = math.log(DECAY_TARGET) / FAST_DECAY_PCT
    min_decay = math.log(DECAY_TARGET) / SLOW_DECAY_PCT
    deltas = jnp.abs(jnp.linspace(min_decay, max_decay, width, dtype=F32)).reshape(1, -1)
    uw = _filters(seq, width, jnp.asarray(bands), w1p, row(filt_b1), filt_w2, row(filt_b2),
                  row(filt_freq), deltas, filt_w3)
    s1, s1i, s2, s2i = _fft_tables(FFT_N1, 2 * seq // FFT_N1)
    kspec = _spectra(uw, s1, s2)
    y_hy = _hyena(pg3, n_gate, conv_w, row(conv_b), hyena_skip, kspec, s1, s1i, s2, s2i, width)

    half = ROT_DIM // 2
    inv_freq = np.power(ROPE_THETA, -2.0 * np.arange(half, dtype=np.float32) / ROT_DIM).astype(np.float32)
    freq_row = np.zeros((1, LANES), np.float32)
    freq_row[0, :half] = inv_freq
    freq_row[0, half:ROT_DIM] = inv_freq
    sign_row = np.zeros((1, LANES), np.float32)
    sign_row[0, :half] = -1.0
    sign_row[0, half:ROT_DIM] = 1.0
    y_at = _attention(pg3, positions.reshape(bsz, seq, 1), jnp.asarray(freq_row), jnp.asarray(sign_row),
                      n_gate + 3 * width)

    w_r = jnp.concatenate([w_group, jnp.transpose(w_expert, (1, 0, 2)).reshape(d, N_EXPERTS)], axis=1)
    w_r = jnp.pad(w_r, ((0, 0), (0, LANES - w_r.shape[1])))
    w_r_hi = w_r.astype(BF16)
    w_r_lo = (w_r - w_r_hi.astype(F32)).astype(BF16)
    b_r = jnp.pad(jnp.concatenate([b_group, b_expert.reshape(-1)]), (0, LANES - N_EXPERT_GROUPS - N_EXPERTS))
    x1, h2, logits = _mix_out(y_hy.reshape(t, width), y_at.reshape(t, -1), pg, x2d, mod,
                              row(g_mix_post), row(g_ffn_pre), w_hy_o.astype(BF16), w_at_o.astype(BF16),
                              w_out.astype(BF16), w_r_hi, w_r_lo, b_r.reshape(1, -1), seq)

    eid, gate = _route(logits)
    flat_e = eid.reshape(t * TOP_K, 1)
    rank, counts = _rank(flat_e)
    counts = counts[0, :N_EXPERTS]
    rb = EXPERT_ROW_BLOCK
    padded = (counts + rb - 1) // rb * rb
    pends = jnp.cumsum(padded)
    pstarts = pends - padded
    pstarts_row = jnp.pad(pstarts.astype(F32), (0, LANES - N_EXPERTS)).reshape(1, LANES)
    dest = _dest(flat_e, rank, pstarts_row)[:, 0]
    n_blocks = t * TOP_K // rb + N_EXPERTS
    block_start = jnp.arange(n_blocks, dtype=jnp.int32) * rb
    block_e = jnp.minimum(jnp.searchsorted(pends, block_start, side='right'), N_EXPERTS - 1).astype(jnp.int32)
    tok = jnp.arange(t * TOP_K, dtype=jnp.int32) // TOP_K
    src_tok = (jnp.arange(n_blocks * rb, dtype=jnp.int32) % t).at[dest].set(tok)
    has_rows = jnp.cumsum((counts > 0).astype(jnp.int32))
    elist = jnp.minimum(jnp.searchsorted(has_rows, jnp.arange(1, N_EXPERTS + 1, dtype=jnp.int32), side='left'),
                        N_EXPERTS - 1).astype(jnp.int32)
    block_ord = (has_rows - 1)[block_e].astype(jnp.int32)
    counts2 = jnp.stack([pends[-1] // rb, has_rows[-1]]).astype(jnp.int32)

    ybuf = _experts(block_ord, elist, src_tok, counts2, h2, w1_exp, w3_exp, w2_exp)
    out = _combine(dest, ybuf, gate, x1, mod, row(g_ffn_post), seq)
    return out.reshape(bsz, seq, d)


def kernel(x, c, positions, w_ada, b_ada, g_mix_pre, g_mix_post, g_ffn_pre, g_ffn_post, w_in, conv_w, conv_b, filt_w1, filt_b1, filt_w2, filt_b2, filt_w3, filt_freq, hyena_skip, w_branch_gate, b_branch_gate, w_hy_o, w_at_o, w_out, w_group, b_group, w_expert, b_expert, w1_exp, w3_exp, w2_exp):
    depth = w_ada.shape[0]
    for l in range(depth):
        x = _layer(x, c, positions, w_ada[l], b_ada[l], g_mix_pre[l], g_mix_post[l], g_ffn_pre[l],
                   g_ffn_post[l], w_in[l], conv_w[l], conv_b[l], filt_w1[l], filt_b1[l], filt_w2[l],
                   filt_b2[l], filt_w3[l], filt_freq[l], hyena_skip[l], w_branch_gate[l],
                   b_branch_gate[l], w_hy_o[l], w_at_o[l], w_out[l], w_group[l], b_group[l],
                   w_expert[l], b_expert[l], w1_exp[l], w3_exp[l], w2_exp[l])
    return x
```

```python
import functools
import math

import numpy as np
import jax
import jax.numpy as jnp
from jax import lax
from jax.experimental import pallas as pl
from jax.experimental.pallas import tpu as pltpu

F32 = jnp.float32
BF16 = jnp.bfloat16

LANES = 128
MIB = 1024 * 1024

RMS_EPS = 1e-6
NEG_INF = -1e30

HEAD_DIM = 128
ROT_DIM = HEAD_DIM // 4
ROPE_THETA = 500000.0
ATTN_GROUPS = ((128, 1), (512, 4), (2048, 16))
HEADS_PER_GROUP = 4
N_ATTN_HEADS = HEADS_PER_GROUP * len(ATTN_GROUPS)

FILTER_BANDS = 16
DECAY_TARGET = 1e-2
FAST_DECAY_PCT = 0.3
SLOW_DECAY_PCT = 1.5

N_EXPERT_GROUPS = 8
EXPERTS_PER_GROUP = 8
N_EXPERTS = N_EXPERT_GROUPS * EXPERTS_PER_GROUP
TOP_K = 2
EXPERT_ROW_BLOCK = 128

FFT_N1 = 128
PITCH = FFT_N1 + 8
FFT_UNROLL = 16
FFT_MID_UNROLL = 8


def _cparams(sem, vmem_mib):
    return pltpu.CompilerParams(dimension_semantics=sem, vmem_limit_bytes=vmem_mib * MIB)


def _rms(x, g):
    return x * lax.rsqrt(jnp.mean(x * x, axis=-1, keepdims=True) + RMS_EPS) * g


def _adaln_body(c_ref, w_ref, b_ref, o_ref):
    c = c_ref[...]
    cond = c * jax.nn.sigmoid(c)
    o_ref[...] = jnp.dot(cond.astype(BF16), w_ref[...].astype(BF16),
                         preferred_element_type=F32) + b_ref[...]


def _adaln(c_pad, w_ada, b_ada):
    rows, d = c_pad.shape
    n = w_ada.shape[1]
    tn = 1024
    return pl.pallas_call(
        _adaln_body,
        grid=(n // tn,),
        in_specs=[pl.BlockSpec((rows, d), lambda j: (0, 0)),
                  pl.BlockSpec((d, tn), lambda j: (0, j)),
                  pl.BlockSpec((1, tn), lambda j: (0, j))],
        out_specs=pl.BlockSpec((rows, tn), lambda j: (0, j)),
        out_shape=jax.ShapeDtypeStruct((rows, n), F32),
        compiler_params=_cparams(("arbitrary",), 40),
        name="adaln",
    )(c_pad, w_ada, b_ada)


IN_PROJ_TM = 1024
IN_PROJ_TN = 1024
IN_PROJ_SUB_M = 256
IN_PROJ_SUB_N = 512


def _in_proj_body(n_gate, x_ref, mod_ref, g_ref, w_ref, b_ref, o_ref, h_scr):
    j = pl.program_id(1)

    @pl.when(j == 0)
    def _():
        x = x_ref[...]
        h = _rms(x, g_ref[...]) * (1.0 + mod_ref[0, 1:2, :]) + mod_ref[0, 0:1, :]
        h_scr[...] = h.astype(BF16)

    is_gate = j < n_gate
    tm, tn = o_ref.shape
    for mi in range(tm // IN_PROJ_SUB_M):
        rows = slice(mi * IN_PROJ_SUB_M, (mi + 1) * IN_PROJ_SUB_M)
        for ni in range(tn // IN_PROJ_SUB_N):
            cols = slice(ni * IN_PROJ_SUB_N, (ni + 1) * IN_PROJ_SUB_N)
            acc = jnp.dot(h_scr[rows, :], w_ref[:, cols], preferred_element_type=F32) + b_ref[:, cols]
            o_ref[rows, cols] = jnp.where(is_gate, jax.nn.sigmoid(acc), acc).astype(o_ref.dtype)


def _in_proj(x2d, mod, g_pre, w_cat, b_cat, n_gate_cols, seq):
    t, d = x2d.shape
    n = w_cat.shape[1]
    tm, tn = IN_PROJ_TM, IN_PROJ_TN
    per_batch = seq // tm
    return pl.pallas_call(
        functools.partial(_in_proj_body, n_gate_cols // tn),
        grid=(t // tm, n // tn),
        in_specs=[pl.BlockSpec((tm, d), lambda i, j: (i, 0)),
                  pl.BlockSpec((1, 6, d), lambda i, j: (i // per_batch, 0, 0)),
                  pl.BlockSpec((1, d), lambda i, j: (0, 0)),
                  pl.BlockSpec((d, tn), lambda i, j: (0, j)),
                  pl.BlockSpec((1, tn), lambda i, j: (0, j))],
        out_specs=pl.BlockSpec((tm, tn), lambda i, j: (i, j)),
        out_shape=jax.ShapeDtypeStruct((t, n), BF16),
        scratch_shapes=[pltpu.VMEM((tm, d), BF16)],
        compiler_params=_cparams(("arbitrary", "arbitrary"), 56),
        name="in_proj",
    )(x2d, mod, g_pre, w_cat, b_cat)


def _filters_body(seq, band_ref, w1_ref, b1_ref, w2_ref, b2_ref, fr_ref, dl_ref,
                  w3a_ref, w3b_ref, o_ref, hid_scr):
    i = pl.program_id(0)
    j = pl.program_id(1)
    tl = hid_scr.shape[0]
    row = (lax.broadcasted_iota(jnp.int32, (tl, 1), 0) + i * tl).astype(F32)

    @pl.when((j == 0) & (pl.program_id(2) == 0))
    def _():
        nfeat = band_ref.shape[0]
        pos = (lax.broadcasted_iota(jnp.int32, (1, tl), 1) + i * tl).astype(F32)
        feat = lax.broadcasted_iota(jnp.int32, (nfeat, tl), 0)
        ang = band_ref[...] * (2.0 * math.pi * pos / seq)
        feats = jnp.where(feat == 0, pos / (seq - 1.0),
                          jnp.where(feat <= FILTER_BANDS, jnp.cos(ang),
                                    jnp.where(feat <= 2 * FILTER_BANDS, -jnp.sin(ang), 0.0)))
        hi = lax.Precision.HIGHEST
        fr = fr_ref[...]
        hid = jnp.sin(fr * (jnp.dot(w1_ref[...], feats, precision=hi, preferred_element_type=F32)
                            + b1_ref[...]))
        hid = jnp.sin(fr * (jnp.dot(w2_ref[...], hid, precision=hi, preferred_element_type=F32)
                            + b2_ref[...]))
        hid_scr[...] = hid.T

    hi = lax.Precision.HIGHEST
    hid = hid_scr[...]
    decay = jnp.exp(-(row / (seq - 1.0)) * dl_ref[...])
    hf = jnp.dot(hid, w3a_ref[...], precision=hi, preferred_element_type=F32) * decay
    hb = jnp.dot(hid, w3b_ref[...], precision=hi, preferred_element_type=F32) * decay
    hb = jnp.where(row == 0.0, 0.0, hb)
    o_ref[0, 0] = hf + hb
    o_ref[0, 1] = hf - hb


def _filters(seq, width, bands, w1p, b1, w2, b2, freq, deltas, w3):
    tl, tc = 512, 256
    nct = width // tc
    hidden = w2.shape[0]
    nfeat = bands.shape[0]
    const = lambda i, j, o: (0, 0)
    return pl.pallas_call(
        functools.partial(_filters_body, float(seq)),
        grid=(seq // tl, nct, 2),
        in_specs=[pl.BlockSpec((nfeat, 1), const),
                  pl.BlockSpec((hidden, nfeat), const),
                  pl.BlockSpec((hidden, 1), const),
                  pl.BlockSpec((hidden, hidden), const),
                  pl.BlockSpec((hidden, 1), const),
                  pl.BlockSpec((hidden, 1), const),
                  pl.BlockSpec((1, tc), lambda i, j, o: (0, j)),
                  pl.BlockSpec((hidden, tc), lambda i, j, o: (0, (2 * o) * nct + j)),
                  pl.BlockSpec((hidden, tc), lambda i, j, o: (0, (2 * o + 1) * nct + j))],
        out_specs=pl.BlockSpec((1, 2, tl, tc), lambda i, j, o: (o, 0, i, j)),
        out_shape=jax.ShapeDtypeStruct((2, 2, seq, width), F32),
        scratch_shapes=[pltpu.VMEM((tl, hidden), F32)],
        compiler_params=_cparams(("arbitrary", "arbitrary", "arbitrary"), 32),
        name="filters",
    )(bands, w1p, b1, w2, b2, freq, deltas, w3, w3)


def _fft_tables(n1, n2):
    n = n1 * n2
    q = np.arange(n2)[:, None]
    b = np.arange(n2 // 2)[None, :]
    a = np.arange(n1)[:, None, None]
    ang = -2.0 * np.pi * (a * q[None] / n + (q * b)[None] / n2)
    stage1 = np.concatenate([np.cos(ang), np.sin(ang)], axis=1)
    stage1_inv = np.transpose(stage1, (0, 2, 1)) / n
    p = np.arange(n1)
    ang2 = -2.0 * np.pi * np.outer(p, p) / n1
    fre, fim = np.cos(ang2), np.sin(ang2)
    stage2 = np.block([[fre, -fim], [fim, fre]])
    stage2_inv = np.block([[fre, fim], [-fim, fre]])
    as_bf16 = lambda m: jnp.asarray(m, dtype=F32).astype(BF16)
    return as_bf16(stage1), as_bf16(stage1_inv), as_bf16(stage2), as_bf16(stage2_inv)


def _halves(ref, rows):
    return jnp.concatenate([ref[0, rows, :], ref[1, rows, :]], axis=1)


def _fft_stage1(z_ref, s1_ref, gre_ref, gim_ref):
    n1, two_n2, n2h = s1_ref.shape
    n2 = two_n2 // 2

    def step(a, carry):
        zrows = _halves(z_ref, pl.ds(a, n2h, stride=PITCH))
        g = jnp.dot(s1_ref[a], zrows.astype(BF16), preferred_element_type=F32)
        for h in range(2):
            cols = slice(h * LANES, (h + 1) * LANES)
            gre_ref[h, pl.ds(a, n2, stride=PITCH), :] = g[:n2, cols]
            gim_ref[h, pl.ds(a, n2, stride=PITCH), :] = g[n2:, cols]
        return carry

    lax.fori_loop(0, n1, step, 0, unroll=FFT_UNROLL)


def _fft_stage1_inv(gre_ref, gim_ref, s1i_ref, y_ref):
    n1, n2h, two_n2 = s1i_ref.shape
    n2 = two_n2 // 2

    def step(a, carry):
        rows = pl.ds(a, n2, stride=PITCH)
        hs = jnp.concatenate([_halves(gre_ref, rows), _halves(gim_ref, rows)], axis=0)
        y = jnp.dot(s1i_ref[a], hs.astype(BF16), preferred_element_type=F32)
        for h in range(2):
            y_ref[h, pl.ds(a, n2h, stride=PITCH), :] = y[:, h * LANES:(h + 1) * LANES]
        return carry

    lax.fori_loop(0, n1, step, 0, unroll=FFT_UNROLL)


def _stage2_block(gre_ref, gim_ref, s2_ref, q):
    n1 = s2_ref.shape[0] // 2
    rows = pl.ds(pl.multiple_of(q * PITCH, 8), n1)
    gs = jnp.concatenate([_halves(gre_ref, rows), _halves(gim_ref, rows)], axis=0)
    x = jnp.dot(s2_ref[...], gs.astype(BF16), preferred_element_type=F32)
    return x[:n1], x[n1:]


def _spectra_body(uw_ref, s1_ref, s2_ref, k_ref, z_scr, gre_scr, gim_scr):
    n1 = s2_ref.shape[0] // 2
    n2 = s1_ref.shape[1] // 2
    for h in range(2):
        for b in range(n2 // 2):
            z_scr[h, pl.ds(b * PITCH, n1), :] = uw_ref[0, h, pl.ds(b * n1, n1), :]
    _fft_stage1(z_scr, s1_ref, gre_scr, gim_scr)

    def step(q, carry):
        xre, xim = _stage2_block(gre_scr, gim_scr, s2_ref, q)
        rows = pl.ds(pl.multiple_of(q * n1, 8), n1)
        k_ref[0, 0, rows, :] = xre[:, :LANES]
        k_ref[0, 1, rows, :] = xim[:, LANES:]
        return carry

    lax.fori_loop(0, n2, step, 0, unroll=FFT_MID_UNROLL)


def _spectra(uw, s1, s2):
    _, _, seq, width = uw.shape
    n1 = FFT_N1
    n2 = 2 * seq // n1
    const3 = lambda j, o: (0, 0, 0)
    return pl.pallas_call(
        _spectra_body,
        grid=(width // LANES, 2),
        in_specs=[pl.BlockSpec((1, 2, seq, LANES), lambda j, o: (o, 0, 0, j)),
                  pl.BlockSpec(s1.shape, const3),
                  pl.BlockSpec(s2.shape, lambda j, o: (0, 0))],
        out_specs=pl.BlockSpec((1, 2, 2 * seq, LANES), lambda j, o: (o, 0, 0, j)),
        out_shape=jax.ShapeDtypeStruct((2, 2, 2 * seq, width), F32),
        scratch_shapes=[pltpu.VMEM((2, (n2 // 2) * PITCH, LANES), F32),
                        pltpu.VMEM((2, n2 * PITCH, LANES), F32),
                        pltpu.VMEM((2, n2 * PITCH, LANES), F32)],
        compiler_params=_cparams(("arbitrary", "arbitrary"), 56),
        name="spectra",
    )(uw, s1, s2)


def _short_conv_block(p_ref, b, i, nblk, w_ref, bias_ref):
    n1 = FFT_N1
    start = pl.multiple_of(i * n1, n1)
    cur = p_ref[b, pl.ds(start, n1), :].astype(F32)
    before = p_ref[b, pl.ds(pl.multiple_of(jnp.maximum(start - 16, 0), 16), 16), :].astype(F32)
    after = p_ref[b, pl.ds(pl.multiple_of(jnp.minimum(start + n1, (nblk - 1) * n1), 16), 16), :].astype(F32)
    last_prev = before[15:16] * jnp.where(i > 0, 1.0, 0.0).astype(F32)
    first_next = after[0:1] * jnp.where(i < nblk - 1, 1.0, 0.0).astype(F32)
    row = lax.broadcasted_iota(jnp.int32, (n1, 1), 0)
    prev = jnp.where(row == 0, last_prev, pltpu.roll(cur, 1, 0))
    nxt = jnp.where(row == n1 - 1, first_next, pltpu.roll(cur, n1 - 1, 0))
    return prev * w_ref[0:1, :] + cur * w_ref[1:2, :] + nxt * w_ref[2:3, :] + bias_ref[...]


def _hyena_body(pv_ref, px1_ref, px2_ref, cwv_ref, cw1_ref, cw2_ref, cbv_ref, cb1_ref, cb2_ref,
                skip_ref, k_ref, s1_ref, s1i_ref, s2_ref, s2i_ref, o_ref,
                z_scr, y_scr, gre_scr, gim_scr):
    c = pl.program_id(1)
    g = pl.program_id(2)
    ngroups = pl.num_programs(2)
    n1 = FFT_N1
    n2 = s1_ref.shape[1] // 2
    nblk = n2 // 2
    per_group = n2 // ngroups

    @pl.when(g == 0)
    def _():
        @pl.when(c == 0)
        def _():
            def fill(i, carry):
                for b in range(2):
                    z_scr[b, pl.ds(pl.multiple_of(i * PITCH, 8), n1), :] = _short_conv_block(
                        pv_ref, b, i, nblk, cwv_ref, cbv_ref)
                return carry
            lax.fori_loop(0, nblk, fill, 0)

        _fft_stage1(z_scr, s1_ref, gre_scr, gim_scr)

    def mid(ql, carry):
        q = g * per_group + ql
        xre, xim = _stage2_block(gre_scr, gim_scr, s2_ref, q)
        krows = pl.ds(pl.multiple_of(ql * n1, 8), n1)
        kre = k_ref[0, 0, krows, :]
        kim = k_ref[0, 1, krows, :]
        kre = jnp.concatenate([kre, kre], axis=1)
        kim = jnp.concatenate([kim, kim], axis=1)
        ys = jnp.concatenate([xre * kre - xim * kim, xre * kim + xim * kre], axis=0)
        hh = jnp.dot(s2i_ref[...], ys.astype(BF16), preferred_element_type=F32)
        rows = pl.ds(pl.multiple_of(q * PITCH, 8), n1)
        for b in range(2):
            cols = slice(b * LANES, (b + 1) * LANES)
            gre_scr[b, rows, :] = hh[:n1, cols]
            gim_scr[b, rows, :] = hh[n1:, cols]
        return carry

    lax.fori_loop(0, per_group, mid, 0, unroll=FFT_MID_UNROLL)

    @pl.when(g == ngroups - 1)
    def _():
        _fft_stage1_inv(gre_scr, gim_scr, s1i_ref, y_scr)

        def post(px_ref, cw_ref, cb_ref, order, store):
            def blk(i, carry):
                rows = pl.ds(pl.multiple_of(i * PITCH, 8), n1)
                for b in range(2):
                    zb = z_scr[b, rows, :]
                    gate = _short_conv_block(px_ref, b, i, nblk, cw_ref, cb_ref)
                    store(b, i, rows, gate * (y_scr[b, rows, :] + zb * skip_ref[order:order + 1, :]))
                return carry
            lax.fori_loop(0, nblk, blk, 0)

        @pl.when(c == 0)
        def _():
            def store(b, i, rows, val):
                z_scr[b, rows, :] = val
            post(px1_ref, cw1_ref, cb1_ref, 0, store)

        @pl.when(c == 1)
        def _():
            def store(b, i, rows, val):
                o_ref[b, pl.ds(pl.multiple_of(i * n1, n1), n1), :] = val.astype(o_ref.dtype)
            post(px2_ref, cw2_ref, cb2_ref, 1, store)


def _hyena(pg3, proj_col0, conv_w, conv_b, skip, kspec, s1, s1i, s2, s2i, width):
    bsz, seq, _ = pg3.shape
    assert bsz == 2
    n1 = FFT_N1
    n2 = 2 * seq // n1
    nct = width // LANES
    ngroups = 4
    krows = (n2 // ngroups) * n1
    col = lambda off: (lambda j, c, g: (0, 0, proj_col0 // LANES + off * nct + j))
    cw = lambda off: (lambda j, c, g: (0, off * nct + j))
    const3 = lambda j, c, g: (0, 0, 0)
    const2 = lambda j, c, g: (0, 0)
    return pl.pallas_call(
        _hyena_body,
        grid=(nct, 2, ngroups),
        in_specs=[pl.BlockSpec((2, seq, LANES), col(0)),
                  pl.BlockSpec((2, seq, LANES), col(1)),
                  pl.BlockSpec((2, seq, LANES), col(2)),
                  pl.BlockSpec((3, LANES), cw(0)),
                  pl.BlockSpec((3, LANES), cw(1)),
                  pl.BlockSpec((3, LANES), cw(2)),
                  pl.BlockSpec((1, LANES), cw(0)),
                  pl.BlockSpec((1, LANES), cw(1)),
                  pl.BlockSpec((1, LANES), cw(2)),
                  pl.BlockSpec((2, LANES), lambda j, c, g: (0, j)),
                  pl.BlockSpec((1, 2, krows, LANES), lambda j, c, g: (c, 0, g, j)),
                  pl.BlockSpec(s1.shape, const3),
                  pl.BlockSpec(s1i.shape, const3),
                  pl.BlockSpec(s2.shape, const2),
                  pl.BlockSpec(s2i.shape, const2)],
        out_specs=pl.BlockSpec((2, seq, LANES), lambda j, c, g: (0, 0, j)),
        out_shape=jax.ShapeDtypeStruct((2, seq, width), BF16),
        scratch_shapes=[pltpu.VMEM((2, (n2 // 2) * PITCH, LANES), F32),
                        pltpu.VMEM((2, (n2 // 2) * PITCH, LANES), F32),
                        pltpu.VMEM((2, n2 * PITCH, LANES), F32),
                        pltpu.VMEM((2, n2 * PITCH, LANES), F32)],
        compiler_params=_cparams(("arbitrary", "arbitrary", "arbitrary"), 56),
        name="hyena",
    )(pg3, pg3, pg3, conv_w, conv_w, conv_w, conv_b, conv_b, conv_b, skip, kspec, s1, s1i, s2, s2i)


ATTN_QBLK = 128
ATTN_UNROLL = 8


def _attention_body(pos_ref, freq_ref, sign_ref, *refs):
    qkv_refs = refs[:9]
    o_ref = refs[9]
    cos_scr, sin_scr, q_scr, k_scr, v_scr, og_scr, lse_scr = refs[10:]
    seq = q_scr.shape[0]
    chunk = 512
    nchunks = seq // chunk

    @pl.when(pl.program_id(1) == 0)
    def _():
        def trig(i, carry):
            rows = pl.ds(pl.multiple_of(i * chunk, chunk), chunk)
            ang = pos_ref[0, rows, :].astype(F32) * freq_ref[...]
            cos_scr[rows, :] = jnp.cos(ang)
            sin_scr[rows, :] = jnp.sin(ang) * sign_ref[...]
            return carry
        lax.fori_loop(0, nchunks, trig, 0)

    def rotate(src_ref, dst_ref):
        def body(i, carry):
            rows = pl.ds(pl.multiple_of(i * chunk, chunk), chunk)
            t = src_ref[0, rows, :].astype(F32)
            lane = lax.broadcasted_iota(jnp.int32, t.shape, 1)
            partner = jnp.where(lane < ROT_DIM // 2,
                                pltpu.roll(t, LANES - ROT_DIM // 2, 1), pltpu.roll(t, ROT_DIM // 2, 1))
            dst_ref[rows, :] = t * cos_scr[rows, :] + partner * sin_scr[rows, :]
            return carry
        lax.fori_loop(0, nchunks, body, 0)

    def widen(src_ref, dst_ref):
        def body(i, carry):
            rows = pl.ds(pl.multiple_of(i * chunk, chunk), chunk)
            dst_ref[rows, :] = src_ref[0, rows, :].astype(F32)
            return carry
        lax.fori_loop(0, nchunks, body, 0)

    scale = HEAD_DIM ** -0.5
    for gi, (window, dil) in enumerate(ATTN_GROUPS):
        rotate(qkv_refs[3 * gi], q_scr)
        rotate(qkv_refs[3 * gi + 1], k_scr)
        widen(qkv_refs[3 * gi + 2], v_scr)
        n = seq // dil
        half = window // (2 * dil)
        tk = min(n, ATTN_QBLK + 2 * half)
        blocks_per_res = n // ATTN_QBLK

        def block(u, carry, dil=dil, n=n, half=half, tk=tk, blocks_per_res=blocks_per_res, gi=gi):
            r = u // blocks_per_res
            m = u % blocks_per_res
            q0 = m * ATTN_QBLK
            k0 = jnp.clip(q0 - half, 0, n - tk)
            qrows = pl.ds(r + dil * q0, ATTN_QBLK, stride=dil)
            krows = pl.ds(r + dil * k0, tk, stride=dil)
            qb = q_scr[qrows, :].astype(BF16)
            kb = k_scr[krows, :].astype(BF16)
            vb = v_scr[krows, :].astype(BF16)
            s = lax.dot_general(qb, kb, (((1,), (1,)), ((), ())), preferred_element_type=F32) * scale
            qi = q0 + lax.broadcasted_iota(jnp.int32, (ATTN_QBLK, tk), 0)
            kj = k0 + lax.broadcasted_iota(jnp.int32, (ATTN_QBLK, tk), 1)
            s = jnp.where(jnp.abs(qi - kj) <= half, s, NEG_INF)
            mx = jnp.max(s, axis=-1, keepdims=True)
            p = jnp.exp(s - mx)
            l = jnp.sum(p, axis=-1, keepdims=True)
            o = jnp.dot(p.astype(BF16), vb, preferred_element_type=F32) / l
            og_scr[gi, qrows, :] = o
            lse_scr[gi, qrows, :] = jnp.broadcast_to(mx + jnp.log(l), (ATTN_QBLK, LANES))
            return carry

        lax.fori_loop(0, seq // ATTN_QBLK, block, 0, unroll=ATTN_UNROLL)

    def merge(i, carry):
        rows = pl.ds(pl.multiple_of(i * chunk, chunk), chunk)
        lses = [lse_scr[gi, rows, :] for gi in range(len(ATTN_GROUPS))]
        mx = functools.reduce(jnp.maximum, lses)
        ws = [jnp.exp(v - mx) for v in lses]
        den = functools.reduce(lambda a, b: a + b, ws)
        num = functools.reduce(lambda a, b: a + b,
                               [w * og_scr[gi, rows, :] for gi, w in enumerate(ws)])
        o_ref[0, rows, :] = (num / den).astype(o_ref.dtype)
        return carry

    lax.fori_loop(0, nchunks, merge, 0)


def _attention(pg3, pos3, freq_row, sign_row, qkv_col0):
    bsz, seq, _ = pg3.shape
    ng = len(ATTN_GROUPS)
    in_specs = [pl.BlockSpec((1, seq, 1), lambda b, h: (b, 0, 0)),
                pl.BlockSpec((1, LANES), lambda b, h: (0, 0)),
                pl.BlockSpec((1, LANES), lambda b, h: (0, 0))]
    for gi in range(ng):
        for which in range(3):
            base = qkv_col0 // LANES + which * N_ATTN_HEADS + gi * HEADS_PER_GROUP
            in_specs.append(pl.BlockSpec((1, seq, LANES), lambda b, h, base=base: (b, 0, base + h)))
    return pl.pallas_call(
        _attention_body,
        grid=(bsz, HEADS_PER_GROUP),
        in_specs=in_specs,
        out_specs=pl.BlockSpec((1, seq, LANES), lambda b, h: (b, 0, h)),
        out_shape=jax.ShapeDtypeStruct((bsz, seq, HEADS_PER_GROUP * HEAD_DIM), BF16),
        scratch_shapes=[pltpu.VMEM((seq, LANES), F32), pltpu.VMEM((seq, LANES), F32),
                        pltpu.VMEM((seq, LANES), F32), pltpu.VMEM((seq, LANES), F32),
                        pltpu.VMEM((seq, LANES), F32),
                        pltpu.VMEM((ng, seq, LANES), F32), pltpu.VMEM((ng, seq, LANES), F32)],
        compiler_params=_cparams(("arbitrary", "arbitrary"), 56),
        name="attention",
    )(pos3, freq_row, sign_row, *([pg3] * (3 * ng)))


def _mix_out_body(yhy_ref, yat_ref, ghy_ref, gat_ref, x_ref, mod_ref, gpost_ref, gpre_ref,
                  whyo_ref, wato_ref, wout_ref, wrhi_ref, wrlo_ref, br_ref, x1_ref, h2_ref, lg_ref):
    a = jnp.dot(yhy_ref[...], whyo_ref[...], preferred_element_type=F32)
    b = jnp.dot(yat_ref[...], wato_ref[...], preferred_element_type=F32)
    merged = ghy_ref[...].astype(F32) * a + gat_ref[...].astype(F32) * b
    y = jnp.dot(merged.astype(BF16), wout_ref[...], preferred_element_type=F32)
    x1 = x_ref[...] + mod_ref[0, 2:3, :] * _rms(y, gpost_ref[...])
    x1_ref[...] = x1
    h2 = _rms(x1, gpre_ref[...]) * (1.0 + mod_ref[0, 4:5, :]) + mod_ref[0, 3:4, :]
    h2_ref[...] = h2
    h2_hi = h2.astype(BF16)
    h2_lo = (h2 - h2_hi.astype(F32)).astype(BF16)
    w_hi = wrhi_ref[...]
    w_lo = wrlo_ref[...]
    lg_ref[...] = (jnp.dot(h2_hi, w_hi, preferred_element_type=F32)
                   + jnp.dot(h2_lo, w_hi, preferred_element_type=F32)
                   + jnp.dot(h2_hi, w_lo, preferred_element_type=F32)
                   + jnp.dot(h2_lo, w_lo, preferred_element_type=F32)) + br_ref[...]


def _mix_out(y_hy, y_at, pg, x2d, mod, g_post, g_pre, w_hy_o, w_at_o, w_out, w_r_hi, w_r_lo, b_r, seq):
    t, d = x2d.shape
    tm = 256
    per_batch = seq // tm
    gblk = 0
    const = lambda i: (0, 0)
    return pl.pallas_call(
        _mix_out_body,
        grid=(t // tm,),
        in_specs=[pl.BlockSpec((tm, y_hy.shape[1]), lambda i: (i, 0)),
                  pl.BlockSpec((tm, y_at.shape[1]), lambda i: (i, 0)),
                  pl.BlockSpec((tm, d), lambda i: (i, gblk)),
                  pl.BlockSpec((tm, d), lambda i: (i, gblk + 1)),
                  pl.BlockSpec((tm, d), lambda i: (i, 0)),
                  pl.BlockSpec((1, 6, d), lambda i: (i // per_batch, 0, 0)),
                  pl.BlockSpec((1, d), const),
                  pl.BlockSpec((1, d), const),
                  pl.BlockSpec(w_hy_o.shape, const),
                  pl.BlockSpec(w_at_o.shape, const),
                  pl.BlockSpec(w_out.shape, const),
                  pl.BlockSpec(w_r_hi.shape, const),
                  pl.BlockSpec(w_r_lo.shape, const),
                  pl.BlockSpec((1, LANES), const)],
        out_specs=[pl.BlockSpec((tm, d), lambda i: (i, 0)),
                   pl.BlockSpec((tm, d), lambda i: (i, 0)),
                   pl.BlockSpec((tm, LANES), lambda i: (i, 0))],
        out_shape=[jax.ShapeDtypeStruct((t, d), F32),
                   jax.ShapeDtypeStruct((t, d), F32),
                   jax.ShapeDtypeStruct((t, LANES), F32)],
        compiler_params=_cparams(("arbitrary",), 56),
        name="mix_out",
    )(y_hy, y_at, pg, pg, x2d, mod, g_post, g_pre, w_hy_o, w_at_o, w_out, w_r_hi, w_r_lo, b_r)


def _route_body(lg_ref, eid_ref, gate_ref):
    lg = lg_ref[...]
    lane = lax.broadcasted_iota(jnp.int32, lg.shape, 1)
    big = jnp.int32(1 << 20)

    def first_argmax(vals, mask):
        v = jnp.where(mask, vals, -jnp.inf)
        mx = jnp.max(v, axis=-1, keepdims=True)
        idx = jnp.min(jnp.where(mask & (v == mx), lane, big), axis=-1, keepdims=True)
        return mx, idx

    gmask = lane < N_EXPERT_GROUPS
    gmax, gidx = first_argmax(lg, gmask)
    gval = 1.0 / jnp.sum(jnp.where(gmask, jnp.exp(lg - gmax), 0.0), axis=-1, keepdims=True)
    lo = N_EXPERT_GROUPS + gidx * EXPERTS_PER_GROUP
    emask = (lane >= lo) & (lane < lo + EXPERTS_PER_GROUP)
    v1, i1 = first_argmax(lg, emask)
    v2, i2 = first_argmax(lg, emask & (lane != i1))
    e2 = jnp.exp(v2 - v1)
    p1 = 1.0 / (1.0 + e2)
    p2 = e2 / (1.0 + e2)
    eid = jnp.where(lane == 0, i1, i2) - N_EXPERT_GROUPS
    gate = gval * jnp.where(lane == 0, p1, p2)
    eid_ref[...] = eid[:, :TOP_K]
    gate_ref[...] = gate[:, :TOP_K]


def _route(logits):
    t = logits.shape[0]
    tm = 512
    return pl.pallas_call(
        _route_body,
        grid=(t // tm,),
        in_specs=[pl.BlockSpec((tm, LANES), lambda i: (i, 0))],
        out_specs=[pl.BlockSpec((tm, TOP_K), lambda i: (i, 0)),
                   pl.BlockSpec((tm, TOP_K), lambda i: (i, 0))],
        out_shape=[jax.ShapeDtypeStruct((t, TOP_K), jnp.int32),
                   jax.ShapeDtypeStruct((t, TOP_K), F32)],
        compiler_params=_cparams(("arbitrary",), 32),
        name="route",
    )(logits)


def _rank_body(e_ref, rank_ref, cnt_ref, carry_scr):
    i = pl.program_id(0)
    r = e_ref.shape[0]

    @pl.when(i == 0)
    def _():
        carry_scr[...] = jnp.zeros_like(carry_scr)

    lane = lax.broadcasted_iota(jnp.int32, (r, LANES), 1)
    onehot = (lane == e_ref[...]).astype(F32)
    tri = (lax.broadcasted_iota(jnp.int32, (r, r), 1)
           < lax.broadcasted_iota(jnp.int32, (r, r), 0)).astype(BF16)
    before = jnp.dot(tri, onehot.astype(BF16), preferred_element_type=F32) + carry_scr[0:1, :]
    rank_ref[...] = jnp.sum(onehot * before, axis=-1, keepdims=True).astype(jnp.int32)
    total = carry_scr[0:1, :] + jnp.sum(onehot, axis=0, keepdims=True)
    carry_scr[...] = jnp.broadcast_to(total, carry_scr.shape)
    cnt_ref[...] = jnp.broadcast_to(total, cnt_ref.shape).astype(jnp.int32)


def _rank(flat_e):
    a = flat_e.shape[0]
    r = 512
    return pl.pallas_call(
        _rank_body,
        grid=(a // r,),
        in_specs=[pl.BlockSpec((r, 1), lambda i: (i, 0))],
        out_specs=[pl.BlockSpec((r, 1), lambda i: (i, 0)),
                   pl.BlockSpec((8, LANES), lambda i: (0, 0))],
        out_shape=[jax.ShapeDtypeStruct((a, 1), jnp.int32),
                   jax.ShapeDtypeStruct((8, LANES), jnp.int32)],
        scratch_shapes=[pltpu.VMEM((8, LANES), F32)],
        compiler_params=_cparams(("arbitrary",), 32),
        name="rank",
    )(flat_e)


def _dest_body(e_ref, rank_ref, ps_ref, o_ref):
    lane = lax.broadcasted_iota(jnp.int32, (e_ref.shape[0], LANES), 1)
    first = jnp.sum(jnp.where(lane == e_ref[...], ps_ref[...], 0.0), axis=-1, keepdims=True)
    o_ref[...] = first.astype(jnp.int32) + rank_ref[...]


def _dest(flat_e, rank, pstarts_row):
    a = flat_e.shape[0]
    r = 2048
    return pl.pallas_call(
        _dest_body,
        grid=(a // r,),
        in_specs=[pl.BlockSpec((r, 1), lambda i: (i, 0)),
                  pl.BlockSpec((r, 1), lambda i: (i, 0)),
                  pl.BlockSpec((1, LANES), lambda i: (0, 0))],
        out_specs=pl.BlockSpec((r, 1), lambda i: (i, 0)),
        out_shape=jax.ShapeDtypeStruct((a, 1), jnp.int32),
        compiler_params=_cparams(("arbitrary",), 32),
        name="dest",
    )(flat_e, rank, pstarts_row)


DMA_UNROLL = 8


def _experts_body(ord_ref, elist_ref, src_ref, cnt_ref, h_hbm, w1_hbm, w3_hbm, w2_hbm, y_ref,
                  x_scr, w1_scr, w3_scr, w2_scr, sem, wsem):
    i = pl.program_id(0)
    nrows = x_scr.shape[1]
    nused = cnt_ref[0]
    nexp = cnt_ref[1]
    k = ord_ref[i]
    first = (i == 0) | (k != ord_ref[jnp.maximum(i - 1, 0)])

    def weights(kk, act):
        e = elist_ref[kk]
        slot = kk % 2
        act(pltpu.make_async_copy(w1_hbm.at[e], w1_scr.at[slot], wsem.at[slot, 0]))
        act(pltpu.make_async_copy(w3_hbm.at[e], w3_scr.at[slot], wsem.at[slot, 1]))
        act(pltpu.make_async_copy(w2_hbm.at[e], w2_scr.at[slot], wsem.at[slot, 2]))

    def gather(blk, act):
        slot = blk % 2
        base = blk * nrows

        def body(r, carry):
            act(pltpu.make_async_copy(h_hbm.at[pl.ds(src_ref[base + r], 1)],
                                      x_scr.at[slot, pl.ds(r, 1)], sem.at[slot]))
            return carry

        lax.fori_loop(0, nrows, body, 0, unroll=DMA_UNROLL)

    @pl.when(i == 0)
    def _():
        weights(k, lambda cp: cp.start())
        gather(i, lambda cp: cp.start())

    @pl.when(i + 1 < nused)
    def _():
        gather(i + 1, lambda cp: cp.start())

    @pl.when((i < nused) & first & (k + 1 < nexp))
    def _():
        weights(k + 1, lambda cp: cp.start())

    @pl.when((i < nused) & first)
    def _():
        weights(k, lambda cp: cp.wait())

    @pl.when(i < nused)
    def _():
        gather(i, lambda cp: cp.wait())
        wslot = k % 2
        xb = x_scr[i % 2].astype(BF16)
        a = jnp.dot(xb, w1_scr[wslot].astype(BF16), preferred_element_type=F32)
        b = jnp.dot(xb, w3_scr[wslot].astype(BF16), preferred_element_type=F32)
        act = (a * jax.nn.sigmoid(a) * b).astype(BF16)
        y_ref[...] = jnp.dot(act, w2_scr[wslot].astype(BF16), preferred_element_type=F32)

    @pl.when(i >= nused)
    def _():
        y_ref[...] = jnp.zeros_like(y_ref)


def _experts(block_ord, elist, src_tok, counts2, h2, w1, w3, w2):
    nblocks = block_ord.shape[0]
    d = h2.shape[1]
    ff = w1.shape[2]
    rb = EXPERT_ROW_BLOCK
    grid_spec = pltpu.PrefetchScalarGridSpec(
        num_scalar_prefetch=4,
        grid=(nblocks,),
        in_specs=[pl.BlockSpec(memory_space=pl.ANY)] * 4,
        out_specs=pl.BlockSpec((rb, d), lambda i, *_: (i, 0)),
        scratch_shapes=[pltpu.VMEM((2, rb, d), F32),
                        pltpu.VMEM((2, d, ff), F32), pltpu.VMEM((2, d, ff), F32),
                        pltpu.VMEM((2, ff, d), F32),
                        pltpu.SemaphoreType.DMA((2,)), pltpu.SemaphoreType.DMA((2, 3))],
    )
    return pl.pallas_call(
        _experts_body,
        grid_spec=grid_spec,
        out_shape=jax.ShapeDtypeStruct((nblocks * rb, d), F32),
        compiler_params=_cparams(("arbitrary",), 56),
        name="experts",
    )(block_ord, elist, src_tok, counts2, h2, w1, w3, w2)


def _combine_body(dest_ref, y_hbm, gate_ref, x1_ref, mod_ref, g_ref, o_ref, buf_scr, sem):
    i = pl.program_id(0)
    tm = x1_ref.shape[0]

    def gather(step, act):
        slot = step % 2
        base = step * tm * TOP_K

        def body(r, carry):
            for k in range(TOP_K):
                act(pltpu.make_async_copy(y_hbm.at[pl.ds(dest_ref[base + TOP_K * r + k], 1)],
                                          buf_scr.at[slot, k, pl.ds(r, 1)], sem.at[slot]))
            return carry

        lax.fori_loop(0, tm, body, 0, unroll=DMA_UNROLL // TOP_K)

    @pl.when(i == 0)
    def _():
        gather(i, lambda cp: cp.start())

    @pl.when(i + 1 < pl.num_programs(0))
    def _():
        gather(i + 1, lambda cp: cp.start())

    gather(i, lambda cp: cp.wait())
    slot = i % 2
    gate = gate_ref[...]
    y = buf_scr[slot, 0] * gate[:, 0:1] + buf_scr[slot, 1] * gate[:, 1:2]
    o_ref[...] = x1_ref[...] + mod_ref[0, 5:6, :] * _rms(y, g_ref[...])


def _combine(dest, ybuf, gate, x1, mod, g_post, seq):
    t, d = x1.shape
    tm = 128
    per_batch = seq // tm
    grid_spec = pltpu.PrefetchScalarGridSpec(
        num_scalar_prefetch=1,
        grid=(t // tm,),
        in_specs=[pl.BlockSpec(memory_space=pl.ANY),
                  pl.BlockSpec((tm, TOP_K), lambda i, dst: (i, 0)),
                  pl.BlockSpec((tm, d), lambda i, dst: (i, 0)),
                  pl.BlockSpec((1, 6, d), lambda i, dst: (i // per_batch, 0, 0)),
                  pl.BlockSpec((1, d), lambda i, dst: (0, 0))],
        out_specs=pl.BlockSpec((tm, d), lambda i, dst: (i, 0)),
        scratch_shapes=[pltpu.VMEM((2, TOP_K, tm, d), F32), pltpu.SemaphoreType.DMA((2,))],
    )
    return pl.pallas_call(
        _combine_body,
        grid_spec=grid_spec,
        out_shape=jax.ShapeDtypeStruct((t, d), F32),
        compiler_params=_cparams(("arbitrary",), 32),
        name="combine",
    )(dest, ybuf, gate, x1, mod, g_post)


def _layer(x, c, positions, w_ada, b_ada, g_mix_pre, g_mix_post, g_ffn_pre, g_ffn_post,
           w_in, conv_w, conv_b, filt_w1, filt_b1, filt_w2, filt_b2, filt_w3, filt_freq,
           hyena_skip, w_branch_gate, b_branch_gate, w_hy_o, w_at_o, w_out,
           w_group, b_group, w_expert, b_expert, w1_exp, w3_exp, w2_exp):
    bsz, seq, d = x.shape
    t = bsz * seq
    width = hyena_skip.shape[1]
    row = lambda v: v.reshape(1, -1)

    c_pad = jnp.pad(c, ((0, 8 - bsz), (0, 0)))
    mod = _adaln(c_pad, w_ada, row(b_ada))[:bsz].reshape(bsz, 6, d)

    x2d = x.reshape(t, d)
    n_gate = w_branch_gate.shape[1]
    n_cat = -(-(n_gate + w_in.shape[1]) // IN_PROJ_TN) * IN_PROJ_TN
    n_zero = n_cat - n_gate - w_in.shape[1]
    w_cat = jnp.concatenate([w_branch_gate.astype(BF16), w_in.astype(BF16),
                             jnp.zeros((d, n_zero), BF16)], axis=1)
    b_cat = jnp.concatenate([b_branch_gate, jnp.zeros((n_cat - n_gate,), F32)]).reshape(1, -1)
    pg = _in_proj(x2d, mod, row(g_mix_pre), w_cat, b_cat, n_gate, seq)
    pg3 = pg.reshape(bsz, seq, -1)

    hidden = filt_w2.shape[0]
    nfeat = -(-filt_w1.shape[0] // 8) * 8
    bands = np.zeros((nfeat, 1), np.float32)
    band_vals = np.linspace(1e-4, FILTER_BANDS - 1, FILTER_BANDS, dtype=np.float32)
    bands[1:1 + FILTER_BANDS, 0] = band_vals
    bands[1 + FILTER_BANDS:1 + 2 * FILTER_BANDS, 0] = band_vals
    w1p = jnp.pad(filt_w1, ((0, nfeat - filt_w1.shape[0]), (0, 0))).T
    col = lambda v: v.reshape(-1, 1)
    max_decay = math.log(DECAY_TARGET) / FAST_DECAY_PCT
    min_decay = math.log(DECAY_TARGET) / SLOW_DECAY_PCT
    deltas = jnp.abs(jnp.linspace(min_decay, max_decay, width, dtype=F32)).reshape(1, -1)
    uw = _filters(seq, width, jnp.asarray(bands), w1p, col(filt_b1), filt_w2.T, col(filt_b2),
                  col(filt_freq), deltas, filt_w3)
    s1, s1i, s2, s2i = _fft_tables(FFT_N1, 2 * seq // FFT_N1)
    kspec = _spectra(uw, s1, s2)
    y_hy = _hyena(pg3, n_gate, conv_w, row(conv_b), hyena_skip, kspec, s1, s1i, s2, s2i, width)

    half = ROT_DIM // 2
    inv_freq = np.power(ROPE_THETA, -2.0 * np.arange(half, dtype=np.float32) / ROT_DIM).astype(np.float32)
    freq_row = np.zeros((1, LANES), np.float32)
    freq_row[0, :half] = inv_freq
    freq_row[0, half:ROT_DIM] = inv_freq
    sign_row = np.zeros((1, LANES), np.float32)
    sign_row[0, :half] = -1.0
    sign_row[0, half:ROT_DIM] = 1.0
    y_at = _attention(pg3, positions.reshape(bsz, seq, 1), jnp.asarray(freq_row), jnp.asarray(sign_row),
                      n_gate + 3 * width)

    w_r = jnp.concatenate([w_group, jnp.transpose(w_expert, (1, 0, 2)).reshape(d, N_EXPERTS)], axis=1)
    w_r = jnp.pad(w_r, ((0, 0), (0, LANES - w_r.shape[1])))
    w_r_hi = w_r.astype(BF16)
    w_r_lo = (w_r - w_r_hi.astype(F32)).astype(BF16)
    b_r = jnp.pad(jnp.concatenate([b_group, b_expert.reshape(-1)]), (0, LANES - N_EXPERT_GROUPS - N_EXPERTS))
    x1, h2, logits = _mix_out(y_hy.reshape(t, width), y_at.reshape(t, -1), pg, x2d, mod,
                              row(g_mix_post), row(g_ffn_pre), w_hy_o.astype(BF16), w_at_o.astype(BF16),
                              w_out.astype(BF16), w_r_hi, w_r_lo, b_r.reshape(1, -1), seq)

    eid, gate = _route(logits)
    flat_e = eid.reshape(t * TOP_K, 1)
    rank, counts = _rank(flat_e)
    counts = counts[0, :N_EXPERTS]
    rb = EXPERT_ROW_BLOCK
    padded = (counts + rb - 1) // rb * rb
    pends = jnp.cumsum(padded)
    pstarts = pends - padded
    pstarts_row = jnp.pad(pstarts.astype(F32), (0, LANES - N_EXPERTS)).reshape(1, LANES)
    dest = _dest(flat_e, rank, pstarts_row)[:, 0]
    n_blocks = t * TOP_K // rb + N_EXPERTS
    block_start = jnp.arange(n_blocks, dtype=jnp.int32) * rb
    block_e = jnp.minimum(jnp.searchsorted(pends, block_start, side='right'), N_EXPERTS - 1).astype(jnp.int32)
    tok = jnp.arange(t * TOP_K, dtype=jnp.int32) // TOP_K
    src_tok = (jnp.arange(n_blocks * rb, dtype=jnp.int32) % t).at[dest].set(tok)
    has_rows = jnp.cumsum((counts > 0).astype(jnp.int32))
    elist = jnp.minimum(jnp.searchsorted(has_rows, jnp.arange(1, N_EXPERTS + 1, dtype=jnp.int32), side='left'),
                        N_EXPERTS - 1).astype(jnp.int32)
    block_ord = (has_rows - 1)[block_e].astype(jnp.int32)
    counts2 = jnp.stack([pends[-1] // rb, has_rows[-1]]).astype(jnp.int32)

    ybuf = _experts(block_ord, elist, src_tok, counts2, h2, w1_exp, w3_exp, w2_exp)
    out = _combine(dest, ybuf, gate, x1, mod, row(g_ffn_post), seq)
    return out.reshape(bsz, seq, d)


def kernel(x, c, positions, w_ada, b_ada, g_mix_pre, g_mix_post, g_ffn_pre, g_ffn_post, w_in, conv_w, conv_b, filt_w1, filt_b1, filt_w2, filt_b2, filt_w3, filt_freq, hyena_skip, w_branch_gate, b_branch_gate, w_hy_o, w_at_o, w_out, w_group, b_group, w_expert, b_expert, w1_exp, w3_exp, w2_exp):
    depth = w_ada.shape[0]
    for l in range(depth):
        x = _layer(x, c, positions, w_ada[l], b_ada[l], g_mix_pre[l], g_mix_post[l], g_ffn_pre[l],
                   g_ffn_post[l], w_in[l], conv_w[l], conv_b[l], filt_w1[l], filt_b1[l], filt_w2[l],
                   filt_b2[l], filt_w3[l], filt_freq[l], hyena_skip[l], w_branch_gate[l],
                   b_branch_gate[l], w_hy_o[l], w_at_o[l], w_out[l], w_group[l], b_group[l],
                   w_expert[l], b_expert[l], w1_exp[l], w3_exp[l], w2_exp[l])
    return x
```

```python
import functools
import math

import numpy as np
import jax
import jax.numpy as jnp
from jax import lax
from jax.experimental import pallas as pl
from jax.experimental.pallas import tpu as pltpu

F32 = jnp.float32
BF16 = jnp.bfloat16

LANES = 128
MIB = 1024 * 1024

RMS_EPS = 1e-6
NEG_INF = -1e30

HEAD_DIM = 128
ROT_DIM = HEAD_DIM // 4
ROPE_THETA = 500000.0
ATTN_GROUPS = ((128, 1), (512, 4), (2048, 16))
HEADS_PER_GROUP = 4
N_ATTN_HEADS = HEADS_PER_GROUP * len(ATTN_GROUPS)

FILTER_BANDS = 16
DECAY_TARGET = 1e-2
FAST_DECAY_PCT = 0.3
SLOW_DECAY_PCT = 1.5

N_EXPERT_GROUPS = 8
EXPERTS_PER_GROUP = 8
N_EXPERTS = N_EXPERT_GROUPS * EXPERTS_PER_GROUP
TOP_K = 2
EXPERT_ROW_BLOCK = 128

FFT_N1 = 128
PITCH = FFT_N1 + 8
FFT_UNROLL = 16
FFT_MID_UNROLL = 8


def _cparams(sem, vmem_mib):
    return pltpu.CompilerParams(dimension_semantics=sem, vmem_limit_bytes=vmem_mib * MIB)


def _rms(x, g):
    return x * lax.rsqrt(jnp.mean(x * x, axis=-1, keepdims=True) + RMS_EPS) * g


def _adaln_body(c_ref, w_ref, b_ref, o_ref):
    c = c_ref[...]
    cond = c * jax.nn.sigmoid(c)
    o_ref[...] = jnp.dot(cond.astype(BF16), w_ref[...].astype(BF16),
                         preferred_element_type=F32) + b_ref[...]


def _adaln(c_pad, w_ada, b_ada):
    rows, d = c_pad.shape
    n = w_ada.shape[1]
    tn = 1024
    return pl.pallas_call(
        _adaln_body,
        grid=(n // tn,),
        in_specs=[pl.BlockSpec((rows, d), lambda j: (0, 0)),
                  pl.BlockSpec((d, tn), lambda j: (0, j)),
                  pl.BlockSpec((1, tn), lambda j: (0, j))],
        out_specs=pl.BlockSpec((rows, tn), lambda j: (0, j)),
        out_shape=jax.ShapeDtypeStruct((rows, n), F32),
        compiler_params=_cparams(("arbitrary",), 40),
        name="adaln",
    )(c_pad, w_ada, b_ada)


IN_PROJ_TM = 1024
IN_PROJ_TN = 1024
IN_PROJ_SUB_M = 256
IN_PROJ_SUB_N = 512


def _in_proj_body(n_gate, x_ref, mod_ref, g_ref, w_ref, b_ref, o_ref, h_scr):
    j = pl.program_id(1)

    @pl.when(j == 0)
    def _():
        x = x_ref[...]
        h = _rms(x, g_ref[...]) * (1.0 + mod_ref[0, 1:2, :]) + mod_ref[0, 0:1, :]
        h_scr[...] = h.astype(BF16)

    is_gate = j < n_gate
    tm, tn = o_ref.shape
    for mi in range(tm // IN_PROJ_SUB_M):
        rows = slice(mi * IN_PROJ_SUB_M, (mi + 1) * IN_PROJ_SUB_M)
        for ni in range(tn // IN_PROJ_SUB_N):
            cols = slice(ni * IN_PROJ_SUB_N, (ni + 1) * IN_PROJ_SUB_N)
            acc = jnp.dot(h_scr[rows, :], w_ref[:, cols], preferred_element_type=F32) + b_ref[:, cols]
            o_ref[rows, cols] = jnp.where(is_gate, jax.nn.sigmoid(acc), acc).astype(o_ref.dtype)


def _in_proj(x2d, mod, g_pre, w_cat, b_cat, n_gate_cols, seq):
    t, d = x2d.shape
    n = w_cat.shape[1]
    tm, tn = IN_PROJ_TM, IN_PROJ_TN
    per_batch = seq // tm
    return pl.pallas_call(
        functools.partial(_in_proj_body, n_gate_cols // tn),
        grid=(t // tm, n // tn),
        in_specs=[pl.BlockSpec((tm, d), lambda i, j: (i, 0)),
                  pl.BlockSpec((1, 6, d), lambda i, j: (i // per_batch, 0, 0)),
                  pl.BlockSpec((1, d), lambda i, j: (0, 0)),
                  pl.BlockSpec((d, tn), lambda i, j: (0, j)),
                  pl.BlockSpec((1, tn), lambda i, j: (0, j))],
        out_specs=pl.BlockSpec((tm, tn), lambda i, j: (i, j)),
        out_shape=jax.ShapeDtypeStruct((t, n), BF16),
        scratch_shapes=[pltpu.VMEM((tm, d), BF16)],
        compiler_params=_cparams(("arbitrary", "arbitrary"), 56),
        name="in_proj",
    )(x2d, mod, g_pre, w_cat, b_cat)


def _filters_body(seq, band_ref, w1_ref, b1_ref, w2_ref, b2_ref, fr_ref, dl_ref,
                  w3a_ref, w3b_ref, o_ref, hid_scr):
    i = pl.program_id(0)
    j = pl.program_id(1)
    tl = hid_scr.shape[0]
    row = (lax.broadcasted_iota(jnp.int32, (tl, 1), 0) + i * tl).astype(F32)

    @pl.when((j == 0) & (pl.program_id(2) == 0))
    def _():
        nfeat = band_ref.shape[0]
        pos = (lax.broadcasted_iota(jnp.int32, (1, tl), 1) + i * tl).astype(F32)
        feat = lax.broadcasted_iota(jnp.int32, (nfeat, tl), 0)
        ang = band_ref[...] * (2.0 * math.pi * pos / seq)
        feats = jnp.where(feat == 0, pos / (seq - 1.0),
                          jnp.where(feat <= FILTER_BANDS, jnp.cos(ang),
                                    jnp.where(feat <= 2 * FILTER_BANDS, -jnp.sin(ang), 0.0)))
        hi = lax.Precision.HIGHEST
        fr = fr_ref[...]
        hid = jnp.sin(fr * (jnp.dot(w1_ref[...], feats, precision=hi, preferred_element_type=F32)
                            + b1_ref[...]))
        hid = jnp.sin(fr * (jnp.dot(w2_ref[...], hid, precision=hi, preferred_element_type=F32)
                            + b2_ref[...]))
        hid_scr[...] = hid.T

    hi = lax.Precision.HIGHEST
    hid = hid_scr[...]
    decay = jnp.exp(-(row / (seq - 1.0)) * dl_ref[...])
    hf = jnp.dot(hid, w3a_ref[...], precision=hi, preferred_element_type=F32) * decay
    hb = jnp.dot(hid, w3b_ref[...], precision=hi, preferred_element_type=F32) * decay
    hb = jnp.where(row == 0.0, 0.0, hb)
    o_ref[0, 0] = hf + hb
    o_ref[0, 1] = hf - hb


def _filters(seq, width, bands, w1p, b1, w2, b2, freq, deltas, w3):
    tl, tc = 512, 512
    nct = width // tc
    hidden = w2.shape[0]
    nfeat = bands.shape[0]
    const = lambda i, j, o: (0, 0)
    return pl.pallas_call(
        functools.partial(_filters_body, float(seq)),
        grid=(seq // tl, nct, 2),
        in_specs=[pl.BlockSpec((nfeat, 1), const),
                  pl.BlockSpec((hidden, nfeat), const),
                  pl.BlockSpec((hidden, 1), const),
                  pl.BlockSpec((hidden, hidden), const),
                  pl.BlockSpec((hidden, 1), const),
                  pl.BlockSpec((hidden, 1), const),
                  pl.BlockSpec((1, tc), lambda i, j, o: (0, j)),
                  pl.BlockSpec((hidden, tc), lambda i, j, o: (0, (2 * o) * nct + j)),
                  pl.BlockSpec((hidden, tc), lambda i, j, o: (0, (2 * o + 1) * nct + j))],
        out_specs=pl.BlockSpec((1, 2, tl, tc), lambda i, j, o: (o, 0, i, j)),
        out_shape=jax.ShapeDtypeStruct((2, 2, seq, width), F32),
        scratch_shapes=[pltpu.VMEM((tl, hidden), F32)],
        compiler_params=_cparams(("arbitrary", "arbitrary", "arbitrary"), 32),
        name="filters",
    )(bands, w1p, b1, w2, b2, freq, deltas, w3, w3)


def _fft_tables(n1, n2):
    n = n1 * n2
    q = np.arange(n2)[:, None]
    b = np.arange(n2 // 2)[None, :]
    a = np.arange(n1)[:, None, None]
    ang = -2.0 * np.pi * (a * q[None] / n + (q * b)[None] / n2)
    stage1 = np.concatenate([np.cos(ang), np.sin(ang)], axis=1)
    stage1_inv = np.transpose(stage1, (0, 2, 1)) / n
    p = np.arange(n1)
    ang2 = -2.0 * np.pi * np.outer(p, p) / n1
    fre, fim = np.cos(ang2), np.sin(ang2)
    stage2 = np.block([[fre, -fim], [fim, fre]])
    stage2_inv = np.block([[fre, fim], [-fim, fre]])
    as_bf16 = lambda m: jnp.asarray(m, dtype=F32).astype(BF16)
    return as_bf16(stage1), as_bf16(stage1_inv), as_bf16(stage2), as_bf16(stage2_inv)


def _halves(ref, rows):
    return jnp.concatenate([ref[0, rows, :], ref[1, rows, :]], axis=1)


def _fft_stage1(z_ref, s1_ref, gre_ref, gim_ref):
    n1, two_n2, n2h = s1_ref.shape
    n2 = two_n2 // 2

    def step(a, carry):
        zrows = _halves(z_ref, pl.ds(a, n2h, stride=PITCH))
        g = jnp.dot(s1_ref[a], zrows.astype(BF16), preferred_element_type=F32)
        for h in range(2):
            cols = slice(h * LANES, (h + 1) * LANES)
            gre_ref[h, pl.ds(a, n2, stride=PITCH), :] = g[:n2, cols]
            gim_ref[h, pl.ds(a, n2, stride=PITCH), :] = g[n2:, cols]
        return carry

    lax.fori_loop(0, n1, step, 0, unroll=FFT_UNROLL)


def _fft_stage1_inv(gre_ref, gim_ref, s1i_ref, y_ref):
    n1, n2h, two_n2 = s1i_ref.shape
    n2 = two_n2 // 2

    def step(a, carry):
        rows = pl.ds(a, n2, stride=PITCH)
        hs = jnp.concatenate([_halves(gre_ref, rows), _halves(gim_ref, rows)], axis=0)
        y = jnp.dot(s1i_ref[a], hs.astype(BF16), preferred_element_type=F32)
        for h in range(2):
            y_ref[h, pl.ds(a, n2h, stride=PITCH), :] = y[:, h * LANES:(h + 1) * LANES]
        return carry

    lax.fori_loop(0, n1, step, 0, unroll=FFT_UNROLL)


def _stage2_block(gre_ref, gim_ref, s2_ref, q):
    n1 = s2_ref.shape[0] // 2
    rows = pl.ds(pl.multiple_of(q * PITCH, 8), n1)
    gs = jnp.concatenate([_halves(gre_ref, rows), _halves(gim_ref, rows)], axis=0)
    x = jnp.dot(s2_ref[...], gs.astype(BF16), preferred_element_type=F32)
    return x[:n1], x[n1:]


def _spectra_body(uw_ref, s1_ref, s2_ref, k_ref, z_scr, gre_scr, gim_scr):
    n1 = s2_ref.shape[0] // 2
    n2 = s1_ref.shape[1] // 2
    for h in range(2):
        for b in range(n2 // 2):
            z_scr[h, pl.ds(b * PITCH, n1), :] = uw_ref[0, h, pl.ds(b * n1, n1), :]
    _fft_stage1(z_scr, s1_ref, gre_scr, gim_scr)

    def step(q, carry):
        xre, xim = _stage2_block(gre_scr, gim_scr, s2_ref, q)
        rows = pl.ds(pl.multiple_of(q * n1, 8), n1)
        k_ref[0, 0, rows, :] = xre[:, :LANES]
        k_ref[0, 1, rows, :] = xim[:, LANES:]
        return carry

    lax.fori_loop(0, n2, step, 0, unroll=FFT_MID_UNROLL)


def _spectra(uw, s1, s2):
    _, _, seq, width = uw.shape
    n1 = FFT_N1
    n2 = 2 * seq // n1
    const3 = lambda j, o: (0, 0, 0)
    return pl.pallas_call(
        _spectra_body,
        grid=(width // LANES, 2),
        in_specs=[pl.BlockSpec((1, 2, seq, LANES), lambda j, o: (o, 0, 0, j)),
                  pl.BlockSpec(s1.shape, const3),
                  pl.BlockSpec(s2.shape, lambda j, o: (0, 0))],
        out_specs=pl.BlockSpec((1, 2, 2 * seq, LANES), lambda j, o: (o, 0, 0, j)),
        out_shape=jax.ShapeDtypeStruct((2, 2, 2 * seq, width), F32),
        scratch_shapes=[pltpu.VMEM((2, (n2 // 2) * PITCH, LANES), F32),
                        pltpu.VMEM((2, n2 * PITCH, LANES), F32),
                        pltpu.VMEM((2, n2 * PITCH, LANES), F32)],
        compiler_params=_cparams(("arbitrary", "arbitrary"), 56),
        name="spectra",
    )(uw, s1, s2)


def _short_conv_block(p_ref, b, i, nblk, w_ref, bias_ref):
    n1 = FFT_N1
    start = pl.multiple_of(i * n1, n1)
    cur = p_ref[b, pl.ds(start, n1), :].astype(F32)
    before = p_ref[b, pl.ds(pl.multiple_of(jnp.maximum(start - 16, 0), 16), 16), :].astype(F32)
    after = p_ref[b, pl.ds(pl.multiple_of(jnp.minimum(start + n1, (nblk - 1) * n1), 16), 16), :].astype(F32)
    last_prev = before[15:16] * jnp.where(i > 0, 1.0, 0.0).astype(F32)
    first_next = after[0:1] * jnp.where(i < nblk - 1, 1.0, 0.0).astype(F32)
    row = lax.broadcasted_iota(jnp.int32, (n1, 1), 0)
    prev = jnp.where(row == 0, last_prev, pltpu.roll(cur, 1, 0))
    nxt = jnp.where(row == n1 - 1, first_next, pltpu.roll(cur, n1 - 1, 0))
    return prev * w_ref[0:1, :] + cur * w_ref[1:2, :] + nxt * w_ref[2:3, :] + bias_ref[...]


def _hyena_body(pv_ref, px1_ref, px2_ref, cwv_ref, cw1_ref, cw2_ref, cbv_ref, cb1_ref, cb2_ref,
                skip_ref, k_ref, s1_ref, s1i_ref, s2_ref, s2i_ref, o_ref,
                z_scr, y_scr, gre_scr, gim_scr):
    c = pl.program_id(1)
    g = pl.program_id(2)
    ngroups = pl.num_programs(2)
    n1 = FFT_N1
    n2 = s1_ref.shape[1] // 2
    nblk = n2 // 2
    per_group = n2 // ngroups

    @pl.when(g == 0)
    def _():
        @pl.when(c == 0)
        def _():
            def fill(i, carry):
                for b in range(2):
                    z_scr[b, pl.ds(pl.multiple_of(i * PITCH, 8), n1), :] = _short_conv_block(
                        pv_ref, b, i, nblk, cwv_ref, cbv_ref)
                return carry
            lax.fori_loop(0, nblk, fill, 0)

        _fft_stage1(z_scr, s1_ref, gre_scr, gim_scr)

    def mid(ql, carry):
        q = g * per_group + ql
        xre, xim = _stage2_block(gre_scr, gim_scr, s2_ref, q)
        krows = pl.ds(pl.multiple_of(ql * n1, 8), n1)
        kre = k_ref[0, 0, krows, :]
        kim = k_ref[0, 1, krows, :]
        kre = jnp.concatenate([kre, kre], axis=1)
        kim = jnp.concatenate([kim, kim], axis=1)
        ys = jnp.concatenate([xre * kre - xim * kim, xre * kim + xim * kre], axis=0)
        hh = jnp.dot(s2i_ref[...], ys.astype(BF16), preferred_element_type=F32)
        rows = pl.ds(pl.multiple_of(q * PITCH, 8), n1)
        for b in range(2):
            cols = slice(b * LANES, (b + 1) * LANES)
            gre_scr[b, rows, :] = hh[:n1, cols]
            gim_scr[b, rows, :] = hh[n1:, cols]
        return carry

    lax.fori_loop(0, per_group, mid, 0, unroll=FFT_MID_UNROLL)

    @pl.when(g == ngroups - 1)
    def _():
        _fft_stage1_inv(gre_scr, gim_scr, s1i_ref, y_scr)

        def post(px_ref, cw_ref, cb_ref, order, store):
            def blk(i, carry):
                rows = pl.ds(pl.multiple_of(i * PITCH, 8), n1)
                for b in range(2):
                    zb = z_scr[b, rows, :]
                    gate = _short_conv_block(px_ref, b, i, nblk, cw_ref, cb_ref)
                    store(b, i, rows, gate * (y_scr[b, rows, :] + zb * skip_ref[order:order + 1, :]))
                return carry
            lax.fori_loop(0, nblk, blk, 0)

        @pl.when(c == 0)
        def _():
            def store(b, i, rows, val):
                z_scr[b, rows, :] = val
            post(px1_ref, cw1_ref, cb1_ref, 0, store)

        @pl.when(c == 1)
        def _():
            def store(b, i, rows, val):
                o_ref[b, pl.ds(pl.multiple_of(i * n1, n1), n1), :] = val.astype(o_ref.dtype)
            post(px2_ref, cw2_ref, cb2_ref, 1, store)


def _hyena(pg3, proj_col0, conv_w, conv_b, skip, kspec, s1, s1i, s2, s2i, width):
    bsz, seq, _ = pg3.shape
    assert bsz == 2
    n1 = FFT_N1
    n2 = 2 * seq // n1
    nct = width // LANES
    ngroups = 4
    krows = (n2 // ngroups) * n1
    col = lambda off: (lambda j, c, g: (0, 0, proj_col0 // LANES + off * nct + j))
    cw = lambda off: (lambda j, c, g: (0, off * nct + j))
    const3 = lambda j, c, g: (0, 0, 0)
    const2 = lambda j, c, g: (0, 0)
    return pl.pallas_call(
        _hyena_body,
        grid=(nct, 2, ngroups),
        in_specs=[pl.BlockSpec((2, seq, LANES), col(0)),
                  pl.BlockSpec((2, seq, LANES), col(1)),
                  pl.BlockSpec((2, seq, LANES), col(2)),
                  pl.BlockSpec((3, LANES), cw(0)),
                  pl.BlockSpec((3, LANES), cw(1)),
                  pl.BlockSpec((3, LANES), cw(2)),
                  pl.BlockSpec((1, LANES), cw(0)),
                  pl.BlockSpec((1, LANES), cw(1)),
                  pl.BlockSpec((1, LANES), cw(2)),
                  pl.BlockSpec((2, LANES), lambda j, c, g: (0, j)),
                  pl.BlockSpec((1, 2, krows, LANES), lambda j, c, g: (c, 0, g, j)),
                  pl.BlockSpec(s1.shape, const3),
                  pl.BlockSpec(s1i.shape, const3),
                  pl.BlockSpec(s2.shape, const2),
                  pl.BlockSpec(s2i.shape, const2)],
        out_specs=pl.BlockSpec((2, seq, LANES), lambda j, c, g: (0, 0, j)),
        out_shape=jax.ShapeDtypeStruct((2, seq, width), BF16),
        scratch_shapes=[pltpu.VMEM((2, (n2 // 2) * PITCH, LANES), F32),
                        pltpu.VMEM((2, (n2 // 2) * PITCH, LANES), F32),
                        pltpu.VMEM((2, n2 * PITCH, LANES), F32),
                        pltpu.VMEM((2, n2 * PITCH, LANES), F32)],
        compiler_params=_cparams(("arbitrary", "arbitrary", "arbitrary"), 56),
        name="hyena",
    )(pg3, pg3, pg3, conv_w, conv_w, conv_w, conv_b, conv_b, conv_b, skip, kspec, s1, s1i, s2, s2i)


ATTN_QBLK = 128
ATTN_UNROLL = 8


def _attention_body(pos_ref, freq_ref, sign_ref, *refs):
    qkv_refs = refs[:9]
    o_ref = refs[9]
    cos_scr, sin_scr, q_scr, k_scr, v_scr, og_scr, lse_scr = refs[10:]
    seq = q_scr.shape[0]
    chunk = 512
    nchunks = seq // chunk

    @pl.when(pl.program_id(1) == 0)
    def _():
        def trig(i, carry):
            rows = pl.ds(pl.multiple_of(i * chunk, chunk), chunk)
            ang = pos_ref[0, rows, :].astype(F32) * freq_ref[...]
            cos_scr[rows, :] = jnp.cos(ang)
            sin_scr[rows, :] = jnp.sin(ang) * sign_ref[...]
            return carry
        lax.fori_loop(0, nchunks, trig, 0)

    def rotate(src_ref, dst_ref):
        def body(i, carry):
            rows = pl.ds(pl.multiple_of(i * chunk, chunk), chunk)
            t = src_ref[0, rows, :].astype(F32)
            lane = lax.broadcasted_iota(jnp.int32, t.shape, 1)
            partner = jnp.where(lane < ROT_DIM // 2,
                                pltpu.roll(t, LANES - ROT_DIM // 2, 1), pltpu.roll(t, ROT_DIM // 2, 1))
            dst_ref[rows, :] = t * cos_scr[rows, :] + partner * sin_scr[rows, :]
            return carry
        lax.fori_loop(0, nchunks, body, 0, unroll=2)

    def widen(src_ref, dst_ref):
        def body(i, carry):
            rows = pl.ds(pl.multiple_of(i * chunk, chunk), chunk)
            dst_ref[rows, :] = src_ref[0, rows, :].astype(F32)
            return carry
        lax.fori_loop(0, nchunks, body, 0, unroll=2)

    scale = HEAD_DIM ** -0.5
    for gi, (window, dil) in enumerate(ATTN_GROUPS):
        rotate(qkv_refs[3 * gi], q_scr)
        rotate(qkv_refs[3 * gi + 1], k_scr)
        widen(qkv_refs[3 * gi + 2], v_scr)
        n = seq // dil
        half = window // (2 * dil)
        tk = min(n, ATTN_QBLK + 2 * half)
        blocks_per_res = n // ATTN_QBLK

        def block(u, carry, dil=dil, n=n, half=half, tk=tk, blocks_per_res=blocks_per_res, gi=gi):
            r = u // blocks_per_res
            m = u % blocks_per_res
            q0 = m * ATTN_QBLK
            k0 = jnp.clip(q0 - half, 0, n - tk)
            qrows = pl.ds(r + dil * q0, ATTN_QBLK, stride=dil)
            krows = pl.ds(r + dil * k0, tk, stride=dil)
            qb = q_scr[qrows, :].astype(BF16)
            kb = k_scr[krows, :].astype(BF16)
            vb = v_scr[krows, :].astype(BF16)
            s = lax.dot_general(qb, kb, (((1,), (1,)), ((), ())), preferred_element_type=F32) * scale
            qi = q0 + lax.broadcasted_iota(jnp.int32, (ATTN_QBLK, tk), 0)
            kj = k0 + lax.broadcasted_iota(jnp.int32, (ATTN_QBLK, tk), 1)
            s = jnp.where(jnp.abs(qi - kj) <= half, s, NEG_INF)
            mx = jnp.max(s, axis=-1, keepdims=True)
            p = jnp.exp(s - mx)
            l = jnp.sum(p, axis=-1, keepdims=True)
            o = jnp.dot(p.astype(BF16), vb, preferred_element_type=F32) / l
            og_scr[gi, qrows, :] = o
            lse_scr[gi, qrows, :] = jnp.broadcast_to(mx + jnp.log(l), (ATTN_QBLK, LANES))
            return carry

        lax.fori_loop(0, seq // ATTN_QBLK, block, 0, unroll=ATTN_UNROLL)

    def merge(i, carry):
        rows = pl.ds(pl.multiple_of(i * chunk, chunk), chunk)
        lses = [lse_scr[gi, rows, :] for gi in range(len(ATTN_GROUPS))]
        mx = functools.reduce(jnp.maximum, lses)
        ws = [jnp.exp(v - mx) for v in lses]
        den = functools.reduce(lambda a, b: a + b, ws)
        num = functools.reduce(lambda a, b: a + b,
                               [w * og_scr[gi, rows, :] for gi, w in enumerate(ws)])
        o_ref[0, rows, :] = (num / den).astype(o_ref.dtype)
        return carry

    lax.fori_loop(0, nchunks, merge, 0)


def _attention(pg3, pos3, freq_row, sign_row, qkv_col0):
    bsz, seq, _ = pg3.shape
    ng = len(ATTN_GROUPS)
    in_specs = [pl.BlockSpec((1, seq, 1), lambda b, h: (b, 0, 0)),
                pl.BlockSpec((1, LANES), lambda b, h: (0, 0)),
                pl.BlockSpec((1, LANES), lambda b, h: (0, 0))]
    for gi in range(ng):
        for which in range(3):
            base = qkv_col0 // LANES + which * N_ATTN_HEADS + gi * HEADS_PER_GROUP
            in_specs.append(pl.BlockSpec((1, seq, LANES), lambda b, h, base=base: (b, 0, base + h)))
    return pl.pallas_call(
        _attention_body,
        grid=(bsz, HEADS_PER_GROUP),
        in_specs=in_specs,
        out_specs=pl.BlockSpec((1, seq, LANES), lambda b, h: (b, 0, h)),
        out_shape=jax.ShapeDtypeStruct((bsz, seq, HEADS_PER_GROUP * HEAD_DIM), BF16),
        scratch_shapes=[pltpu.VMEM((seq, LANES), F32), pltpu.VMEM((seq, LANES), F32),
                        pltpu.VMEM((seq, LANES), F32), pltpu.VMEM((seq, LANES), F32),
                        pltpu.VMEM((seq, LANES), F32),
                        pltpu.VMEM((ng, seq, LANES), F32), pltpu.VMEM((ng, seq, LANES), F32)],
        compiler_params=_cparams(("arbitrary", "arbitrary"), 56),
        name="attention",
    )(pos3, freq_row, sign_row, *([pg3] * (3 * ng)))


def _mix_out_body(yhy_ref, yat_ref, ghy_ref, gat_ref, x_ref, mod_ref, gpost_ref, gpre_ref,
                  whyo_ref, wato_ref, wout_ref, wrhi_ref, wrlo_ref, br_ref, x1_ref, h2_ref, lg_ref):
    a = jnp.dot(yhy_ref[...], whyo_ref[...], preferred_element_type=F32)
    b = jnp.dot(yat_ref[...], wato_ref[...], preferred_element_type=F32)
    merged = ghy_ref[...].astype(F32) * a + gat_ref[...].astype(F32) * b
    y = jnp.dot(merged.astype(BF16), wout_ref[...], preferred_element_type=F32)
    x1 = x_ref[...] + mod_ref[0, 2:3, :] * _rms(y, gpost_ref[...])
    x1_ref[...] = x1
    h2 = _rms(x1, gpre_ref[...]) * (1.0 + mod_ref[0, 4:5, :]) + mod_ref[0, 3:4, :]
    h2_ref[...] = h2
    h2_hi = h2.astype(BF16)
    h2_lo = (h2 - h2_hi.astype(F32)).astype(BF16)
    w_hi = wrhi_ref[...]
    w_lo = wrlo_ref[...]
    lg_ref[...] = (jnp.dot(h2_hi, w_hi, preferred_element_type=F32)
                   + jnp.dot(h2_lo, w_hi, preferred_element_type=F32)
                   + jnp.dot(h2_hi, w_lo, preferred_element_type=F32)
                   + jnp.dot(h2_lo, w_lo, preferred_element_type=F32)) + br_ref[...]


def _mix_out(y_hy, y_at, pg, x2d, mod, g_post, g_pre, w_hy_o, w_at_o, w_out, w_r_hi, w_r_lo, b_r, seq):
    t, d = x2d.shape
    tm = 256
    per_batch = seq // tm
    gblk = 0
    const = lambda i: (0, 0)
    return pl.pallas_call(
        _mix_out_body,
        grid=(t // tm,),
        in_specs=[pl.BlockSpec((tm, y_hy.shape[1]), lambda i: (i, 0)),
                  pl.BlockSpec((tm, y_at.shape[1]), lambda i: (i, 0)),
                  pl.BlockSpec((tm, d), lambda i: (i, gblk)),
                  pl.BlockSpec((tm, d), lambda i: (i, gblk + 1)),
                  pl.BlockSpec((tm, d), lambda i: (i, 0)),
                  pl.BlockSpec((1, 6, d), lambda i: (i // per_batch, 0, 0)),
                  pl.BlockSpec((1, d), const),
                  pl.BlockSpec((1, d), const),
                  pl.BlockSpec(w_hy_o.shape, const),
                  pl.BlockSpec(w_at_o.shape, const),
                  pl.BlockSpec(w_out.shape, const),
                  pl.BlockSpec(w_r_hi.shape, const),
                  pl.BlockSpec(w_r_lo.shape, const),
                  pl.BlockSpec((1, LANES), const)],
        out_specs=[pl.BlockSpec((tm, d), lambda i: (i, 0)),
                   pl.BlockSpec((tm, d), lambda i: (i, 0)),
                   pl.BlockSpec((tm, LANES), lambda i: (i, 0))],
        out_shape=[jax.ShapeDtypeStruct((t, d), F32),
                   jax.ShapeDtypeStruct((t, d), F32),
                   jax.ShapeDtypeStruct((t, LANES), F32)],
        compiler_params=_cparams(("arbitrary",), 56),
        name="mix_out",
    )(y_hy, y_at, pg, pg, x2d, mod, g_post, g_pre, w_hy_o, w_at_o, w_out, w_r_hi, w_r_lo, b_r)


def _route_body(lg_ref, eid_ref, gate_ref):
    lg = lg_ref[...]
    lane = lax.broadcasted_iota(jnp.int32, lg.shape, 1)
    big = jnp.int32(1 << 20)

    def first_argmax(vals, mask):
        v = jnp.where(mask, vals, -jnp.inf)
        mx = jnp.max(v, axis=-1, keepdims=True)
        idx = jnp.min(jnp.where(mask & (v == mx), lane, big), axis=-1, keepdims=True)
        return mx, idx

    gmask = lane < N_EXPERT_GROUPS
    gmax, gidx = first_argmax(lg, gmask)
    gval = 1.0 / jnp.sum(jnp.where(gmask, jnp.exp(lg - gmax), 0.0), axis=-1, keepdims=True)
    lo = N_EXPERT_GROUPS + gidx * EXPERTS_PER_GROUP
    emask = (lane >= lo) & (lane < lo + EXPERTS_PER_GROUP)
    v1, i1 = first_argmax(lg, emask)
    v2, i2 = first_argmax(lg, emask & (lane != i1))
    e2 = jnp.exp(v2 - v1)
    p1 = 1.0 / (1.0 + e2)
    p2 = e2 / (1.0 + e2)
    eid = jnp.where(lane == 0, i1, i2) - N_EXPERT_GROUPS
    gate = gval * jnp.where(lane == 0, p1, p2)
    eid_ref[...] = eid[:, :TOP_K]
    gate_ref[...] = gate[:, :TOP_K]


def _route(logits):
    t = logits.shape[0]
    tm = 512
    return pl.pallas_call(
        _route_body,
        grid=(t // tm,),
        in_specs=[pl.BlockSpec((tm, LANES), lambda i: (i, 0))],
        out_specs=[pl.BlockSpec((tm, TOP_K), lambda i: (i, 0)),
                   pl.BlockSpec((tm, TOP_K), lambda i: (i, 0))],
        out_shape=[jax.ShapeDtypeStruct((t, TOP_K), jnp.int32),
                   jax.ShapeDtypeStruct((t, TOP_K), F32)],
        compiler_params=_cparams(("arbitrary",), 32),
        name="route",
    )(logits)


def _rank_body(e_ref, rank_ref, cnt_ref, carry_scr):
    i = pl.program_id(0)
    r = e_ref.shape[0]

    @pl.when(i == 0)
    def _():
        carry_scr[...] = jnp.zeros_like(carry_scr)

    lane = lax.broadcasted_iota(jnp.int32, (r, LANES), 1)
    onehot = (lane == e_ref[...]).astype(F32)
    tri = (lax.broadcasted_iota(jnp.int32, (r, r), 1)
           < lax.broadcasted_iota(jnp.int32, (r, r), 0)).astype(BF16)
    before = jnp.dot(tri, onehot.astype(BF16), preferred_element_type=F32) + carry_scr[0:1, :]
    rank_ref[...] = jnp.sum(onehot * before, axis=-1, keepdims=True).astype(jnp.int32)
    total = carry_scr[0:1, :] + jnp.sum(onehot, axis=0, keepdims=True)
    carry_scr[...] = jnp.broadcast_to(total, carry_scr.shape)
    cnt_ref[...] = jnp.broadcast_to(total, cnt_ref.shape).astype(jnp.int32)


def _rank(flat_e):
    a = flat_e.shape[0]
    r = 512
    return pl.pallas_call(
        _rank_body,
        grid=(a // r,),
        in_specs=[pl.BlockSpec((r, 1), lambda i: (i, 0))],
        out_specs=[pl.BlockSpec((r, 1), lambda i: (i, 0)),
                   pl.BlockSpec((8, LANES), lambda i: (0, 0))],
        out_shape=[jax.ShapeDtypeStruct((a, 1), jnp.int32),
                   jax.ShapeDtypeStruct((8, LANES), jnp.int32)],
        scratch_shapes=[pltpu.VMEM((8, LANES), F32)],
        compiler_params=_cparams(("arbitrary",), 32),
        name="rank",
    )(flat_e)


def _dest_body(e_ref, rank_ref, ps_ref, o_ref):
    lane = lax.broadcasted_iota(jnp.int32, (e_ref.shape[0], LANES), 1)
    first = jnp.sum(jnp.where(lane == e_ref[...], ps_ref[...], 0.0), axis=-1, keepdims=True)
    o_ref[...] = first.astype(jnp.int32) + rank_ref[...]


def _dest(flat_e, rank, pstarts_row):
    a = flat_e.shape[0]
    r = 2048
    return pl.pallas_call(
        _dest_body,
        grid=(a // r,),
        in_specs=[pl.BlockSpec((r, 1), lambda i: (i, 0)),
                  pl.BlockSpec((r, 1), lambda i: (i, 0)),
                  pl.BlockSpec((1, LANES), lambda i: (0, 0))],
        out_specs=pl.BlockSpec((r, 1), lambda i: (i, 0)),
        out_shape=jax.ShapeDtypeStruct((a, 1), jnp.int32),
        compiler_params=_cparams(("arbitrary",), 32),
        name="dest",
    )(flat_e, rank, pstarts_row)


DMA_UNROLL = 16


def _experts_body(ord_ref, elist_ref, src_ref, cnt_ref, h_hbm, w1_hbm, w3_hbm, w2_hbm, y_ref,
                  x_scr, w1_scr, w3_scr, w2_scr, sem, wsem):
    i = pl.program_id(0)
    nrows = x_scr.shape[1]
    nused = cnt_ref[0]
    nexp = cnt_ref[1]
    k = ord_ref[i]
    first = (i == 0) | (k != ord_ref[jnp.maximum(i - 1, 0)])

    def weights(kk, act):
        e = elist_ref[kk]
        slot = kk % 2
        act(pltpu.make_async_copy(w1_hbm.at[e], w1_scr.at[slot], wsem.at[slot, 0]))
        act(pltpu.make_async_copy(w3_hbm.at[e], w3_scr.at[slot], wsem.at[slot, 1]))
        act(pltpu.make_async_copy(w2_hbm.at[e], w2_scr.at[slot], wsem.at[slot, 2]))

    def gather(blk, act):
        slot = blk % 2
        base = blk * nrows

        def body(r, carry):
            act(pltpu.make_async_copy(h_hbm.at[pl.ds(src_ref[base + r], 1)],
                                      x_scr.at[slot, pl.ds(r, 1)], sem.at[slot]))
            return carry

        lax.fori_loop(0, nrows, body, 0, unroll=DMA_UNROLL)

    @pl.when(i == 0)
    def _():
        weights(k, lambda cp: cp.start())
        gather(i, lambda cp: cp.start())

    @pl.when(i + 1 < nused)
    def _():
        gather(i + 1, lambda cp: cp.start())

    @pl.when((i < nused) & first & (k + 1 < nexp))
    def _():
        weights(k + 1, lambda cp: cp.start())

    @pl.when((i < nused) & first)
    def _():
        weights(k, lambda cp: cp.wait())

    @pl.when(i < nused)
    def _():
        gather(i, lambda cp: cp.wait())
        wslot = k % 2
        xb = x_scr[i % 2].astype(BF16)
        a = jnp.dot(xb, w1_scr[wslot].astype(BF16), preferred_element_type=F32)
        b = jnp.dot(xb, w3_scr[wslot].astype(BF16), preferred_element_type=F32)
        act = (a * jax.nn.sigmoid(a) * b).astype(BF16)
        y_ref[...] = jnp.dot(act, w2_scr[wslot].astype(BF16), preferred_element_type=F32)

    @pl.when(i >= nused)
    def _():
        y_ref[...] = jnp.zeros_like(y_ref)


def _experts(block_ord, elist, src_tok, counts2, h2, w1, w3, w2):
    nblocks = block_ord.shape[0]
    d = h2.shape[1]
    ff = w1.shape[2]
    rb = EXPERT_ROW_BLOCK
    grid_spec = pltpu.PrefetchScalarGridSpec(
        num_scalar_prefetch=4,
        grid=(nblocks,),
        in_specs=[pl.BlockSpec(memory_space=pl.ANY)] * 4,
        out_specs=pl.BlockSpec((rb, d), lambda i, *_: (i, 0)),
        scratch_shapes=[pltpu.VMEM((2, rb, d), F32),
                        pltpu.VMEM((2, d, ff), F32), pltpu.VMEM((2, d, ff), F32),
                        pltpu.VMEM((2, ff, d), F32),
                        pltpu.SemaphoreType.DMA((2,)), pltpu.SemaphoreType.DMA((2, 3))],
    )
    return pl.pallas_call(
        _experts_body,
        grid_spec=grid_spec,
        out_shape=jax.ShapeDtypeStruct((nblocks * rb, d), F32),
        compiler_params=_cparams(("arbitrary",), 56),
        name="experts",
    )(block_ord, elist, src_tok, counts2, h2, w1, w3, w2)


def _combine_body(dest_ref, y_hbm, gate_ref, x1_ref, mod_ref, g_ref, o_ref, buf_scr, sem):
    i = pl.program_id(0)
    tm = x1_ref.shape[0]

    def gather(step, act):
        slot = step % 2
        base = step * tm * TOP_K

        def body(r, carry):
            for k in range(TOP_K):
                act(pltpu.make_async_copy(y_hbm.at[pl.ds(dest_ref[base + TOP_K * r + k], 1)],
                                          buf_scr.at[slot, k, pl.ds(r, 1)], sem.at[slot]))
            return carry

        lax.fori_loop(0, tm, body, 0, unroll=DMA_UNROLL // TOP_K)

    @pl.when(i == 0)
    def _():
        gather(i, lambda cp: cp.start())

    @pl.when(i + 1 < pl.num_programs(0))
    def _():
        gather(i + 1, lambda cp: cp.start())

    gather(i, lambda cp: cp.wait())
    slot = i % 2
    gate = gate_ref[...]
    y = buf_scr[slot, 0] * gate[:, 0:1] + buf_scr[slot, 1] * gate[:, 1:2]
    o_ref[...] = x1_ref[...] + mod_ref[0, 5:6, :] * _rms(y, g_ref[...])


def _combine(dest, ybuf, gate, x1, mod, g_post, seq):
    t, d = x1.shape
    tm = 256
    per_batch = seq // tm
    grid_spec = pltpu.PrefetchScalarGridSpec(
        num_scalar_prefetch=1,
        grid=(t // tm,),
        in_specs=[pl.BlockSpec(memory_space=pl.ANY),
                  pl.BlockSpec((tm, TOP_K), lambda i, dst: (i, 0)),
                  pl.BlockSpec((tm, d), lambda i, dst: (i, 0)),
                  pl.BlockSpec((1, 6, d), lambda i, dst: (i // per_batch, 0, 0)),
                  pl.BlockSpec((1, d), lambda i, dst: (0, 0))],
        out_specs=pl.BlockSpec((tm, d), lambda i, dst: (i, 0)),
        scratch_shapes=[pltpu.VMEM((2, TOP_K, tm, d), F32), pltpu.SemaphoreType.DMA((2,))],
    )
    return pl.pallas_call(
        _combine_body,
        grid_spec=grid_spec,
        out_shape=jax.ShapeDtypeStruct((t, d), F32),
        compiler_params=_cparams(("arbitrary",), 40),
        name="combine",
    )(dest, ybuf, gate, x1, mod, g_post)


def _layer(x, c, positions, w_ada, b_ada, g_mix_pre, g_mix_post, g_ffn_pre, g_ffn_post,
           w_in, conv_w, conv_b, filt_w1, filt_b1, filt_w2, filt_b2, filt_w3, filt_freq,
           hyena_skip, w_branch_gate, b_branch_gate, w_hy_o, w_at_o, w_out,
           w_group, b_group, w_expert, b_expert, w1_exp, w3_exp, w2_exp):
    bsz, seq, d = x.shape
    t = bsz * seq
    width = hyena_skip.shape[1]
    row = lambda v: v.reshape(1, -1)

    c_pad = jnp.pad(c, ((0, 8 - bsz), (0, 0)))
    mod = _adaln(c_pad, w_ada, row(b_ada))[:bsz].reshape(bsz, 6, d)

    x2d = x.reshape(t, d)
    n_gate = w_branch_gate.shape[1]
    n_cat = -(-(n_gate + w_in.shape[1]) // IN_PROJ_TN) * IN_PROJ_TN
    n_zero = n_cat - n_gate - w_in.shape[1]
    w_cat = jnp.concatenate([w_branch_gate.astype(BF16), w_in.astype(BF16),
                             jnp.zeros((d, n_zero), BF16)], axis=1)
    b_cat = jnp.concatenate([b_branch_gate, jnp.zeros((n_cat - n_gate,), F32)]).reshape(1, -1)
    pg = _in_proj(x2d, mod, row(g_mix_pre), w_cat, b_cat, n_gate, seq)
    pg3 = pg.reshape(bsz, seq, -1)

    hidden = filt_w2.shape[0]
    nfeat = -(-filt_w1.shape[0] // 8) * 8
    bands = np.zeros((nfeat, 1), np.float32)
    band_vals = np.linspace(1e-4, FILTER_BANDS - 1, FILTER_BANDS, dtype=np.float32)
    bands[1:1 + FILTER_BANDS, 0] = band_vals
    bands[1 + FILTER_BANDS:1 + 2 * FILTER_BANDS, 0] = band_vals
    w1p = jnp.pad(filt_w1, ((0, nfeat - filt_w1.shape[0]), (0, 0))).T
    col = lambda v: v.reshape(-1, 1)
    max_decay = math.log(DECAY_TARGET) / FAST_DECAY_PCT
    min_decay = math.log(DECAY_TARGET) / SLOW_DECAY_PCT
    deltas = jnp.abs(jnp.linspace(min_decay, max_decay, width, dtype=F32)).reshape(1, -1)
    uw = _filters(seq, width, jnp.asarray(bands), w1p, col(filt_b1), filt_w2.T, col(filt_b2),
                  col(filt_freq), deltas, filt_w3)
    s1, s1i, s2, s2i = _fft_tables(FFT_N1, 2 * seq // FFT_N1)
    kspec = _spectra(uw, s1, s2)
    y_hy = _hyena(pg3, n_gate, conv_w, row(conv_b), hyena_skip, kspec, s1, s1i, s2, s2i, width)

    half = ROT_DIM // 2
    inv_freq = np.power(ROPE_THETA, -2.0 * np.arange(half, dtype=np.float32) / ROT_DIM).astype(np.float32)
    freq_row = np.zeros((1, LANES), np.float32)
    freq_row[0, :half] = inv_freq
    freq_row[0, half:ROT_DIM] = inv_freq
    sign_row = np.zeros((1, LANES), np.float32)
    sign_row[0, :half] = -1.0
    sign_row[0, half:ROT_DIM] = 1.0
    y_at = _attention(pg3, positions.reshape(bsz, seq, 1), jnp.asarray(freq_row), jnp.asarray(sign_row),
                      n_gate + 3 * width)

    w_r = jnp.concatenate([w_group, jnp.transpose(w_expert, (1, 0, 2)).reshape(d, N_EXPERTS)], axis=1)
    w_r = jnp.pad(w_r, ((0, 0), (0, LANES - w_r.shape[1])))
    w_r_hi = w_r.astype(BF16)
    w_r_lo = (w_r - w_r_hi.astype(F32)).astype(BF16)
    b_r = jnp.pad(jnp.concatenate([b_group, b_expert.reshape(-1)]), (0, LANES - N_EXPERT_GROUPS - N_EXPERTS))
    x1, h2, logits = _mix_out(y_hy.reshape(t, width), y_at.reshape(t, -1), pg, x2d, mod,
                              row(g_mix_post), row(g_ffn_pre), w_hy_o.astype(BF16), w_at_o.astype(BF16),
                              w_out.astype(BF16), w_r_hi, w_r_lo, b_r.reshape(1, -1), seq)

    eid, gate = _route(logits)
    flat_e = eid.reshape(t * TOP_K, 1)
    rank, counts = _rank(flat_e)
    counts = counts[0, :N_EXPERTS]
    rb = EXPERT_ROW_BLOCK
    padded = (counts + rb - 1) // rb * rb
    pends = jnp.cumsum(padded)
    pstarts = pends - padded
    pstarts_row = jnp.pad(pstarts.astype(F32), (0, LANES - N_EXPERTS)).reshape(1, LANES)
    dest = _dest(flat_e, rank, pstarts_row)[:, 0]
    n_blocks = t * TOP_K // rb + N_EXPERTS
    block_start = jnp.arange(n_blocks, dtype=jnp.int32) * rb
    block_e = jnp.minimum(jnp.searchsorted(pends, block_start, side='right'), N_EXPERTS - 1).astype(jnp.int32)
    tok = jnp.arange(t * TOP_K, dtype=jnp.int32) // TOP_K
    src_tok = (jnp.arange(n_blocks * rb, dtype=jnp.int32) % t).at[dest].set(tok)
    has_rows = jnp.cumsum((counts > 0).astype(jnp.int32))
    elist = jnp.minimum(jnp.searchsorted(has_rows, jnp.arange(1, N_EXPERTS + 1, dtype=jnp.int32), side='left'),
                        N_EXPERTS - 1).astype(jnp.int32)
    block_ord = (has_rows - 1)[block_e].astype(jnp.int32)
    counts2 = jnp.stack([pends[-1] // rb, has_rows[-1]]).astype(jnp.int32)

    ybuf = _experts(block_ord, elist, src_tok, counts2, h2, w1_exp, w3_exp, w2_exp)
    out = _combine(dest, ybuf, gate, x1, mod, row(g_ffn_post), seq)
    return out.reshape(bsz, seq, d)


def kernel(x, c, positions, w_ada, b_ada, g_mix_pre, g_mix_post, g_ffn_pre, g_ffn_post, w_in, conv_w, conv_b, filt_w1, filt_b1, filt_w2, filt_b2, filt_w3, filt_freq, hyena_skip, w_branch_gate, b_branch_gate, w_hy_o, w_at_o, w_out, w_group, b_group, w_expert, b_expert, w1_exp, w3_exp, w2_exp):
    depth = w_ada.shape[0]
    for l in range(depth):
        x = _layer(x, c, positions, w_ada[l], b_ada[l], g_mix_pre[l], g_mix_post[l], g_ffn_pre[l],
                   g_ffn_post[l], w_in[l], conv_w[l], conv_b[l], filt_w1[l], filt_b1[l], filt_w2[l],
                   filt_b2[l], filt_w3[l], filt_freq[l], hyena_skip[l], w_branch_gate[l],
                   b_branch_gate[l], w_hy_o[l], w_at_o[l], w_out[l], w_group[l], b_group[l],
                   w_expert[l], b_expert[l], w1_exp[l], w3_exp[l], w2_exp[l])
    return x
```

```python
import functools
import math

import numpy as np
import jax
import jax.numpy as jnp
from jax import lax
from jax.experimental import pallas as pl
from jax.experimental.pallas import tpu as pltpu

F32 = jnp.float32
BF16 = jnp.bfloat16

LANES = 128
MIB = 1024 * 1024

RMS_EPS = 1e-6
NEG_INF = -1e30

HEAD_DIM = 128
ROT_DIM = HEAD_DIM // 4
ROPE_THETA = 500000.0
ATTN_GROUPS = ((128, 1), (512, 4), (2048, 16))
HEADS_PER_GROUP = 4
N_ATTN_HEADS = HEADS_PER_GROUP * len(ATTN_GROUPS)

FILTER_BANDS = 16
DECAY_TARGET = 1e-2
FAST_DECAY_PCT = 0.3
SLOW_DECAY_PCT = 1.5

N_EXPERT_GROUPS = 8
EXPERTS_PER_GROUP = 8
N_EXPERTS = N_EXPERT_GROUPS * EXPERTS_PER_GROUP
TOP_K = 2
EXPERT_ROW_BLOCK = 128

FFT_N1 = 128
PITCH = FFT_N1 + 8
FFT_UNROLL = 16
FFT_MID_UNROLL = 8


def _cparams(sem, vmem_mib):
    return pltpu.CompilerParams(dimension_semantics=sem, vmem_limit_bytes=vmem_mib * MIB)


def _rms(x, g):
    return x * lax.rsqrt(jnp.mean(x * x, axis=-1, keepdims=True) + RMS_EPS) * g


def _adaln_body(c_ref, w_ref, b_ref, o_ref):
    c = c_ref[...]
    cond = c * jax.nn.sigmoid(c)
    o_ref[...] = jnp.dot(cond.astype(BF16), w_ref[...].astype(BF16),
                         preferred_element_type=F32) + b_ref[...]


def _adaln(c_pad, w_ada, b_ada):
    rows, d = c_pad.shape
    n = w_ada.shape[1]
    tn = 1024
    return pl.pallas_call(
        _adaln_body,
        grid=(n // tn,),
        in_specs=[pl.BlockSpec((rows, d), lambda j: (0, 0)),
                  pl.BlockSpec((d, tn), lambda j: (0, j)),
                  pl.BlockSpec((1, tn), lambda j: (0, j))],
        out_specs=pl.BlockSpec((rows, tn), lambda j: (0, j)),
        out_shape=jax.ShapeDtypeStruct((rows, n), F32),
        compiler_params=_cparams(("arbitrary",), 40),
        name="adaln",
    )(c_pad, w_ada, b_ada)


IN_PROJ_TM = 1024
IN_PROJ_TN = 1024
IN_PROJ_SUB_M = 256
IN_PROJ_SUB_N = 512


def _in_proj_body(n_gate, last_cols, x_ref, mod_ref, g_ref, wg_ref, wi_ref, b_ref, o_ref, h_scr):
    j = pl.program_id(1)
    last = pl.num_programs(1) - 1

    @pl.when(j == 0)
    def _():
        x = x_ref[...]
        h = _rms(x, g_ref[...]) * (1.0 + mod_ref[0, 1:2, :]) + mod_ref[0, 0:1, :]
        h_scr[...] = h.astype(BF16)

    tm, tn = o_ref.shape

    def tile(w_ref, gate, ncols):
        for mi in range(tm // IN_PROJ_SUB_M):
            rows = slice(mi * IN_PROJ_SUB_M, (mi + 1) * IN_PROJ_SUB_M)
            for ni in range(tn // IN_PROJ_SUB_N):
                cols = slice(ni * IN_PROJ_SUB_N, (ni + 1) * IN_PROJ_SUB_N)
                if ni * IN_PROJ_SUB_N >= ncols:
                    o_ref[rows, cols] = jnp.zeros((IN_PROJ_SUB_M, IN_PROJ_SUB_N), o_ref.dtype)
                    continue
                acc = jnp.dot(h_scr[rows, :], w_ref[:, cols], preferred_element_type=F32) + b_ref[:, cols]
                o_ref[rows, cols] = (jax.nn.sigmoid(acc) if gate else acc).astype(o_ref.dtype)

    @pl.when(j < n_gate)
    def _():
        tile(wg_ref, True, tn)

    @pl.when((j >= n_gate) & (j < last))
    def _():
        tile(wi_ref, False, tn)

    @pl.when(j == last)
    def _():
        tile(wi_ref, False, last_cols)


def _in_proj(x2d, mod, g_pre, w_gate, w_in, b_cat, seq):
    t, d = x2d.shape
    tm, tn = IN_PROJ_TM, IN_PROJ_TN
    ng = w_gate.shape[1] // tn
    ni = pl.cdiv(w_in.shape[1], tn)
    last_cols = w_in.shape[1] - (ni - 1) * tn
    assert w_gate.shape[1] % tn == 0 and last_cols % IN_PROJ_SUB_N == 0
    per_batch = seq // tm
    return pl.pallas_call(
        functools.partial(_in_proj_body, ng, last_cols),
        grid=(t // tm, ng + ni),
        in_specs=[pl.BlockSpec((tm, d), lambda i, j: (i, 0)),
                  pl.BlockSpec((1, 6, d), lambda i, j: (i // per_batch, 0, 0)),
                  pl.BlockSpec((1, d), lambda i, j: (0, 0)),
                  pl.BlockSpec((d, tn), lambda i, j: (0, jnp.minimum(j, ng - 1))),
                  pl.BlockSpec((d, tn), lambda i, j: (0, jnp.clip(j - ng, 0, ni - 1))),
                  pl.BlockSpec((1, tn), lambda i, j: (0, j))],
        out_specs=pl.BlockSpec((tm, tn), lambda i, j: (i, j)),
        out_shape=jax.ShapeDtypeStruct((t, (ng + ni) * tn), BF16),
        scratch_shapes=[pltpu.VMEM((tm, d), BF16)],
        compiler_params=_cparams(("arbitrary", "arbitrary"), 56),
        name="in_proj",
    )(x2d, mod, g_pre, w_gate, w_in, b_cat)


def _filters_body(seq, band_ref, w1_ref, b1_ref, w2_ref, b2_ref, fr_ref, dl_ref,
                  w3a_ref, w3b_ref, o_ref, hid_scr):
    i = pl.program_id(0)
    j = pl.program_id(1)
    tl = hid_scr.shape[0]
    row = (lax.broadcasted_iota(jnp.int32, (tl, 1), 0) + i * tl).astype(F32)

    @pl.when((j == 0) & (pl.program_id(2) == 0))
    def _():
        nfeat = band_ref.shape[0]
        pos = (lax.broadcasted_iota(jnp.int32, (1, tl), 1) + i * tl).astype(F32)
        feat = lax.broadcasted_iota(jnp.int32, (nfeat, tl), 0)
        ang = band_ref[...] * (2.0 * math.pi * pos / seq)
        feats = jnp.where(feat == 0, pos / (seq - 1.0),
                          jnp.where(feat <= FILTER_BANDS, jnp.cos(ang),
                                    jnp.where(feat <= 2 * FILTER_BANDS, -jnp.sin(ang), 0.0)))
        hi = lax.Precision.HIGHEST
        fr = fr_ref[...]
        hid = jnp.sin(fr * (jnp.dot(w1_ref[...], feats, precision=hi, preferred_element_type=F32)
                            + b1_ref[...]))
        hid = jnp.sin(fr * (jnp.dot(w2_ref[...], hid, precision=hi, preferred_element_type=F32)
                            + b2_ref[...]))
        hid_scr[...] = hid.T

    hi = lax.Precision.HIGHEST
    hid = hid_scr[...]
    decay = jnp.exp(-(row / (seq - 1.0)) * dl_ref[...])
    hf = jnp.dot(hid, w3a_ref[...], precision=hi, preferred_element_type=F32) * decay
    hb = jnp.dot(hid, w3b_ref[...], precision=hi, preferred_element_type=F32) * decay
    hb = jnp.where(row == 0.0, 0.0, hb)
    o_ref[0, 0] = hf + hb
    o_ref[0, 1] = hf - hb


def _filters(seq, width, bands, w1p, b1, w2, b2, freq, deltas, w3):
    tl, tc = 512, 512
    nct = width // tc
    hidden = w2.shape[0]
    nfeat = bands.shape[0]
    const = lambda i, j, o: (0, 0)
    return pl.pallas_call(
        functools.partial(_filters_body, float(seq)),
        grid=(seq // tl, nct, 2),
        in_specs=[pl.BlockSpec((nfeat, 1), const),
                  pl.BlockSpec((hidden, nfeat), const),
                  pl.BlockSpec((hidden, 1), const),
                  pl.BlockSpec((hidden, hidden), const),
                  pl.BlockSpec((hidden, 1), const),
                  pl.BlockSpec((hidden, 1), const),
                  pl.BlockSpec((1, tc), lambda i, j, o: (0, j)),
                  pl.BlockSpec((hidden, tc), lambda i, j, o: (0, (2 * o) * nct + j)),
                  pl.BlockSpec((hidden, tc), lambda i, j, o: (0, (2 * o + 1) * nct + j))],
        out_specs=pl.BlockSpec((1, 2, tl, tc), lambda i, j, o: (o, 0, i, j)),
        out_shape=jax.ShapeDtypeStruct((2, 2, seq, width), F32),
        scratch_shapes=[pltpu.VMEM((tl, hidden), F32)],
        compiler_params=_cparams(("arbitrary", "arbitrary", "arbitrary"), 32),
        name="filters",
    )(bands, w1p, b1, w2, b2, freq, deltas, w3, w3)


def _fft_tables(n1, n2):
    n = n1 * n2
    q = np.arange(n2)[:, None]
    b = np.arange(n2 // 2)[None, :]
    a = np.arange(n1)[:, None, None]
    ang = -2.0 * np.pi * (a * q[None] / n + (q * b)[None] / n2)
    stage1 = np.concatenate([np.cos(ang), np.sin(ang)], axis=1)
    stage1_inv = np.transpose(stage1, (0, 2, 1)) / n
    p = np.arange(n1)
    ang2 = -2.0 * np.pi * np.outer(p, p) / n1
    fre, fim = np.cos(ang2), np.sin(ang2)
    stage2 = np.block([[fre, -fim], [fim, fre]])
    stage2_inv = np.block([[fre, fim], [-fim, fre]])
    as_bf16 = lambda m: jnp.asarray(m, dtype=F32).astype(BF16)
    return as_bf16(stage1), as_bf16(stage1_inv), as_bf16(stage2), as_bf16(stage2_inv)


def _halves(ref, rows):
    return jnp.concatenate([ref[0, rows, :], ref[1, rows, :]], axis=1)


def _fft_stage1(z_ref, s1_ref, gre_ref, gim_ref):
    n1, two_n2, n2h = s1_ref.shape
    n2 = two_n2 // 2

    def step(a, carry):
        zrows = _halves(z_ref, pl.ds(a, n2h, stride=PITCH))
        g = jnp.dot(s1_ref[a], zrows.astype(BF16), preferred_element_type=F32)
        for h in range(2):
            cols = slice(h * LANES, (h + 1) * LANES)
            gre_ref[h, pl.ds(a, n2, stride=PITCH), :] = g[:n2, cols]
            gim_ref[h, pl.ds(a, n2, stride=PITCH), :] = g[n2:, cols]
        return carry

    lax.fori_loop(0, n1, step, 0, unroll=FFT_UNROLL)


def _fft_stage1_inv(gre_ref, gim_ref, s1i_ref, y_ref):
    n1, n2h, two_n2 = s1i_ref.shape
    n2 = two_n2 // 2

    def step(a, carry):
        rows = pl.ds(a, n2, stride=PITCH)
        hs = jnp.concatenate([_halves(gre_ref, rows), _halves(gim_ref, rows)], axis=0)
        y = jnp.dot(s1i_ref[a], hs.astype(BF16), preferred_element_type=F32)
        for h in range(2):
            y_ref[h, pl.ds(a, n2h, stride=PITCH), :] = y[:, h * LANES:(h + 1) * LANES]
        return carry

    lax.fori_loop(0, n1, step, 0, unroll=FFT_UNROLL)


def _stage2_block(gre_ref, gim_ref, s2_ref, q):
    n1 = s2_ref.shape[0] // 2
    rows = pl.ds(pl.multiple_of(q * PITCH, 8), n1)
    gs = jnp.concatenate([_halves(gre_ref, rows), _halves(gim_ref, rows)], axis=0)
    x = jnp.dot(s2_ref[...], gs.astype(BF16), preferred_element_type=F32)
    return x[:n1], x[n1:]


def _spectra_body(uw_ref, s1_ref, s2_ref, k_ref, z_scr, gre_scr, gim_scr):
    n1 = s2_ref.shape[0] // 2
    n2 = s1_ref.shape[1] // 2
    for h in range(2):
        for b in range(n2 // 2):
            z_scr[h, pl.ds(b * PITCH, n1), :] = uw_ref[0, h, pl.ds(b * n1, n1), :]
    _fft_stage1(z_scr, s1_ref, gre_scr, gim_scr)

    def step(q, carry):
        xre, xim = _stage2_block(gre_scr, gim_scr, s2_ref, q)
        rows = pl.ds(pl.multiple_of(q * n1, 8), n1)
        k_ref[0, 0, rows, :] = xre[:, :LANES]
        k_ref[0, 1, rows, :] = xim[:, LANES:]
        return carry

    lax.fori_loop(0, n2, step, 0, unroll=FFT_MID_UNROLL)


def _spectra(uw, s1, s2):
    _, _, seq, width = uw.shape
    n1 = FFT_N1
    n2 = 2 * seq // n1
    const3 = lambda j, o: (0, 0, 0)
    return pl.pallas_call(
        _spectra_body,
        grid=(width // LANES, 2),
        in_specs=[pl.BlockSpec((1, 2, seq, LANES), lambda j, o: (o, 0, 0, j)),
                  pl.BlockSpec(s1.shape, const3),
                  pl.BlockSpec(s2.shape, lambda j, o: (0, 0))],
        out_specs=pl.BlockSpec((1, 2, 2 * seq, LANES), lambda j, o: (o, 0, 0, j)),
        out_shape=jax.ShapeDtypeStruct((2, 2, 2 * seq, width), F32),
        scratch_shapes=[pltpu.VMEM((2, (n2 // 2) * PITCH, LANES), F32),
                        pltpu.VMEM((2, n2 * PITCH, LANES), F32),
                        pltpu.VMEM((2, n2 * PITCH, LANES), F32)],
        compiler_params=_cparams(("arbitrary", "arbitrary"), 56),
        name="spectra",
    )(uw, s1, s2)


def _short_conv_block(p_ref, b, i, nblk, w_ref, bias_ref):
    n1 = FFT_N1
    start = pl.multiple_of(i * n1, n1)
    cur = p_ref[b, pl.ds(start, n1), :].astype(F32)
    before = p_ref[b, pl.ds(pl.multiple_of(jnp.maximum(start - 16, 0), 16), 16), :].astype(F32)
    after = p_ref[b, pl.ds(pl.multiple_of(jnp.minimum(start + n1, (nblk - 1) * n1), 16), 16), :].astype(F32)
    last_prev = before[15:16] * jnp.where(i > 0, 1.0, 0.0).astype(F32)
    first_next = after[0:1] * jnp.where(i < nblk - 1, 1.0, 0.0).astype(F32)
    row = lax.broadcasted_iota(jnp.int32, (n1, 1), 0)
    prev = jnp.where(row == 0, last_prev, pltpu.roll(cur, 1, 0))
    nxt = jnp.where(row == n1 - 1, first_next, pltpu.roll(cur, n1 - 1, 0))
    return prev * w_ref[0:1, :] + cur * w_ref[1:2, :] + nxt * w_ref[2:3, :] + bias_ref[...]


def _hyena_body(pv_ref, px1_ref, px2_ref, cwv_ref, cw1_ref, cw2_ref, cbv_ref, cb1_ref, cb2_ref,
                skip_ref, k_ref, s1_ref, s1i_ref, s2_ref, s2i_ref, o_ref,
                z_scr, y_scr, gre_scr, gim_scr):
    c = pl.program_id(1)
    g = pl.program_id(2)
    ngroups = pl.num_programs(2)
    n1 = FFT_N1
    n2 = s1_ref.shape[1] // 2
    nblk = n2 // 2
    per_group = n2 // ngroups

    @pl.when(g == 0)
    def _():
        @pl.when(c == 0)
        def _():
            def fill(i, carry):
                for b in range(2):
                    z_scr[b, pl.ds(pl.multiple_of(i * PITCH, 8), n1), :] = _short_conv_block(
                        pv_ref, b, i, nblk, cwv_ref, cbv_ref)
                return carry
            lax.fori_loop(0, nblk, fill, 0)

        _fft_stage1(z_scr, s1_ref, gre_scr, gim_scr)

    def mid(ql, carry):
        q = g * per_group + ql
        xre, xim = _stage2_block(gre_scr, gim_scr, s2_ref, q)
        krows = pl.ds(pl.multiple_of(ql * n1, 8), n1)
        kre = k_ref[0, 0, krows, :]
        kim = k_ref[0, 1, krows, :]
        kre = jnp.concatenate([kre, kre], axis=1)
        kim = jnp.concatenate([kim, kim], axis=1)
        ys = jnp.concatenate([xre * kre - xim * kim, xre * kim + xim * kre], axis=0)
        hh = jnp.dot(s2i_ref[...], ys.astype(BF16), preferred_element_type=F32)
        rows = pl.ds(pl.multiple_of(q * PITCH, 8), n1)
        for b in range(2):
            cols = slice(b * LANES, (b + 1) * LANES)
            gre_scr[b, rows, :] = hh[:n1, cols]
            gim_scr[b, rows, :] = hh[n1:, cols]
        return carry

    lax.fori_loop(0, per_group, mid, 0, unroll=FFT_MID_UNROLL)

    @pl.when(g == ngroups - 1)
    def _():
        _fft_stage1_inv(gre_scr, gim_scr, s1i_ref, y_scr)

        def post(px_ref, cw_ref, cb_ref, order, store):
            def blk(i, carry):
                rows = pl.ds(pl.multiple_of(i * PITCH, 8), n1)
                for b in range(2):
                    zb = z_scr[b, rows, :]
                    gate = _short_conv_block(px_ref, b, i, nblk, cw_ref, cb_ref)
                    store(b, i, rows, gate * (y_scr[b, rows, :] + zb * skip_ref[order:order + 1, :]))
                return carry
            lax.fori_loop(0, nblk, blk, 0)

        @pl.when(c == 0)
        def _():
            def store(b, i, rows, val):
                z_scr[b, rows, :] = val
            post(px1_ref, cw1_ref, cb1_ref, 0, store)

        @pl.when(c == 1)
        def _():
            def store(b, i, rows, val):
                o_ref[b, pl.ds(pl.multiple_of(i * n1, n1), n1), :] = val.astype(o_ref.dtype)
            post(px2_ref, cw2_ref, cb2_ref, 1, store)


def _hyena(pg3, proj_col0, conv_w, conv_b, skip, kspec, s1, s1i, s2, s2i, width):
    bsz, seq, _ = pg3.shape
    assert bsz == 2
    n1 = FFT_N1
    n2 = 2 * seq // n1
    nct = width // LANES
    ngroups = 4
    krows = (n2 // ngroups) * n1
    col = lambda off: (lambda j, c, g: (0, 0, proj_col0 // LANES + off * nct + j))
    cw = lambda off: (lambda j, c, g: (0, off * nct + j))
    const3 = lambda j, c, g: (0, 0, 0)
    const2 = lambda j, c, g: (0, 0)
    return pl.pallas_call(
        _hyena_body,
        grid=(nct, 2, ngroups),
        in_specs=[pl.BlockSpec((2, seq, LANES), col(0)),
                  pl.BlockSpec((2, seq, LANES), col(1)),
                  pl.BlockSpec((2, seq, LANES), col(2)),
                  pl.BlockSpec((3, LANES), cw(0)),
                  pl.BlockSpec((3, LANES), cw(1)),
                  pl.BlockSpec((3, LANES), cw(2)),
                  pl.BlockSpec((1, LANES), cw(0)),
                  pl.BlockSpec((1, LANES), cw(1)),
                  pl.BlockSpec((1, LANES), cw(2)),
                  pl.BlockSpec((2, LANES), lambda j, c, g: (0, j)),
                  pl.BlockSpec((1, 2, krows, LANES), lambda j, c, g: (c, 0, g, j)),
                  pl.BlockSpec(s1.shape, const3),
                  pl.BlockSpec(s1i.shape, const3),
                  pl.BlockSpec(s2.shape, const2),
                  pl.BlockSpec(s2i.shape, const2)],
        out_specs=pl.BlockSpec((2, seq, LANES), lambda j, c, g: (0, 0, j)),
        out_shape=jax.ShapeDtypeStruct((2, seq, width), BF16),
        scratch_shapes=[pltpu.VMEM((2, (n2 // 2) * PITCH, LANES), F32),
                        pltpu.VMEM((2, (n2 // 2) * PITCH, LANES), F32),
                        pltpu.VMEM((2, n2 * PITCH, LANES), F32),
                        pltpu.VMEM((2, n2 * PITCH, LANES), F32)],
        compiler_params=_cparams(("arbitrary", "arbitrary", "arbitrary"), 56),
        name="hyena",
    )(pg3, pg3, pg3, conv_w, conv_w, conv_w, conv_b, conv_b, conv_b, skip, kspec, s1, s1i, s2, s2i)


ATTN_QBLK = 128
ATTN_UNROLL = 8


def _attention_body(pos_ref, freq_ref, sign_ref, *refs):
    qkv_refs = refs[:9]
    o_ref = refs[9]
    cos_scr, sin_scr, q_scr, k_scr, v_scr, og_scr, lse_scr = refs[10:]
    seq = q_scr.shape[0]
    chunk = 512
    nchunks = seq // chunk

    @pl.when(pl.program_id(1) == 0)
    def _():
        def trig(i, carry):
            rows = pl.ds(pl.multiple_of(i * chunk, chunk), chunk)
            ang = pos_ref[0, rows, :].astype(F32) * freq_ref[...]
            cos_scr[rows, :] = jnp.cos(ang)
            sin_scr[rows, :] = jnp.sin(ang) * sign_ref[...]
            return carry
        lax.fori_loop(0, nchunks, trig, 0)

    def rotate(src_ref, dst_ref):
        def body(i, carry):
            rows = pl.ds(pl.multiple_of(i * chunk, chunk), chunk)
            t = src_ref[0, rows, :].astype(F32)
            lane = lax.broadcasted_iota(jnp.int32, t.shape, 1)
            partner = jnp.where(lane < ROT_DIM // 2,
                                pltpu.roll(t, LANES - ROT_DIM // 2, 1), pltpu.roll(t, ROT_DIM // 2, 1))
            dst_ref[rows, :] = t * cos_scr[rows, :] + partner * sin_scr[rows, :]
            return carry
        lax.fori_loop(0, nchunks, body, 0, unroll=2)

    def widen(src_ref, dst_ref):
        def body(i, carry):
            rows = pl.ds(pl.multiple_of(i * chunk, chunk), chunk)
            dst_ref[rows, :] = src_ref[0, rows, :].astype(F32)
            return carry
        lax.fori_loop(0, nchunks, body, 0, unroll=2)

    scale = HEAD_DIM ** -0.5
    for gi, (window, dil) in enumerate(ATTN_GROUPS):
        rotate(qkv_refs[3 * gi], q_scr)
        rotate(qkv_refs[3 * gi + 1], k_scr)
        widen(qkv_refs[3 * gi + 2], v_scr)
        n = seq // dil
        half = window // (2 * dil)
        tk = min(n, ATTN_QBLK + 2 * half)
        blocks_per_res = n // ATTN_QBLK

        def block(u, carry, dil=dil, n=n, half=half, tk=tk, blocks_per_res=blocks_per_res, gi=gi):
            r = u // blocks_per_res
            m = u % blocks_per_res
            q0 = m * ATTN_QBLK
            k0 = jnp.clip(q0 - half, 0, n - tk)
            qrows = pl.ds(r + dil * q0, ATTN_QBLK, stride=dil)
            krows = pl.ds(r + dil * k0, tk, stride=dil)
            qb = q_scr[qrows, :].astype(BF16)
            kb = k_scr[krows, :].astype(BF16)
            vb = v_scr[krows, :].astype(BF16)
            s = lax.dot_general(qb, kb, (((1,), (1,)), ((), ())), preferred_element_type=F32) * scale
            qi = q0 + lax.broadcasted_iota(jnp.int32, (ATTN_QBLK, tk), 0)
            kj = k0 + lax.broadcasted_iota(jnp.int32, (ATTN_QBLK, tk), 1)
            s = jnp.where(jnp.abs(qi - kj) <= half, s, NEG_INF)
            mx = jnp.max(s, axis=-1, keepdims=True)
            p = jnp.exp(s - mx)
            l = jnp.sum(p, axis=-1, keepdims=True)
            o = jnp.dot(p.astype(BF16), vb, preferred_element_type=F32) / l
            og_scr[gi, qrows, :] = o
            lse_scr[gi, qrows, :] = jnp.broadcast_to(mx + jnp.log(l), (ATTN_QBLK, LANES))
            return carry

        lax.fori_loop(0, seq // ATTN_QBLK, block, 0, unroll=ATTN_UNROLL)

    def merge(i, carry):
        rows = pl.ds(pl.multiple_of(i * chunk, chunk), chunk)
        lses = [lse_scr[gi, rows, :] for gi in range(len(ATTN_GROUPS))]
        mx = functools.reduce(jnp.maximum, lses)
        ws = [jnp.exp(v - mx) for v in lses]
        den = functools.reduce(lambda a, b: a + b, ws)
        num = functools.reduce(lambda a, b: a + b,
                               [w * og_scr[gi, rows, :] for gi, w in enumerate(ws)])
        o_ref[0, rows, :] = (num / den).astype(o_ref.dtype)
        return carry

    lax.fori_loop(0, nchunks, merge, 0)


def _attention(pg3, pos3, freq_row, sign_row, qkv_col0):
    bsz, seq, _ = pg3.shape
    ng = len(ATTN_GROUPS)
    in_specs = [pl.BlockSpec((1, seq, 1), lambda b, h: (b, 0, 0)),
                pl.BlockSpec((1, LANES), lambda b, h: (0, 0)),
                pl.BlockSpec((1, LANES), lambda b, h: (0, 0))]
    for gi in range(ng):
        for which in range(3):
            base = qkv_col0 // LANES + which * N_ATTN_HEADS + gi * HEADS_PER_GROUP
            in_specs.append(pl.BlockSpec((1, seq, LANES), lambda b, h, base=base: (b, 0, base + h)))
    return pl.pallas_call(
        _attention_body,
        grid=(bsz, HEADS_PER_GROUP),
        in_specs=in_specs,
        out_specs=pl.BlockSpec((1, seq, LANES), lambda b, h: (b, 0, h)),
        out_shape=jax.ShapeDtypeStruct((bsz, seq, HEADS_PER_GROUP * HEAD_DIM), BF16),
        scratch_shapes=[pltpu.VMEM((seq, LANES), F32), pltpu.VMEM((seq, LANES), F32),
                        pltpu.VMEM((seq, LANES), F32), pltpu.VMEM((seq, LANES), F32),
                        pltpu.VMEM((seq, LANES), F32),
                        pltpu.VMEM((ng, seq, LANES), F32), pltpu.VMEM((ng, seq, LANES), F32)],
        compiler_params=_cparams(("arbitrary", "arbitrary"), 56),
        name="attention",
    )(pos3, freq_row, sign_row, *([pg3] * (3 * ng)))


def _mix_out_body(yhy_ref, yat_ref, ghy_ref, gat_ref, x_ref, mod_ref, gpost_ref, gpre_ref,
                  whyo_ref, wato_ref, wout_ref, wrhi_ref, wrlo_ref, br_ref, x1_ref, h2_ref, lg_ref):
    a = jnp.dot(yhy_ref[...], whyo_ref[...], preferred_element_type=F32)
    b = jnp.dot(yat_ref[...], wato_ref[...], preferred_element_type=F32)
    merged = ghy_ref[...].astype(F32) * a + gat_ref[...].astype(F32) * b
    y = jnp.dot(merged.astype(BF16), wout_ref[...], preferred_element_type=F32)
    x1 = x_ref[...] + mod_ref[0, 2:3, :] * _rms(y, gpost_ref[...])
    x1_ref[...] = x1
    h2 = _rms(x1, gpre_ref[...]) * (1.0 + mod_ref[0, 4:5, :]) + mod_ref[0, 3:4, :]
    h2_ref[...] = h2
    h2_hi = h2.astype(BF16)
    h2_lo = (h2 - h2_hi.astype(F32)).astype(BF16)
    w_hi = wrhi_ref[...]
    w_lo = wrlo_ref[...]
    lg_ref[...] = (jnp.dot(h2_hi, w_hi, preferred_element_type=F32)
                   + jnp.dot(h2_lo, w_hi, preferred_element_type=F32)
                   + jnp.dot(h2_hi, w_lo, preferred_element_type=F32)
                   + jnp.dot(h2_lo, w_lo, preferred_element_type=F32)) + br_ref[...]


def _mix_out(y_hy, y_at, pg, x2d, mod, g_post, g_pre, w_hy_o, w_at_o, w_out, w_r_hi, w_r_lo, b_r, seq):
    t, d = x2d.shape
    tm = 256
    per_batch = seq // tm
    gblk = 0
    const = lambda i: (0, 0)
    return pl.pallas_call(
        _mix_out_body,
        grid=(t // tm,),
        in_specs=[pl.BlockSpec((tm, y_hy.shape[1]), lambda i: (i, 0)),
                  pl.BlockSpec((tm, y_at.shape[1]), lambda i: (i, 0)),
                  pl.BlockSpec((tm, d), lambda i: (i, gblk)),
                  pl.BlockSpec((tm, d), lambda i: (i, gblk + 1)),
                  pl.BlockSpec((tm, d), lambda i: (i, 0)),
                  pl.BlockSpec((1, 6, d), lambda i: (i // per_batch, 0, 0)),
                  pl.BlockSpec((1, d), const),
                  pl.BlockSpec((1, d), const),
                  pl.BlockSpec(w_hy_o.shape, const),
                  pl.BlockSpec(w_at_o.shape, const),
                  pl.BlockSpec(w_out.shape, const),
                  pl.BlockSpec(w_r_hi.shape, const),
                  pl.BlockSpec(w_r_lo.shape, const),
                  pl.BlockSpec((1, LANES), const)],
        out_specs=[pl.BlockSpec((tm, d), lambda i: (i, 0)),
                   pl.BlockSpec((tm, d), lambda i: (i, 0)),
                   pl.BlockSpec((tm, LANES), lambda i: (i, 0))],
        out_shape=[jax.ShapeDtypeStruct((t, d), F32),
                   jax.ShapeDtypeStruct((t, d), F32),
                   jax.ShapeDtypeStruct((t, LANES), F32)],
        compiler_params=_cparams(("arbitrary",), 56),
        name="mix_out",
    )(y_hy, y_at, pg, pg, x2d, mod, g_post, g_pre, w_hy_o, w_at_o, w_out, w_r_hi, w_r_lo, b_r)


def _route_body(lg_ref, eid_ref, gate_ref):
    lg = lg_ref[...]
    lane = lax.broadcasted_iota(jnp.int32, lg.shape, 1)
    big = jnp.int32(1 << 20)

    def first_argmax(vals, mask):
        v = jnp.where(mask, vals, -jnp.inf)
        mx = jnp.max(v, axis=-1, keepdims=True)
        idx = jnp.min(jnp.where(mask & (v == mx), lane, big), axis=-1, keepdims=True)
        return mx, idx

    gmask = lane < N_EXPERT_GROUPS
    gmax, gidx = first_argmax(lg, gmask)
    gval = 1.0 / jnp.sum(jnp.where(gmask, jnp.exp(lg - gmax), 0.0), axis=-1, keepdims=True)
    lo = N_EXPERT_GROUPS + gidx * EXPERTS_PER_GROUP
    emask = (lane >= lo) & (lane < lo + EXPERTS_PER_GROUP)
    v1, i1 = first_argmax(lg, emask)
    v2, i2 = first_argmax(lg, emask & (lane != i1))
    e2 = jnp.exp(v2 - v1)
    p1 = 1.0 / (1.0 + e2)
    p2 = e2 / (1.0 + e2)
    eid = jnp.where(lane == 0, i1, i2) - N_EXPERT_GROUPS
    gate = gval * jnp.where(lane == 0, p1, p2)
    eid_ref[...] = eid[:, :TOP_K]
    gate_ref[...] = gate[:, :TOP_K]


def _route(logits):
    t = logits.shape[0]
    tm = 512
    return pl.pallas_call(
        _route_body,
        grid=(t // tm,),
        in_specs=[pl.BlockSpec((tm, LANES), lambda i: (i, 0))],
        out_specs=[pl.BlockSpec((tm, TOP_K), lambda i: (i, 0)),
                   pl.BlockSpec((tm, TOP_K), lambda i: (i, 0))],
        out_shape=[jax.ShapeDtypeStruct((t, TOP_K), jnp.int32),
                   jax.ShapeDtypeStruct((t, TOP_K), F32)],
        compiler_params=_cparams(("arbitrary",), 32),
        name="route",
    )(logits)


def _rank_body(e_ref, rank_ref, cnt_ref, carry_scr):
    i = pl.program_id(0)
    r = e_ref.shape[0]

    @pl.when(i == 0)
    def _():
        carry_scr[...] = jnp.zeros_like(carry_scr)

    lane = lax.broadcasted_iota(jnp.int32, (r, LANES), 1)
    onehot = (lane == e_ref[...]).astype(F32)
    tri = (lax.broadcasted_iota(jnp.int32, (r, r), 1)
           < lax.broadcasted_iota(jnp.int32, (r, r), 0)).astype(BF16)
    before = jnp.dot(tri, onehot.astype(BF16), preferred_element_type=F32) + carry_scr[0:1, :]
    rank_ref[...] = jnp.sum(onehot * before, axis=-1, keepdims=True).astype(jnp.int32)
    total = carry_scr[0:1, :] + jnp.sum(onehot, axis=0, keepdims=True)
    carry_scr[...] = jnp.broadcast_to(total, carry_scr.shape)
    cnt_ref[...] = jnp.broadcast_to(total, cnt_ref.shape).astype(jnp.int32)


def _rank(flat_e):
    a = flat_e.shape[0]
    r = 512
    return pl.pallas_call(
        _rank_body,
        grid=(a // r,),
        in_specs=[pl.BlockSpec((r, 1), lambda i: (i, 0))],
        out_specs=[pl.BlockSpec((r, 1), lambda i: (i, 0)),
                   pl.BlockSpec((8, LANES), lambda i: (0, 0))],
        out_shape=[jax.ShapeDtypeStruct((a, 1), jnp.int32),
                   jax.ShapeDtypeStruct((8, LANES), jnp.int32)],
        scratch_shapes=[pltpu.VMEM((8, LANES), F32)],
        compiler_params=_cparams(("arbitrary",), 32),
        name="rank",
    )(flat_e)


def _dest_body(e_ref, rank_ref, ps_ref, o_ref):
    lane = lax.broadcasted_iota(jnp.int32, (e_ref.shape[0], LANES), 1)
    first = jnp.sum(jnp.where(lane == e_ref[...], ps_ref[...], 0.0), axis=-1, keepdims=True)
    o_ref[...] = first.astype(jnp.int32) + rank_ref[...]


def _dest(flat_e, rank, pstarts_row):
    a = flat_e.shape[0]
    r = 2048
    return pl.pallas_call(
        _dest_body,
        grid=(a // r,),
        in_specs=[pl.BlockSpec((r, 1), lambda i: (i, 0)),
                  pl.BlockSpec((r, 1), lambda i: (i, 0)),
                  pl.BlockSpec((1, LANES), lambda i: (0, 0))],
        out_specs=pl.BlockSpec((r, 1), lambda i: (i, 0)),
        out_shape=jax.ShapeDtypeStruct((a, 1), jnp.int32),
        compiler_params=_cparams(("arbitrary",), 32),
        name="dest",
    )(flat_e, rank, pstarts_row)


DMA_UNROLL = 16


def _experts_body(ord_ref, elist_ref, src_ref, cnt_ref, h_hbm, w1_hbm, w3_hbm, w2_hbm, y_ref,
                  x_scr, w1_scr, w3_scr, w2_scr, sem, wsem):
    i = pl.program_id(0)
    nrows = x_scr.shape[1]
    nused = cnt_ref[0]
    nexp = cnt_ref[1]
    k = ord_ref[i]
    first = (i == 0) | (k != ord_ref[jnp.maximum(i - 1, 0)])

    def weights(kk, act):
        e = elist_ref[kk]
        slot = kk % 2
        act(pltpu.make_async_copy(w1_hbm.at[e], w1_scr.at[slot], wsem.at[slot, 0]))
        act(pltpu.make_async_copy(w3_hbm.at[e], w3_scr.at[slot], wsem.at[slot, 1]))
        act(pltpu.make_async_copy(w2_hbm.at[e], w2_scr.at[slot], wsem.at[slot, 2]))

    def gather(blk, act):
        slot = blk % 2
        base = blk * nrows

        def body(r, carry):
            act(pltpu.make_async_copy(h_hbm.at[pl.ds(src_ref[base + r], 1)],
                                      x_scr.at[slot, pl.ds(r, 1)], sem.at[slot]))
            return carry

        lax.fori_loop(0, nrows, body, 0, unroll=DMA_UNROLL)

    @pl.when(i == 0)
    def _():
        weights(k, lambda cp: cp.start())
        gather(i, lambda cp: cp.start())

    @pl.when(i + 1 < nused)
    def _():
        gather(i + 1, lambda cp: cp.start())

    @pl.when((i < nused) & first & (k + 1 < nexp))
    def _():
        weights(k + 1, lambda cp: cp.start())

    @pl.when((i < nused) & first)
    def _():
        weights(k, lambda cp: cp.wait())

    @pl.when(i < nused)
    def _():
        gather(i, lambda cp: cp.wait())
        wslot = k % 2
        xb = x_scr[i % 2].astype(BF16)
        a = jnp.dot(xb, w1_scr[wslot].astype(BF16), preferred_element_type=F32)
        b = jnp.dot(xb, w3_scr[wslot].astype(BF16), preferred_element_type=F32)
        act = (a * jax.nn.sigmoid(a) * b).astype(BF16)
        y_ref[...] = jnp.dot(act, w2_scr[wslot].astype(BF16), preferred_element_type=F32)

    @pl.when(i >= nused)
    def _():
        y_ref[...] = jnp.zeros_like(y_ref)


def _experts(block_ord, elist, src_tok, counts2, h2, w1, w3, w2):
    nblocks = block_ord.shape[0]
    d = h2.shape[1]
    ff = w1.shape[2]
    rb = EXPERT_ROW_BLOCK
    grid_spec = pltpu.PrefetchScalarGridSpec(
        num_scalar_prefetch=4,
        grid=(nblocks,),
        in_specs=[pl.BlockSpec(memory_space=pl.ANY)] * 4,
        out_specs=pl.BlockSpec((rb, d), lambda i, *_: (i, 0)),
        scratch_shapes=[pltpu.VMEM((2, rb, d), F32),
                        pltpu.VMEM((2, d, ff), F32), pltpu.VMEM((2, d, ff), F32),
                        pltpu.VMEM((2, ff, d), F32),
                        pltpu.SemaphoreType.DMA((2,)), pltpu.SemaphoreType.DMA((2, 3))],
    )
    return pl.pallas_call(
        _experts_body,
        grid_spec=grid_spec,
        out_shape=jax.ShapeDtypeStruct((nblocks * rb, d), F32),
        compiler_params=_cparams(("arbitrary",), 56),
        name="experts",
    )(block_ord, elist, src_tok, counts2, h2, w1, w3, w2)


def _combine_body(dest_ref, y_hbm, gate_ref, x1_ref, mod_ref, g_ref, o_ref, buf_scr, sem):
    i = pl.program_id(0)
    tm = x1_ref.shape[0]

    def gather(step, act):
        slot = step % 2
        base = step * tm * TOP_K

        def body(r, carry):
            for k in range(TOP_K):
                act(pltpu.make_async_copy(y_hbm.at[pl.ds(dest_ref[base + TOP_K * r + k], 1)],
                                          buf_scr.at[slot, k, pl.ds(r, 1)], sem.at[slot]))
            return carry

        lax.fori_loop(0, tm, body, 0, unroll=DMA_UNROLL // TOP_K)

    @pl.when(i == 0)
    def _():
        gather(i, lambda cp: cp.start())

    @pl.when(i + 1 < pl.num_programs(0))
    def _():
        gather(i + 1, lambda cp: cp.start())

    gather(i, lambda cp: cp.wait())
    slot = i % 2
    gate = gate_ref[...]
    y = buf_scr[slot, 0] * gate[:, 0:1] + buf_scr[slot, 1] * gate[:, 1:2]
    o_ref[...] = x1_ref[...] + mod_ref[0, 5:6, :] * _rms(y, g_ref[...])


def _combine(dest, ybuf, gate, x1, mod, g_post, seq):
    t, d = x1.shape
    tm = 256
    per_batch = seq // tm
    grid_spec = pltpu.PrefetchScalarGridSpec(
        num_scalar_prefetch=1,
        grid=(t // tm,),
        in_specs=[pl.BlockSpec(memory_space=pl.ANY),
                  pl.BlockSpec((tm, TOP_K), lambda i, dst: (i, 0)),
                  pl.BlockSpec((tm, d), lambda i, dst: (i, 0)),
                  pl.BlockSpec((1, 6, d), lambda i, dst: (i // per_batch, 0, 0)),
                  pl.BlockSpec((1, d), lambda i, dst: (0, 0))],
        out_specs=pl.BlockSpec((tm, d), lambda i, dst: (i, 0)),
        scratch_shapes=[pltpu.VMEM((2, TOP_K, tm, d), F32), pltpu.SemaphoreType.DMA((2,))],
    )
    return pl.pallas_call(
        _combine_body,
        grid_spec=grid_spec,
        out_shape=jax.ShapeDtypeStruct((t, d), F32),
        compiler_params=_cparams(("arbitrary",), 40),
        name="combine",
    )(dest, ybuf, gate, x1, mod, g_post)


def _layer(x, c, positions, w_ada, b_ada, g_mix_pre, g_mix_post, g_ffn_pre, g_ffn_post,
           w_in, conv_w, conv_b, filt_w1, filt_b1, filt_w2, filt_b2, filt_w3, filt_freq,
           hyena_skip, w_branch_gate, b_branch_gate, w_hy_o, w_at_o, w_out,
           w_group, b_group, w_expert, b_expert, w1_exp, w3_exp, w2_exp):
    bsz, seq, d = x.shape
    t = bsz * seq
    width = hyena_skip.shape[1]
    row = lambda v: v.reshape(1, -1)

    c_pad = jnp.pad(c, ((0, 8 - bsz), (0, 0)))
    mod = _adaln(c_pad, w_ada, row(b_ada))[:bsz].reshape(bsz, 6, d)

    x2d = x.reshape(t, d)
    n_gate = w_branch_gate.shape[1]
    n_cat = n_gate + -(-w_in.shape[1] // IN_PROJ_TN) * IN_PROJ_TN
    b_cat = jnp.concatenate([b_branch_gate, jnp.zeros((n_cat - n_gate,), F32)]).reshape(1, -1)
    pg = _in_proj(x2d, mod, row(g_mix_pre), w_branch_gate.astype(BF16), w_in.astype(BF16), b_cat, seq)
    pg3 = pg.reshape(bsz, seq, -1)

    hidden = filt_w2.shape[0]
    nfeat = -(-filt_w1.shape[0] // 8) * 8
    bands = np.zeros((nfeat, 1), np.float32)
    band_vals = np.linspace(1e-4, FILTER_BANDS - 1, FILTER_BANDS, dtype=np.float32)
    bands[1:1 + FILTER_BANDS, 0] = band_vals
    bands[1 + FILTER_BANDS:1 + 2 * FILTER_BANDS, 0] = band_vals
    w1p = jnp.pad(filt_w1, ((0, nfeat - filt_w1.shape[0]), (0, 0))).T
    col = lambda v: v.reshape(-1, 1)
    max_decay = math.log(DECAY_TARGET) / FAST_DECAY_PCT
    min_decay = math.log(DECAY_TARGET) / SLOW_DECAY_PCT
    deltas = jnp.abs(jnp.linspace(min_decay, max_decay, width, dtype=F32)).reshape(1, -1)
    uw = _filters(seq, width, jnp.asarray(bands), w1p, col(filt_b1), filt_w2.T, col(filt_b2),
                  col(filt_freq), deltas, filt_w3)
    s1, s1i, s2, s2i = _fft_tables(FFT_N1, 2 * seq // FFT_N1)
    kspec = _spectra(uw, s1, s2)
    y_hy = _hyena(pg3, n_gate, conv_w, row(conv_b), hyena_skip, kspec, s1, s1i, s2, s2i, width)

    half = ROT_DIM // 2
    inv_freq = np.power(ROPE_THETA, -2.0 * np.arange(half, dtype=np.float32) / ROT_DIM).astype(np.float32)
    freq_row = np.zeros((1, LANES), np.float32)
    freq_row[0, :half] = inv_freq
    freq_row[0, half:ROT_DIM] = inv_freq
    sign_row = np.zeros((1, LANES), np.float32)
    sign_row[0, :half] = -1.0
    sign_row[0, half:ROT_DIM] = 1.0
    y_at = _attention(pg3, positions.reshape(bsz, seq, 1), jnp.asarray(freq_row), jnp.asarray(sign_row),
                      n_gate + 3 * width)

    w_r = jnp.concatenate([w_group, jnp.transpose(w_expert, (1, 0, 2)).reshape(d, N_EXPERTS)], axis=1)
    w_r = jnp.pad(w_r, ((0, 0), (0, LANES - w_r.shape[1])))
    w_r_hi = w_r.astype(BF16)
    w_r_lo = (w_r - w_r_hi.astype(F32)).astype(BF16)
    b_r = jnp.pad(jnp.concatenate([b_group, b_expert.reshape(-1)]), (0, LANES - N_EXPERT_GROUPS - N_EXPERTS))
    x1, h2, logits = _mix_out(y_hy.reshape(t, width), y_at.reshape(t, -1), pg, x2d, mod,
                              row(g_mix_post), row(g_ffn_pre), w_hy_o.astype(BF16), w_at_o.astype(BF16),
                              w_out.astype(BF16), w_r_hi, w_r_lo, b_r.reshape(1, -1), seq)

    eid, gate = _route(logits)
    flat_e = eid.reshape(t * TOP_K, 1)
    rank, counts = _rank(flat_e)
    counts = counts[0, :N_EXPERTS]
    rb = EXPERT_ROW_BLOCK
    padded = (counts + rb - 1) // rb * rb
    pends = jnp.cumsum(padded)
    pstarts = pends - padded
    pstarts_row = jnp.pad(pstarts.astype(F32), (0, LANES - N_EXPERTS)).reshape(1, LANES)
    dest = _dest(flat_e, rank, pstarts_row)[:, 0]
    n_blocks = t * TOP_K // rb + N_EXPERTS
    block_start = jnp.arange(n_blocks, dtype=jnp.int32) * rb
    block_e = jnp.minimum(jnp.searchsorted(pends, block_start, side='right'), N_EXPERTS - 1).astype(jnp.int32)
    tok = jnp.arange(t * TOP_K, dtype=jnp.int32) // TOP_K
    src_tok = (jnp.arange(n_blocks * rb, dtype=jnp.int32) % t).at[dest].set(tok)
    has_rows = jnp.cumsum((counts > 0).astype(jnp.int32))
    elist = jnp.minimum(jnp.searchsorted(has_rows, jnp.arange(1, N_EXPERTS + 1, dtype=jnp.int32), side='left'),
                        N_EXPERTS - 1).astype(jnp.int32)
    block_ord = (has_rows - 1)[block_e].astype(jnp.int32)
    counts2 = jnp.stack([pends[-1] // rb, has_rows[-1]]).astype(jnp.int32)

    ybuf = _experts(block_ord, elist, src_tok, counts2, h2, w1_exp, w3_exp, w2_exp)
    out = _combine(dest, ybuf, gate, x1, mod, row(g_ffn_post), seq)
    return out.reshape(bsz, seq, d)


def kernel(x, c, positions, w_ada, b_ada, g_mix_pre, g_mix_post, g_ffn_pre, g_ffn_post, w_in, conv_w, conv_b, filt_w1, filt_b1, filt_w2, filt_b2, filt_w3, filt_freq, hyena_skip, w_branch_gate, b_branch_gate, w_hy_o, w_at_o, w_out, w_group, b_group, w_expert, b_expert, w1_exp, w3_exp, w2_exp):
    depth = w_ada.shape[0]
    for l in range(depth):
        x = _layer(x, c, positions, w_ada[l], b_ada[l], g_mix_pre[l], g_mix_post[l], g_ffn_pre[l],
                   g_ffn_post[l], w_in[l], conv_w[l], conv_b[l], filt_w1[l], filt_b1[l], filt_w2[l],
                   filt_b2[l], filt_w3[l], filt_freq[l], hyena_skip[l], w_branch_gate[l],
                   b_branch_gate[l], w_hy_o[l], w_at_o[l], w_out[l], w_group[l], b_group[l],
                   w_expert[l], b_expert[l], w1_exp[l], w3_exp[l], w2_exp[l])
    return x
```

```python
import functools
import math

import numpy as np
import jax
import jax.numpy as jnp
from jax import lax
from jax.experimental import pallas as pl
from jax.experimental.pallas import tpu as pltpu

F32 = jnp.float32
BF16 = jnp.bfloat16

LANES = 128
MIB = 1024 * 1024

RMS_EPS = 1e-6
NEG_INF = -1e30

HEAD_DIM = 128
ROT_DIM = HEAD_DIM // 4
ROPE_THETA = 500000.0
ATTN_GROUPS = ((128, 1), (512, 4), (2048, 16))
HEADS_PER_GROUP = 4
N_ATTN_HEADS = HEADS_PER_GROUP * len(ATTN_GROUPS)

FILTER_BANDS = 16
DECAY_TARGET = 1e-2
FAST_DECAY_PCT = 0.3
SLOW_DECAY_PCT = 1.5

N_EXPERT_GROUPS = 8
EXPERTS_PER_GROUP = 8
N_EXPERTS = N_EXPERT_GROUPS * EXPERTS_PER_GROUP
TOP_K = 2
EXPERT_ROW_BLOCK = 128

FFT_N1 = 128
PITCH = FFT_N1 + 8
FFT_UNROLL = 32
FFT_MID_UNROLL = 16


def _cparams(sem, vmem_mib):
    return pltpu.CompilerParams(dimension_semantics=sem, vmem_limit_bytes=vmem_mib * MIB)


def _rms(x, g):
    return x * lax.rsqrt(jnp.mean(x * x, axis=-1, keepdims=True) + RMS_EPS) * g


def _adaln_body(c_ref, w_ref, b_ref, o_ref):
    c = c_ref[...]
    cond = c * jax.nn.sigmoid(c)
    o_ref[...] = jnp.dot(cond.astype(BF16), w_ref[...].astype(BF16),
                         preferred_element_type=F32) + b_ref[...]


def _adaln(c_pad, w_ada, b_ada):
    rows, d = c_pad.shape
    n = w_ada.shape[1]
    tn = 1024
    return pl.pallas_call(
        _adaln_body,
        grid=(n // tn,),
        in_specs=[pl.BlockSpec((rows, d), lambda j: (0, 0)),
                  pl.BlockSpec((d, tn), lambda j: (0, j)),
                  pl.BlockSpec((1, tn), lambda j: (0, j))],
        out_specs=pl.BlockSpec((rows, tn), lambda j: (0, j)),
        out_shape=jax.ShapeDtypeStruct((rows, n), F32),
        compiler_params=_cparams(("arbitrary",), 40),
        name="adaln",
    )(c_pad, w_ada, b_ada)


IN_PROJ_TM = 1024
IN_PROJ_TN = 1024
IN_PROJ_SUB_M = 256
IN_PROJ_SUB_N = 512


def _in_proj_body(n_gate, last_cols, x_ref, mod_ref, g_ref, wg_ref, wi_ref, b_ref, o_ref, h_scr):
    j = pl.program_id(1)
    last = pl.num_programs(1) - 1

    @pl.when(j == 0)
    def _():
        x = x_ref[...]
        h = _rms(x, g_ref[...]) * (1.0 + mod_ref[0, 1:2, :]) + mod_ref[0, 0:1, :]
        h_scr[...] = h.astype(BF16)

    tm, tn = o_ref.shape

    def tile(w_ref, gate, ncols):
        for mi in range(tm // IN_PROJ_SUB_M):
            rows = slice(mi * IN_PROJ_SUB_M, (mi + 1) * IN_PROJ_SUB_M)
            for ni in range(tn // IN_PROJ_SUB_N):
                cols = slice(ni * IN_PROJ_SUB_N, (ni + 1) * IN_PROJ_SUB_N)
                if ni * IN_PROJ_SUB_N >= ncols:
                    o_ref[rows, cols] = jnp.zeros((IN_PROJ_SUB_M, IN_PROJ_SUB_N), o_ref.dtype)
                    continue
                acc = jnp.dot(h_scr[rows, :], w_ref[:, cols], preferred_element_type=F32) + b_ref[:, cols]
                o_ref[rows, cols] = (jax.nn.sigmoid(acc) if gate else acc).astype(o_ref.dtype)

    @pl.when(j < n_gate)
    def _():
        tile(wg_ref, True, tn)

    @pl.when((j >= n_gate) & (j < last))
    def _():
        tile(wi_ref, False, tn)

    @pl.when(j == last)
    def _():
        tile(wi_ref, False, last_cols)


def _in_proj(x2d, mod, g_pre, w_gate, w_in, b_cat, seq):
    t, d = x2d.shape
    tm, tn = IN_PROJ_TM, IN_PROJ_TN
    ng = w_gate.shape[1] // tn
    ni = pl.cdiv(w_in.shape[1], tn)
    last_cols = w_in.shape[1] - (ni - 1) * tn
    assert w_gate.shape[1] % tn == 0 and last_cols % IN_PROJ_SUB_N == 0
    per_batch = seq // tm
    return pl.pallas_call(
        functools.partial(_in_proj_body, ng, last_cols),
        grid=(t // tm, ng + ni),
        in_specs=[pl.BlockSpec((tm, d), lambda i, j: (i, 0)),
                  pl.BlockSpec((1, 6, d), lambda i, j: (i // per_batch, 0, 0)),
                  pl.BlockSpec((1, d), lambda i, j: (0, 0)),
                  pl.BlockSpec((d, tn), lambda i, j: (0, jnp.minimum(j, ng - 1))),
                  pl.BlockSpec((d, tn), lambda i, j: (0, jnp.clip(j - ng, 0, ni - 1))),
                  pl.BlockSpec((1, tn), lambda i, j: (0, j))],
        out_specs=pl.BlockSpec((tm, tn), lambda i, j: (i, j)),
        out_shape=jax.ShapeDtypeStruct((t, (ng + ni) * tn), BF16),
        scratch_shapes=[pltpu.VMEM((tm, d), BF16)],
        compiler_params=_cparams(("arbitrary", "arbitrary"), 56),
        name="in_proj",
    )(x2d, mod, g_pre, w_gate, w_in, b_cat)


def _filters_body(seq, band_ref, w1_ref, b1_ref, w2_ref, b2_ref, fr_ref, dl_ref,
                  w3a_ref, w3b_ref, o_ref, hid_scr):
    i = pl.program_id(0)
    j = pl.program_id(1)
    tl = hid_scr.shape[0]
    row = (lax.broadcasted_iota(jnp.int32, (tl, 1), 0) + i * tl).astype(F32)

    @pl.when((j == 0) & (pl.program_id(2) == 0))
    def _():
        nfeat = band_ref.shape[0]
        pos = (lax.broadcasted_iota(jnp.int32, (1, tl), 1) + i * tl).astype(F32)
        feat = lax.broadcasted_iota(jnp.int32, (nfeat, tl), 0)
        ang = band_ref[...] * (2.0 * math.pi * pos / seq)
        feats = jnp.where(feat == 0, pos / (seq - 1.0),
                          jnp.where(feat <= FILTER_BANDS, jnp.cos(ang),
                                    jnp.where(feat <= 2 * FILTER_BANDS, -jnp.sin(ang), 0.0)))
        hi = lax.Precision.HIGHEST
        fr = fr_ref[...]
        hid = jnp.sin(fr * (jnp.dot(w1_ref[...], feats, precision=hi, preferred_element_type=F32)
                            + b1_ref[...]))
        hid = jnp.sin(fr * (jnp.dot(w2_ref[...], hid, precision=hi, preferred_element_type=F32)
                            + b2_ref[...]))
        hid_scr[...] = hid.T

    hi = lax.Precision.HIGHEST
    hid = hid_scr[...]
    decay = jnp.exp(-(row / (seq - 1.0)) * dl_ref[...])
    hf = jnp.dot(hid, w3a_ref[...], precision=hi, preferred_element_type=F32) * decay
    hb = jnp.dot(hid, w3b_ref[...], precision=hi, preferred_element_type=F32) * decay
    hb = jnp.where(row == 0.0, 0.0, hb)
    o_ref[0, 0] = hf + hb
    o_ref[0, 1] = hf - hb


def _filters(seq, width, bands, w1p, b1, w2, b2, freq, deltas, w3):
    tl, tc = 512, 512
    nct = width // tc
    hidden = w2.shape[0]
    nfeat = bands.shape[0]
    const = lambda i, j, o: (0, 0)
    return pl.pallas_call(
        functools.partial(_filters_body, float(seq)),
        grid=(seq // tl, nct, 2),
        in_specs=[pl.BlockSpec((nfeat, 1), const),
                  pl.BlockSpec((hidden, nfeat), const),
                  pl.BlockSpec((hidden, 1), const),
                  pl.BlockSpec((hidden, hidden), const),
                  pl.BlockSpec((hidden, 1), const),
                  pl.BlockSpec((hidden, 1), const),
                  pl.BlockSpec((1, tc), lambda i, j, o: (0, j)),
                  pl.BlockSpec((hidden, tc), lambda i, j, o: (0, (2 * o) * nct + j)),
                  pl.BlockSpec((hidden, tc), lambda i, j, o: (0, (2 * o + 1) * nct + j))],
        out_specs=pl.BlockSpec((1, 2, tl, tc), lambda i, j, o: (o, 0, i, j)),
        out_shape=jax.ShapeDtypeStruct((2, 2, seq, width), F32),
        scratch_shapes=[pltpu.VMEM((tl, hidden), F32)],
        compiler_params=_cparams(("arbitrary", "arbitrary", "arbitrary"), 32),
        name="filters",
    )(bands, w1p, b1, w2, b2, freq, deltas, w3, w3)


def _fft_tables(n1, n2):
    n = n1 * n2
    q = np.arange(n2)[:, None]
    b = np.arange(n2 // 2)[None, :]
    a = np.arange(n1)[:, None, None]
    ang = -2.0 * np.pi * (a * q[None] / n + (q * b)[None] / n2)
    stage1 = np.concatenate([np.cos(ang), np.sin(ang)], axis=1)
    stage1_inv = np.transpose(stage1, (0, 2, 1)) / n
    p = np.arange(n1)
    ang2 = -2.0 * np.pi * np.outer(p, p) / n1
    fre, fim = np.cos(ang2), np.sin(ang2)
    stage2 = np.block([[fre, -fim], [fim, fre]])
    stage2_inv = np.block([[fre, fim], [-fim, fre]])
    as_bf16 = lambda m: jnp.asarray(m, dtype=F32).astype(BF16)
    return as_bf16(stage1), as_bf16(stage1_inv), as_bf16(stage2), as_bf16(stage2_inv)


def _halves(ref, rows):
    return jnp.concatenate([ref[0, rows, :], ref[1, rows, :]], axis=1)


def _fft_stage1(z_ref, s1_ref, gre_ref, gim_ref):
    n1, two_n2, n2h = s1_ref.shape
    n2 = two_n2 // 2

    def step(a, carry):
        zrows = _halves(z_ref, pl.ds(a, n2h, stride=PITCH))
        g = jnp.dot(s1_ref[a], zrows.astype(BF16), preferred_element_type=F32)
        for h in range(2):
            cols = slice(h * LANES, (h + 1) * LANES)
            gre_ref[h, pl.ds(a, n2, stride=PITCH), :] = g[:n2, cols]
            gim_ref[h, pl.ds(a, n2, stride=PITCH), :] = g[n2:, cols]
        return carry

    lax.fori_loop(0, n1, step, 0, unroll=FFT_UNROLL)


def _fft_stage1_inv(gre_ref, gim_ref, s1i_ref, y_ref):
    n1, n2h, two_n2 = s1i_ref.shape
    n2 = two_n2 // 2

    def step(a, carry):
        rows = pl.ds(a, n2, stride=PITCH)
        hs = jnp.concatenate([_halves(gre_ref, rows), _halves(gim_ref, rows)], axis=0)
        y = jnp.dot(s1i_ref[a], hs.astype(BF16), preferred_element_type=F32)
        for h in range(2):
            y_ref[h, pl.ds(a, n2h, stride=PITCH), :] = y[:, h * LANES:(h + 1) * LANES]
        return carry

    lax.fori_loop(0, n1, step, 0, unroll=FFT_UNROLL)


def _stage2_block(gre_ref, gim_ref, s2_ref, q):
    n1 = s2_ref.shape[0] // 2
    rows = pl.ds(pl.multiple_of(q * PITCH, 8), n1)
    gs = jnp.concatenate([_halves(gre_ref, rows), _halves(gim_ref, rows)], axis=0)
    x = jnp.dot(s2_ref[...], gs.astype(BF16), preferred_element_type=F32)
    return x[:n1], x[n1:]


def _spectra_body(uw_ref, s1_ref, s2_ref, k_ref, z_scr, gre_scr, gim_scr):
    n1 = s2_ref.shape[0] // 2
    n2 = s1_ref.shape[1] // 2
    for h in range(2):
        for b in range(n2 // 2):
            z_scr[h, pl.ds(b * PITCH, n1), :] = uw_ref[0, h, pl.ds(b * n1, n1), :]
    _fft_stage1(z_scr, s1_ref, gre_scr, gim_scr)

    def step(q, carry):
        xre, xim = _stage2_block(gre_scr, gim_scr, s2_ref, q)
        rows = pl.ds(pl.multiple_of(q * n1, 8), n1)
        k_ref[0, 0, rows, :] = xre[:, :LANES]
        k_ref[0, 1, rows, :] = xim[:, LANES:]
        return carry

    lax.fori_loop(0, n2, step, 0, unroll=FFT_MID_UNROLL)


def _spectra(uw, s1, s2):
    _, _, seq, width = uw.shape
    n1 = FFT_N1
    n2 = 2 * seq // n1
    const3 = lambda j, o: (0, 0, 0)
    return pl.pallas_call(
        _spectra_body,
        grid=(width // LANES, 2),
        in_specs=[pl.BlockSpec((1, 2, seq, LANES), lambda j, o: (o, 0, 0, j)),
                  pl.BlockSpec(s1.shape, const3),
                  pl.BlockSpec(s2.shape, lambda j, o: (0, 0))],
        out_specs=pl.BlockSpec((1, 2, 2 * seq, LANES), lambda j, o: (o, 0, 0, j)),
        out_shape=jax.ShapeDtypeStruct((2, 2, 2 * seq, width), F32),
        scratch_shapes=[pltpu.VMEM((2, (n2 // 2) * PITCH, LANES), F32),
                        pltpu.VMEM((2, n2 * PITCH, LANES), F32),
                        pltpu.VMEM((2, n2 * PITCH, LANES), F32)],
        compiler_params=_cparams(("arbitrary", "arbitrary"), 56),
        name="spectra",
    )(uw, s1, s2)


def _short_conv_block(p_ref, b, i, nblk, w_ref, bias_ref):
    n1 = FFT_N1
    start = pl.multiple_of(i * n1, n1)
    cur = p_ref[b, pl.ds(start, n1), :].astype(F32)
    before = p_ref[b, pl.ds(pl.multiple_of(jnp.maximum(start - 16, 0), 16), 16), :].astype(F32)
    after = p_ref[b, pl.ds(pl.multiple_of(jnp.minimum(start + n1, (nblk - 1) * n1), 16), 16), :].astype(F32)
    last_prev = before[15:16] * jnp.where(i > 0, 1.0, 0.0).astype(F32)
    first_next = after[0:1] * jnp.where(i < nblk - 1, 1.0, 0.0).astype(F32)
    row = lax.broadcasted_iota(jnp.int32, (n1, 1), 0)
    prev = jnp.where(row == 0, last_prev, pltpu.roll(cur, 1, 0))
    nxt = jnp.where(row == n1 - 1, first_next, pltpu.roll(cur, n1 - 1, 0))
    return prev * w_ref[0:1, :] + cur * w_ref[1:2, :] + nxt * w_ref[2:3, :] + bias_ref[...]


def _hyena_body(pv_ref, px1_ref, px2_ref, cwv_ref, cw1_ref, cw2_ref, cbv_ref, cb1_ref, cb2_ref,
                skip_ref, k_ref, s1_ref, s1i_ref, s2_ref, s2i_ref, o_ref,
                z_scr, y_scr, gre_scr, gim_scr):
    c = pl.program_id(1)
    g = pl.program_id(2)
    ngroups = pl.num_programs(2)
    n1 = FFT_N1
    n2 = s1_ref.shape[1] // 2
    nblk = n2 // 2
    per_group = n2 // ngroups

    @pl.when(g == 0)
    def _():
        @pl.when(c == 0)
        def _():
            def fill(i, carry):
                for b in range(2):
                    z_scr[b, pl.ds(pl.multiple_of(i * PITCH, 8), n1), :] = _short_conv_block(
                        pv_ref, b, i, nblk, cwv_ref, cbv_ref)
                return carry
            lax.fori_loop(0, nblk, fill, 0)

        _fft_stage1(z_scr, s1_ref, gre_scr, gim_scr)

    def mid(ql, carry):
        q = g * per_group + ql
        xre, xim = _stage2_block(gre_scr, gim_scr, s2_ref, q)
        krows = pl.ds(pl.multiple_of(ql * n1, 8), n1)
        kre = k_ref[0, 0, krows, :]
        kim = k_ref[0, 1, krows, :]
        kre = jnp.concatenate([kre, kre], axis=1)
        kim = jnp.concatenate([kim, kim], axis=1)
        ys = jnp.concatenate([xre * kre - xim * kim, xre * kim + xim * kre], axis=0)
        hh = jnp.dot(s2i_ref[...], ys.astype(BF16), preferred_element_type=F32)
        rows = pl.ds(pl.multiple_of(q * PITCH, 8), n1)
        for b in range(2):
            cols = slice(b * LANES, (b + 1) * LANES)
            gre_scr[b, rows, :] = hh[:n1, cols]
            gim_scr[b, rows, :] = hh[n1:, cols]
        return carry

    lax.fori_loop(0, per_group, mid, 0, unroll=FFT_MID_UNROLL)

    @pl.when(g == ngroups - 1)
    def _():
        _fft_stage1_inv(gre_scr, gim_scr, s1i_ref, y_scr)

        def post(px_ref, cw_ref, cb_ref, order, store):
            def blk(i, carry):
                rows = pl.ds(pl.multiple_of(i * PITCH, 8), n1)
                for b in range(2):
                    zb = z_scr[b, rows, :]
                    gate = _short_conv_block(px_ref, b, i, nblk, cw_ref, cb_ref)
                    store(b, i, rows, gate * (y_scr[b, rows, :] + zb * skip_ref[order:order + 1, :]))
                return carry
            lax.fori_loop(0, nblk, blk, 0)

        @pl.when(c == 0)
        def _():
            def store(b, i, rows, val):
                z_scr[b, rows, :] = val
            post(px1_ref, cw1_ref, cb1_ref, 0, store)

        @pl.when(c == 1)
        def _():
            def store(b, i, rows, val):
                o_ref[b, pl.ds(pl.multiple_of(i * n1, n1), n1), :] = val.astype(o_ref.dtype)
            post(px2_ref, cw2_ref, cb2_ref, 1, store)


def _hyena(pg3, proj_col0, conv_w, conv_b, skip, kspec, s1, s1i, s2, s2i, width):
    bsz, seq, _ = pg3.shape
    assert bsz == 2
    n1 = FFT_N1
    n2 = 2 * seq // n1
    nct = width // LANES
    ngroups = 4
    krows = (n2 // ngroups) * n1
    col = lambda off: (lambda j, c, g: (0, 0, proj_col0 // LANES + off * nct + j))
    cw = lambda off: (lambda j, c, g: (0, off * nct + j))
    const3 = lambda j, c, g: (0, 0, 0)
    const2 = lambda j, c, g: (0, 0)
    return pl.pallas_call(
        _hyena_body,
        grid=(nct, 2, ngroups),
        in_specs=[pl.BlockSpec((2, seq, LANES), col(0)),
                  pl.BlockSpec((2, seq, LANES), col(1)),
                  pl.BlockSpec((2, seq, LANES), col(2)),
                  pl.BlockSpec((3, LANES), cw(0)),
                  pl.BlockSpec((3, LANES), cw(1)),
                  pl.BlockSpec((3, LANES), cw(2)),
                  pl.BlockSpec((1, LANES), cw(0)),
                  pl.BlockSpec((1, LANES), cw(1)),
                  pl.BlockSpec((1, LANES), cw(2)),
                  pl.BlockSpec((2, LANES), lambda j, c, g: (0, j)),
                  pl.BlockSpec((1, 2, krows, LANES), lambda j, c, g: (c, 0, g, j)),
                  pl.BlockSpec(s1.shape, const3),
                  pl.BlockSpec(s1i.shape, const3),
                  pl.BlockSpec(s2.shape, const2),
                  pl.BlockSpec(s2i.shape, const2)],
        out_specs=pl.BlockSpec((2, seq, LANES), lambda j, c, g: (0, 0, j)),
        out_shape=jax.ShapeDtypeStruct((2, seq, width), BF16),
        scratch_shapes=[pltpu.VMEM((2, (n2 // 2) * PITCH, LANES), F32),
                        pltpu.VMEM((2, (n2 // 2) * PITCH, LANES), F32),
                        pltpu.VMEM((2, n2 * PITCH, LANES), F32),
                        pltpu.VMEM((2, n2 * PITCH, LANES), F32)],
        compiler_params=_cparams(("arbitrary", "arbitrary", "arbitrary"), 56),
        name="hyena",
    )(pg3, pg3, pg3, conv_w, conv_w, conv_w, conv_b, conv_b, conv_b, skip, kspec, s1, s1i, s2, s2i)


ATTN_QBLK = 128
ATTN_UNROLL = 8


def _attention_body(pos_ref, freq_ref, sign_ref, *refs):
    qkv_refs = refs[:9]
    o_ref = refs[9]
    cos_scr, sin_scr, q_scr, k_scr, v_scr, og_scr, lse_scr = refs[10:]
    seq = q_scr.shape[0]
    chunk = 512
    nchunks = seq // chunk

    @pl.when(pl.program_id(1) == 0)
    def _():
        def trig(i, carry):
            rows = pl.ds(pl.multiple_of(i * chunk, chunk), chunk)
            ang = pos_ref[0, rows, :].astype(F32) * freq_ref[...]
            cos_scr[rows, :] = jnp.cos(ang)
            sin_scr[rows, :] = jnp.sin(ang) * sign_ref[...]
            return carry
        lax.fori_loop(0, nchunks, trig, 0)

    def rotate(src_ref, dst_ref):
        def body(i, carry):
            rows = pl.ds(pl.multiple_of(i * chunk, chunk), chunk)
            t = src_ref[0, rows, :].astype(F32)
            lane = lax.broadcasted_iota(jnp.int32, t.shape, 1)
            partner = jnp.where(lane < ROT_DIM // 2,
                                pltpu.roll(t, LANES - ROT_DIM // 2, 1), pltpu.roll(t, ROT_DIM // 2, 1))
            dst_ref[rows, :] = t * cos_scr[rows, :] + partner * sin_scr[rows, :]
            return carry
        lax.fori_loop(0, nchunks, body, 0, unroll=2)

    def widen(src_ref, dst_ref):
        def body(i, carry):
            rows = pl.ds(pl.multiple_of(i * chunk, chunk), chunk)
            dst_ref[rows, :] = src_ref[0, rows, :].astype(F32)
            return carry
        lax.fori_loop(0, nchunks, body, 0, unroll=2)

    scale = HEAD_DIM ** -0.5
    for gi, (window, dil) in enumerate(ATTN_GROUPS):
        rotate(qkv_refs[3 * gi], q_scr)
        rotate(qkv_refs[3 * gi + 1], k_scr)
        widen(qkv_refs[3 * gi + 2], v_scr)
        n = seq // dil
        half = window // (2 * dil)
        tk = min(n, ATTN_QBLK + 2 * half)
        blocks_per_res = n // ATTN_QBLK

        def block(u, carry, dil=dil, n=n, half=half, tk=tk, blocks_per_res=blocks_per_res, gi=gi):
            r = u // blocks_per_res
            m = u % blocks_per_res
            q0 = m * ATTN_QBLK
            k0 = jnp.clip(q0 - half, 0, n - tk)
            qrows = pl.ds(r + dil * q0, ATTN_QBLK, stride=dil)
            krows = pl.ds(r + dil * k0, tk, stride=dil)
            qb = q_scr[qrows, :].astype(BF16)
            kb = k_scr[krows, :].astype(BF16)
            vb = v_scr[krows, :].astype(BF16)
            s = lax.dot_general(qb, kb, (((1,), (1,)), ((), ())), preferred_element_type=F32) * scale
            qi = q0 + lax.broadcasted_iota(jnp.int32, (ATTN_QBLK, tk), 0)
            kj = k0 + lax.broadcasted_iota(jnp.int32, (ATTN_QBLK, tk), 1)
            s = jnp.where(jnp.abs(qi - kj) <= half, s, NEG_INF)
            mx = jnp.max(s, axis=-1, keepdims=True)
            p = jnp.exp(s - mx)
            l = jnp.sum(p, axis=-1, keepdims=True)
            o = jnp.dot(p.astype(BF16), vb, preferred_element_type=F32) / l
            og_scr[gi, qrows, :] = o
            lse_scr[gi, qrows, :] = jnp.broadcast_to(mx + jnp.log(l), (ATTN_QBLK, LANES))
            return carry

        lax.fori_loop(0, seq // ATTN_QBLK, block, 0, unroll=ATTN_UNROLL)

    def merge(i, carry):
        rows = pl.ds(pl.multiple_of(i * chunk, chunk), chunk)
        lses = [lse_scr[gi, rows, :] for gi in range(len(ATTN_GROUPS))]
        mx = functools.reduce(jnp.maximum, lses)
        ws = [jnp.exp(v - mx) for v in lses]
        den = functools.reduce(lambda a, b: a + b, ws)
        num = functools.reduce(lambda a, b: a + b,
                               [w * og_scr[gi, rows, :] for gi, w in enumerate(ws)])
        o_ref[0, rows, :] = (num / den).astype(o_ref.dtype)
        return carry

    lax.fori_loop(0, nchunks, merge, 0)


def _attention(pg3, pos3, freq_row, sign_row, qkv_col0):
    bsz, seq, _ = pg3.shape
    ng = len(ATTN_GROUPS)
    in_specs = [pl.BlockSpec((1, seq, 1), lambda b, h: (b, 0, 0)),
                pl.BlockSpec((1, LANES), lambda b, h: (0, 0)),
                pl.BlockSpec((1, LANES), lambda b, h: (0, 0))]
    for gi in range(ng):
        for which in range(3):
            base = qkv_col0 // LANES + which * N_ATTN_HEADS + gi * HEADS_PER_GROUP
            in_specs.append(pl.BlockSpec((1, seq, LANES), lambda b, h, base=base: (b, 0, base + h)))
    return pl.pallas_call(
        _attention_body,
        grid=(bsz, HEADS_PER_GROUP),
        in_specs=in_specs,
        out_specs=pl.BlockSpec((1, seq, LANES), lambda b, h: (b, 0, h)),
        out_shape=jax.ShapeDtypeStruct((bsz, seq, HEADS_PER_GROUP * HEAD_DIM), BF16),
        scratch_shapes=[pltpu.VMEM((seq, LANES), F32), pltpu.VMEM((seq, LANES), F32),
                        pltpu.VMEM((seq, LANES), F32), pltpu.VMEM((seq, LANES), F32),
                        pltpu.VMEM((seq, LANES), F32),
                        pltpu.VMEM((ng, seq, LANES), F32), pltpu.VMEM((ng, seq, LANES), F32)],
        compiler_params=_cparams(("arbitrary", "arbitrary"), 56),
        name="attention",
    )(pos3, freq_row, sign_row, *([pg3] * (3 * ng)))


def _mix_out_body(yhy_ref, yat_ref, ghy_ref, gat_ref, x_ref, mod_ref, gpost_ref, gpre_ref,
                  whyo_ref, wato_ref, wout_ref, wrhi_ref, wrlo_ref, br_ref, x1_ref, h2_ref, lg_ref):
    a = jnp.dot(yhy_ref[...], whyo_ref[...], preferred_element_type=F32)
    b = jnp.dot(yat_ref[...], wato_ref[...], preferred_element_type=F32)
    merged = ghy_ref[...].astype(F32) * a + gat_ref[...].astype(F32) * b
    y = jnp.dot(merged.astype(BF16), wout_ref[...], preferred_element_type=F32)
    x1 = x_ref[...] + mod_ref[0, 2:3, :] * _rms(y, gpost_ref[...])
    x1_ref[...] = x1
    h2 = _rms(x1, gpre_ref[...]) * (1.0 + mod_ref[0, 4:5, :]) + mod_ref[0, 3:4, :]
    h2_ref[...] = h2
    h2_hi = h2.astype(BF16)
    h2_lo = (h2 - h2_hi.astype(F32)).astype(BF16)
    w_hi = wrhi_ref[...]
    w_lo = wrlo_ref[...]
    lg_ref[...] = (jnp.dot(h2_hi, w_hi, preferred_element_type=F32)
                   + jnp.dot(h2_lo, w_hi, preferred_element_type=F32)
                   + jnp.dot(h2_hi, w_lo, preferred_element_type=F32)
                   + jnp.dot(h2_lo, w_lo, preferred_element_type=F32)) + br_ref[...]


def _mix_out(y_hy, y_at, pg, x2d, mod, g_post, g_pre, w_hy_o, w_at_o, w_out, w_r_hi, w_r_lo, b_r, seq):
    t, d = x2d.shape
    tm = 256
    per_batch = seq // tm
    gblk = 0
    const = lambda i: (0, 0)
    return pl.pallas_call(
        _mix_out_body,
        grid=(t // tm,),
        in_specs=[pl.BlockSpec((tm, y_hy.shape[1]), lambda i: (i, 0)),
                  pl.BlockSpec((tm, y_at.shape[1]), lambda i: (i, 0)),
                  pl.BlockSpec((tm, d), lambda i: (i, gblk)),
                  pl.BlockSpec((tm, d), lambda i: (i, gblk + 1)),
                  pl.BlockSpec((tm, d), lambda i: (i, 0)),
                  pl.BlockSpec((1, 6, d), lambda i: (i // per_batch, 0, 0)),
                  pl.BlockSpec((1, d), const),
                  pl.BlockSpec((1, d), const),
                  pl.BlockSpec(w_hy_o.shape, const),
                  pl.BlockSpec(w_at_o.shape, const),
                  pl.BlockSpec(w_out.shape, const),
                  pl.BlockSpec(w_r_hi.shape, const),
                  pl.BlockSpec(w_r_lo.shape, const),
                  pl.BlockSpec((1, LANES), const)],
        out_specs=[pl.BlockSpec((tm, d), lambda i: (i, 0)),
                   pl.BlockSpec((tm, d), lambda i: (i, 0)),
                   pl.BlockSpec((tm, LANES), lambda i: (i, 0))],
        out_shape=[jax.ShapeDtypeStruct((t, d), F32),
                   jax.ShapeDtypeStruct((t, d), F32),
                   jax.ShapeDtypeStruct((t, LANES), F32)],
        compiler_params=_cparams(("arbitrary",), 56),
        name="mix_out",
    )(y_hy, y_at, pg, pg, x2d, mod, g_post, g_pre, w_hy_o, w_at_o, w_out, w_r_hi, w_r_lo, b_r)


def _route_body(lg_ref, eid_ref, gate_ref):
    lg = lg_ref[...]
    lane = lax.broadcasted_iota(jnp.int32, lg.shape, 1)
    big = jnp.int32(1 << 20)

    def first_argmax(vals, mask):
        v = jnp.where(mask, vals, -jnp.inf)
        mx = jnp.max(v, axis=-1, keepdims=True)
        idx = jnp.min(jnp.where(mask & (v == mx), lane, big), axis=-1, keepdims=True)
        return mx, idx

    gmask = lane < N_EXPERT_GROUPS
    gmax, gidx = first_argmax(lg, gmask)
    gval = 1.0 / jnp.sum(jnp.where(gmask, jnp.exp(lg - gmax), 0.0), axis=-1, keepdims=True)
    lo = N_EXPERT_GROUPS + gidx * EXPERTS_PER_GROUP
    emask = (lane >= lo) & (lane < lo + EXPERTS_PER_GROUP)
    v1, i1 = first_argmax(lg, emask)
    v2, i2 = first_argmax(lg, emask & (lane != i1))
    e2 = jnp.exp(v2 - v1)
    p1 = 1.0 / (1.0 + e2)
    p2 = e2 / (1.0 + e2)
    eid = jnp.where(lane == 0, i1, i2) - N_EXPERT_GROUPS
    gate = gval * jnp.where(lane == 0, p1, p2)
    eid_ref[...] = eid[:, :TOP_K]
    gate_ref[...] = gate[:, :TOP_K]


def _route(logits):
    t = logits.shape[0]
    tm = 512
    return pl.pallas_call(
        _route_body,
        grid=(t // tm,),
        in_specs=[pl.BlockSpec((tm, LANES), lambda i: (i, 0))],
        out_specs=[pl.BlockSpec((tm, TOP_K), lambda i: (i, 0)),
                   pl.BlockSpec((tm, TOP_K), lambda i: (i, 0))],
        out_shape=[jax.ShapeDtypeStruct((t, TOP_K), jnp.int32),
                   jax.ShapeDtypeStruct((t, TOP_K), F32)],
        compiler_params=_cparams(("arbitrary",), 32),
        name="route",
    )(logits)


def _rank_body(e_ref, rank_ref, cnt_ref, carry_scr):
    i = pl.program_id(0)
    r = e_ref.shape[0]

    @pl.when(i == 0)
    def _():
        carry_scr[...] = jnp.zeros_like(carry_scr)

    lane = lax.broadcasted_iota(jnp.int32, (r, LANES), 1)
    onehot = (lane == e_ref[...]).astype(F32)
    tri = (lax.broadcasted_iota(jnp.int32, (r, r), 1)
           < lax.broadcasted_iota(jnp.int32, (r, r), 0)).astype(BF16)
    before = jnp.dot(tri, onehot.astype(BF16), preferred_element_type=F32) + carry_scr[0:1, :]
    rank_ref[...] = jnp.sum(onehot * before, axis=-1, keepdims=True).astype(jnp.int32)
    total = carry_scr[0:1, :] + jnp.sum(onehot, axis=0, keepdims=True)
    carry_scr[...] = jnp.broadcast_to(total, carry_scr.shape)
    cnt_ref[...] = jnp.broadcast_to(total, cnt_ref.shape).astype(jnp.int32)


def _rank(flat_e):
    a = flat_e.shape[0]
    r = 512
    return pl.pallas_call(
        _rank_body,
        grid=(a // r,),
        in_specs=[pl.BlockSpec((r, 1), lambda i: (i, 0))],
        out_specs=[pl.BlockSpec((r, 1), lambda i: (i, 0)),
                   pl.BlockSpec((8, LANES), lambda i: (0, 0))],
        out_shape=[jax.ShapeDtypeStruct((a, 1), jnp.int32),
                   jax.ShapeDtypeStruct((8, LANES), jnp.int32)],
        scratch_shapes=[pltpu.VMEM((8, LANES), F32)],
        compiler_params=_cparams(("arbitrary",), 32),
        name="rank",
    )(flat_e)


def _dest_body(e_ref, rank_ref, ps_ref, o_ref):
    lane = lax.broadcasted_iota(jnp.int32, (e_ref.shape[0], LANES), 1)
    first = jnp.sum(jnp.where(lane == e_ref[...], ps_ref[...], 0.0), axis=-1, keepdims=True)
    o_ref[...] = first.astype(jnp.int32) + rank_ref[...]


def _dest(flat_e, rank, pstarts_row):
    a = flat_e.shape[0]
    r = 2048
    return pl.pallas_call(
        _dest_body,
        grid=(a // r,),
        in_specs=[pl.BlockSpec((r, 1), lambda i: (i, 0)),
                  pl.BlockSpec((r, 1), lambda i: (i, 0)),
                  pl.BlockSpec((1, LANES), lambda i: (0, 0))],
        out_specs=pl.BlockSpec((r, 1), lambda i: (i, 0)),
        out_shape=jax.ShapeDtypeStruct((a, 1), jnp.int32),
        compiler_params=_cparams(("arbitrary",), 32),
        name="dest",
    )(flat_e, rank, pstarts_row)


DMA_UNROLL = 16


def _experts_body(ord_ref, elist_ref, src_ref, cnt_ref, h_hbm, w1_hbm, w3_hbm, w2_hbm, y_ref,
                  x_scr, w1_scr, w3_scr, w2_scr, sem, wsem):
    i = pl.program_id(0)
    nrows = x_scr.shape[1]
    nused = cnt_ref[0]
    nexp = cnt_ref[1]
    k = ord_ref[i]
    first = (i == 0) | (k != ord_ref[jnp.maximum(i - 1, 0)])

    def weights(kk, act):
        e = elist_ref[kk]
        slot = kk % 2
        act(pltpu.make_async_copy(w1_hbm.at[e], w1_scr.at[slot], wsem.at[slot, 0]))
        act(pltpu.make_async_copy(w3_hbm.at[e], w3_scr.at[slot], wsem.at[slot, 1]))
        act(pltpu.make_async_copy(w2_hbm.at[e], w2_scr.at[slot], wsem.at[slot, 2]))

    def gather(blk, act):
        slot = blk % 2
        base = blk * nrows

        def body(r, carry):
            act(pltpu.make_async_copy(h_hbm.at[pl.ds(src_ref[base + r], 1)],
                                      x_scr.at[slot, pl.ds(r, 1)], sem.at[slot]))
            return carry

        lax.fori_loop(0, nrows, body, 0, unroll=DMA_UNROLL)

    @pl.when(i == 0)
    def _():
        weights(k, lambda cp: cp.start())
        gather(i, lambda cp: cp.start())

    @pl.when(i + 1 < nused)
    def _():
        gather(i + 1, lambda cp: cp.start())

    @pl.when((i < nused) & first & (k + 1 < nexp))
    def _():
        weights(k + 1, lambda cp: cp.start())

    @pl.when((i < nused) & first)
    def _():
        weights(k, lambda cp: cp.wait())

    @pl.when(i < nused)
    def _():
        gather(i, lambda cp: cp.wait())
        wslot = k % 2
        xb = x_scr[i % 2].astype(BF16)
        a = jnp.dot(xb, w1_scr[wslot].astype(BF16), preferred_element_type=F32)
        b = jnp.dot(xb, w3_scr[wslot].astype(BF16), preferred_element_type=F32)
        act = (a * jax.nn.sigmoid(a) * b).astype(BF16)
        y_ref[...] = jnp.dot(act, w2_scr[wslot].astype(BF16), preferred_element_type=F32)

    @pl.when(i >= nused)
    def _():
        y_ref[...] = jnp.zeros_like(y_ref)


def _experts(block_ord, elist, src_tok, counts2, h2, w1, w3, w2):
    nblocks = block_ord.shape[0]
    d = h2.shape[1]
    ff = w1.shape[2]
    rb = EXPERT_ROW_BLOCK
    grid_spec = pltpu.PrefetchScalarGridSpec(
        num_scalar_prefetch=4,
        grid=(nblocks,),
        in_specs=[pl.BlockSpec(memory_space=pl.ANY)] * 4,
        out_specs=pl.BlockSpec((rb, d), lambda i, *_: (i, 0)),
        scratch_shapes=[pltpu.VMEM((2, rb, d), F32),
                        pltpu.VMEM((2, d, ff), F32), pltpu.VMEM((2, d, ff), F32),
                        pltpu.VMEM((2, ff, d), F32),
                        pltpu.SemaphoreType.DMA((2,)), pltpu.SemaphoreType.DMA((2, 3))],
    )
    return pl.pallas_call(
        _experts_body,
        grid_spec=grid_spec,
        out_shape=jax.ShapeDtypeStruct((nblocks * rb, d), F32),
        compiler_params=_cparams(("arbitrary",), 56),
        name="experts",
    )(block_ord, elist, src_tok, counts2, h2, w1, w3, w2)


def _combine_body(dest_ref, y_hbm, gate_ref, x1_ref, mod_ref, g_ref, o_ref, buf_scr, sem):
    i = pl.program_id(0)
    tm = x1_ref.shape[0]

    def gather(step, act):
        slot = step % 2
        base = step * tm * TOP_K

        def body(r, carry):
            for k in range(TOP_K):
                act(pltpu.make_async_copy(y_hbm.at[pl.ds(dest_ref[base + TOP_K * r + k], 1)],
                                          buf_scr.at[slot, k, pl.ds(r, 1)], sem.at[slot]))
            return carry

        lax.fori_loop(0, tm, body, 0, unroll=DMA_UNROLL // TOP_K)

    @pl.when(i == 0)
    def _():
        gather(i, lambda cp: cp.start())

    @pl.when(i + 1 < pl.num_programs(0))
    def _():
        gather(i + 1, lambda cp: cp.start())

    gather(i, lambda cp: cp.wait())
    slot = i % 2
    gate = gate_ref[...]
    y = buf_scr[slot, 0] * gate[:, 0:1] + buf_scr[slot, 1] * gate[:, 1:2]
    o_ref[...] = x1_ref[...] + mod_ref[0, 5:6, :] * _rms(y, g_ref[...])


def _combine(dest, ybuf, gate, x1, mod, g_post, seq):
    t, d = x1.shape
    tm = 256
    per_batch = seq // tm
    grid_spec = pltpu.PrefetchScalarGridSpec(
        num_scalar_prefetch=1,
        grid=(t // tm,),
        in_specs=[pl.BlockSpec(memory_space=pl.ANY),
                  pl.BlockSpec((tm, TOP_K), lambda i, dst: (i, 0)),
                  pl.BlockSpec((tm, d), lambda i, dst: (i, 0)),
                  pl.BlockSpec((1, 6, d), lambda i, dst: (i // per_batch, 0, 0)),
                  pl.BlockSpec((1, d), lambda i, dst: (0, 0))],
        out_specs=pl.BlockSpec((tm, d), lambda i, dst: (i, 0)),
        scratch_shapes=[pltpu.VMEM((2, TOP_K, tm, d), F32), pltpu.SemaphoreType.DMA((2,))],
    )
    return pl.pallas_call(
        _combine_body,
        grid_spec=grid_spec,
        out_shape=jax.ShapeDtypeStruct((t, d), F32),
        compiler_params=_cparams(("arbitrary",), 40),
        name="combine",
    )(dest, ybuf, gate, x1, mod, g_post)


def _layer(x, c, positions, w_ada, b_ada, g_mix_pre, g_mix_post, g_ffn_pre, g_ffn_post,
           w_in, conv_w, conv_b, filt_w1, filt_b1, filt_w2, filt_b2, filt_w3, filt_freq,
           hyena_skip, w_branch_gate, b_branch_gate, w_hy_o, w_at_o, w_out,
           w_group, b_group, w_expert, b_expert, w1_exp, w3_exp, w2_exp):
    bsz, seq, d = x.shape
    t = bsz * seq
    width = hyena_skip.shape[1]
    row = lambda v: v.reshape(1, -1)

    c_pad = jnp.pad(c, ((0, 8 - bsz), (0, 0)))
    mod = _adaln(c_pad, w_ada, row(b_ada))[:bsz].reshape(bsz, 6, d)

    x2d = x.reshape(t, d)
    n_gate = w_branch_gate.shape[1]
    n_cat = n_gate + -(-w_in.shape[1] // IN_PROJ_TN) * IN_PROJ_TN
    b_cat = jnp.concatenate([b_branch_gate, jnp.zeros((n_cat - n_gate,), F32)]).reshape(1, -1)
    pg = _in_proj(x2d, mod, row(g_mix_pre), w_branch_gate.astype(BF16), w_in.astype(BF16), b_cat, seq)
    pg3 = pg.reshape(bsz, seq, -1)

    hidden = filt_w2.shape[0]
    nfeat = -(-filt_w1.shape[0] // 8) * 8
    bands = np.zeros((nfeat, 1), np.float32)
    band_vals = np.linspace(1e-4, FILTER_BANDS - 1, FILTER_BANDS, dtype=np.float32)
    bands[1:1 + FILTER_BANDS, 0] = band_vals
    bands[1 + FILTER_BANDS:1 + 2 * FILTER_BANDS, 0] = band_vals
    w1p = jnp.pad(filt_w1, ((0, nfeat - filt_w1.shape[0]), (0, 0))).T
    col = lambda v: v.reshape(-1, 1)
    max_decay = math.log(DECAY_TARGET) / FAST_DECAY_PCT
    min_decay = math.log(DECAY_TARGET) / SLOW_DECAY_PCT
    deltas = jnp.abs(jnp.linspace(min_decay, max_decay, width, dtype=F32)).reshape(1, -1)
    uw = _filters(seq, width, jnp.asarray(bands), w1p, col(filt_b1), filt_w2.T, col(filt_b2),
                  col(filt_freq), deltas, filt_w3)
    s1, s1i, s2, s2i = _fft_tables(FFT_N1, 2 * seq // FFT_N1)
    kspec = _spectra(uw, s1, s2)
    y_hy = _hyena(pg3, n_gate, conv_w, row(conv_b), hyena_skip, kspec, s1, s1i, s2, s2i, width)

    half = ROT_DIM // 2
    inv_freq = np.power(ROPE_THETA, -2.0 * np.arange(half, dtype=np.float32) / ROT_DIM).astype(np.float32)
    freq_row = np.zeros((1, LANES), np.float32)
    freq_row[0, :half] = inv_freq
    freq_row[0, half:ROT_DIM] = inv_freq
    sign_row = np.zeros((1, LANES), np.float32)
    sign_row[0, :half] = -1.0
    sign_row[0, half:ROT_DIM] = 1.0
    y_at = _attention(pg3, positions.reshape(bsz, seq, 1), jnp.asarray(freq_row), jnp.asarray(sign_row),
                      n_gate + 3 * width)

    w_r = jnp.concatenate([w_group, jnp.transpose(w_expert, (1, 0, 2)).reshape(d, N_EXPERTS)], axis=1)
    w_r = jnp.pad(w_r, ((0, 0), (0, LANES - w_r.shape[1])))
    w_r_hi = w_r.astype(BF16)
    w_r_lo = (w_r - w_r_hi.astype(F32)).astype(BF16)
    b_r = jnp.pad(jnp.concatenate([b_group, b_expert.reshape(-1)]), (0, LANES - N_EXPERT_GROUPS - N_EXPERTS))
    x1, h2, logits = _mix_out(y_hy.reshape(t, width), y_at.reshape(t, -1), pg, x2d, mod,
                              row(g_mix_post), row(g_ffn_pre), w_hy_o.astype(BF16), w_at_o.astype(BF16),
                              w_out.astype(BF16), w_r_hi, w_r_lo, b_r.reshape(1, -1), seq)

    eid, gate = _route(logits)
    flat_e = eid.reshape(t * TOP_K, 1)
    rank, counts = _rank(flat_e)
    counts = counts[0, :N_EXPERTS]
    rb = EXPERT_ROW_BLOCK
    padded = (counts + rb - 1) // rb * rb
    pends = jnp.cumsum(padded)
    pstarts = pends - padded
    pstarts_row = jnp.pad(pstarts.astype(F32), (0, LANES - N_EXPERTS)).reshape(1, LANES)
    dest = _dest(flat_e, rank, pstarts_row)[:, 0]
    n_blocks = t * TOP_K // rb + N_EXPERTS
    block_start = jnp.arange(n_blocks, dtype=jnp.int32) * rb
    block_e = jnp.minimum(jnp.searchsorted(pends, block_start, side='right'), N_EXPERTS - 1).astype(jnp.int32)
    tok = jnp.arange(t * TOP_K, dtype=jnp.int32) // TOP_K
    src_tok = (jnp.arange(n_blocks * rb, dtype=jnp.int32) % t).at[dest].set(tok)
    has_rows = jnp.cumsum((counts > 0).astype(jnp.int32))
    elist = jnp.minimum(jnp.searchsorted(has_rows, jnp.arange(1, N_EXPERTS + 1, dtype=jnp.int32), side='left'),
                        N_EXPERTS - 1).astype(jnp.int32)
    block_ord = (has_rows - 1)[block_e].astype(jnp.int32)
    counts2 = jnp.stack([pends[-1] // rb, has_rows[-1]]).astype(jnp.int32)

    ybuf = _experts(block_ord, elist, src_tok, counts2, h2, w1_exp, w3_exp, w2_exp)
    out = _combine(dest, ybuf, gate, x1, mod, row(g_ffn_post), seq)
    return out.reshape(bsz, seq, d)


def kernel(x, c, positions, w_ada, b_ada, g_mix_pre, g_mix_post, g_ffn_pre, g_ffn_post, w_in, conv_w, conv_b, filt_w1, filt_b1, filt_w2, filt_b2, filt_w3, filt_freq, hyena_skip, w_branch_gate, b_branch_gate, w_hy_o, w_at_o, w_out, w_group, b_group, w_expert, b_expert, w1_exp, w3_exp, w2_exp):
    depth = w_ada.shape[0]
    for l in range(depth):
        x = _layer(x, c, positions, w_ada[l], b_ada[l], g_mix_pre[l], g_mix_post[l], g_ffn_pre[l],
                   g_ffn_post[l], w_in[l], conv_w[l], conv_b[l], filt_w1[l], filt_b1[l], filt_w2[l],
                   filt_b2[l], filt_w3[l], filt_freq[l], hyena_skip[l], w_branch_gate[l],
                   b_branch_gate[l], w_hy_o[l], w_at_o[l], w_out[l], w_group[l], b_group[l],
                   w_expert[l], b_expert[l], w1_exp[l], w3_exp[l], w2_exp[l])
    return x
```

```python
import functools
import math

import numpy as np
import jax
import jax.numpy as jnp
from jax import lax
from jax.experimental import pallas as pl
from jax.experimental.pallas import tpu as pltpu

F32 = jnp.float32
BF16 = jnp.bfloat16

LANES = 128
MIB = 1024 * 1024

RMS_EPS = 1e-6
NEG_INF = -1e30

HEAD_DIM = 128
ROT_DIM = HEAD_DIM // 4
ROPE_THETA = 500000.0
ATTN_GROUPS = ((128, 1), (512, 4), (2048, 16))
HEADS_PER_GROUP = 4
N_ATTN_HEADS = HEADS_PER_GROUP * len(ATTN_GROUPS)

FILTER_BANDS = 16
DECAY_TARGET = 1e-2
FAST_DECAY_PCT = 0.3
SLOW_DECAY_PCT = 1.5

N_EXPERT_GROUPS = 8
EXPERTS_PER_GROUP = 8
N_EXPERTS = N_EXPERT_GROUPS * EXPERTS_PER_GROUP
TOP_K = 2
EXPERT_ROW_BLOCK = 128

FFT_N1 = 128
PITCH = FFT_N1 + 8
FFT_UNROLL = 32
FFT_MID_UNROLL = 16


def _cparams(sem, vmem_mib):
    return pltpu.CompilerParams(dimension_semantics=sem, vmem_limit_bytes=vmem_mib * MIB)


def _rms(x, g):
    return x * lax.rsqrt(jnp.mean(x * x, axis=-1, keepdims=True) + RMS_EPS) * g


def _adaln_body(c_ref, w_ref, b_ref, o_ref):
    c = c_ref[...]
    cond = c * jax.nn.sigmoid(c)
    o_ref[...] = jnp.dot(cond.astype(BF16), w_ref[...].astype(BF16),
                         preferred_element_type=F32) + b_ref[...]


def _adaln(c_pad, w_ada, b_ada):
    rows, d = c_pad.shape
    n = w_ada.shape[1]
    tn = 1024
    return pl.pallas_call(
        _adaln_body,
        grid=(n // tn,),
        in_specs=[pl.BlockSpec((rows, d), lambda j: (0, 0)),
                  pl.BlockSpec((d, tn), lambda j: (0, j)),
                  pl.BlockSpec((1, tn), lambda j: (0, j))],
        out_specs=pl.BlockSpec((rows, tn), lambda j: (0, j)),
        out_shape=jax.ShapeDtypeStruct((rows, n), F32),
        compiler_params=_cparams(("arbitrary",), 40),
        name="adaln",
    )(c_pad, w_ada, b_ada)


IN_PROJ_TM = 1024
IN_PROJ_TN = 1024
IN_PROJ_SUB_M = 256
IN_PROJ_SUB_N = 512


def _in_proj_body(n_gate, last_cols, x_ref, mod_ref, g_ref, wg_ref, wi_ref, b_ref, o_ref, h_scr):
    j = pl.program_id(1)
    last = pl.num_programs(1) - 1

    tm, tn = o_ref.shape

    @pl.when(j == 0)
    def _():
        def norm(r, carry):
            rows = pl.ds(pl.multiple_of(r * IN_PROJ_SUB_M, IN_PROJ_SUB_M), IN_PROJ_SUB_M)
            h = _rms(x_ref[rows, :], g_ref[...]) * (1.0 + mod_ref[0, 1:2, :]) + mod_ref[0, 0:1, :]
            h_scr[rows, :] = h.astype(BF16)
            return carry
        lax.fori_loop(0, tm // IN_PROJ_SUB_M, norm, 0)


    def tile(w_ref, gate, ncols):
        for mi in range(tm // IN_PROJ_SUB_M):
            rows = slice(mi * IN_PROJ_SUB_M, (mi + 1) * IN_PROJ_SUB_M)
            for ni in range(tn // IN_PROJ_SUB_N):
                cols = slice(ni * IN_PROJ_SUB_N, (ni + 1) * IN_PROJ_SUB_N)
                if ni * IN_PROJ_SUB_N >= ncols:
                    o_ref[rows, cols] = jnp.zeros((IN_PROJ_SUB_M, IN_PROJ_SUB_N), o_ref.dtype)
                    continue
                acc = jnp.dot(h_scr[rows, :], w_ref[:, cols], preferred_element_type=F32) + b_ref[:, cols]
                o_ref[rows, cols] = (jax.nn.sigmoid(acc) if gate else acc).astype(o_ref.dtype)

    @pl.when(j < n_gate)
    def _():
        tile(wg_ref, True, tn)

    @pl.when((j >= n_gate) & (j < last))
    def _():
        tile(wi_ref, False, tn)

    @pl.when(j == last)
    def _():
        tile(wi_ref, False, last_cols)


def _in_proj(x2d, mod, g_pre, w_gate, w_in, b_cat, seq):
    t, d = x2d.shape
    tm, tn = IN_PROJ_TM, IN_PROJ_TN
    ng = w_gate.shape[1] // tn
    ni = pl.cdiv(w_in.shape[1], tn)
    last_cols = w_in.shape[1] - (ni - 1) * tn
    assert w_gate.shape[1] % tn == 0 and last_cols % IN_PROJ_SUB_N == 0
    per_batch = seq // tm
    return pl.pallas_call(
        functools.partial(_in_proj_body, ng, last_cols),
        grid=(t // tm, ng + ni),
        in_specs=[pl.BlockSpec((tm, d), lambda i, j: (i, 0)),
                  pl.BlockSpec((1, 6, d), lambda i, j: (i // per_batch, 0, 0)),
                  pl.BlockSpec((1, d), lambda i, j: (0, 0)),
                  pl.BlockSpec((d, tn), lambda i, j: (0, jnp.minimum(j, ng - 1))),
                  pl.BlockSpec((d, tn), lambda i, j: (0, jnp.clip(j - ng, 0, ni - 1))),
                  pl.BlockSpec((1, tn), lambda i, j: (0, j))],
        out_specs=pl.BlockSpec((tm, tn), lambda i, j: (i, j)),
        out_shape=jax.ShapeDtypeStruct((t, (ng + ni) * tn), BF16),
        scratch_shapes=[pltpu.VMEM((tm, d), BF16)],
        compiler_params=_cparams(("arbitrary", "arbitrary"), 56),
        name="in_proj",
    )(x2d, mod, g_pre, w_gate, w_in, b_cat)


def _filters_body(seq, band_ref, w1_ref, b1_ref, w2_ref, b2_ref, fr_ref, dl_ref,
                  w3a_ref, w3b_ref, o_ref, hid_scr):
    i = pl.program_id(0)
    j = pl.program_id(1)
    tl = hid_scr.shape[0]
    row = (lax.broadcasted_iota(jnp.int32, (tl, 1), 0) + i * tl).astype(F32)

    @pl.when((j == 0) & (pl.program_id(2) == 0))
    def _():
        nfeat = band_ref.shape[0]
        pos = (lax.broadcasted_iota(jnp.int32, (1, tl), 1) + i * tl).astype(F32)
        feat = lax.broadcasted_iota(jnp.int32, (nfeat, tl), 0)
        ang = band_ref[...] * (2.0 * math.pi * pos / seq)
        feats = jnp.where(feat == 0, pos / (seq - 1.0),
                          jnp.where(feat <= FILTER_BANDS, jnp.cos(ang),
                                    jnp.where(feat <= 2 * FILTER_BANDS, -jnp.sin(ang), 0.0)))
        hi = lax.Precision.HIGHEST
        fr = fr_ref[...]
        hid = jnp.sin(fr * (jnp.dot(w1_ref[...], feats, precision=hi, preferred_element_type=F32)
                            + b1_ref[...]))
        hid = jnp.sin(fr * (jnp.dot(w2_ref[...], hid, precision=hi, preferred_element_type=F32)
                            + b2_ref[...]))
        hid_scr[...] = hid.T

    hi = lax.Precision.HIGHEST
    hid = hid_scr[...]
    decay = jnp.exp(-(row / (seq - 1.0)) * dl_ref[...])
    hf = jnp.dot(hid, w3a_ref[...], precision=hi, preferred_element_type=F32) * decay
    hb = jnp.dot(hid, w3b_ref[...], precision=hi, preferred_element_type=F32) * decay
    hb = jnp.where(row == 0.0, 0.0, hb)
    o_ref[0, 0] = hf + hb
    o_ref[0, 1] = hf - hb


def _filters(seq, width, bands, w1p, b1, w2, b2, freq, deltas, w3):
    tl, tc = 512, 512
    nct = width // tc
    hidden = w2.shape[0]
    nfeat = bands.shape[0]
    const = lambda i, j, o: (0, 0)
    return pl.pallas_call(
        functools.partial(_filters_body, float(seq)),
        grid=(seq // tl, nct, 2),
        in_specs=[pl.BlockSpec((nfeat, 1), const),
                  pl.BlockSpec((hidden, nfeat), const),
                  pl.BlockSpec((hidden, 1), const),
                  pl.BlockSpec((hidden, hidden), const),
                  pl.BlockSpec((hidden, 1), const),
                  pl.BlockSpec((hidden, 1), const),
                  pl.BlockSpec((1, tc), lambda i, j, o: (0, j)),
                  pl.BlockSpec((hidden, tc), lambda i, j, o: (0, (2 * o) * nct + j)),
                  pl.BlockSpec((hidden, tc), lambda i, j, o: (0, (2 * o + 1) * nct + j))],
        out_specs=pl.BlockSpec((1, 2, tl, tc), lambda i, j, o: (o, 0, i, j)),
        out_shape=jax.ShapeDtypeStruct((2, 2, seq, width), F32),
        scratch_shapes=[pltpu.VMEM((tl, hidden), F32)],
        compiler_params=_cparams(("arbitrary", "arbitrary", "arbitrary"), 32),
        name="filters",
    )(bands, w1p, b1, w2, b2, freq, deltas, w3, w3)


def _fft_tables(n1, n2):
    n = n1 * n2
    q = np.arange(n2)[:, None]
    b = np.arange(n2 // 2)[None, :]
    a = np.arange(n1)[:, None, None]
    ang = -2.0 * np.pi * (a * q[None] / n + (q * b)[None] / n2)
    stage1 = np.concatenate([np.cos(ang), np.sin(ang)], axis=1)
    stage1_inv = np.transpose(stage1, (0, 2, 1)) / n
    p = np.arange(n1)
    ang2 = -2.0 * np.pi * np.outer(p, p) / n1
    fre, fim = np.cos(ang2), np.sin(ang2)
    stage2 = np.block([[fre, -fim], [fim, fre]])
    stage2_inv = np.block([[fre, fim], [-fim, fre]])
    as_bf16 = lambda m: jnp.asarray(m, dtype=F32).astype(BF16)
    return as_bf16(stage1), as_bf16(stage1_inv), as_bf16(stage2), as_bf16(stage2_inv)


def _halves(ref, rows):
    return jnp.concatenate([ref[0, rows, :], ref[1, rows, :]], axis=1)


def _fft_stage1(z_ref, s1_ref, gre_ref, gim_ref):
    n1, two_n2, n2h = s1_ref.shape
    n2 = two_n2 // 2

    def step(a, carry):
        zrows = _halves(z_ref, pl.ds(a, n2h, stride=PITCH))
        g = jnp.dot(s1_ref[a], zrows.astype(BF16), preferred_element_type=F32)
        for h in range(2):
            cols = slice(h * LANES, (h + 1) * LANES)
            gre_ref[h, pl.ds(a, n2, stride=PITCH), :] = g[:n2, cols]
            gim_ref[h, pl.ds(a, n2, stride=PITCH), :] = g[n2:, cols]
        return carry

    lax.fori_loop(0, n1, step, 0, unroll=FFT_UNROLL)


def _fft_stage1_inv(gre_ref, gim_ref, s1i_ref, y_ref):
    n1, n2h, two_n2 = s1i_ref.shape
    n2 = two_n2 // 2

    def step(a, carry):
        rows = pl.ds(a, n2, stride=PITCH)
        hs = jnp.concatenate([_halves(gre_ref, rows), _halves(gim_ref, rows)], axis=0)
        y = jnp.dot(s1i_ref[a], hs.astype(BF16), preferred_element_type=F32)
        for h in range(2):
            y_ref[h, pl.ds(a, n2h, stride=PITCH), :] = y[:, h * LANES:(h + 1) * LANES]
        return carry

    lax.fori_loop(0, n1, step, 0, unroll=FFT_UNROLL)


def _stage2_block(gre_ref, gim_ref, s2_ref, q):
    n1 = s2_ref.shape[0] // 2
    rows = pl.ds(pl.multiple_of(q * PITCH, 8), n1)
    gs = jnp.concatenate([_halves(gre_ref, rows), _halves(gim_ref, rows)], axis=0)
    x = jnp.dot(s2_ref[...], gs.astype(BF16), preferred_element_type=F32)
    return x[:n1], x[n1:]


def _spectra_body(uw_ref, s1_ref, s2_ref, k_ref, z_scr, gre_scr, gim_scr):
    n1 = s2_ref.shape[0] // 2
    n2 = s1_ref.shape[1] // 2
    for h in range(2):
        for b in range(n2 // 2):
            z_scr[h, pl.ds(b * PITCH, n1), :] = uw_ref[0, h, pl.ds(b * n1, n1), :]
    _fft_stage1(z_scr, s1_ref, gre_scr, gim_scr)

    def step(q, carry):
        xre, xim = _stage2_block(gre_scr, gim_scr, s2_ref, q)
        rows = pl.ds(pl.multiple_of(q * n1, 8), n1)
        k_ref[0, 0, rows, :] = xre[:, :LANES]
        k_ref[0, 1, rows, :] = xim[:, LANES:]
        return carry

    lax.fori_loop(0, n2, step, 0, unroll=FFT_MID_UNROLL)


def _spectra(uw, s1, s2):
    _, _, seq, width = uw.shape
    n1 = FFT_N1
    n2 = 2 * seq // n1
    const3 = lambda j, o: (0, 0, 0)
    return pl.pallas_call(
        _spectra_body,
        grid=(width // LANES, 2),
        in_specs=[pl.BlockSpec((1, 2, seq, LANES), lambda j, o: (o, 0, 0, j)),
                  pl.BlockSpec(s1.shape, const3),
                  pl.BlockSpec(s2.shape, lambda j, o: (0, 0))],
        out_specs=pl.BlockSpec((1, 2, 2 * seq, LANES), lambda j, o: (o, 0, 0, j)),
        out_shape=jax.ShapeDtypeStruct((2, 2, 2 * seq, width), F32),
        scratch_shapes=[pltpu.VMEM((2, (n2 // 2) * PITCH, LANES), F32),
                        pltpu.VMEM((2, n2 * PITCH, LANES), F32),
                        pltpu.VMEM((2, n2 * PITCH, LANES), F32)],
        compiler_params=_cparams(("arbitrary", "arbitrary"), 56),
        name="spectra",
    )(uw, s1, s2)


def _short_conv_block(p_ref, b, i, nblk, w_ref, bias_ref):
    n1 = FFT_N1
    start = pl.multiple_of(i * n1, n1)
    cur = p_ref[b, pl.ds(start, n1), :].astype(F32)
    before = p_ref[b, pl.ds(pl.multiple_of(jnp.maximum(start - 16, 0), 16), 16), :].astype(F32)
    after = p_ref[b, pl.ds(pl.multiple_of(jnp.minimum(start + n1, (nblk - 1) * n1), 16), 16), :].astype(F32)
    last_prev = before[15:16] * jnp.where(i > 0, 1.0, 0.0).astype(F32)
    first_next = after[0:1] * jnp.where(i < nblk - 1, 1.0, 0.0).astype(F32)
    row = lax.broadcasted_iota(jnp.int32, (n1, 1), 0)
    prev = jnp.where(row == 0, last_prev, pltpu.roll(cur, 1, 0))
    nxt = jnp.where(row == n1 - 1, first_next, pltpu.roll(cur, n1 - 1, 0))
    return prev * w_ref[0:1, :] + cur * w_ref[1:2, :] + nxt * w_ref[2:3, :] + bias_ref[...]


def _hyena_body(pv_ref, px1_ref, px2_ref, cwv_ref, cw1_ref, cw2_ref, cbv_ref, cb1_ref, cb2_ref,
                skip_ref, k_ref, s1_ref, s1i_ref, s2_ref, s2i_ref, o_ref,
                z_scr, y_scr, gre_scr, gim_scr):
    c = pl.program_id(1)
    g = pl.program_id(2)
    ngroups = pl.num_programs(2)
    n1 = FFT_N1
    n2 = s1_ref.shape[1] // 2
    nblk = n2 // 2
    per_group = n2 // ngroups

    @pl.when(g == 0)
    def _():
        @pl.when(c == 0)
        def _():
            def fill(i, carry):
                for b in range(2):
                    z_scr[b, pl.ds(pl.multiple_of(i * PITCH, 8), n1), :] = _short_conv_block(
                        pv_ref, b, i, nblk, cwv_ref, cbv_ref)
                return carry
            lax.fori_loop(0, nblk, fill, 0)

        _fft_stage1(z_scr, s1_ref, gre_scr, gim_scr)

    def mid(ql, carry):
        q = g * per_group + ql
        xre, xim = _stage2_block(gre_scr, gim_scr, s2_ref, q)
        krows = pl.ds(pl.multiple_of(ql * n1, 8), n1)
        kre = k_ref[0, 0, krows, :]
        kim = k_ref[0, 1, krows, :]
        kre = jnp.concatenate([kre, kre], axis=1)
        kim = jnp.concatenate([kim, kim], axis=1)
        ys = jnp.concatenate([xre * kre - xim * kim, xre * kim + xim * kre], axis=0)
        hh = jnp.dot(s2i_ref[...], ys.astype(BF16), preferred_element_type=F32)
        rows = pl.ds(pl.multiple_of(q * PITCH, 8), n1)
        for b in range(2):
            cols = slice(b * LANES, (b + 1) * LANES)
            gre_scr[b, rows, :] = hh[:n1, cols]
            gim_scr[b, rows, :] = hh[n1:, cols]
        return carry

    lax.fori_loop(0, per_group, mid, 0, unroll=FFT_MID_UNROLL)

    @pl.when(g == ngroups - 1)
    def _():
        _fft_stage1_inv(gre_scr, gim_scr, s1i_ref, y_scr)

        def post(px_ref, cw_ref, cb_ref, order, store):
            def blk(i, carry):
                rows = pl.ds(pl.multiple_of(i * PITCH, 8), n1)
                for b in range(2):
                    zb = z_scr[b, rows, :]
                    gate = _short_conv_block(px_ref, b, i, nblk, cw_ref, cb_ref)
                    store(b, i, rows, gate * (y_scr[b, rows, :] + zb * skip_ref[order:order + 1, :]))
                return carry
            lax.fori_loop(0, nblk, blk, 0)

        @pl.when(c == 0)
        def _():
            def store(b, i, rows, val):
                z_scr[b, rows, :] = val
            post(px1_ref, cw1_ref, cb1_ref, 0, store)

        @pl.when(c == 1)
        def _():
            def store(b, i, rows, val):
                o_ref[b, pl.ds(pl.multiple_of(i * n1, n1), n1), :] = val.astype(o_ref.dtype)
            post(px2_ref, cw2_ref, cb2_ref, 1, store)


def _hyena(pg3, proj_col0, conv_w, conv_b, skip, kspec, s1, s1i, s2, s2i, width):
    bsz, seq, _ = pg3.shape
    assert bsz == 2
    n1 = FFT_N1
    n2 = 2 * seq // n1
    nct = width // LANES
    ngroups = 4
    krows = (n2 // ngroups) * n1
    col = lambda off: (lambda j, c, g: (0, 0, proj_col0 // LANES + off * nct + j))
    cw = lambda off: (lambda j, c, g: (0, off * nct + j))
    const3 = lambda j, c, g: (0, 0, 0)
    const2 = lambda j, c, g: (0, 0)
    return pl.pallas_call(
        _hyena_body,
        grid=(nct, 2, ngroups),
        in_specs=[pl.BlockSpec((2, seq, LANES), col(0)),
                  pl.BlockSpec((2, seq, LANES), col(1)),
                  pl.BlockSpec((2, seq, LANES), col(2)),
                  pl.BlockSpec((3, LANES), cw(0)),
                  pl.BlockSpec((3, LANES), cw(1)),
                  pl.BlockSpec((3, LANES), cw(2)),
                  pl.BlockSpec((1, LANES), cw(0)),
                  pl.BlockSpec((1, LANES), cw(1)),
                  pl.BlockSpec((1, LANES), cw(2)),
                  pl.BlockSpec((2, LANES), lambda j, c, g: (0, j)),
                  pl.BlockSpec((1, 2, krows, LANES), lambda j, c, g: (c, 0, g, j)),
                  pl.BlockSpec(s1.shape, const3),
                  pl.BlockSpec(s1i.shape, const3),
                  pl.BlockSpec(s2.shape, const2),
                  pl.BlockSpec(s2i.shape, const2)],
        out_specs=pl.BlockSpec((2, seq, LANES), lambda j, c, g: (0, 0, j)),
        out_shape=jax.ShapeDtypeStruct((2, seq, width), BF16),
        scratch_shapes=[pltpu.VMEM((2, (n2 // 2) * PITCH, LANES), F32),
                        pltpu.VMEM((2, (n2 // 2) * PITCH, LANES), F32),
                        pltpu.VMEM((2, n2 * PITCH, LANES), F32),
                        pltpu.VMEM((2, n2 * PITCH, LANES), F32)],
        compiler_params=_cparams(("arbitrary", "arbitrary", "arbitrary"), 56),
        name="hyena",
    )(pg3, pg3, pg3, conv_w, conv_w, conv_w, conv_b, conv_b, conv_b, skip, kspec, s1, s1i, s2, s2i)


ATTN_QBLK = 128
ATTN_UNROLL = 16


def _attention_body(pos_ref, freq_ref, sign_ref, *refs):
    qkv_refs = refs[:9]
    o_ref = refs[9]
    cos_scr, sin_scr, q_scr, k_scr, v_scr, og_scr, lse_scr = refs[10:]
    seq = q_scr.shape[0]
    chunk = 512
    nchunks = seq // chunk

    @pl.when(pl.program_id(1) == 0)
    def _():
        def trig(i, carry):
            rows = pl.ds(pl.multiple_of(i * chunk, chunk), chunk)
            ang = pos_ref[0, rows, :].astype(F32) * freq_ref[...]
            cos_scr[rows, :] = jnp.cos(ang)
            sin_scr[rows, :] = jnp.sin(ang) * sign_ref[...]
            return carry
        lax.fori_loop(0, nchunks, trig, 0)

    def rotate(src_ref, dst_ref):
        def body(i, carry):
            rows = pl.ds(pl.multiple_of(i * chunk, chunk), chunk)
            t = src_ref[0, rows, :].astype(F32)
            lane = lax.broadcasted_iota(jnp.int32, t.shape, 1)
            partner = jnp.where(lane < ROT_DIM // 2,
                                pltpu.roll(t, LANES - ROT_DIM // 2, 1), pltpu.roll(t, ROT_DIM // 2, 1))
            dst_ref[rows, :] = t * cos_scr[rows, :] + partner * sin_scr[rows, :]
            return carry
        lax.fori_loop(0, nchunks, body, 0, unroll=2)

    def widen(src_ref, dst_ref):
        def body(i, carry):
            rows = pl.ds(pl.multiple_of(i * chunk, chunk), chunk)
            dst_ref[rows, :] = src_ref[0, rows, :].astype(F32)
            return carry
        lax.fori_loop(0, nchunks, body, 0, unroll=2)

    scale = HEAD_DIM ** -0.5
    for gi, (window, dil) in enumerate(ATTN_GROUPS):
        rotate(qkv_refs[3 * gi], q_scr)
        rotate(qkv_refs[3 * gi + 1], k_scr)
        widen(qkv_refs[3 * gi + 2], v_scr)
        n = seq // dil
        half = window // (2 * dil)
        tk = min(n, ATTN_QBLK + 2 * half)
        blocks_per_res = n // ATTN_QBLK

        def block(u, carry, dil=dil, n=n, half=half, tk=tk, blocks_per_res=blocks_per_res, gi=gi):
            r = u // blocks_per_res
            m = u % blocks_per_res
            q0 = m * ATTN_QBLK
            k0 = jnp.clip(q0 - half, 0, n - tk)
            qrows = pl.ds(r + dil * q0, ATTN_QBLK, stride=dil)
            krows = pl.ds(r + dil * k0, tk, stride=dil)
            qb = q_scr[qrows, :].astype(BF16)
            kb = k_scr[krows, :].astype(BF16)
            vb = v_scr[krows, :].astype(BF16)
            s = lax.dot_general(qb, kb, (((1,), (1,)), ((), ())), preferred_element_type=F32) * scale
            qi = q0 + lax.broadcasted_iota(jnp.int32, (ATTN_QBLK, tk), 0)
            kj = k0 + lax.broadcasted_iota(jnp.int32, (ATTN_QBLK, tk), 1)
            s = jnp.where(jnp.abs(qi - kj) <= half, s, NEG_INF)
            mx = jnp.max(s, axis=-1, keepdims=True)
            p = jnp.exp(s - mx)
            l = jnp.sum(p, axis=-1, keepdims=True)
            o = jnp.dot(p.astype(BF16), vb, preferred_element_type=F32) / l
            og_scr[gi, qrows, :] = o
            lse_scr[gi, qrows, :] = jnp.broadcast_to(mx + jnp.log(l), (ATTN_QBLK, LANES))
            return carry

        lax.fori_loop(0, seq // ATTN_QBLK, block, 0, unroll=ATTN_UNROLL)

    def merge(i, carry):
        rows = pl.ds(pl.multiple_of(i * chunk, chunk), chunk)
        lses = [lse_scr[gi, rows, :] for gi in range(len(ATTN_GROUPS))]
        mx = functools.reduce(jnp.maximum, lses)
        ws = [jnp.exp(v - mx) for v in lses]
        den = functools.reduce(lambda a, b: a + b, ws)
        num = functools.reduce(lambda a, b: a + b,
                               [w * og_scr[gi, rows, :] for gi, w in enumerate(ws)])
        o_ref[0, rows, :] = (num / den).astype(o_ref.dtype)
        return carry

    lax.fori_loop(0, nchunks, merge, 0)


def _attention(pg3, pos3, freq_row, sign_row, qkv_col0):
    bsz, seq, _ = pg3.shape
    ng = len(ATTN_GROUPS)
    in_specs = [pl.BlockSpec((1, seq, 1), lambda b, h: (b, 0, 0)),
                pl.BlockSpec((1, LANES), lambda b, h: (0, 0)),
                pl.BlockSpec((1, LANES), lambda b, h: (0, 0))]
    for gi in range(ng):
        for which in range(3):
            base = qkv_col0 // LANES + which * N_ATTN_HEADS + gi * HEADS_PER_GROUP
            in_specs.append(pl.BlockSpec((1, seq, LANES), lambda b, h, base=base: (b, 0, base + h)))
    return pl.pallas_call(
        _attention_body,
        grid=(bsz, HEADS_PER_GROUP),
        in_specs=in_specs,
        out_specs=pl.BlockSpec((1, seq, LANES), lambda b, h: (b, 0, h)),
        out_shape=jax.ShapeDtypeStruct((bsz, seq, HEADS_PER_GROUP * HEAD_DIM), BF16),
        scratch_shapes=[pltpu.VMEM((seq, LANES), F32), pltpu.VMEM((seq, LANES), F32),
                        pltpu.VMEM((seq, LANES), F32), pltpu.VMEM((seq, LANES), F32),
                        pltpu.VMEM((seq, LANES), F32),
                        pltpu.VMEM((ng, seq, LANES), F32), pltpu.VMEM((ng, seq, LANES), F32)],
        compiler_params=_cparams(("arbitrary", "arbitrary"), 56),
        name="attention",
    )(pos3, freq_row, sign_row, *([pg3] * (3 * ng)))


def _mix_out_body(yhy_ref, yat_ref, ghy_ref, gat_ref, x_ref, mod_ref, gpost_ref, gpre_ref,
                  whyo_ref, wato_ref, wout_ref, wrhi_ref, wrlo_ref, br_ref, x1_ref, h2_ref, lg_ref):
    a = jnp.dot(yhy_ref[...], whyo_ref[...], preferred_element_type=F32)
    b = jnp.dot(yat_ref[...], wato_ref[...], preferred_element_type=F32)
    merged = ghy_ref[...].astype(F32) * a + gat_ref[...].astype(F32) * b
    y = jnp.dot(merged.astype(BF16), wout_ref[...], preferred_element_type=F32)
    x1 = x_ref[...] + mod_ref[0, 2:3, :] * _rms(y, gpost_ref[...])
    x1_ref[...] = x1
    h2 = _rms(x1, gpre_ref[...]) * (1.0 + mod_ref[0, 4:5, :]) + mod_ref[0, 3:4, :]
    h2_ref[...] = h2
    h2_hi = h2.astype(BF16)
    h2_lo = (h2 - h2_hi.astype(F32)).astype(BF16)
    w_hi = wrhi_ref[...]
    w_lo = wrlo_ref[...]
    lg_ref[...] = (jnp.dot(h2_hi, w_hi, preferred_element_type=F32)
                   + jnp.dot(h2_lo, w_hi, preferred_element_type=F32)
                   + jnp.dot(h2_hi, w_lo, preferred_element_type=F32)
                   + jnp.dot(h2_lo, w_lo, preferred_element_type=F32)) + br_ref[...]


def _mix_out(y_hy, y_at, pg, x2d, mod, g_post, g_pre, w_hy_o, w_at_o, w_out, w_r_hi, w_r_lo, b_r, seq):
    t, d = x2d.shape
    tm = 256
    per_batch = seq // tm
    gblk = 0
    const = lambda i: (0, 0)
    return pl.pallas_call(
        _mix_out_body,
        grid=(t // tm,),
        in_specs=[pl.BlockSpec((tm, y_hy.shape[1]), lambda i: (i, 0)),
                  pl.BlockSpec((tm, y_at.shape[1]), lambda i: (i, 0)),
                  pl.BlockSpec((tm, d), lambda i: (i, gblk)),
                  pl.BlockSpec((tm, d), lambda i: (i, gblk + 1)),
                  pl.BlockSpec((tm, d), lambda i: (i, 0)),
                  pl.BlockSpec((1, 6, d), lambda i: (i // per_batch, 0, 0)),
                  pl.BlockSpec((1, d), const),
                  pl.BlockSpec((1, d), const),
                  pl.BlockSpec(w_hy_o.shape, const),
                  pl.BlockSpec(w_at_o.shape, const),
                  pl.BlockSpec(w_out.shape, const),
                  pl.BlockSpec(w_r_hi.shape, const),
                  pl.BlockSpec(w_r_lo.shape, const),
                  pl.BlockSpec((1, LANES), const)],
        out_specs=[pl.BlockSpec((tm, d), lambda i: (i, 0)),
                   pl.BlockSpec((tm, d), lambda i: (i, 0)),
                   pl.BlockSpec((tm, LANES), lambda i: (i, 0))],
        out_shape=[jax.ShapeDtypeStruct((t, d), F32),
                   jax.ShapeDtypeStruct((t, d), F32),
                   jax.ShapeDtypeStruct((t, LANES), F32)],
        compiler_params=_cparams(("arbitrary",), 56),
        name="mix_out",
    )(y_hy, y_at, pg, pg, x2d, mod, g_post, g_pre, w_hy_o, w_at_o, w_out, w_r_hi, w_r_lo, b_r)


def _route_body(lg_ref, eid_ref, gate_ref):
    lg = lg_ref[...]
    lane = lax.broadcasted_iota(jnp.int32, lg.shape, 1)
    big = jnp.int32(1 << 20)

    def first_argmax(vals, mask):
        v = jnp.where(mask, vals, -jnp.inf)
        mx = jnp.max(v, axis=-1, keepdims=True)
        idx = jnp.min(jnp.where(mask & (v == mx), lane, big), axis=-1, keepdims=True)
        return mx, idx

    gmask = lane < N_EXPERT_GROUPS
    gmax, gidx = first_argmax(lg, gmask)
    gval = 1.0 / jnp.sum(jnp.where(gmask, jnp.exp(lg - gmax), 0.0), axis=-1, keepdims=True)
    lo = N_EXPERT_GROUPS + gidx * EXPERTS_PER_GROUP
    emask = (lane >= lo) & (lane < lo + EXPERTS_PER_GROUP)
    v1, i1 = first_argmax(lg, emask)
    v2, i2 = first_argmax(lg, emask & (lane != i1))
    e2 = jnp.exp(v2 - v1)
    p1 = 1.0 / (1.0 + e2)
    p2 = e2 / (1.0 + e2)
    eid = jnp.where(lane == 0, i1, i2) - N_EXPERT_GROUPS
    gate = gval * jnp.where(lane == 0, p1, p2)
    eid_ref[...] = eid[:, :TOP_K]
    gate_ref[...] = gate[:, :TOP_K]


def _route(logits):
    t = logits.shape[0]
    tm = 512
    return pl.pallas_call(
        _route_body,
        grid=(t // tm,),
        in_specs=[pl.BlockSpec((tm, LANES), lambda i: (i, 0))],
        out_specs=[pl.BlockSpec((tm, TOP_K), lambda i: (i, 0)),
                   pl.BlockSpec((tm, TOP_K), lambda i: (i, 0))],
        out_shape=[jax.ShapeDtypeStruct((t, TOP_K), jnp.int32),
                   jax.ShapeDtypeStruct((t, TOP_K), F32)],
        compiler_params=_cparams(("arbitrary",), 32),
        name="route",
    )(logits)


def _rank_body(e_ref, rank_ref, cnt_ref, carry_scr):
    i = pl.program_id(0)
    r = e_ref.shape[0]

    @pl.when(i == 0)
    def _():
        carry_scr[...] = jnp.zeros_like(carry_scr)

    lane = lax.broadcasted_iota(jnp.int32, (r, LANES), 1)
    onehot = (lane == e_ref[...]).astype(F32)
    tri = (lax.broadcasted_iota(jnp.int32, (r, r), 1)
           < lax.broadcasted_iota(jnp.int32, (r, r), 0)).astype(BF16)
    before = jnp.dot(tri, onehot.astype(BF16), preferred_element_type=F32) + carry_scr[0:1, :]
    rank_ref[...] = jnp.sum(onehot * before, axis=-1, keepdims=True).astype(jnp.int32)
    total = carry_scr[0:1, :] + jnp.sum(onehot, axis=0, keepdims=True)
    carry_scr[...] = jnp.broadcast_to(total, carry_scr.shape)
    cnt_ref[...] = jnp.broadcast_to(total, cnt_ref.shape).astype(jnp.int32)


def _rank(flat_e):
    a = flat_e.shape[0]
    r = 512
    return pl.pallas_call(
        _rank_body,
        grid=(a // r,),
        in_specs=[pl.BlockSpec((r, 1), lambda i: (i, 0))],
        out_specs=[pl.BlockSpec((r, 1), lambda i: (i, 0)),
                   pl.BlockSpec((8, LANES), lambda i: (0, 0))],
        out_shape=[jax.ShapeDtypeStruct((a, 1), jnp.int32),
                   jax.ShapeDtypeStruct((8, LANES), jnp.int32)],
        scratch_shapes=[pltpu.VMEM((8, LANES), F32)],
        compiler_params=_cparams(("arbitrary",), 32),
        name="rank",
    )(flat_e)


def _dest_body(e_ref, rank_ref, ps_ref, o_ref):
    lane = lax.broadcasted_iota(jnp.int32, (e_ref.shape[0], LANES), 1)
    first = jnp.sum(jnp.where(lane == e_ref[...], ps_ref[...], 0.0), axis=-1, keepdims=True)
    o_ref[...] = first.astype(jnp.int32) + rank_ref[...]


def _dest(flat_e, rank, pstarts_row):
    a = flat_e.shape[0]
    r = 2048
    return pl.pallas_call(
        _dest_body,
        grid=(a // r,),
        in_specs=[pl.BlockSpec((r, 1), lambda i: (i, 0)),
                  pl.BlockSpec((r, 1), lambda i: (i, 0)),
                  pl.BlockSpec((1, LANES), lambda i: (0, 0))],
        out_specs=pl.BlockSpec((r, 1), lambda i: (i, 0)),
        out_shape=jax.ShapeDtypeStruct((a, 1), jnp.int32),
        compiler_params=_cparams(("arbitrary",), 32),
        name="dest",
    )(flat_e, rank, pstarts_row)


DMA_UNROLL = 16


def _experts_body(ord_ref, elist_ref, src_ref, cnt_ref, h_hbm, w1_hbm, w3_hbm, w2_hbm, y_ref,
                  x_scr, w1_scr, w3_scr, w2_scr, sem, wsem):
    i = pl.program_id(0)
    nrows = x_scr.shape[1]
    nused = cnt_ref[0]
    nexp = cnt_ref[1]
    k = ord_ref[i]
    first = (i == 0) | (k != ord_ref[jnp.maximum(i - 1, 0)])

    def weights(kk, act):
        e = elist_ref[kk]
        slot = kk % 2
        act(pltpu.make_async_copy(w1_hbm.at[e], w1_scr.at[slot], wsem.at[slot, 0]))
        act(pltpu.make_async_copy(w3_hbm.at[e], w3_scr.at[slot], wsem.at[slot, 1]))
        act(pltpu.make_async_copy(w2_hbm.at[e], w2_scr.at[slot], wsem.at[slot, 2]))

    def gather(blk, act):
        slot = blk % 2
        base = blk * nrows

        def body(r, carry):
            act(pltpu.make_async_copy(h_hbm.at[pl.ds(src_ref[base + r], 1)],
                                      x_scr.at[slot, pl.ds(r, 1)], sem.at[slot]))
            return carry

        lax.fori_loop(0, nrows, body, 0, unroll=DMA_UNROLL)

    @pl.when(i == 0)
    def _():
        weights(k, lambda cp: cp.start())
        gather(i, lambda cp: cp.start())

    @pl.when(i + 1 < nused)
    def _():
        gather(i + 1, lambda cp: cp.start())

    @pl.when((i < nused) & first & (k + 1 < nexp))
    def _():
        weights(k + 1, lambda cp: cp.start())

    @pl.when((i < nused) & first)
    def _():
        weights(k, lambda cp: cp.wait())

    @pl.when(i < nused)
    def _():
        gather(i, lambda cp: cp.wait())
        wslot = k % 2
        xb = x_scr[i % 2].astype(BF16)
        a = jnp.dot(xb, w1_scr[wslot].astype(BF16), preferred_element_type=F32)
        b = jnp.dot(xb, w3_scr[wslot].astype(BF16), preferred_element_type=F32)
        act = (a * jax.nn.sigmoid(a) * b).astype(BF16)
        y_ref[...] = jnp.dot(act, w2_scr[wslot].astype(BF16), preferred_element_type=F32)

    @pl.when(i >= nused)
    def _():
        y_ref[...] = jnp.zeros_like(y_ref)


def _experts(block_ord, elist, src_tok, counts2, h2, w1, w3, w2):
    nblocks = block_ord.shape[0]
    d = h2.shape[1]
    ff = w1.shape[2]
    rb = EXPERT_ROW_BLOCK
    grid_spec = pltpu.PrefetchScalarGridSpec(
        num_scalar_prefetch=4,
        grid=(nblocks,),
        in_specs=[pl.BlockSpec(memory_space=pl.ANY)] * 4,
        out_specs=pl.BlockSpec((rb, d), lambda i, *_: (i, 0)),
        scratch_shapes=[pltpu.VMEM((2, rb, d), F32),
                        pltpu.VMEM((2, d, ff), F32), pltpu.VMEM((2, d, ff), F32),
                        pltpu.VMEM((2, ff, d), F32),
                        pltpu.SemaphoreType.DMA((2,)), pltpu.SemaphoreType.DMA((2, 3))],
    )
    return pl.pallas_call(
        _experts_body,
        grid_spec=grid_spec,
        out_shape=jax.ShapeDtypeStruct((nblocks * rb, d), F32),
        compiler_params=_cparams(("arbitrary",), 56),
        name="experts",
    )(block_ord, elist, src_tok, counts2, h2, w1, w3, w2)


def _combine_body(dest_ref, y_hbm, gate_ref, x1_ref, mod_ref, g_ref, o_ref, buf_scr, sem):
    i = pl.program_id(0)
    tm = x1_ref.shape[0]

    def gather(step, act):
        slot = step % 2
        base = step * tm * TOP_K

        def body(r, carry):
            for k in range(TOP_K):
                act(pltpu.make_async_copy(y_hbm.at[pl.ds(dest_ref[base + TOP_K * r + k], 1)],
                                          buf_scr.at[slot, k, pl.ds(r, 1)], sem.at[slot]))
            return carry

        lax.fori_loop(0, tm, body, 0, unroll=DMA_UNROLL // TOP_K)

    @pl.when(i == 0)
    def _():
        gather(i, lambda cp: cp.start())

    @pl.when(i + 1 < pl.num_programs(0))
    def _():
        gather(i + 1, lambda cp: cp.start())

    gather(i, lambda cp: cp.wait())
    slot = i % 2
    gate = gate_ref[...]
    y = buf_scr[slot, 0] * gate[:, 0:1] + buf_scr[slot, 1] * gate[:, 1:2]
    o_ref[...] = x1_ref[...] + mod_ref[0, 5:6, :] * _rms(y, g_ref[...])


def _combine(dest, ybuf, gate, x1, mod, g_post, seq):
    t, d = x1.shape
    tm = 256
    per_batch = seq // tm
    grid_spec = pltpu.PrefetchScalarGridSpec(
        num_scalar_prefetch=1,
        grid=(t // tm,),
        in_specs=[pl.BlockSpec(memory_space=pl.ANY),
                  pl.BlockSpec((tm, TOP_K), lambda i, dst: (i, 0)),
                  pl.BlockSpec((tm, d), lambda i, dst: (i, 0)),
                  pl.BlockSpec((1, 6, d), lambda i, dst: (i // per_batch, 0, 0)),
                  pl.BlockSpec((1, d), lambda i, dst: (0, 0))],
        out_specs=pl.BlockSpec((tm, d), lambda i, dst: (i, 0)),
        scratch_shapes=[pltpu.VMEM((2, TOP_K, tm, d), F32), pltpu.SemaphoreType.DMA((2,))],
    )
    return pl.pallas_call(
        _combine_body,
        grid_spec=grid_spec,
        out_shape=jax.ShapeDtypeStruct((t, d), F32),
        compiler_params=_cparams(("arbitrary",), 40),
        name="combine",
    )(dest, ybuf, gate, x1, mod, g_post)


def _layer(x, c, positions, w_ada, b_ada, g_mix_pre, g_mix_post, g_ffn_pre, g_ffn_post,
           w_in, conv_w, conv_b, filt_w1, filt_b1, filt_w2, filt_b2, filt_w3, filt_freq,
           hyena_skip, w_branch_gate, b_branch_gate, w_hy_o, w_at_o, w_out,
           w_group, b_group, w_expert, b_expert, w1_exp, w3_exp, w2_exp):
    bsz, seq, d = x.shape
    t = bsz * seq
    width = hyena_skip.shape[1]
    row = lambda v: v.reshape(1, -1)

    c_pad = jnp.pad(c, ((0, 8 - bsz), (0, 0)))
    mod = _adaln(c_pad, w_ada, row(b_ada))[:bsz].reshape(bsz, 6, d)

    x2d = x.reshape(t, d)
    n_gate = w_branch_gate.shape[1]
    n_cat = n_gate + -(-w_in.shape[1] // IN_PROJ_TN) * IN_PROJ_TN
    b_cat = jnp.concatenate([b_branch_gate, jnp.zeros((n_cat - n_gate,), F32)]).reshape(1, -1)
    pg = _in_proj(x2d, mod, row(g_mix_pre), w_branch_gate.astype(BF16), w_in.astype(BF16), b_cat, seq)
    pg3 = pg.reshape(bsz, seq, -1)

    hidden = filt_w2.shape[0]
    nfeat = -(-filt_w1.shape[0] // 8) * 8
    bands = np.zeros((nfeat, 1), np.float32)
    band_vals = np.linspace(1e-4, FILTER_BANDS - 1, FILTER_BANDS, dtype=np.float32)
    bands[1:1 + FILTER_BANDS, 0] = band_vals
    bands[1 + FILTER_BANDS:1 + 2 * FILTER_BANDS, 0] = band_vals
    w1p = jnp.pad(filt_w1, ((0, nfeat - filt_w1.shape[0]), (0, 0))).T
    col = lambda v: v.reshape(-1, 1)
    max_decay = math.log(DECAY_TARGET) / FAST_DECAY_PCT
    min_decay = math.log(DECAY_TARGET) / SLOW_DECAY_PCT
    deltas = jnp.abs(jnp.linspace(min_decay, max_decay, width, dtype=F32)).reshape(1, -1)
    uw = _filters(seq, width, jnp.asarray(bands), w1p, col(filt_b1), filt_w2.T, col(filt_b2),
                  col(filt_freq), deltas, filt_w3)
    s1, s1i, s2, s2i = _fft_tables(FFT_N1, 2 * seq // FFT_N1)
    kspec = _spectra(uw, s1, s2)
    y_hy = _hyena(pg3, n_gate, conv_w, row(conv_b), hyena_skip, kspec, s1, s1i, s2, s2i, width)

    half = ROT_DIM // 2
    inv_freq = np.power(ROPE_THETA, -2.0 * np.arange(half, dtype=np.float32) / ROT_DIM).astype(np.float32)
    freq_row = np.zeros((1, LANES), np.float32)
    freq_row[0, :half] = inv_freq
    freq_row[0, half:ROT_DIM] = inv_freq
    sign_row = np.zeros((1, LANES), np.float32)
    sign_row[0, :half] = -1.0
    sign_row[0, half:ROT_DIM] = 1.0
    y_at = _attention(pg3, positions.reshape(bsz, seq, 1), jnp.asarray(freq_row), jnp.asarray(sign_row),
                      n_gate + 3 * width)

    w_r = jnp.concatenate([w_group, jnp.transpose(w_expert, (1, 0, 2)).reshape(d, N_EXPERTS)], axis=1)
    w_r = jnp.pad(w_r, ((0, 0), (0, LANES - w_r.shape[1])))
    w_r_hi = w_r.astype(BF16)
    w_r_lo = (w_r - w_r_hi.astype(F32)).astype(BF16)
    b_r = jnp.pad(jnp.concatenate([b_group, b_expert.reshape(-1)]), (0, LANES - N_EXPERT_GROUPS - N_EXPERTS))
    x1, h2, logits = _mix_out(y_hy.reshape(t, width), y_at.reshape(t, -1), pg, x2d, mod,
                              row(g_mix_post), row(g_ffn_pre), w_hy_o.astype(BF16), w_at_o.astype(BF16),
                              w_out.astype(BF16), w_r_hi, w_r_lo, b_r.reshape(1, -1), seq)

    eid, gate = _route(logits)
    flat_e = eid.reshape(t * TOP_K, 1)
    rank, counts = _rank(flat_e)
    counts = counts[0, :N_EXPERTS]
    rb = EXPERT_ROW_BLOCK
    padded = (counts + rb - 1) // rb * rb
    pends = jnp.cumsum(padded)
    pstarts = pends - padded
    pstarts_row = jnp.pad(pstarts.astype(F32), (0, LANES - N_EXPERTS)).reshape(1, LANES)
    dest = _dest(flat_e, rank, pstarts_row)[:, 0]
    n_blocks = t * TOP_K // rb + N_EXPERTS
    block_start = jnp.arange(n_blocks, dtype=jnp.int32) * rb
    block_e = jnp.minimum(jnp.searchsorted(pends, block_start, side='right'), N_EXPERTS - 1).astype(jnp.int32)
    tok = jnp.arange(t * TOP_K, dtype=jnp.int32) // TOP_K
    src_tok = (jnp.arange(n_blocks * rb, dtype=jnp.int32) % t).at[dest].set(tok)
    has_rows = jnp.cumsum((counts > 0).astype(jnp.int32))
    elist = jnp.minimum(jnp.searchsorted(has_rows, jnp.arange(1, N_EXPERTS + 1, dtype=jnp.int32), side='left'),
                        N_EXPERTS - 1).astype(jnp.int32)
    block_ord = (has_rows - 1)[block_e].astype(jnp.int32)
    counts2 = jnp.stack([pends[-1] // rb, has_rows[-1]]).astype(jnp.int32)

    ybuf = _experts(block_ord, elist, src_tok, counts2, h2, w1_exp, w3_exp, w2_exp)
    out = _combine(dest, ybuf, gate, x1, mod, row(g_ffn_post), seq)
    return out.reshape(bsz, seq, d)


def kernel(x, c, positions, w_ada, b_ada, g_mix_pre, g_mix_post, g_ffn_pre, g_ffn_post, w_in, conv_w, conv_b, filt_w1, filt_b1, filt_w2, filt_b2, filt_w3, filt_freq, hyena_skip, w_branch_gate, b_branch_gate, w_hy_o, w_at_o, w_out, w_group, b_group, w_expert, b_expert, w1_exp, w3_exp, w2_exp):
    depth = w_ada.shape[0]
    for l in range(depth):
        x = _layer(x, c, positions, w_ada[l], b_ada[l], g_mix_pre[l], g_mix_post[l], g_ffn_pre[l],
                   g_ffn_post[l], w_in[l], conv_w[l], conv_b[l], filt_w1[l], filt_b1[l], filt_w2[l],
                   filt_b2[l], filt_w3[l], filt_freq[l], hyena_skip[l], w_branch_gate[l],
                   b_branch_gate[l], w_hy_o[l], w_at_o[l], w_out[l], w_group[l], b_group[l],
                   w_expert[l], b_expert[l], w1_exp[l], w3_exp[l], w2_exp[l])
    return x
```

```python
import functools
import math

import numpy as np
import jax
import jax.numpy as jnp
from jax import lax
from jax.experimental import pallas as pl
from jax.experimental.pallas import tpu as pltpu

F32 = jnp.float32
BF16 = jnp.bfloat16

LANES = 128
MIB = 1024 * 1024

RMS_EPS = 1e-6
NEG_INF = -1e30

HEAD_DIM = 128
ROT_DIM = HEAD_DIM // 4
ROPE_THETA = 500000.0
ATTN_GROUPS = ((128, 1), (512, 4), (2048, 16))
HEADS_PER_GROUP = 4
N_ATTN_HEADS = HEADS_PER_GROUP * len(ATTN_GROUPS)

FILTER_BANDS = 16
DECAY_TARGET = 1e-2
FAST_DECAY_PCT = 0.3
SLOW_DECAY_PCT = 1.5

N_EXPERT_GROUPS = 8
EXPERTS_PER_GROUP = 8
N_EXPERTS = N_EXPERT_GROUPS * EXPERTS_PER_GROUP
TOP_K = 2
EXPERT_ROW_BLOCK = 128

FFT_N1 = 128
PITCH = FFT_N1 + 8
FFT_UNROLL = 32
FFT_MID_UNROLL = 16


def _cparams(sem, vmem_mib):
    return pltpu.CompilerParams(dimension_semantics=sem, vmem_limit_bytes=vmem_mib * MIB)


def _rms(x, g):
    return x * lax.rsqrt(jnp.mean(x * x, axis=-1, keepdims=True) + RMS_EPS) * g


def _dot_hilo(a, b):
    a_hi = a.astype(BF16)
    a_lo = (a - a_hi.astype(F32)).astype(BF16)
    b_hi = b.astype(BF16)
    b_lo = (b - b_hi.astype(F32)).astype(BF16)
    dot = functools.partial(jnp.dot, preferred_element_type=F32)
    return dot(a_hi, b_hi) + dot(a_lo, b_hi) + dot(a_hi, b_lo) + dot(a_lo, b_lo)


def _adaln_body(c_ref, w_ref, b_ref, o_ref):
    c = c_ref[...]
    cond = c * jax.nn.sigmoid(c)
    o_ref[...] = jnp.dot(cond.astype(BF16), w_ref[...].astype(BF16),
                         preferred_element_type=F32) + b_ref[...]


def _adaln(c_pad, w_ada, b_ada):
    rows, d = c_pad.shape
    n = w_ada.shape[1]
    tn = 1024
    return pl.pallas_call(
        _adaln_body,
        grid=(n // tn,),
        in_specs=[pl.BlockSpec((rows, d), lambda j: (0, 0)),
                  pl.BlockSpec((d, tn), lambda j: (0, j)),
                  pl.BlockSpec((1, tn), lambda j: (0, j))],
        out_specs=pl.BlockSpec((rows, tn), lambda j: (0, j)),
        out_shape=jax.ShapeDtypeStruct((rows, n), F32),
        compiler_params=_cparams(("arbitrary",), 40),
        name="adaln",
    )(c_pad, w_ada, b_ada)


IN_PROJ_TM = 1024
IN_PROJ_TN = 1024
IN_PROJ_SUB_M = 256
IN_PROJ_SUB_N = 512


def _in_proj_body(n_gate, last_cols, x_ref, mod_ref, g_ref, wg_ref, wi_ref, b_ref, o_ref, h_scr):
    j = pl.program_id(1)
    last = pl.num_programs(1) - 1

    tm, tn = o_ref.shape

    @pl.when(j == 0)
    def _():
        def norm(r, carry):
            rows = pl.ds(pl.multiple_of(r * IN_PROJ_SUB_M, IN_PROJ_SUB_M), IN_PROJ_SUB_M)
            h = _rms(x_ref[rows, :], g_ref[...]) * (1.0 + mod_ref[0, 1:2, :]) + mod_ref[0, 0:1, :]
            h_scr[rows, :] = h.astype(BF16)
            return carry
        lax.fori_loop(0, tm // IN_PROJ_SUB_M, norm, 0)


    def tile(w_ref, gate, ncols):
        for mi in range(tm // IN_PROJ_SUB_M):
            rows = slice(mi * IN_PROJ_SUB_M, (mi + 1) * IN_PROJ_SUB_M)
            for ni in range(tn // IN_PROJ_SUB_N):
                cols = slice(ni * IN_PROJ_SUB_N, (ni + 1) * IN_PROJ_SUB_N)
                if ni * IN_PROJ_SUB_N >= ncols:
                    o_ref[rows, cols] = jnp.zeros((IN_PROJ_SUB_M, IN_PROJ_SUB_N), o_ref.dtype)
                    continue
                acc = jnp.dot(h_scr[rows, :], w_ref[:, cols], preferred_element_type=F32) + b_ref[:, cols]
                o_ref[rows, cols] = (jax.nn.sigmoid(acc) if gate else acc).astype(o_ref.dtype)

    @pl.when(j < n_gate)
    def _():
        tile(wg_ref, True, tn)

    @pl.when((j >= n_gate) & (j < last))
    def _():
        tile(wi_ref, False, tn)

    @pl.when(j == last)
    def _():
        tile(wi_ref, False, last_cols)


def _in_proj(x2d, mod, g_pre, w_gate, w_in, b_cat, seq):
    t, d = x2d.shape
    tm, tn = IN_PROJ_TM, IN_PROJ_TN
    ng = w_gate.shape[1] // tn
    ni = pl.cdiv(w_in.shape[1], tn)
    last_cols = w_in.shape[1] - (ni - 1) * tn
    assert w_gate.shape[1] % tn == 0 and last_cols % IN_PROJ_SUB_N == 0
    per_batch = seq // tm
    return pl.pallas_call(
        functools.partial(_in_proj_body, ng, last_cols),
        grid=(t // tm, ng + ni),
        in_specs=[pl.BlockSpec((tm, d), lambda i, j: (i, 0)),
                  pl.BlockSpec((1, 6, d), lambda i, j: (i // per_batch, 0, 0)),
                  pl.BlockSpec((1, d), lambda i, j: (0, 0)),
                  pl.BlockSpec((d, tn), lambda i, j: (0, jnp.minimum(j, ng - 1))),
                  pl.BlockSpec((d, tn), lambda i, j: (0, jnp.clip(j - ng, 0, ni - 1))),
                  pl.BlockSpec((1, tn), lambda i, j: (0, j))],
        out_specs=pl.BlockSpec((tm, tn), lambda i, j: (i, j)),
        out_shape=jax.ShapeDtypeStruct((t, (ng + ni) * tn), BF16),
        scratch_shapes=[pltpu.VMEM((tm, d), BF16)],
        compiler_params=_cparams(("arbitrary", "arbitrary"), 56),
        name="in_proj",
    )(x2d, mod, g_pre, w_gate, w_in, b_cat)


def _filters_body(seq, band_ref, w1_ref, b1_ref, w2_ref, b2_ref, fr_ref, dl_ref,
                  w3a_ref, w3b_ref, o_ref, hid_scr):
    i = pl.program_id(0)
    j = pl.program_id(1)
    tl = hid_scr.shape[0]
    row = (lax.broadcasted_iota(jnp.int32, (tl, 1), 0) + i * tl).astype(F32)

    @pl.when((j == 0) & (pl.program_id(2) == 0))
    def _():
        nfeat = band_ref.shape[0]
        pos = (lax.broadcasted_iota(jnp.int32, (1, tl), 1) + i * tl).astype(F32)
        feat = lax.broadcasted_iota(jnp.int32, (nfeat, tl), 0)
        ang = band_ref[...] * (2.0 * math.pi * pos / seq)
        feats = jnp.where(feat == 0, pos / (seq - 1.0),
                          jnp.where(feat <= FILTER_BANDS, jnp.cos(ang),
                                    jnp.where(feat <= 2 * FILTER_BANDS, -jnp.sin(ang), 0.0)))
        hi = lax.Precision.HIGHEST
        fr = fr_ref[...]
        hid = jnp.sin(fr * (jnp.dot(w1_ref[...], feats, precision=hi, preferred_element_type=F32)
                            + b1_ref[...]))
        hid = jnp.sin(fr * (jnp.dot(w2_ref[...], hid, precision=hi, preferred_element_type=F32)
                            + b2_ref[...]))
        hid_scr[...] = hid.T

    hid = hid_scr[...]
    decay = jnp.exp(-(row / (seq - 1.0)) * dl_ref[...])
    hf = _dot_hilo(hid, w3a_ref[...]) * decay
    hb = _dot_hilo(hid, w3b_ref[...]) * decay
    hb = jnp.where(row == 0.0, 0.0, hb)
    o_ref[0, 0] = hf + hb
    o_ref[0, 1] = hf - hb


def _filters(seq, width, bands, w1p, b1, w2, b2, freq, deltas, w3):
    tl, tc = 512, 512
    nct = width // tc
    hidden = w2.shape[0]
    nfeat = bands.shape[0]
    const = lambda i, j, o: (0, 0)
    return pl.pallas_call(
        functools.partial(_filters_body, float(seq)),
        grid=(seq // tl, nct, 2),
        in_specs=[pl.BlockSpec((nfeat, 1), const),
                  pl.BlockSpec((hidden, nfeat), const),
                  pl.BlockSpec((hidden, 1), const),
                  pl.BlockSpec((hidden, hidden), const),
                  pl.BlockSpec((hidden, 1), const),
                  pl.BlockSpec((hidden, 1), const),
                  pl.BlockSpec((1, tc), lambda i, j, o: (0, j)),
                  pl.BlockSpec((hidden, tc), lambda i, j, o: (0, (2 * o) * nct + j)),
                  pl.BlockSpec((hidden, tc), lambda i, j, o: (0, (2 * o + 1) * nct + j))],
        out_specs=pl.BlockSpec((1, 2, tl, tc), lambda i, j, o: (o, 0, i, j)),
        out_shape=jax.ShapeDtypeStruct((2, 2, seq, width), F32),
        scratch_shapes=[pltpu.VMEM((tl, hidden), F32)],
        compiler_params=_cparams(("arbitrary", "arbitrary", "arbitrary"), 32),
        name="filters",
    )(bands, w1p, b1, w2, b2, freq, deltas, w3, w3)


def _fft_tables(n1, n2):
    n = n1 * n2
    q = np.arange(n2)[:, None]
    b = np.arange(n2 // 2)[None, :]
    a = np.arange(n1)[:, None, None]
    ang = -2.0 * np.pi * (a * q[None] / n + (q * b)[None] / n2)
    stage1 = np.concatenate([np.cos(ang), np.sin(ang)], axis=1)
    stage1_inv = np.transpose(stage1, (0, 2, 1)) / n
    p = np.arange(n1)
    ang2 = -2.0 * np.pi * np.outer(p, p) / n1
    fre, fim = np.cos(ang2), np.sin(ang2)
    stage2 = np.block([[fre, -fim], [fim, fre]])
    stage2_inv = np.block([[fre, fim], [-fim, fre]])
    as_bf16 = lambda m: jnp.asarray(m, dtype=F32).astype(BF16)
    return as_bf16(stage1), as_bf16(stage1_inv), as_bf16(stage2), as_bf16(stage2_inv)


def _halves(ref, rows):
    return jnp.concatenate([ref[0, rows, :], ref[1, rows, :]], axis=1)


def _fft_stage1(z_ref, s1_ref, gre_ref, gim_ref):
    n1, two_n2, n2h = s1_ref.shape
    n2 = two_n2 // 2

    def step(a, carry):
        zrows = _halves(z_ref, pl.ds(a, n2h, stride=PITCH))
        g = jnp.dot(s1_ref[a], zrows.astype(BF16), preferred_element_type=F32)
        for h in range(2):
            cols = slice(h * LANES, (h + 1) * LANES)
            gre_ref[h, pl.ds(a, n2, stride=PITCH), :] = g[:n2, cols]
            gim_ref[h, pl.ds(a, n2, stride=PITCH), :] = g[n2:, cols]
        return carry

    lax.fori_loop(0, n1, step, 0, unroll=FFT_UNROLL)


def _fft_stage1_inv(gre_ref, gim_ref, s1i_ref, y_ref):
    n1, n2h, two_n2 = s1i_ref.shape
    n2 = two_n2 // 2

    def step(a, carry):
        rows = pl.ds(a, n2, stride=PITCH)
        hs = jnp.concatenate([_halves(gre_ref, rows), _halves(gim_ref, rows)], axis=0)
        y = jnp.dot(s1i_ref[a], hs.astype(BF16), preferred_element_type=F32)
        for h in range(2):
            y_ref[h, pl.ds(a, n2h, stride=PITCH), :] = y[:, h * LANES:(h + 1) * LANES]
        return carry

    lax.fori_loop(0, n1, step, 0, unroll=FFT_UNROLL)


def _stage2_block(gre_ref, gim_ref, s2_ref, q):
    n1 = s2_ref.shape[0] // 2
    rows = pl.ds(pl.multiple_of(q * PITCH, 8), n1)
    gs = jnp.concatenate([_halves(gre_ref, rows), _halves(gim_ref, rows)], axis=0)
    x = jnp.dot(s2_ref[...], gs.astype(BF16), preferred_element_type=F32)
    return x[:n1], x[n1:]


def _spectra_body(uw_ref, s1_ref, s2_ref, k_ref, z_scr, gre_scr, gim_scr):
    n1 = s2_ref.shape[0] // 2
    n2 = s1_ref.shape[1] // 2
    for h in range(2):
        for b in range(n2 // 2):
            z_scr[h, pl.ds(b * PITCH, n1), :] = uw_ref[0, h, pl.ds(b * n1, n1), :]
    _fft_stage1(z_scr, s1_ref, gre_scr, gim_scr)

    def step(q, carry):
        xre, xim = _stage2_block(gre_scr, gim_scr, s2_ref, q)
        rows = pl.ds(pl.multiple_of(q * n1, 8), n1)
        k_ref[0, 0, rows, :] = xre[:, :LANES]
        k_ref[0, 1, rows, :] = xim[:, LANES:]
        return carry

    lax.fori_loop(0, n2, step, 0, unroll=FFT_MID_UNROLL)


def _spectra(uw, s1, s2):
    _, _, seq, width = uw.shape
    n1 = FFT_N1
    n2 = 2 * seq // n1
    const3 = lambda j, o: (0, 0, 0)
    return pl.pallas_call(
        _spectra_body,
        grid=(width // LANES, 2),
        in_specs=[pl.BlockSpec((1, 2, seq, LANES), lambda j, o: (o, 0, 0, j)),
                  pl.BlockSpec(s1.shape, const3),
                  pl.BlockSpec(s2.shape, lambda j, o: (0, 0))],
        out_specs=pl.BlockSpec((1, 2, 2 * seq, LANES), lambda j, o: (o, 0, 0, j)),
        out_shape=jax.ShapeDtypeStruct((2, 2, 2 * seq, width), F32),
        scratch_shapes=[pltpu.VMEM((2, (n2 // 2) * PITCH, LANES), F32),
                        pltpu.VMEM((2, n2 * PITCH, LANES), F32),
                        pltpu.VMEM((2, n2 * PITCH, LANES), F32)],
        compiler_params=_cparams(("arbitrary", "arbitrary"), 56),
        name="spectra",
    )(uw, s1, s2)


def _short_conv_block(p_ref, b, i, nblk, w_ref, bias_ref):
    n1 = FFT_N1
    start = pl.multiple_of(i * n1, n1)
    cur = p_ref[b, pl.ds(start, n1), :].astype(F32)
    before = p_ref[b, pl.ds(pl.multiple_of(jnp.maximum(start - 16, 0), 16), 16), :].astype(F32)
    after = p_ref[b, pl.ds(pl.multiple_of(jnp.minimum(start + n1, (nblk - 1) * n1), 16), 16), :].astype(F32)
    last_prev = before[15:16] * jnp.where(i > 0, 1.0, 0.0).astype(F32)
    first_next = after[0:1] * jnp.where(i < nblk - 1, 1.0, 0.0).astype(F32)
    row = lax.broadcasted_iota(jnp.int32, (n1, 1), 0)
    prev = jnp.where(row == 0, last_prev, pltpu.roll(cur, 1, 0))
    nxt = jnp.where(row == n1 - 1, first_next, pltpu.roll(cur, n1 - 1, 0))
    return prev * w_ref[0:1, :] + cur * w_ref[1:2, :] + nxt * w_ref[2:3, :] + bias_ref[...]


def _hyena_body(pv_ref, px1_ref, px2_ref, cwv_ref, cw1_ref, cw2_ref, cbv_ref, cb1_ref, cb2_ref,
                skip_ref, k_ref, s1_ref, s1i_ref, s2_ref, s2i_ref, o_ref,
                z_scr, y_scr, gre_scr, gim_scr):
    c = pl.program_id(1)
    g = pl.program_id(2)
    ngroups = pl.num_programs(2)
    n1 = FFT_N1
    n2 = s1_ref.shape[1] // 2
    nblk = n2 // 2
    per_group = n2 // ngroups

    @pl.when(g == 0)
    def _():
        @pl.when(c == 0)
        def _():
            def fill(i, carry):
                for b in range(2):
                    z_scr[b, pl.ds(pl.multiple_of(i * PITCH, 8), n1), :] = _short_conv_block(
                        pv_ref, b, i, nblk, cwv_ref, cbv_ref)
                return carry
            lax.fori_loop(0, nblk, fill, 0)

        _fft_stage1(z_scr, s1_ref, gre_scr, gim_scr)

    def mid(ql, carry):
        q = g * per_group + ql
        xre, xim = _stage2_block(gre_scr, gim_scr, s2_ref, q)
        krows = pl.ds(pl.multiple_of(ql * n1, 8), n1)
        kre = k_ref[0, 0, krows, :]
        kim = k_ref[0, 1, krows, :]
        kre = jnp.concatenate([kre, kre], axis=1)
        kim = jnp.concatenate([kim, kim], axis=1)
        ys = jnp.concatenate([xre * kre - xim * kim, xre * kim + xim * kre], axis=0)
        hh = jnp.dot(s2i_ref[...], ys.astype(BF16), preferred_element_type=F32)
        rows = pl.ds(pl.multiple_of(q * PITCH, 8), n1)
        for b in range(2):
            cols = slice(b * LANES, (b + 1) * LANES)
            gre_scr[b, rows, :] = hh[:n1, cols]
            gim_scr[b, rows, :] = hh[n1:, cols]
        return carry

    lax.fori_loop(0, per_group, mid, 0, unroll=FFT_MID_UNROLL)

    @pl.when(g == ngroups - 1)
    def _():
        _fft_stage1_inv(gre_scr, gim_scr, s1i_ref, y_scr)

        def post(px_ref, cw_ref, cb_ref, order, store):
            def blk(i, carry):
                rows = pl.ds(pl.multiple_of(i * PITCH, 8), n1)
                for b in range(2):
                    zb = z_scr[b, rows, :]
                    gate = _short_conv_block(px_ref, b, i, nblk, cw_ref, cb_ref)
                    store(b, i, rows, gate * (y_scr[b, rows, :] + zb * skip_ref[order:order + 1, :]))
                return carry
            lax.fori_loop(0, nblk, blk, 0)

        @pl.when(c == 0)
        def _():
            def store(b, i, rows, val):
                z_scr[b, rows, :] = val
            post(px1_ref, cw1_ref, cb1_ref, 0, store)

        @pl.when(c == 1)
        def _():
            def store(b, i, rows, val):
                o_ref[b, pl.ds(pl.multiple_of(i * n1, n1), n1), :] = val.astype(o_ref.dtype)
            post(px2_ref, cw2_ref, cb2_ref, 1, store)


def _hyena(pg3, proj_col0, conv_w, conv_b, skip, kspec, s1, s1i, s2, s2i, width):
    bsz, seq, _ = pg3.shape
    assert bsz == 2
    n1 = FFT_N1
    n2 = 2 * seq // n1
    nct = width // LANES
    ngroups = 4
    krows = (n2 // ngroups) * n1
    col = lambda off: (lambda j, c, g: (0, 0, proj_col0 // LANES + off * nct + j))
    cw = lambda off: (lambda j, c, g: (0, off * nct + j))
    const3 = lambda j, c, g: (0, 0, 0)
    const2 = lambda j, c, g: (0, 0)
    return pl.pallas_call(
        _hyena_body,
        grid=(nct, 2, ngroups),
        in_specs=[pl.BlockSpec((2, seq, LANES), col(0)),
                  pl.BlockSpec((2, seq, LANES), col(1)),
                  pl.BlockSpec((2, seq, LANES), col(2)),
                  pl.BlockSpec((3, LANES), cw(0)),
                  pl.BlockSpec((3, LANES), cw(1)),
                  pl.BlockSpec((3, LANES), cw(2)),
                  pl.BlockSpec((1, LANES), cw(0)),
                  pl.BlockSpec((1, LANES), cw(1)),
                  pl.BlockSpec((1, LANES), cw(2)),
                  pl.BlockSpec((2, LANES), lambda j, c, g: (0, j)),
                  pl.BlockSpec((1, 2, krows, LANES), lambda j, c, g: (c, 0, g, j)),
                  pl.BlockSpec(s1.shape, const3),
                  pl.BlockSpec(s1i.shape, const3),
                  pl.BlockSpec(s2.shape, const2),
                  pl.BlockSpec(s2i.shape, const2)],
        out_specs=pl.BlockSpec((2, seq, LANES), lambda j, c, g: (0, 0, j)),
        out_shape=jax.ShapeDtypeStruct((2, seq, width), BF16),
        scratch_shapes=[pltpu.VMEM((2, (n2 // 2) * PITCH, LANES), F32),
                        pltpu.VMEM((2, (n2 // 2) * PITCH, LANES), F32),
                        pltpu.VMEM((2, n2 * PITCH, LANES), F32),
                        pltpu.VMEM((2, n2 * PITCH, LANES), F32)],
        compiler_params=_cparams(("arbitrary", "arbitrary", "arbitrary"), 56),
        name="hyena",
    )(pg3, pg3, pg3, conv_w, conv_w, conv_w, conv_b, conv_b, conv_b, skip, kspec, s1, s1i, s2, s2i)


ATTN_QBLK = 128
ATTN_UNROLL = 16


def _attention_body(pos_ref, freq_ref, sign_ref, *refs):
    qkv_refs = refs[:9]
    o_ref = refs[9]
    cos_scr, sin_scr, q_scr, k_scr, v_scr, og_scr, lse_scr = refs[10:]
    seq = q_scr.shape[0]
    chunk = 512
    nchunks = seq // chunk

    @pl.when(pl.program_id(1) == 0)
    def _():
        def trig(i, carry):
            rows = pl.ds(pl.multiple_of(i * chunk, chunk), chunk)
            ang = pos_ref[0, rows, :].astype(F32) * freq_ref[...]
            cos_scr[rows, :] = jnp.cos(ang)
            sin_scr[rows, :] = jnp.sin(ang) * sign_ref[...]
            return carry
        lax.fori_loop(0, nchunks, trig, 0)

    def rotate(src_ref, dst_ref):
        def body(i, carry):
            rows = pl.ds(pl.multiple_of(i * chunk, chunk), chunk)
            t = src_ref[0, rows, :].astype(F32)
            lane = lax.broadcasted_iota(jnp.int32, t.shape, 1)
            partner = jnp.where(lane < ROT_DIM // 2,
                                pltpu.roll(t, LANES - ROT_DIM // 2, 1), pltpu.roll(t, ROT_DIM // 2, 1))
            dst_ref[rows, :] = t * cos_scr[rows, :] + partner * sin_scr[rows, :]
            return carry
        lax.fori_loop(0, nchunks, body, 0, unroll=2)

    def widen(src_ref, dst_ref):
        def body(i, carry):
            rows = pl.ds(pl.multiple_of(i * chunk, chunk), chunk)
            dst_ref[rows, :] = src_ref[0, rows, :].astype(F32)
            return carry
        lax.fori_loop(0, nchunks, body, 0, unroll=2)

    scale = HEAD_DIM ** -0.5
    for gi, (window, dil) in enumerate(ATTN_GROUPS):
        rotate(qkv_refs[3 * gi], q_scr)
        rotate(qkv_refs[3 * gi + 1], k_scr)
        widen(qkv_refs[3 * gi + 2], v_scr)
        n = seq // dil
        half = window // (2 * dil)
        tk = min(n, ATTN_QBLK + 2 * half)
        blocks_per_res = n // ATTN_QBLK

        def block(u, carry, dil=dil, n=n, half=half, tk=tk, blocks_per_res=blocks_per_res, gi=gi):
            r = u // blocks_per_res
            m = u % blocks_per_res
            q0 = m * ATTN_QBLK
            k0 = jnp.clip(q0 - half, 0, n - tk)
            qrows = pl.ds(r + dil * q0, ATTN_QBLK, stride=dil)
            krows = pl.ds(r + dil * k0, tk, stride=dil)
            qb = q_scr[qrows, :].astype(BF16)
            kb = k_scr[krows, :].astype(BF16)
            vb = v_scr[krows, :].astype(BF16)
            s = lax.dot_general(qb, kb, (((1,), (1,)), ((), ())), preferred_element_type=F32) * scale
            qi = q0 + lax.broadcasted_iota(jnp.int32, (ATTN_QBLK, tk), 0)
            kj = k0 + lax.broadcasted_iota(jnp.int32, (ATTN_QBLK, tk), 1)
            s = jnp.where(jnp.abs(qi - kj) <= half, s, NEG_INF)
            mx = jnp.max(s, axis=-1, keepdims=True)
            p = jnp.exp(s - mx)
            l = jnp.sum(p, axis=-1, keepdims=True)
            o = jnp.dot(p.astype(BF16), vb, preferred_element_type=F32) / l
            og_scr[gi, qrows, :] = o
            lse_scr[gi, qrows, :] = jnp.broadcast_to(mx + jnp.log(l), (ATTN_QBLK, LANES))
            return carry

        lax.fori_loop(0, seq // ATTN_QBLK, block, 0, unroll=ATTN_UNROLL)

    def merge(i, carry):
        rows = pl.ds(pl.multiple_of(i * chunk, chunk), chunk)
        lses = [lse_scr[gi, rows, :] for gi in range(len(ATTN_GROUPS))]
        mx = functools.reduce(jnp.maximum, lses)
        ws = [jnp.exp(v - mx) for v in lses]
        den = functools.reduce(lambda a, b: a + b, ws)
        num = functools.reduce(lambda a, b: a + b,
                               [w * og_scr[gi, rows, :] for gi, w in enumerate(ws)])
        o_ref[0, rows, :] = (num / den).astype(o_ref.dtype)
        return carry

    lax.fori_loop(0, nchunks, merge, 0)


def _attention(pg3, pos3, freq_row, sign_row, qkv_col0):
    bsz, seq, _ = pg3.shape
    ng = len(ATTN_GROUPS)
    in_specs = [pl.BlockSpec((1, seq, 1), lambda b, h: (b, 0, 0)),
                pl.BlockSpec((1, LANES), lambda b, h: (0, 0)),
                pl.BlockSpec((1, LANES), lambda b, h: (0, 0))]
    for gi in range(ng):
        for which in range(3):
            base = qkv_col0 // LANES + which * N_ATTN_HEADS + gi * HEADS_PER_GROUP
            in_specs.append(pl.BlockSpec((1, seq, LANES), lambda b, h, base=base: (b, 0, base + h)))
    return pl.pallas_call(
        _attention_body,
        grid=(bsz, HEADS_PER_GROUP),
        in_specs=in_specs,
        out_specs=pl.BlockSpec((1, seq, LANES), lambda b, h: (b, 0, h)),
        out_shape=jax.ShapeDtypeStruct((bsz, seq, HEADS_PER_GROUP * HEAD_DIM), BF16),
        scratch_shapes=[pltpu.VMEM((seq, LANES), F32), pltpu.VMEM((seq, LANES), F32),
                        pltpu.VMEM((seq, LANES), F32), pltpu.VMEM((seq, LANES), F32),
                        pltpu.VMEM((seq, LANES), F32),
                        pltpu.VMEM((ng, seq, LANES), F32), pltpu.VMEM((ng, seq, LANES), F32)],
        compiler_params=_cparams(("arbitrary", "arbitrary"), 56),
        name="attention",
    )(pos3, freq_row, sign_row, *([pg3] * (3 * ng)))


def _mix_out_body(yhy_ref, yat_ref, ghy_ref, gat_ref, x_ref, mod_ref, gpost_ref, gpre_ref,
                  whyo_ref, wato_ref, wout_ref, wrhi_ref, wrlo_ref, br_ref, x1_ref, h2_ref, lg_ref):
    a = jnp.dot(yhy_ref[...], whyo_ref[...], preferred_element_type=F32)
    b = jnp.dot(yat_ref[...], wato_ref[...], preferred_element_type=F32)
    merged = ghy_ref[...].astype(F32) * a + gat_ref[...].astype(F32) * b
    y = jnp.dot(merged.astype(BF16), wout_ref[...], preferred_element_type=F32)
    x1 = x_ref[...] + mod_ref[0, 2:3, :] * _rms(y, gpost_ref[...])
    x1_ref[...] = x1
    h2 = _rms(x1, gpre_ref[...]) * (1.0 + mod_ref[0, 4:5, :]) + mod_ref[0, 3:4, :]
    h2_ref[...] = h2
    h2_hi = h2.astype(BF16)
    h2_lo = (h2 - h2_hi.astype(F32)).astype(BF16)
    w_hi = wrhi_ref[...]
    w_lo = wrlo_ref[...]
    lg_ref[...] = (jnp.dot(h2_hi, w_hi, preferred_element_type=F32)
                   + jnp.dot(h2_lo, w_hi, preferred_element_type=F32)
                   + jnp.dot(h2_hi, w_lo, preferred_element_type=F32)
                   + jnp.dot(h2_lo, w_lo, preferred_element_type=F32)) + br_ref[...]


def _mix_out(y_hy, y_at, pg, x2d, mod, g_post, g_pre, w_hy_o, w_at_o, w_out, w_r_hi, w_r_lo, b_r, seq):
    t, d = x2d.shape
    tm = 256
    per_batch = seq // tm
    gblk = 0
    const = lambda i: (0, 0)
    return pl.pallas_call(
        _mix_out_body,
        grid=(t // tm,),
        in_specs=[pl.BlockSpec((tm, y_hy.shape[1]), lambda i: (i, 0)),
                  pl.BlockSpec((tm, y_at.shape[1]), lambda i: (i, 0)),
                  pl.BlockSpec((tm, d), lambda i: (i, gblk)),
                  pl.BlockSpec((tm, d), lambda i: (i, gblk + 1)),
                  pl.BlockSpec((tm, d), lambda i: (i, 0)),
                  pl.BlockSpec((1, 6, d), lambda i: (i // per_batch, 0, 0)),
                  pl.BlockSpec((1, d), const),
                  pl.BlockSpec((1, d), const),
                  pl.BlockSpec(w_hy_o.shape, const),
                  pl.BlockSpec(w_at_o.shape, const),
                  pl.BlockSpec(w_out.shape, const),
                  pl.BlockSpec(w_r_hi.shape, const),
                  pl.BlockSpec(w_r_lo.shape, const),
                  pl.BlockSpec((1, LANES), const)],
        out_specs=[pl.BlockSpec((tm, d), lambda i: (i, 0)),
                   pl.BlockSpec((tm, d), lambda i: (i, 0)),
                   pl.BlockSpec((tm, LANES), lambda i: (i, 0))],
        out_shape=[jax.ShapeDtypeStruct((t, d), F32),
                   jax.ShapeDtypeStruct((t, d), F32),
                   jax.ShapeDtypeStruct((t, LANES), F32)],
        compiler_params=_cparams(("arbitrary",), 56),
        name="mix_out",
    )(y_hy, y_at, pg, pg, x2d, mod, g_post, g_pre, w_hy_o, w_at_o, w_out, w_r_hi, w_r_lo, b_r)


def _route_body(lg_ref, eid_ref, gate_ref):
    lg = lg_ref[...]
    lane = lax.broadcasted_iota(jnp.int32, lg.shape, 1)
    big = jnp.int32(1 << 20)

    def first_argmax(vals, mask):
        v = jnp.where(mask, vals, -jnp.inf)
        mx = jnp.max(v, axis=-1, keepdims=True)
        idx = jnp.min(jnp.where(mask & (v == mx), lane, big), axis=-1, keepdims=True)
        return mx, idx

    gmask = lane < N_EXPERT_GROUPS
    gmax, gidx = first_argmax(lg, gmask)
    gval = 1.0 / jnp.sum(jnp.where(gmask, jnp.exp(lg - gmax), 0.0), axis=-1, keepdims=True)
    lo = N_EXPERT_GROUPS + gidx * EXPERTS_PER_GROUP
    emask = (lane >= lo) & (lane < lo + EXPERTS_PER_GROUP)
    v1, i1 = first_argmax(lg, emask)
    v2, i2 = first_argmax(lg, emask & (lane != i1))
    e2 = jnp.exp(v2 - v1)
    p1 = 1.0 / (1.0 + e2)
    p2 = e2 / (1.0 + e2)
    eid = jnp.where(lane == 0, i1, i2) - N_EXPERT_GROUPS
    gate = gval * jnp.where(lane == 0, p1, p2)
    eid_ref[...] = eid[:, :TOP_K]
    gate_ref[...] = gate[:, :TOP_K]


def _route(logits):
    t = logits.shape[0]
    tm = 512
    return pl.pallas_call(
        _route_body,
        grid=(t // tm,),
        in_specs=[pl.BlockSpec((tm, LANES), lambda i: (i, 0))],
        out_specs=[pl.BlockSpec((tm, TOP_K), lambda i: (i, 0)),
                   pl.BlockSpec((tm, TOP_K), lambda i: (i, 0))],
        out_shape=[jax.ShapeDtypeStruct((t, TOP_K), jnp.int32),
                   jax.ShapeDtypeStruct((t, TOP_K), F32)],
        compiler_params=_cparams(("arbitrary",), 32),
        name="route",
    )(logits)


def _rank_body(e_ref, rank_ref, cnt_ref, carry_scr):
    i = pl.program_id(0)
    r = e_ref.shape[0]

    @pl.when(i == 0)
    def _():
        carry_scr[...] = jnp.zeros_like(carry_scr)

    lane = lax.broadcasted_iota(jnp.int32, (r, LANES), 1)
    onehot = (lane == e_ref[...]).astype(F32)
    tri = (lax.broadcasted_iota(jnp.int32, (r, r), 1)
           < lax.broadcasted_iota(jnp.int32, (r, r), 0)).astype(BF16)
    before = jnp.dot(tri, onehot.astype(BF16), preferred_element_type=F32) + carry_scr[0:1, :]
    rank_ref[...] = jnp.sum(onehot * before, axis=-1, keepdims=True).astype(jnp.int32)
    total = carry_scr[0:1, :] + jnp.sum(onehot, axis=0, keepdims=True)
    carry_scr[...] = jnp.broadcast_to(total, carry_scr.shape)
    cnt_ref[...] = jnp.broadcast_to(total, cnt_ref.shape).astype(jnp.int32)


def _rank(flat_e):
    a = flat_e.shape[0]
    r = 512
    return pl.pallas_call(
        _rank_body,
        grid=(a // r,),
        in_specs=[pl.BlockSpec((r, 1), lambda i: (i, 0))],
        out_specs=[pl.BlockSpec((r, 1), lambda i: (i, 0)),
                   pl.BlockSpec((8, LANES), lambda i: (0, 0))],
        out_shape=[jax.ShapeDtypeStruct((a, 1), jnp.int32),
                   jax.ShapeDtypeStruct((8, LANES), jnp.int32)],
        scratch_shapes=[pltpu.VMEM((8, LANES), F32)],
        compiler_params=_cparams(("arbitrary",), 32),
        name="rank",
    )(flat_e)


def _dest_body(e_ref, rank_ref, ps_ref, o_ref):
    lane = lax.broadcasted_iota(jnp.int32, (e_ref.shape[0], LANES), 1)
    first = jnp.sum(jnp.where(lane == e_ref[...], ps_ref[...], 0.0), axis=-1, keepdims=True)
    o_ref[...] = first.astype(jnp.int32) + rank_ref[...]


def _dest(flat_e, rank, pstarts_row):
    a = flat_e.shape[0]
    r = 2048
    return pl.pallas_call(
        _dest_body,
        grid=(a // r,),
        in_specs=[pl.BlockSpec((r, 1), lambda i: (i, 0)),
                  pl.BlockSpec((r, 1), lambda i: (i, 0)),
                  pl.BlockSpec((1, LANES), lambda i: (0, 0))],
        out_specs=pl.BlockSpec((r, 1), lambda i: (i, 0)),
        out_shape=jax.ShapeDtypeStruct((a, 1), jnp.int32),
        compiler_params=_cparams(("arbitrary",), 32),
        name="dest",
    )(flat_e, rank, pstarts_row)


DMA_UNROLL = 16


def _experts_body(ord_ref, elist_ref, src_ref, cnt_ref, h_hbm, w1_hbm, w3_hbm, w2_hbm, y_ref,
                  x_scr, w1_scr, w3_scr, w2_scr, sem, wsem):
    i = pl.program_id(0)
    nrows = x_scr.shape[1]
    nused = cnt_ref[0]
    nexp = cnt_ref[1]
    k = ord_ref[i]
    first = (i == 0) | (k != ord_ref[jnp.maximum(i - 1, 0)])

    def weights(kk, act):
        e = elist_ref[kk]
        slot = kk % 2
        act(pltpu.make_async_copy(w1_hbm.at[e], w1_scr.at[slot], wsem.at[slot, 0]))
        act(pltpu.make_async_copy(w3_hbm.at[e], w3_scr.at[slot], wsem.at[slot, 1]))
        act(pltpu.make_async_copy(w2_hbm.at[e], w2_scr.at[slot], wsem.at[slot, 2]))

    def gather(blk, act):
        slot = blk % 2
        base = blk * nrows

        def body(r, carry):
            act(pltpu.make_async_copy(h_hbm.at[pl.ds(src_ref[base + r], 1)],
                                      x_scr.at[slot, pl.ds(r, 1)], sem.at[slot]))
            return carry

        lax.fori_loop(0, nrows, body, 0, unroll=DMA_UNROLL)

    @pl.when(i == 0)
    def _():
        weights(k, lambda cp: cp.start())
        gather(i, lambda cp: cp.start())

    @pl.when(i + 1 < nused)
    def _():
        gather(i + 1, lambda cp: cp.start())

    @pl.when((i < nused) & first & (k + 1 < nexp))
    def _():
        weights(k + 1, lambda cp: cp.start())

    @pl.when((i < nused) & first)
    def _():
        weights(k, lambda cp: cp.wait())

    @pl.when(i < nused)
    def _():
        gather(i, lambda cp: cp.wait())
        wslot = k % 2
        xb = x_scr[i % 2].astype(BF16)
        a = jnp.dot(xb, w1_scr[wslot].astype(BF16), preferred_element_type=F32)
        b = jnp.dot(xb, w3_scr[wslot].astype(BF16), preferred_element_type=F32)
        act = (a * jax.nn.sigmoid(a) * b).astype(BF16)
        y_ref[...] = jnp.dot(act, w2_scr[wslot].astype(BF16), preferred_element_type=F32)

    @pl.when(i >= nused)
    def _():
        y_ref[...] = jnp.zeros_like(y_ref)


def _experts(block_ord, elist, src_tok, counts2, h2, w1, w3, w2):
    nblocks = block_ord.shape[0]
    d = h2.shape[1]
    ff = w1.shape[2]
    rb = EXPERT_ROW_BLOCK
    grid_spec = pltpu.PrefetchScalarGridSpec(
        num_scalar_prefetch=4,
        grid=(nblocks,),
        in_specs=[pl.BlockSpec(memory_space=pl.ANY)] * 4,
        out_specs=pl.BlockSpec((rb, d), lambda i, *_: (i, 0)),
        scratch_shapes=[pltpu.VMEM((2, rb, d), F32),
                        pltpu.VMEM((2, d, ff), F32), pltpu.VMEM((2, d, ff), F32),
                        pltpu.VMEM((2, ff, d), F32),
                        pltpu.SemaphoreType.DMA((2,)), pltpu.SemaphoreType.DMA((2, 3))],
    )
    return pl.pallas_call(
        _experts_body,
        grid_spec=grid_spec,
        out_shape=jax.ShapeDtypeStruct((nblocks * rb, d), F32),
        compiler_params=_cparams(("arbitrary",), 56),
        name="experts",
    )(block_ord, elist, src_tok, counts2, h2, w1, w3, w2)


def _combine_body(dest_ref, y_hbm, gate_ref, x1_ref, mod_ref, g_ref, o_ref, buf_scr, sem):
    i = pl.program_id(0)
    tm = x1_ref.shape[0]

    def gather(step, act):
        slot = step % 2
        base = step * tm * TOP_K

        def body(r, carry):
            for k in range(TOP_K):
                act(pltpu.make_async_copy(y_hbm.at[pl.ds(dest_ref[base + TOP_K * r + k], 1)],
                                          buf_scr.at[slot, k, pl.ds(r, 1)], sem.at[slot]))
            return carry

        lax.fori_loop(0, tm, body, 0, unroll=DMA_UNROLL // TOP_K)

    @pl.when(i == 0)
    def _():
        gather(i, lambda cp: cp.start())

    @pl.when(i + 1 < pl.num_programs(0))
    def _():
        gather(i + 1, lambda cp: cp.start())

    gather(i, lambda cp: cp.wait())
    slot = i % 2
    gate = gate_ref[...]
    y = buf_scr[slot, 0] * gate[:, 0:1] + buf_scr[slot, 1] * gate[:, 1:2]
    o_ref[...] = x1_ref[...] + mod_ref[0, 5:6, :] * _rms(y, g_ref[...])


def _combine(dest, ybuf, gate, x1, mod, g_post, seq):
    t, d = x1.shape
    tm = 256
    per_batch = seq // tm
    grid_spec = pltpu.PrefetchScalarGridSpec(
        num_scalar_prefetch=1,
        grid=(t // tm,),
        in_specs=[pl.BlockSpec(memory_space=pl.ANY),
                  pl.BlockSpec((tm, TOP_K), lambda i, dst: (i, 0)),
                  pl.BlockSpec((tm, d), lambda i, dst: (i, 0)),
                  pl.BlockSpec((1, 6, d), lambda i, dst: (i // per_batch, 0, 0)),
                  pl.BlockSpec((1, d), lambda i, dst: (0, 0))],
        out_specs=pl.BlockSpec((tm, d), lambda i, dst: (i, 0)),
        scratch_shapes=[pltpu.VMEM((2, TOP_K, tm, d), F32), pltpu.SemaphoreType.DMA((2,))],
    )
    return pl.pallas_call(
        _combine_body,
        grid_spec=grid_spec,
        out_shape=jax.ShapeDtypeStruct((t, d), F32),
        compiler_params=_cparams(("arbitrary",), 40),
        name="combine",
    )(dest, ybuf, gate, x1, mod, g_post)


def _layer(x, c, positions, w_ada, b_ada, g_mix_pre, g_mix_post, g_ffn_pre, g_ffn_post,
           w_in, conv_w, conv_b, filt_w1, filt_b1, filt_w2, filt_b2, filt_w3, filt_freq,
           hyena_skip, w_branch_gate, b_branch_gate, w_hy_o, w_at_o, w_out,
           w_group, b_group, w_expert, b_expert, w1_exp, w3_exp, w2_exp):
    bsz, seq, d = x.shape
    t = bsz * seq
    width = hyena_skip.shape[1]
    row = lambda v: v.reshape(1, -1)

    c_pad = jnp.pad(c, ((0, 8 - bsz), (0, 0)))
    mod = _adaln(c_pad, w_ada, row(b_ada))[:bsz].reshape(bsz, 6, d)

    x2d = x.reshape(t, d)
    n_gate = w_branch_gate.shape[1]
    n_cat = n_gate + -(-w_in.shape[1] // IN_PROJ_TN) * IN_PROJ_TN
    b_cat = jnp.concatenate([b_branch_gate, jnp.zeros((n_cat - n_gate,), F32)]).reshape(1, -1)
    pg = _in_proj(x2d, mod, row(g_mix_pre), w_branch_gate.astype(BF16), w_in.astype(BF16), b_cat, seq)
    pg3 = pg.reshape(bsz, seq, -1)

    hidden = filt_w2.shape[0]
    nfeat = -(-filt_w1.shape[0] // 8) * 8
    bands = np.zeros((nfeat, 1), np.float32)
    band_vals = np.linspace(1e-4, FILTER_BANDS - 1, FILTER_BANDS, dtype=np.float32)
    bands[1:1 + FILTER_BANDS, 0] = band_vals
    bands[1 + FILTER_BANDS:1 + 2 * FILTER_BANDS, 0] = band_vals
    w1p = jnp.pad(filt_w1, ((0, nfeat - filt_w1.shape[0]), (0, 0))).T
    col = lambda v: v.reshape(-1, 1)
    max_decay = math.log(DECAY_TARGET) / FAST_DECAY_PCT
    min_decay = math.log(DECAY_TARGET) / SLOW_DECAY_PCT
    deltas = jnp.abs(jnp.linspace(min_decay, max_decay, width, dtype=F32)).reshape(1, -1)
    uw = _filters(seq, width, jnp.asarray(bands), w1p, col(filt_b1), filt_w2.T, col(filt_b2),
                  col(filt_freq), deltas, filt_w3)
    s1, s1i, s2, s2i = _fft_tables(FFT_N1, 2 * seq // FFT_N1)
    kspec = _spectra(uw, s1, s2)
    y_hy = _hyena(pg3, n_gate, conv_w, row(conv_b), hyena_skip, kspec, s1, s1i, s2, s2i, width)

    half = ROT_DIM // 2
    inv_freq = np.power(ROPE_THETA, -2.0 * np.arange(half, dtype=np.float32) / ROT_DIM).astype(np.float32)
    freq_row = np.zeros((1, LANES), np.float32)
    freq_row[0, :half] = inv_freq
    freq_row[0, half:ROT_DIM] = inv_freq
    sign_row = np.zeros((1, LANES), np.float32)
    sign_row[0, :half] = -1.0
    sign_row[0, half:ROT_DIM] = 1.0
    y_at = _attention(pg3, positions.reshape(bsz, seq, 1), jnp.asarray(freq_row), jnp.asarray(sign_row),
                      n_gate + 3 * width)

    w_r = jnp.concatenate([w_group, jnp.transpose(w_expert, (1, 0, 2)).reshape(d, N_EXPERTS)], axis=1)
    w_r = jnp.pad(w_r, ((0, 0), (0, LANES - w_r.shape[1])))
    w_r_hi = w_r.astype(BF16)
    w_r_lo = (w_r - w_r_hi.astype(F32)).astype(BF16)
    b_r = jnp.pad(jnp.concatenate([b_group, b_expert.reshape(-1)]), (0, LANES - N_EXPERT_GROUPS - N_EXPERTS))
    x1, h2, logits = _mix_out(y_hy.reshape(t, width), y_at.reshape(t, -1), pg, x2d, mod,
                              row(g_mix_post), row(g_ffn_pre), w_hy_o.astype(BF16), w_at_o.astype(BF16),
                              w_out.astype(BF16), w_r_hi, w_r_lo, b_r.reshape(1, -1), seq)

    eid, gate = _route(logits)
    flat_e = eid.reshape(t * TOP_K, 1)
    rank, counts = _rank(flat_e)
    counts = counts[0, :N_EXPERTS]
    rb = EXPERT_ROW_BLOCK
    padded = (counts + rb - 1) // rb * rb
    pends = jnp.cumsum(padded)
    pstarts = pends - padded
    pstarts_row = jnp.pad(pstarts.astype(F32), (0, LANES - N_EXPERTS)).reshape(1, LANES)
    dest = _dest(flat_e, rank, pstarts_row)[:, 0]
    n_blocks = t * TOP_K // rb + N_EXPERTS
    block_start = jnp.arange(n_blocks, dtype=jnp.int32) * rb
    block_e = jnp.minimum(jnp.searchsorted(pends, block_start, side='right'), N_EXPERTS - 1).astype(jnp.int32)
    tok = jnp.arange(t * TOP_K, dtype=jnp.int32) // TOP_K
    src_tok = (jnp.arange(n_blocks * rb, dtype=jnp.int32) % t).at[dest].set(tok)
    has_rows = jnp.cumsum((counts > 0).astype(jnp.int32))
    elist = jnp.minimum(jnp.searchsorted(has_rows, jnp.arange(1, N_EXPERTS + 1, dtype=jnp.int32), side='left'),
                        N_EXPERTS - 1).astype(jnp.int32)
    block_ord = (has_rows - 1)[block_e].astype(jnp.int32)
    counts2 = jnp.stack([pends[-1] // rb, has_rows[-1]]).astype(jnp.int32)

    ybuf = _experts(block_ord, elist, src_tok, counts2, h2, w1_exp, w3_exp, w2_exp)
    out = _combine(dest, ybuf, gate, x1, mod, row(g_ffn_post), seq)
    return out.reshape(bsz, seq, d)


def kernel(x, c, positions, w_ada, b_ada, g_mix_pre, g_mix_post, g_ffn_pre, g_ffn_post, w_in, conv_w, conv_b, filt_w1, filt_b1, filt_w2, filt_b2, filt_w3, filt_freq, hyena_skip, w_branch_gate, b_branch_gate, w_hy_o, w_at_o, w_out, w_group, b_group, w_expert, b_expert, w1_exp, w3_exp, w2_exp):
    depth = w_ada.shape[0]
    for l in range(depth):
        x = _layer(x, c, positions, w_ada[l], b_ada[l], g_mix_pre[l], g_mix_post[l], g_ffn_pre[l],
                   g_ffn_post[l], w_in[l], conv_w[l], conv_b[l], filt_w1[l], filt_b1[l], filt_w2[l],
                   filt_b2[l], filt_w3[l], filt_freq[l], hyena_skip[l], w_branch_gate[l],
                   b_branch_gate[l], w_hy_o[l], w_at_o[l], w_out[l], w_group[l], b_group[l],
                   w_expert[l], b_expert[l], w1_exp[l], w3_exp[l], w2_exp[l])
    return x
```

```python
import functools
import math

import numpy as np
import jax
import jax.numpy as jnp
from jax import lax
from jax.experimental import pallas as pl
from jax.experimental.pallas import tpu as pltpu

F32 = jnp.float32
BF16 = jnp.bfloat16

LANES = 128
MIB = 1024 * 1024

RMS_EPS = 1e-6
NEG_INF = -1e30

HEAD_DIM = 128
ROT_DIM = HEAD_DIM // 4
ROPE_THETA = 500000.0
ATTN_GROUPS = ((128, 1), (512, 4), (2048, 16))
HEADS_PER_GROUP = 4
N_ATTN_HEADS = HEADS_PER_GROUP * len(ATTN_GROUPS)

FILTER_BANDS = 16
DECAY_TARGET = 1e-2
FAST_DECAY_PCT = 0.3
SLOW_DECAY_PCT = 1.5

N_EXPERT_GROUPS = 8
EXPERTS_PER_GROUP = 8
N_EXPERTS = N_EXPERT_GROUPS * EXPERTS_PER_GROUP
TOP_K = 2
EXPERT_ROW_BLOCK = 128

FFT_N1 = 128
PITCH = FFT_N1 + 8
FFT_UNROLL = 32
FFT_MID_UNROLL = 16


def _cparams(sem, vmem_mib):
    return pltpu.CompilerParams(dimension_semantics=sem, vmem_limit_bytes=vmem_mib * MIB)


def _rms(x, g):
    return x * lax.rsqrt(jnp.mean(x * x, axis=-1, keepdims=True) + RMS_EPS) * g


def _dot_hilo(a, b):
    a_hi = a.astype(BF16)
    a_lo = (a - a_hi.astype(F32)).astype(BF16)
    b_hi = b.astype(BF16)
    b_lo = (b - b_hi.astype(F32)).astype(BF16)
    dot = functools.partial(jnp.dot, preferred_element_type=F32)
    return dot(a_hi, b_hi) + dot(a_lo, b_hi) + dot(a_hi, b_lo) + dot(a_lo, b_lo)


def _adaln_body(c_ref, w_ref, b_ref, o_ref):
    c = c_ref[...]
    cond = c * jax.nn.sigmoid(c)
    o_ref[...] = jnp.dot(cond.astype(BF16), w_ref[...].astype(BF16),
                         preferred_element_type=F32) + b_ref[...]


def _adaln(c_pad, w_ada, b_ada):
    rows, d = c_pad.shape
    n = w_ada.shape[1]
    tn = 1024
    return pl.pallas_call(
        _adaln_body,
        grid=(n // tn,),
        in_specs=[pl.BlockSpec((rows, d), lambda j: (0, 0)),
                  pl.BlockSpec((d, tn), lambda j: (0, j)),
                  pl.BlockSpec((1, tn), lambda j: (0, j))],
        out_specs=pl.BlockSpec((rows, tn), lambda j: (0, j)),
        out_shape=jax.ShapeDtypeStruct((rows, n), F32),
        compiler_params=_cparams(("arbitrary",), 40),
        name="adaln",
    )(c_pad, w_ada, b_ada)


IN_PROJ_TM = 1024
IN_PROJ_TN = 1024
IN_PROJ_SUB_M = 256
IN_PROJ_SUB_N = 512


def _in_proj_body(n_gate, last_cols, x_ref, mod_ref, g_ref, wg_ref, wi_ref, b_ref, o_ref, h_scr):
    j = pl.program_id(1)
    last = pl.num_programs(1) - 1

    tm, tn = o_ref.shape

    @pl.when(j == 0)
    def _():
        def norm(r, carry):
            rows = pl.ds(pl.multiple_of(r * IN_PROJ_SUB_M, IN_PROJ_SUB_M), IN_PROJ_SUB_M)
            h = _rms(x_ref[rows, :], g_ref[...]) * (1.0 + mod_ref[0, 1:2, :]) + mod_ref[0, 0:1, :]
            h_scr[rows, :] = h.astype(BF16)
            return carry
        lax.fori_loop(0, tm // IN_PROJ_SUB_M, norm, 0)


    def tile(w_ref, gate, ncols):
        for mi in range(tm // IN_PROJ_SUB_M):
            rows = slice(mi * IN_PROJ_SUB_M, (mi + 1) * IN_PROJ_SUB_M)
            for ni in range(tn // IN_PROJ_SUB_N):
                cols = slice(ni * IN_PROJ_SUB_N, (ni + 1) * IN_PROJ_SUB_N)
                if ni * IN_PROJ_SUB_N >= ncols:
                    o_ref[rows, cols] = jnp.zeros((IN_PROJ_SUB_M, IN_PROJ_SUB_N), o_ref.dtype)
                    continue
                acc = jnp.dot(h_scr[rows, :], w_ref[:, cols], preferred_element_type=F32) + b_ref[:, cols]
                o_ref[rows, cols] = (jax.nn.sigmoid(acc) if gate else acc).astype(o_ref.dtype)

    @pl.when(j < n_gate)
    def _():
        tile(wg_ref, True, tn)

    @pl.when((j >= n_gate) & (j < last))
    def _():
        tile(wi_ref, False, tn)

    @pl.when(j == last)
    def _():
        tile(wi_ref, False, last_cols)


def _in_proj(x2d, mod, g_pre, w_gate, w_in, b_cat, seq):
    t, d = x2d.shape
    tm, tn = IN_PROJ_TM, IN_PROJ_TN
    ng = w_gate.shape[1] // tn
    ni = pl.cdiv(w_in.shape[1], tn)
    last_cols = w_in.shape[1] - (ni - 1) * tn
    assert w_gate.shape[1] % tn == 0 and last_cols % IN_PROJ_SUB_N == 0
    per_batch = seq // tm
    return pl.pallas_call(
        functools.partial(_in_proj_body, ng, last_cols),
        grid=(t // tm, ng + ni),
        in_specs=[pl.BlockSpec((tm, d), lambda i, j: (i, 0)),
                  pl.BlockSpec((1, 6, d), lambda i, j: (i // per_batch, 0, 0)),
                  pl.BlockSpec((1, d), lambda i, j: (0, 0)),
                  pl.BlockSpec((d, tn), lambda i, j: (0, jnp.minimum(j, ng - 1))),
                  pl.BlockSpec((d, tn), lambda i, j: (0, jnp.clip(j - ng, 0, ni - 1))),
                  pl.BlockSpec((1, tn), lambda i, j: (0, j))],
        out_specs=pl.BlockSpec((tm, tn), lambda i, j: (i, j)),
        out_shape=jax.ShapeDtypeStruct((t, (ng + ni) * tn), BF16),
        scratch_shapes=[pltpu.VMEM((tm, d), BF16)],
        compiler_params=_cparams(("arbitrary", "arbitrary"), 56),
        name="in_proj",
    )(x2d, mod, g_pre, w_gate, w_in, b_cat)


def _filters_body(seq, band_ref, w1_ref, b1_ref, w2_ref, b2_ref, fr_ref, dl_ref,
                  w3a_ref, w3b_ref, o_ref, hid_scr):
    i = pl.program_id(0)
    j = pl.program_id(1)
    tl = hid_scr.shape[0]
    row = (lax.broadcasted_iota(jnp.int32, (tl, 1), 0) + i * tl).astype(F32)

    @pl.when((j == 0) & (pl.program_id(2) == 0))
    def _():
        nfeat = band_ref.shape[0]
        pos = (lax.broadcasted_iota(jnp.int32, (1, tl), 1) + i * tl).astype(F32)
        feat = lax.broadcasted_iota(jnp.int32, (nfeat, tl), 0)
        ang = band_ref[...] * (2.0 * math.pi * pos / seq)
        feats = jnp.where(feat == 0, pos / (seq - 1.0),
                          jnp.where(feat <= FILTER_BANDS, jnp.cos(ang),
                                    jnp.where(feat <= 2 * FILTER_BANDS, -jnp.sin(ang), 0.0)))
        hi = lax.Precision.HIGHEST
        fr = fr_ref[...]
        hid = jnp.sin(fr * (jnp.dot(w1_ref[...], feats, precision=hi, preferred_element_type=F32)
                            + b1_ref[...]))
        hid = jnp.sin(fr * (jnp.dot(w2_ref[...], hid, precision=hi, preferred_element_type=F32)
                            + b2_ref[...]))
        hid_scr[...] = hid.T

    hid = hid_scr[...]
    decay = jnp.exp(-(row / (seq - 1.0)) * dl_ref[...])
    hf = _dot_hilo(hid, w3a_ref[...]) * decay
    hb = _dot_hilo(hid, w3b_ref[...]) * decay
    hb = jnp.where(row == 0.0, 0.0, hb)
    o_ref[0, 0] = hf + hb
    o_ref[0, 1] = hf - hb


def _filters(seq, width, bands, w1p, b1, w2, b2, freq, deltas, w3):
    tl, tc = 512, 512
    nct = width // tc
    hidden = w2.shape[0]
    nfeat = bands.shape[0]
    const = lambda i, j, o: (0, 0)
    return pl.pallas_call(
        functools.partial(_filters_body, float(seq)),
        grid=(seq // tl, nct, 2),
        in_specs=[pl.BlockSpec((nfeat, 1), const),
                  pl.BlockSpec((hidden, nfeat), const),
                  pl.BlockSpec((hidden, 1), const),
                  pl.BlockSpec((hidden, hidden), const),
                  pl.BlockSpec((hidden, 1), const),
                  pl.BlockSpec((hidden, 1), const),
                  pl.BlockSpec((1, tc), lambda i, j, o: (0, j)),
                  pl.BlockSpec((hidden, tc), lambda i, j, o: (0, (2 * o) * nct + j)),
                  pl.BlockSpec((hidden, tc), lambda i, j, o: (0, (2 * o + 1) * nct + j))],
        out_specs=pl.BlockSpec((1, 2, tl, tc), lambda i, j, o: (o, 0, i, j)),
        out_shape=jax.ShapeDtypeStruct((2, 2, seq, width), F32),
        scratch_shapes=[pltpu.VMEM((tl, hidden), F32)],
        compiler_params=_cparams(("arbitrary", "arbitrary", "arbitrary"), 32),
        name="filters",
    )(bands, w1p, b1, w2, b2, freq, deltas, w3, w3)


def _fft_tables(n1, n2):
    n = n1 * n2
    q = np.arange(n2)[:, None]
    b = np.arange(n2 // 2)[None, :]
    a = np.arange(n1)[:, None, None]
    ang = -2.0 * np.pi * (a * q[None] / n + (q * b)[None] / n2)
    stage1 = np.concatenate([np.cos(ang), np.sin(ang)], axis=1)
    stage1_inv = np.transpose(stage1, (0, 2, 1)) / n
    p = np.arange(n1)
    ang2 = -2.0 * np.pi * np.outer(p, p) / n1
    fre, fim = np.cos(ang2), np.sin(ang2)
    stage2 = np.block([[fre, -fim], [fim, fre]])
    stage2_inv = np.block([[fre, fim], [-fim, fre]])
    as_bf16 = lambda m: jnp.asarray(m, dtype=F32).astype(BF16)
    return as_bf16(stage1), as_bf16(stage1_inv), as_bf16(stage2), as_bf16(stage2_inv)


def _halves(ref, rows):
    return jnp.concatenate([ref[0, rows, :], ref[1, rows, :]], axis=1)


def _fft_stage1(z_ref, s1_ref, gre_ref, gim_ref):
    n1, two_n2, n2h = s1_ref.shape
    n2 = two_n2 // 2

    def step(a, carry):
        zrows = _halves(z_ref, pl.ds(a, n2h, stride=PITCH))
        g = jnp.dot(s1_ref[a], zrows.astype(BF16), preferred_element_type=F32)
        for h in range(2):
            cols = slice(h * LANES, (h + 1) * LANES)
            gre_ref[h, pl.ds(a, n2, stride=PITCH), :] = g[:n2, cols]
            gim_ref[h, pl.ds(a, n2, stride=PITCH), :] = g[n2:, cols]
        return carry

    lax.fori_loop(0, n1, step, 0, unroll=FFT_UNROLL)


def _fft_stage1_inv(gre_ref, gim_ref, s1i_ref, y_ref):
    n1, n2h, two_n2 = s1i_ref.shape
    n2 = two_n2 // 2

    def step(a, carry):
        rows = pl.ds(a, n2, stride=PITCH)
        hs = jnp.concatenate([_halves(gre_ref, rows), _halves(gim_ref, rows)], axis=0)
        y = jnp.dot(s1i_ref[a], hs.astype(BF16), preferred_element_type=F32)
        for h in range(2):
            y_ref[h, pl.ds(a, n2h, stride=PITCH), :] = y[:, h * LANES:(h + 1) * LANES]
        return carry

    lax.fori_loop(0, n1, step, 0, unroll=FFT_UNROLL)


def _stage2_block(gre_ref, gim_ref, s2_ref, q):
    n1 = s2_ref.shape[0] // 2
    rows = pl.ds(pl.multiple_of(q * PITCH, 8), n1)
    gs = jnp.concatenate([_halves(gre_ref, rows), _halves(gim_ref, rows)], axis=0)
    x = jnp.dot(s2_ref[...], gs.astype(BF16), preferred_element_type=F32)
    return x[:n1], x[n1:]


def _spectra_body(uw_ref, s1_ref, s2_ref, k_ref, z_scr, gre_scr, gim_scr):
    n1 = s2_ref.shape[0] // 2
    n2 = s1_ref.shape[1] // 2
    for h in range(2):
        for b in range(n2 // 2):
            z_scr[h, pl.ds(b * PITCH, n1), :] = uw_ref[0, h, pl.ds(b * n1, n1), :]
    _fft_stage1(z_scr, s1_ref, gre_scr, gim_scr)

    def step(q, carry):
        xre, xim = _stage2_block(gre_scr, gim_scr, s2_ref, q)
        rows = pl.ds(pl.multiple_of(q * n1, 8), n1)
        k_ref[0, 0, rows, :] = xre[:, :LANES]
        k_ref[0, 1, rows, :] = xim[:, LANES:]
        return carry

    lax.fori_loop(0, n2, step, 0, unroll=FFT_MID_UNROLL)


def _spectra(uw, s1, s2):
    _, _, seq, width = uw.shape
    n1 = FFT_N1
    n2 = 2 * seq // n1
    const3 = lambda j, o: (0, 0, 0)
    return pl.pallas_call(
        _spectra_body,
        grid=(width // LANES, 2),
        in_specs=[pl.BlockSpec((1, 2, seq, LANES), lambda j, o: (o, 0, 0, j)),
                  pl.BlockSpec(s1.shape, const3),
                  pl.BlockSpec(s2.shape, lambda j, o: (0, 0))],
        out_specs=pl.BlockSpec((1, 2, 2 * seq, LANES), lambda j, o: (o, 0, 0, j)),
        out_shape=jax.ShapeDtypeStruct((2, 2, 2 * seq, width), F32),
        scratch_shapes=[pltpu.VMEM((2, (n2 // 2) * PITCH, LANES), F32),
                        pltpu.VMEM((2, n2 * PITCH, LANES), F32),
                        pltpu.VMEM((2, n2 * PITCH, LANES), F32)],
        compiler_params=_cparams(("arbitrary", "arbitrary"), 56),
        name="spectra",
    )(uw, s1, s2)


def _short_conv_block(p_ref, b, i, nblk, w_ref, bias_ref):
    n1 = FFT_N1
    start = pl.multiple_of(i * n1, n1)
    cur = p_ref[b, pl.ds(start, n1), :].astype(F32)
    before = p_ref[b, pl.ds(pl.multiple_of(jnp.maximum(start - 16, 0), 16), 16), :].astype(F32)
    after = p_ref[b, pl.ds(pl.multiple_of(jnp.minimum(start + n1, (nblk - 1) * n1), 16), 16), :].astype(F32)
    last_prev = before[15:16] * jnp.where(i > 0, 1.0, 0.0).astype(F32)
    first_next = after[0:1] * jnp.where(i < nblk - 1, 1.0, 0.0).astype(F32)
    row = lax.broadcasted_iota(jnp.int32, (n1, 1), 0)
    prev = jnp.where(row == 0, last_prev, pltpu.roll(cur, 1, 0))
    nxt = jnp.where(row == n1 - 1, first_next, pltpu.roll(cur, n1 - 1, 0))
    return prev * w_ref[0:1, :] + cur * w_ref[1:2, :] + nxt * w_ref[2:3, :] + bias_ref[...]


def _hyena_body(pv_ref, px1_ref, px2_ref, cwv_ref, cw1_ref, cw2_ref, cbv_ref, cb1_ref, cb2_ref,
                skip_ref, k_ref, s1_ref, s1i_ref, s2_ref, s2i_ref, o_ref,
                z_scr, y_scr, gre_scr, gim_scr):
    c = pl.program_id(1)
    g = pl.program_id(2)
    ngroups = pl.num_programs(2)
    n1 = FFT_N1
    n2 = s1_ref.shape[1] // 2
    nblk = n2 // 2
    per_group = n2 // ngroups

    @pl.when(g == 0)
    def _():
        @pl.when(c == 0)
        def _():
            def fill(i, carry):
                for b in range(2):
                    z_scr[b, pl.ds(pl.multiple_of(i * PITCH, 8), n1), :] = _short_conv_block(
                        pv_ref, b, i, nblk, cwv_ref, cbv_ref)
                return carry
            lax.fori_loop(0, nblk, fill, 0)

        _fft_stage1(z_scr, s1_ref, gre_scr, gim_scr)

    def mid(ql, carry):
        q = g * per_group + ql
        xre, xim = _stage2_block(gre_scr, gim_scr, s2_ref, q)
        krows = pl.ds(pl.multiple_of(ql * n1, 8), n1)
        kre = k_ref[0, 0, krows, :]
        kim = k_ref[0, 1, krows, :]
        kre = jnp.concatenate([kre, kre], axis=1)
        kim = jnp.concatenate([kim, kim], axis=1)
        ys = jnp.concatenate([xre * kre - xim * kim, xre * kim + xim * kre], axis=0)
        hh = jnp.dot(s2i_ref[...], ys.astype(BF16), preferred_element_type=F32)
        rows = pl.ds(pl.multiple_of(q * PITCH, 8), n1)
        for b in range(2):
            cols = slice(b * LANES, (b + 1) * LANES)
            gre_scr[b, rows, :] = hh[:n1, cols]
            gim_scr[b, rows, :] = hh[n1:, cols]
        return carry

    lax.fori_loop(0, per_group, mid, 0, unroll=FFT_MID_UNROLL)

    @pl.when(g == ngroups - 1)
    def _():
        _fft_stage1_inv(gre_scr, gim_scr, s1i_ref, y_scr)

        def post(px_ref, cw_ref, cb_ref, order, store):
            def blk(i, carry):
                rows = pl.ds(pl.multiple_of(i * PITCH, 8), n1)
                for b in range(2):
                    zb = z_scr[b, rows, :]
                    gate = _short_conv_block(px_ref, b, i, nblk, cw_ref, cb_ref)
                    store(b, i, rows, gate * (y_scr[b, rows, :] + zb * skip_ref[order:order + 1, :]))
                return carry
            lax.fori_loop(0, nblk, blk, 0)

        @pl.when(c == 0)
        def _():
            def store(b, i, rows, val):
                z_scr[b, rows, :] = val
            post(px1_ref, cw1_ref, cb1_ref, 0, store)

        @pl.when(c == 1)
        def _():
            def store(b, i, rows, val):
                o_ref[b, pl.ds(pl.multiple_of(i * n1, n1), n1), :] = val.astype(o_ref.dtype)
            post(px2_ref, cw2_ref, cb2_ref, 1, store)


def _hyena(pg3, proj_col0, conv_w, conv_b, skip, kspec, s1, s1i, s2, s2i, width):
    bsz, seq, _ = pg3.shape
    assert bsz == 2
    n1 = FFT_N1
    n2 = 2 * seq // n1
    nct = width // LANES
    ngroups = 4
    krows = (n2 // ngroups) * n1
    col = lambda off: (lambda j, c, g: (0, 0, proj_col0 // LANES + off * nct + j))
    cw = lambda off: (lambda j, c, g: (0, off * nct + j))
    const3 = lambda j, c, g: (0, 0, 0)
    const2 = lambda j, c, g: (0, 0)
    return pl.pallas_call(
        _hyena_body,
        grid=(nct, 2, ngroups),
        in_specs=[pl.BlockSpec((2, seq, LANES), col(0)),
                  pl.BlockSpec((2, seq, LANES), col(1)),
                  pl.BlockSpec((2, seq, LANES), col(2)),
                  pl.BlockSpec((3, LANES), cw(0)),
                  pl.BlockSpec((3, LANES), cw(1)),
                  pl.BlockSpec((3, LANES), cw(2)),
                  pl.BlockSpec((1, LANES), cw(0)),
                  pl.BlockSpec((1, LANES), cw(1)),
                  pl.BlockSpec((1, LANES), cw(2)),
                  pl.BlockSpec((2, LANES), lambda j, c, g: (0, j)),
                  pl.BlockSpec((1, 2, krows, LANES), lambda j, c, g: (c, 0, g, j)),
                  pl.BlockSpec(s1.shape, const3),
                  pl.BlockSpec(s1i.shape, const3),
                  pl.BlockSpec(s2.shape, const2),
                  pl.BlockSpec(s2i.shape, const2)],
        out_specs=pl.BlockSpec((2, seq, LANES), lambda j, c, g: (0, 0, j)),
        out_shape=jax.ShapeDtypeStruct((2, seq, width), BF16),
        scratch_shapes=[pltpu.VMEM((2, (n2 // 2) * PITCH, LANES), F32),
                        pltpu.VMEM((2, (n2 // 2) * PITCH, LANES), F32),
                        pltpu.VMEM((2, n2 * PITCH, LANES), F32),
                        pltpu.VMEM((2, n2 * PITCH, LANES), F32)],
        compiler_params=_cparams(("arbitrary", "arbitrary", "arbitrary"), 56),
        name="hyena",
    )(pg3, pg3, pg3, conv_w, conv_w, conv_w, conv_b, conv_b, conv_b, skip, kspec, s1, s1i, s2, s2i)


ATTN_QBLK = 128
ATTN_UNROLL = 16


def _attention_body(pos_ref, freq_ref, sign_ref, *refs):
    qkv_refs = refs[:9]
    o_ref = refs[9]
    cos_scr, sin_scr, q_scr, k_scr, v_scr, og_scr, lse_scr = refs[10:]
    seq = q_scr.shape[0]
    chunk = 512
    nchunks = seq // chunk

    @pl.when(pl.program_id(1) == 0)
    def _():
        def trig(i, carry):
            rows = pl.ds(pl.multiple_of(i * chunk, chunk), chunk)
            ang = pos_ref[0, rows, :].astype(F32) * freq_ref[...]
            cos_scr[rows, :] = jnp.cos(ang)
            sin_scr[rows, :] = jnp.sin(ang) * sign_ref[...]
            return carry
        lax.fori_loop(0, nchunks, trig, 0)

    def rotate(src_ref, dst_ref):
        def body(i, carry):
            rows = pl.ds(pl.multiple_of(i * chunk, chunk), chunk)
            t = src_ref[0, rows, :].astype(F32)
            lane = lax.broadcasted_iota(jnp.int32, t.shape, 1)
            partner = jnp.where(lane < ROT_DIM // 2,
                                pltpu.roll(t, LANES - ROT_DIM // 2, 1), pltpu.roll(t, ROT_DIM // 2, 1))
            dst_ref[rows, :] = t * cos_scr[rows, :] + partner * sin_scr[rows, :]
            return carry
        lax.fori_loop(0, nchunks, body, 0, unroll=2)

    def widen(src_ref, dst_ref):
        def body(i, carry):
            rows = pl.ds(pl.multiple_of(i * chunk, chunk), chunk)
            dst_ref[rows, :] = src_ref[0, rows, :].astype(F32)
            return carry
        lax.fori_loop(0, nchunks, body, 0, unroll=2)

    scale = HEAD_DIM ** -0.5
    for gi, (window, dil) in enumerate(ATTN_GROUPS):
        rotate(qkv_refs[3 * gi], q_scr)
        rotate(qkv_refs[3 * gi + 1], k_scr)
        widen(qkv_refs[3 * gi + 2], v_scr)
        n = seq // dil
        half = window // (2 * dil)
        tk = min(n, ATTN_QBLK + 2 * half)
        blocks_per_res = n // ATTN_QBLK

        def block(u, carry, dil=dil, n=n, half=half, tk=tk, blocks_per_res=blocks_per_res, gi=gi):
            r = u // blocks_per_res
            m = u % blocks_per_res
            q0 = m * ATTN_QBLK
            k0 = jnp.clip(q0 - half, 0, n - tk)
            qrows = pl.ds(r + dil * q0, ATTN_QBLK, stride=dil)
            krows = pl.ds(r + dil * k0, tk, stride=dil)
            qb = q_scr[qrows, :].astype(BF16)
            kb = k_scr[krows, :].astype(BF16)
            vb = v_scr[krows, :].astype(BF16)
            s = lax.dot_general(qb, kb, (((1,), (1,)), ((), ())), preferred_element_type=F32) * scale
            qi = q0 + lax.broadcasted_iota(jnp.int32, (ATTN_QBLK, tk), 0)
            kj = k0 + lax.broadcasted_iota(jnp.int32, (ATTN_QBLK, tk), 1)
            s = jnp.where(jnp.abs(qi - kj) <= half, s, NEG_INF)
            mx = jnp.max(s, axis=-1, keepdims=True)
            p = jnp.exp(s - mx)
            l = jnp.sum(p, axis=-1, keepdims=True)
            o = jnp.dot(p.astype(BF16), vb, preferred_element_type=F32) / l
            og_scr[gi, qrows, :] = o
            lse_scr[gi, qrows, :] = jnp.broadcast_to(mx + jnp.log(l), (ATTN_QBLK, LANES))
            return carry

        lax.fori_loop(0, seq // ATTN_QBLK, block, 0, unroll=ATTN_UNROLL)

    def merge(i, carry):
        rows = pl.ds(pl.multiple_of(i * chunk, chunk), chunk)
        lses = [lse_scr[gi, rows, :] for gi in range(len(ATTN_GROUPS))]
        mx = functools.reduce(jnp.maximum, lses)
        ws = [jnp.exp(v - mx) for v in lses]
        den = functools.reduce(lambda a, b: a + b, ws)
        num = functools.reduce(lambda a, b: a + b,
                               [w * og_scr[gi, rows, :] for gi, w in enumerate(ws)])
        o_ref[0, rows, :] = (num / den).astype(o_ref.dtype)
        return carry

    lax.fori_loop(0, nchunks, merge, 0)


def _attention(pg3, pos3, freq_row, sign_row, qkv_col0):
    bsz, seq, _ = pg3.shape
    ng = len(ATTN_GROUPS)
    in_specs = [pl.BlockSpec((1, seq, 1), lambda b, h: (b, 0, 0)),
                pl.BlockSpec((1, LANES), lambda b, h: (0, 0)),
                pl.BlockSpec((1, LANES), lambda b, h: (0, 0))]
    for gi in range(ng):
        for which in range(3):
            base = qkv_col0 // LANES + which * N_ATTN_HEADS + gi * HEADS_PER_GROUP
            in_specs.append(pl.BlockSpec((1, seq, LANES), lambda b, h, base=base: (b, 0, base + h)))
    return pl.pallas_call(
        _attention_body,
        grid=(bsz, HEADS_PER_GROUP),
        in_specs=in_specs,
        out_specs=pl.BlockSpec((1, seq, LANES), lambda b, h: (b, 0, h)),
        out_shape=jax.ShapeDtypeStruct((bsz, seq, HEADS_PER_GROUP * HEAD_DIM), BF16),
        scratch_shapes=[pltpu.VMEM((seq, LANES), F32), pltpu.VMEM((seq, LANES), F32),
                        pltpu.VMEM((seq, LANES), F32), pltpu.VMEM((seq, LANES), F32),
                        pltpu.VMEM((seq, LANES), F32),
                        pltpu.VMEM((ng, seq, LANES), F32), pltpu.VMEM((ng, seq, LANES), F32)],
        compiler_params=_cparams(("arbitrary", "arbitrary"), 56),
        name="attention",
    )(pos3, freq_row, sign_row, *([pg3] * (3 * ng)))


def _mix_out_body(yhy_ref, yat_ref, ghy_ref, gat_ref, x_ref, mod_ref, gpost_ref, gpre_ref,
                  whyo_ref, wato_ref, wout_ref, wrhi_ref, wrlo_ref, br_ref, x1_ref, h2_ref, lg_ref):
    a = jnp.dot(yhy_ref[...], whyo_ref[...], preferred_element_type=F32)
    b = jnp.dot(yat_ref[...], wato_ref[...], preferred_element_type=F32)
    merged = ghy_ref[...].astype(F32) * a + gat_ref[...].astype(F32) * b
    y = jnp.dot(merged.astype(BF16), wout_ref[...], preferred_element_type=F32)
    x1 = x_ref[...] + mod_ref[0, 2:3, :] * _rms(y, gpost_ref[...])
    x1_ref[...] = x1
    h2 = _rms(x1, gpre_ref[...]) * (1.0 + mod_ref[0, 4:5, :]) + mod_ref[0, 3:4, :]
    h2_ref[...] = h2
    h2_hi = h2.astype(BF16)
    h2_lo = (h2 - h2_hi.astype(F32)).astype(BF16)
    w_hi = wrhi_ref[...]
    w_lo = wrlo_ref[...]
    lg_ref[...] = (jnp.dot(h2_hi, w_hi, preferred_element_type=F32)
                   + jnp.dot(h2_lo, w_hi, preferred_element_type=F32)
                   + jnp.dot(h2_hi, w_lo, preferred_element_type=F32)
                   + jnp.dot(h2_lo, w_lo, preferred_element_type=F32)) + br_ref[...]


def _mix_out(y_hy, y_at, pg, x2d, mod, g_post, g_pre, w_hy_o, w_at_o, w_out, w_r_hi, w_r_lo, b_r, seq):
    t, d = x2d.shape
    tm = 256
    per_batch = seq // tm
    gblk = 0
    const = lambda i: (0, 0)
    return pl.pallas_call(
        _mix_out_body,
        grid=(t // tm,),
        in_specs=[pl.BlockSpec((tm, y_hy.shape[1]), lambda i: (i, 0)),
                  pl.BlockSpec((tm, y_at.shape[1]), lambda i: (i, 0)),
                  pl.BlockSpec((tm, d), lambda i: (i, gblk)),
                  pl.BlockSpec((tm, d), lambda i: (i, gblk + 1)),
                  pl.BlockSpec((tm, d), lambda i: (i, 0)),
                  pl.BlockSpec((1, 6, d), lambda i: (i // per_batch, 0, 0)),
                  pl.BlockSpec((1, d), const),
                  pl.BlockSpec((1, d), const),
                  pl.BlockSpec(w_hy_o.shape, const),
                  pl.BlockSpec(w_at_o.shape, const),
                  pl.BlockSpec(w_out.shape, const),
                  pl.BlockSpec(w_r_hi.shape, const),
                  pl.BlockSpec(w_r_lo.shape, const),
                  pl.BlockSpec((1, LANES), const)],
        out_specs=[pl.BlockSpec((tm, d), lambda i: (i, 0)),
                   pl.BlockSpec((tm, d), lambda i: (i, 0)),
                   pl.BlockSpec((tm, LANES), lambda i: (i, 0))],
        out_shape=[jax.ShapeDtypeStruct((t, d), F32),
                   jax.ShapeDtypeStruct((t, d), F32),
                   jax.ShapeDtypeStruct((t, LANES), F32)],
        compiler_params=_cparams(("arbitrary",), 56),
        name="mix_out",
    )(y_hy, y_at, pg, pg, x2d, mod, g_post, g_pre, w_hy_o, w_at_o, w_out, w_r_hi, w_r_lo, b_r)


def _route_body(lg_ref, eid_ref, gate_ref):
    lg = lg_ref[...]
    lane = lax.broadcasted_iota(jnp.int32, lg.shape, 1)
    big = jnp.int32(1 << 20)

    def first_argmax(vals, mask):
        v = jnp.where(mask, vals, -jnp.inf)
        mx = jnp.max(v, axis=-1, keepdims=True)
        idx = jnp.min(jnp.where(mask & (v == mx), lane, big), axis=-1, keepdims=True)
        return mx, idx

    gmask = lane < N_EXPERT_GROUPS
    gmax, gidx = first_argmax(lg, gmask)
    gval = 1.0 / jnp.sum(jnp.where(gmask, jnp.exp(lg - gmax), 0.0), axis=-1, keepdims=True)
    lo = N_EXPERT_GROUPS + gidx * EXPERTS_PER_GROUP
    emask = (lane >= lo) & (lane < lo + EXPERTS_PER_GROUP)
    v1, i1 = first_argmax(lg, emask)
    v2, i2 = first_argmax(lg, emask & (lane != i1))
    e2 = jnp.exp(v2 - v1)
    p1 = 1.0 / (1.0 + e2)
    p2 = e2 / (1.0 + e2)
    eid = jnp.where(lane == 0, i1, i2) - N_EXPERT_GROUPS
    gate = gval * jnp.where(lane == 0, p1, p2)
    eid_ref[...] = eid[:, :TOP_K]
    gate_ref[...] = gate[:, :TOP_K]


def _route(logits):
    t = logits.shape[0]
    tm = 512
    return pl.pallas_call(
        _route_body,
        grid=(t // tm,),
        in_specs=[pl.BlockSpec((tm, LANES), lambda i: (i, 0))],
        out_specs=[pl.BlockSpec((tm, TOP_K), lambda i: (i, 0)),
                   pl.BlockSpec((tm, TOP_K), lambda i: (i, 0))],
        out_shape=[jax.ShapeDtypeStruct((t, TOP_K), jnp.int32),
                   jax.ShapeDtypeStruct((t, TOP_K), F32)],
        compiler_params=_cparams(("arbitrary",), 32),
        name="route",
    )(logits)


def _rank_body(e_ref, rank_ref, cnt_ref, carry_scr):
    i = pl.program_id(0)
    r = e_ref.shape[0]

    @pl.when(i == 0)
    def _():
        carry_scr[...] = jnp.zeros_like(carry_scr)

    lane = lax.broadcasted_iota(jnp.int32, (r, LANES), 1)
    onehot = (lane == e_ref[...]).astype(F32)
    tri = (lax.broadcasted_iota(jnp.int32, (r, r), 1)
           < lax.broadcasted_iota(jnp.int32, (r, r), 0)).astype(BF16)
    before = jnp.dot(tri, onehot.astype(BF16), preferred_element_type=F32) + carry_scr[0:1, :]
    rank_ref[...] = jnp.sum(onehot * before, axis=-1, keepdims=True).astype(jnp.int32)
    total = carry_scr[0:1, :] + jnp.sum(onehot, axis=0, keepdims=True)
    carry_scr[...] = jnp.broadcast_to(total, carry_scr.shape)
    cnt_ref[...] = jnp.broadcast_to(total, cnt_ref.shape).astype(jnp.int32)


def _rank(flat_e):
    a = flat_e.shape[0]
    r = 512
    return pl.pallas_call(
        _rank_body,
        grid=(a // r,),
        in_specs=[pl.BlockSpec((r, 1), lambda i: (i, 0))],
        out_specs=[pl.BlockSpec((r, 1), lambda i: (i, 0)),
                   pl.BlockSpec((8, LANES), lambda i: (0, 0))],
        out_shape=[jax.ShapeDtypeStruct((a, 1), jnp.int32),
                   jax.ShapeDtypeStruct((8, LANES), jnp.int32)],
        scratch_shapes=[pltpu.VMEM((8, LANES), F32)],
        compiler_params=_cparams(("arbitrary",), 32),
        name="rank",
    )(flat_e)


def _dest_body(e_ref, rank_ref, ps_ref, o_ref):
    lane = lax.broadcasted_iota(jnp.int32, (e_ref.shape[0], LANES), 1)
    first = jnp.sum(jnp.where(lane == e_ref[...], ps_ref[...], 0.0), axis=-1, keepdims=True)
    o_ref[...] = first.astype(jnp.int32) + rank_ref[...]


def _dest(flat_e, rank, pstarts_row):
    a = flat_e.shape[0]
    r = 2048
    return pl.pallas_call(
        _dest_body,
        grid=(a // r,),
        in_specs=[pl.BlockSpec((r, 1), lambda i: (i, 0)),
                  pl.BlockSpec((r, 1), lambda i: (i, 0)),
                  pl.BlockSpec((1, LANES), lambda i: (0, 0))],
        out_specs=pl.BlockSpec((r, 1), lambda i: (i, 0)),
        out_shape=jax.ShapeDtypeStruct((a, 1), jnp.int32),
        compiler_params=_cparams(("arbitrary",), 32),
        name="dest",
    )(flat_e, rank, pstarts_row)


DMA_UNROLL = 16


def _experts_body(ord_ref, elist_ref, src_ref, cnt_ref, h_hbm, w1_hbm, w3_hbm, w2_hbm, y_ref,
                  x_scr, w1_scr, w3_scr, w2_scr, sem, wsem):
    i = pl.program_id(0)
    nrows = x_scr.shape[1]
    nused = cnt_ref[0]
    nexp = cnt_ref[1]
    k = ord_ref[i]
    first = (i == 0) | (k != ord_ref[jnp.maximum(i - 1, 0)])

    def weights(kk, act):
        e = elist_ref[kk]
        slot = kk % 2
        act(pltpu.make_async_copy(w1_hbm.at[e], w1_scr.at[slot], wsem.at[slot, 0]))
        act(pltpu.make_async_copy(w3_hbm.at[e], w3_scr.at[slot], wsem.at[slot, 1]))
        act(pltpu.make_async_copy(w2_hbm.at[e], w2_scr.at[slot], wsem.at[slot, 2]))

    def gather(blk, act):
        slot = blk % 2
        base = blk * nrows

        def body(r, carry):
            act(pltpu.make_async_copy(h_hbm.at[pl.ds(src_ref[base + r], 1)],
                                      x_scr.at[slot, pl.ds(r, 1)], sem.at[slot]))
            return carry

        lax.fori_loop(0, nrows, body, 0, unroll=DMA_UNROLL)

    @pl.when(i == 0)
    def _():
        weights(k, lambda cp: cp.start())
        gather(i, lambda cp: cp.start())

    @pl.when(i + 1 < nused)
    def _():
        gather(i + 1, lambda cp: cp.start())

    @pl.when((i < nused) & first & (k + 1 < nexp))
    def _():
        weights(k + 1, lambda cp: cp.start())

    @pl.when((i < nused) & first)
    def _():
        weights(k, lambda cp: cp.wait())

    @pl.when(i < nused)
    def _():
        gather(i, lambda cp: cp.wait())
        wslot = k % 2
        xb = x_scr[i % 2].astype(BF16)
        a = jnp.dot(xb, w1_scr[wslot].astype(BF16), preferred_element_type=F32)
        b = jnp.dot(xb, w3_scr[wslot].astype(BF16), preferred_element_type=F32)
        act = (a * jax.nn.sigmoid(a) * b).astype(BF16)
        y_ref[...] = jnp.dot(act, w2_scr[wslot].astype(BF16), preferred_element_type=F32)

    @pl.when(i >= nused)
    def _():
        y_ref[...] = jnp.zeros_like(y_ref)


def _experts(block_ord, elist, src_tok, counts2, h2, w1, w3, w2):
    nblocks = block_ord.shape[0]
    d = h2.shape[1]
    ff = w1.shape[2]
    rb = EXPERT_ROW_BLOCK
    grid_spec = pltpu.PrefetchScalarGridSpec(
        num_scalar_prefetch=4,
        grid=(nblocks,),
        in_specs=[pl.BlockSpec(memory_space=pl.ANY)] * 4,
        out_specs=pl.BlockSpec((rb, d), lambda i, *_: (i, 0)),
        scratch_shapes=[pltpu.VMEM((2, rb, d), F32),
                        pltpu.VMEM((2, d, ff), F32), pltpu.VMEM((2, d, ff), F32),
                        pltpu.VMEM((2, ff, d), F32),
                        pltpu.SemaphoreType.DMA((2,)), pltpu.SemaphoreType.DMA((2, 3))],
    )
    return pl.pallas_call(
        _experts_body,
        grid_spec=grid_spec,
        out_shape=jax.ShapeDtypeStruct((nblocks * rb, d), F32),
        compiler_params=_cparams(("arbitrary",), 56),
        name="experts",
    )(block_ord, elist, src_tok, counts2, h2, w1, w3, w2)


def _combine_body(dest_ref, y_hbm, gate_ref, x1_ref, mod_ref, g_ref, o_ref, buf_scr, sem):
    i = pl.program_id(0)
    tm = x1_ref.shape[0]

    def gather(step, act):
        slot = step % 2
        base = step * tm * TOP_K

        def body(r, carry):
            for k in range(TOP_K):
                act(pltpu.make_async_copy(y_hbm.at[pl.ds(dest_ref[base + TOP_K * r + k], 1)],
                                          buf_scr.at[slot, k, pl.ds(r, 1)], sem.at[slot]))
            return carry

        lax.fori_loop(0, tm, body, 0, unroll=DMA_UNROLL // TOP_K)

    @pl.when(i == 0)
    def _():
        gather(i, lambda cp: cp.start())

    @pl.when(i + 1 < pl.num_programs(0))
    def _():
        gather(i + 1, lambda cp: cp.start())

    gather(i, lambda cp: cp.wait())
    slot = i % 2
    gate = gate_ref[...]
    y = buf_scr[slot, 0] * gate[:, 0:1] + buf_scr[slot, 1] * gate[:, 1:2]
    o_ref[...] = x1_ref[...] + mod_ref[0, 5:6, :] * _rms(y, g_ref[...])


def _combine(dest, ybuf, gate, x1, mod, g_post, seq):
    t, d = x1.shape
    tm = 512
    per_batch = seq // tm
    grid_spec = pltpu.PrefetchScalarGridSpec(
        num_scalar_prefetch=1,
        grid=(t // tm,),
        in_specs=[pl.BlockSpec(memory_space=pl.ANY),
                  pl.BlockSpec((tm, TOP_K), lambda i, dst: (i, 0)),
                  pl.BlockSpec((tm, d), lambda i, dst: (i, 0)),
                  pl.BlockSpec((1, 6, d), lambda i, dst: (i // per_batch, 0, 0)),
                  pl.BlockSpec((1, d), lambda i, dst: (0, 0))],
        out_specs=pl.BlockSpec((tm, d), lambda i, dst: (i, 0)),
        scratch_shapes=[pltpu.VMEM((2, TOP_K, tm, d), F32), pltpu.SemaphoreType.DMA((2,))],
    )
    return pl.pallas_call(
        _combine_body,
        grid_spec=grid_spec,
        out_shape=jax.ShapeDtypeStruct((t, d), F32),
        compiler_params=_cparams(("arbitrary",), 56),
        name="combine",
    )(dest, ybuf, gate, x1, mod, g_post)


def _layer(x, c, positions, w_ada, b_ada, g_mix_pre, g_mix_post, g_ffn_pre, g_ffn_post,
           w_in, conv_w, conv_b, filt_w1, filt_b1, filt_w2, filt_b2, filt_w3, filt_freq,
           hyena_skip, w_branch_gate, b_branch_gate, w_hy_o, w_at_o, w_out,
           w_group, b_group, w_expert, b_expert, w1_exp, w3_exp, w2_exp):
    bsz, seq, d = x.shape
    t = bsz * seq
    width = hyena_skip.shape[1]
    row = lambda v: v.reshape(1, -1)

    c_pad = jnp.pad(c, ((0, 8 - bsz), (0, 0)))
    mod = _adaln(c_pad, w_ada, row(b_ada))[:bsz].reshape(bsz, 6, d)

    x2d = x.reshape(t, d)
    n_gate = w_branch_gate.shape[1]
    n_cat = n_gate + -(-w_in.shape[1] // IN_PROJ_TN) * IN_PROJ_TN
    b_cat = jnp.concatenate([b_branch_gate, jnp.zeros((n_cat - n_gate,), F32)]).reshape(1, -1)
    pg = _in_proj(x2d, mod, row(g_mix_pre), w_branch_gate.astype(BF16), w_in.astype(BF16), b_cat, seq)
    pg3 = pg.reshape(bsz, seq, -1)

    hidden = filt_w2.shape[0]
    nfeat = -(-filt_w1.shape[0] // 8) * 8
    bands = np.zeros((nfeat, 1), np.float32)
    band_vals = np.linspace(1e-4, FILTER_BANDS - 1, FILTER_BANDS, dtype=np.float32)
    bands[1:1 + FILTER_BANDS, 0] = band_vals
    bands[1 + FILTER_BANDS:1 + 2 * FILTER_BANDS, 0] = band_vals
    w1p = jnp.pad(filt_w1, ((0, nfeat - filt_w1.shape[0]), (0, 0))).T
    col = lambda v: v.reshape(-1, 1)
    max_decay = math.log(DECAY_TARGET) / FAST_DECAY_PCT
    min_decay = math.log(DECAY_TARGET) / SLOW_DECAY_PCT
    deltas = jnp.abs(jnp.linspace(min_decay, max_decay, width, dtype=F32)).reshape(1, -1)
    uw = _filters(seq, width, jnp.asarray(bands), w1p, col(filt_b1), filt_w2.T, col(filt_b2),
                  col(filt_freq), deltas, filt_w3)
    s1, s1i, s2, s2i = _fft_tables(FFT_N1, 2 * seq // FFT_N1)
    kspec = _spectra(uw, s1, s2)
    y_hy = _hyena(pg3, n_gate, conv_w, row(conv_b), hyena_skip, kspec, s1, s1i, s2, s2i, width)

    half = ROT_DIM // 2
    inv_freq = np.power(ROPE_THETA, -2.0 * np.arange(half, dtype=np.float32) / ROT_DIM).astype(np.float32)
    freq_row = np.zeros((1, LANES), np.float32)
    freq_row[0, :half] = inv_freq
    freq_row[0, half:ROT_DIM] = inv_freq
    sign_row = np.zeros((1, LANES), np.float32)
    sign_row[0, :half] = -1.0
    sign_row[0, half:ROT_DIM] = 1.0
    y_at = _attention(pg3, positions.reshape(bsz, seq, 1), jnp.asarray(freq_row), jnp.asarray(sign_row),
                      n_gate + 3 * width)

    w_r = jnp.concatenate([w_group, jnp.transpose(w_expert, (1, 0, 2)).reshape(d, N_EXPERTS)], axis=1)
    w_r = jnp.pad(w_r, ((0, 0), (0, LANES - w_r.shape[1])))
    w_r_hi = w_r.astype(BF16)
    w_r_lo = (w_r - w_r_hi.astype(F32)).astype(BF16)
    b_r = jnp.pad(jnp.concatenate([b_group, b_expert.reshape(-1)]), (0, LANES - N_EXPERT_GROUPS - N_EXPERTS))
    x1, h2, logits = _mix_out(y_hy.reshape(t, width), y_at.reshape(t, -1), pg, x2d, mod,
                              row(g_mix_post), row(g_ffn_pre), w_hy_o.astype(BF16), w_at_o.astype(BF16),
                              w_out.astype(BF16), w_r_hi, w_r_lo, b_r.reshape(1, -1), seq)

    eid, gate = _route(logits)
    flat_e = eid.reshape(t * TOP_K, 1)
    rank, counts = _rank(flat_e)
    counts = counts[0, :N_EXPERTS]
    rb = EXPERT_ROW_BLOCK
    padded = (counts + rb - 1) // rb * rb
    pends = jnp.cumsum(padded)
    pstarts = pends - padded
    pstarts_row = jnp.pad(pstarts.astype(F32), (0, LANES - N_EXPERTS)).reshape(1, LANES)
    dest = _dest(flat_e, rank, pstarts_row)[:, 0]
    n_blocks = t * TOP_K // rb + N_EXPERTS
    block_start = jnp.arange(n_blocks, dtype=jnp.int32) * rb
    block_e = jnp.minimum(jnp.searchsorted(pends, block_start, side='right'), N_EXPERTS - 1).astype(jnp.int32)
    tok = jnp.arange(t * TOP_K, dtype=jnp.int32) // TOP_K
    src_tok = (jnp.arange(n_blocks * rb, dtype=jnp.int32) % t).at[dest].set(tok)
    has_rows = jnp.cumsum((counts > 0).astype(jnp.int32))
    elist = jnp.minimum(jnp.searchsorted(has_rows, jnp.arange(1, N_EXPERTS + 1, dtype=jnp.int32), side='left'),
                        N_EXPERTS - 1).astype(jnp.int32)
    block_ord = (has_rows - 1)[block_e].astype(jnp.int32)
    counts2 = jnp.stack([pends[-1] // rb, has_rows[-1]]).astype(jnp.int32)

    ybuf = _experts(block_ord, elist, src_tok, counts2, h2, w1_exp, w3_exp, w2_exp)
    out = _combine(dest, ybuf, gate, x1, mod, row(g_ffn_post), seq)
    return out.reshape(bsz, seq, d)


def kernel(x, c, positions, w_ada, b_ada, g_mix_pre, g_mix_post, g_ffn_pre, g_ffn_post, w_in, conv_w, conv_b, filt_w1, filt_b1, filt_w2, filt_b2, filt_w3, filt_freq, hyena_skip, w_branch_gate, b_branch_gate, w_hy_o, w_at_o, w_out, w_group, b_group, w_expert, b_expert, w1_exp, w3_exp, w2_exp):
    depth = w_ada.shape[0]
    for l in range(depth):
        x = _layer(x, c, positions, w_ada[l], b_ada[l], g_mix_pre[l], g_mix_post[l], g_ffn_pre[l],
                   g_ffn_post[l], w_in[l], conv_w[l], conv_b[l], filt_w1[l], filt_b1[l], filt_w2[l],
                   filt_b2[l], filt_w3[l], filt_freq[l], hyena_skip[l], w_branch_gate[l],
                   b_branch_gate[l], w_hy_o[l], w_at_o[l], w_out[l], w_group[l], b_group[l],
                   w_expert[l], b_expert[l], w1_exp[l], w3_exp[l], w2_exp[l])
    return x
```
